```python
import jax, jax.numpy as jnp
from jax import lax
import numpy as np

D_MODEL = 4096
BATCH = 16
SEQ = 2048
DEPTH = 2

N_EVEN = (DEPTH + 1) // 2
N_ODD = DEPTH // 2

MIX_WIDTH = D_MODEL
RWKV_WIDTH = MIX_WIDTH // 2
RWKV_HEAD = 64
RWKV_HEADS = RWKV_WIDTH // RWKV_HEAD
DECAY_LORA = 128
ICLR_LORA = 128
SB_WIDTH = MIX_WIDTH - RWKV_WIDTH
SB_HEAD = 128
SB_HEADS = SB_WIDTH // SB_HEAD
SB_BLOCK = 128
SGU_WIDTH = MIX_WIDTH
SGU_CHUNK = 128
SGU_GROUPS = 16
SGU_GROUP_DIM = SGU_WIDTH // SGU_GROUPS

RMS_EPS = 1e-6
GN_EPS = 64e-5
LN_EPS = 1e-5
L2_EPS = 1e-12

RWKV_SHIFT_COLS = 3 * RWKV_WIDTH + DECAY_LORA + ICLR_LORA
EVEN_IN_COLS = RWKV_SHIFT_COLS + RWKV_WIDTH + 4 * SB_WIDTH
ODD_IN_COLS = 3 * SGU_WIDTH

kernel_name = 'hybrid_rwkv7_stickbreak_chunksgu'


def rmsnorm(x, g):
    xf = x.astype(jnp.float32)
    y = xf * lax.rsqrt(jnp.mean(xf * xf, axis=-1, keepdims=True) + RMS_EPS)
    return (y * g.astype(jnp.float32)).astype(x.dtype)


def token_shift(p):
    return jnp.pad(p[:, :-1], ((0, 0), (1, 0), (0, 0)))


def rwkv7_step(state, inp):
    r_t, w_t, k_t, v_t, kk_t, a_t = inp
    sa = jnp.einsum('bhvk,bhk->bhv', state, -kk_t)
    state = (state * w_t[:, :, None, :]
             + sa[..., None] * (kk_t * a_t)[:, :, None, :]
             + v_t[..., None] * k_t[:, :, None, :])
    y = jnp.einsum('bhvk,bhk->bhv', state, r_t)
    return state, y


def rwkv7_mix(p, w_dec_up, w0, a_up, a0, k_k, k_a, r_k, gn_g, gn_b):
    f32 = jnp.float32
    B, S, _ = p.shape
    r, k, v, w_lo, a_lo = jnp.split(
        p.astype(f32),
        [RWKV_WIDTH, 2 * RWKV_WIDTH, 3 * RWKV_WIDTH, 3 * RWKV_WIDTH + DECAY_LORA], axis=-1)
    w_log = -jax.nn.softplus(-(w0.astype(f32) + jnp.tanh(w_lo) @ w_dec_up.astype(f32))) - 0.5
    decay = jnp.exp(-jnp.exp(w_log))
    a = jax.nn.sigmoid(a0.astype(f32) + a_lo @ a_up.astype(f32))

    def heads(t):
        return t.reshape(B, S, RWKV_HEADS, RWKV_HEAD)

    kk = heads(k * k_k.astype(f32))
    kk = kk / jnp.maximum(jnp.sqrt(jnp.sum(kk * kk, axis=-1, keepdims=True)), L2_EPS)
    k = k * (1.0 + (a - 1.0) * k_a.astype(f32))
    r_h, k_h, v_h, w_h, a_h = heads(r), heads(k), heads(v), heads(decay), heads(a)

    xs = tuple(jnp.moveaxis(t, 1, 0) for t in (r_h, w_h, k_h, v_h, kk, a_h))
    state0 = jnp.zeros((B, RWKV_HEADS, RWKV_HEAD, RWKV_HEAD), f32)
    _, ys = lax.scan(rwkv7_step, state0, xs)
    y = jnp.moveaxis(ys, 0, 1)

    mu = jnp.mean(y, axis=-1, keepdims=True)
    var = jnp.mean(jnp.square(y - mu), axis=-1, keepdims=True)
    y = ((y - mu) * lax.rsqrt(var + GN_EPS)).reshape(B, S, RWKV_WIDTH)
    y = y * gn_g.astype(f32) + gn_b.astype(f32)
    bonus = jnp.sum(r_h * k_h * r_k.astype(f32).reshape(RWKV_HEADS, RWKV_HEAD),
                    axis=-1, keepdims=True) * v_h
    return y + bonus.reshape(B, S, RWKV_WIDTH)


def stick_breaking_attention(q, k, v):
    B, S, _ = q.shape

    def heads(t):
        return jnp.transpose(t.reshape(B, S, SB_HEADS, SB_HEAD), (0, 2, 1, 3))

    qh, kh, vh = heads(q), heads(k), heads(v)
    scale = 1.0 / np.sqrt(SB_HEAD)
    outs = []
    for blk in range(S // SB_BLOCK):
        q0 = blk * SB_BLOCK
        k_end = q0 + SB_BLOCK
        z = jnp.einsum('bhtd,bhsd->bhts', qh[:, :, q0:k_end], kh[:, :, :k_end]).astype(jnp.float32) * scale
        t_pos = q0 + jnp.arange(SB_BLOCK)[:, None]
        s_pos = jnp.arange(k_end)[None, :]
        causal = s_pos < t_pos
        log_keep = jnp.where(causal, jax.nn.log_sigmoid(-z), 0.0)
        later = lax.cumsum(log_keep, axis=3, reverse=True) - log_keep
        att = jnp.where(causal, jnp.exp(jax.nn.log_sigmoid(z) + later), 0.0)
        outs.append(jnp.einsum('bhts,bhsd->bhtd', att.astype(vh.dtype), vh[:, :, :k_end]))
    o = jnp.concatenate(outs, axis=2)
    return jnp.transpose(o, (0, 2, 1, 3)).reshape(B, S, SB_WIDTH)


def even_layer(h, w_in, shift_mu, w_dec_up, w0, a_up, a0, k_k, k_a, r_k, gn_g, gn_b, w_out):
    p = h @ w_in
    s1 = RWKV_SHIFT_COLS
    s2 = s1 + RWKV_WIDTH
    s3 = s2 + SB_WIDTH
    s4 = s3 + SB_WIDTH
    s5 = s4 + SB_WIDTH
    p_rwkv, g_rwkv, q, k, v, g_sb = jnp.split(p, [s1, s2, s3, s4, s5], axis=-1)
    p_rwkv = p_rwkv + shift_mu * (token_shift(p_rwkv) - p_rwkv)
    y_a = rwkv7_mix(p_rwkv, w_dec_up, w0, a_up, a0, k_k, k_a, r_k, gn_g, gn_b)
    y_a = y_a * jax.nn.silu(g_rwkv.astype(jnp.float32))
    y_b = stick_breaking_attention(q, k, v).astype(jnp.float32) * jax.nn.silu(g_sb.astype(jnp.float32))
    y = jnp.concatenate([y_a, y_b], axis=-1).astype(h.dtype)
    return (y @ w_out).astype(h.dtype)


def odd_layer(h, w_in, ln_g, ln_b, w_s, b_s, w_out):
    B, S, _ = h.shape
    u, v, g = jnp.split(h @ w_in, 3, axis=-1)
    u = jax.nn.gelu(u.astype(jnp.float32), approximate=False)
    v = jax.nn.gelu(v.astype(jnp.float32), approximate=False)
    mu = jnp.mean(v, axis=-1, keepdims=True)
    var = jnp.mean(jnp.square(v - mu), axis=-1, keepdims=True)
    v = (v - mu) * lax.rsqrt(var + LN_EPS) * ln_g.astype(jnp.float32) + ln_b.astype(jnp.float32)
    vc = v.reshape(B, S // SGU_CHUNK, SGU_CHUNK, SGU_GROUPS, SGU_GROUP_DIM)
    causal = jnp.tril(jnp.ones((SGU_CHUNK, SGU_CHUNK), dtype=bool))
    ws = jnp.where(causal[None], w_s.astype(jnp.float32), 0.0)
    mixed = jnp.einsum('gts,bcsgd->bctgd', ws, vc) + b_s.astype(jnp.float32).T[:, :, None]
    mixed = mixed.reshape(B, S, SGU_WIDTH)
    y = (u * mixed * jax.nn.silu(g.astype(jnp.float32))).astype(h.dtype)
    return (y @ w_out).astype(h.dtype)


def _fwd_setup_inputs(seed: int = 0) -> dict:
    key = jax.random.key(seed)
    ks = jax.random.split(key, 24)
    f32 = jnp.float32
    W = RWKV_WIDTH

    def nrm(k, shape, s):
        return jax.random.normal(k, shape, f32) * s

    return {
        'x': jax.random.normal(ks[0], (BATCH, SEQ, D_MODEL), f32),
        'norm_g': 1.0 + nrm(ks[1], (DEPTH, D_MODEL), 0.02),
        'final_norm_g': 1.0 + nrm(ks[2], (D_MODEL,), 0.02),
        'e_w_in': nrm(ks[3], (N_EVEN, D_MODEL, EVEN_IN_COLS), D_MODEL ** -0.5),
        'e_shift_mu': jax.random.uniform(ks[4], (N_EVEN, RWKV_SHIFT_COLS), f32),
        'e_w_decay_up': nrm(ks[5], (N_EVEN, DECAY_LORA, W), 0.5 * DECAY_LORA ** -0.5),
        'e_w0': jax.random.uniform(ks[6], (N_EVEN, W), f32, -6.0, -1.0),
        'e_a_up': nrm(ks[7], (N_EVEN, ICLR_LORA, W), 0.5 * ICLR_LORA ** -0.5),
        'e_a0': nrm(ks[8], (N_EVEN, W), 0.1),
        'e_k_k': 0.85 + nrm(ks[9], (N_EVEN, W), 0.05),
        'e_k_a': 1.0 + nrm(ks[10], (N_EVEN, W), 0.05),
        'e_r_k': nrm(ks[11], (N_EVEN, W), 0.1),
        'e_gn_g': 1.0 + nrm(ks[12], (N_EVEN, W), 0.02),
        'e_gn_b': nrm(ks[13], (N_EVEN, W), 0.02),
        'e_w_out': nrm(ks[14], (N_EVEN, MIX_WIDTH, D_MODEL), MIX_WIDTH ** -0.5),
        'o_w_in': nrm(ks[15], (N_ODD, D_MODEL, ODD_IN_COLS), D_MODEL ** -0.5),
        'o_ln_g': 1.0 + nrm(ks[16], (N_ODD, SGU_WIDTH), 0.02),
        'o_ln_b': nrm(ks[17], (N_ODD, SGU_WIDTH), 0.02),
        'o_w_s': nrm(ks[18], (N_ODD, SGU_GROUPS, SGU_CHUNK, SGU_CHUNK), 0.5 * SGU_CHUNK ** -0.5),
        'o_b_s': 1.0 + nrm(ks[19], (N_ODD, SGU_GROUPS, SGU_CHUNK), 0.1),
        'o_w_out': nrm(ks[20], (N_ODD, SGU_WIDTH, D_MODEL), SGU_WIDTH ** -0.5),
    }


def _fwd_reference(x, norm_g, final_norm_g, e_w_in, e_shift_mu, e_w_decay_up, e_w0, e_a_up, e_a0,
              e_k_k, e_k_a, e_r_k, e_gn_g, e_gn_b, e_w_out, o_w_in, o_ln_g, o_ln_b, o_w_s,
              o_b_s, o_w_out):
    for layer in range(DEPTH):
        h = rmsnorm(x, norm_g[layer])
        i = layer // 2
        if layer % 2 == 0:
            x = x + even_layer(h, e_w_in[i], e_shift_mu[i], e_w_decay_up[i], e_w0[i], e_a_up[i],
                               e_a0[i], e_k_k[i], e_k_a[i], e_r_k[i], e_gn_g[i], e_gn_b[i], e_w_out[i])
        else:
            x = x + odd_layer(h, o_w_in[i], o_ln_g[i], o_ln_b[i], o_w_s[i], o_b_s[i], o_w_out[i])
    return rmsnorm(x, final_norm_g)


import jax as _jax
import jax.numpy as _jnp

TWIN_FORMAT = 'train_step'
FWD_PARAMS = ['x', 'norm_g', 'final_norm_g', 'e_w_in', 'e_shift_mu', 'e_w_decay_up', 'e_w0', 'e_a_up', 'e_a0', 'e_k_k', 'e_k_a', 'e_r_k', 'e_gn_g', 'e_gn_b', 'e_w_out', 'o_w_in', 'o_ln_g', 'o_ln_b', 'o_w_s', 'o_b_s', 'o_w_out']
TWIN_WEIGHTS = ['norm_g', 'final_norm_g', 'e_w_in', 'e_shift_mu', 'e_w_decay_up', 'e_w0', 'e_a_up', 'e_a0', 'e_k_k', 'e_k_a', 'e_r_k', 'e_gn_g', 'e_gn_b', 'e_w_out', 'o_w_in', 'o_ln_g', 'o_ln_b', 'o_w_s', 'o_b_s', 'o_w_out']
TWIN_DIFF_INPUT = 'x'
TWIN_INPUTS = ['x', 'norm_g', 'final_norm_g', 'e_w_in', 'e_shift_mu', 'e_w_decay_up', 'e_w0', 'e_a_up', 'e_a0', 'e_k_k', 'e_k_a', 'e_r_k', 'e_gn_g', 'e_gn_b', 'e_w_out', 'o_w_in', 'o_ln_g', 'o_ln_b', 'o_w_s', 'o_b_s', 'o_w_out', 'loss_target', 'm_norm_g', 'm_final_norm_g', 'm_e_w_in', 'm_e_shift_mu', 'm_e_w_decay_up', 'm_e_w0', 'm_e_a_up', 'm_e_a0', 'm_e_k_k', 'm_e_k_a', 'm_e_r_k', 'm_e_gn_g', 'm_e_gn_b', 'm_e_w_out', 'm_o_w_in', 'm_o_ln_g', 'm_o_ln_b', 'm_o_w_s', 'm_o_b_s', 'm_o_w_out', 'v_norm_g', 'v_final_norm_g', 'v_e_w_in', 'v_e_shift_mu', 'v_e_w_decay_up', 'v_e_w0', 'v_e_a_up', 'v_e_a0', 'v_e_k_k', 'v_e_k_a', 'v_e_r_k', 'v_e_gn_g', 'v_e_gn_b', 'v_e_w_out', 'v_o_w_in', 'v_o_ln_g', 'v_o_ln_b', 'v_o_w_s', 'v_o_b_s', 'v_o_w_out']
TWIN_OUTPUTS = ['loss', 'grad_x', 'grad_norm_g', 'grad_final_norm_g', 'grad_e_w_in', 'grad_e_shift_mu', 'grad_e_w_decay_up', 'grad_e_w0', 'grad_e_a_up', 'grad_e_a0', 'grad_e_k_k', 'grad_e_k_a', 'grad_e_r_k', 'grad_e_gn_g', 'grad_e_gn_b', 'grad_e_w_out', 'grad_o_w_in', 'grad_o_ln_g', 'grad_o_ln_b', 'grad_o_w_s', 'grad_o_b_s', 'grad_o_w_out', 'delta_norm_g', 'delta_final_norm_g', 'delta_e_w_in', 'delta_e_shift_mu', 'delta_e_w_decay_up', 'delta_e_w0', 'delta_e_a_up', 'delta_e_a0', 'delta_e_k_k', 'delta_e_k_a', 'delta_e_r_k', 'delta_e_gn_g', 'delta_e_gn_b', 'delta_e_w_out', 'delta_o_w_in', 'delta_o_ln_g', 'delta_o_ln_b', 'delta_o_w_s', 'delta_o_b_s', 'delta_o_w_out', 'new_m_norm_g', 'new_m_final_norm_g', 'new_m_e_w_in', 'new_m_e_shift_mu', 'new_m_e_w_decay_up', 'new_m_e_w0', 'new_m_e_a_up', 'new_m_e_a0', 'new_m_e_k_k', 'new_m_e_k_a', 'new_m_e_r_k', 'new_m_e_gn_g', 'new_m_e_gn_b', 'new_m_e_w_out', 'new_m_o_w_in', 'new_m_o_ln_g', 'new_m_o_ln_b', 'new_m_o_w_s', 'new_m_o_b_s', 'new_m_o_w_out', 'new_v_norm_g', 'new_v_final_norm_g', 'new_v_e_w_in', 'new_v_e_shift_mu', 'new_v_e_w_decay_up', 'new_v_e_w0', 'new_v_e_a_up', 'new_v_e_a0', 'new_v_e_k_k', 'new_v_e_k_a', 'new_v_e_r_k', 'new_v_e_gn_g', 'new_v_e_gn_b', 'new_v_e_w_out', 'new_v_o_w_in', 'new_v_o_ln_g', 'new_v_o_ln_b', 'new_v_o_w_s', 'new_v_o_b_s', 'new_v_o_w_out']
TWIN_LEAF_KINDS = {'loss': 'loss', 'grad_x': 'grad_x', 'grad_norm_g': 'grad_w', 'grad_final_norm_g': 'grad_w', 'grad_e_w_in': 'grad_w', 'grad_e_shift_mu': 'grad_w', 'grad_e_w_decay_up': 'grad_w', 'grad_e_w0': 'grad_w', 'grad_e_a_up': 'grad_w', 'grad_e_a0': 'grad_w', 'grad_e_k_k': 'grad_w', 'grad_e_k_a': 'grad_w', 'grad_e_r_k': 'grad_w', 'grad_e_gn_g': 'grad_w', 'grad_e_gn_b': 'grad_w', 'grad_e_w_out': 'grad_w', 'grad_o_w_in': 'grad_w', 'grad_o_ln_g': 'grad_w', 'grad_o_ln_b': 'grad_w', 'grad_o_w_s': 'grad_w', 'grad_o_b_s': 'grad_w', 'grad_o_w_out': 'grad_w', 'delta_norm_g': 'delta_w', 'delta_final_norm_g': 'delta_w', 'delta_e_w_in': 'delta_w', 'delta_e_shift_mu': 'delta_w', 'delta_e_w_decay_up': 'delta_w', 'delta_e_w0': 'delta_w', 'delta_e_a_up': 'delta_w', 'delta_e_a0': 'delta_w', 'delta_e_k_k': 'delta_w', 'delta_e_k_a': 'delta_w', 'delta_e_r_k': 'delta_w', 'delta_e_gn_g': 'delta_w', 'delta_e_gn_b': 'delta_w', 'delta_e_w_out': 'delta_w', 'delta_o_w_in': 'delta_w', 'delta_o_ln_g': 'delta_w', 'delta_o_ln_b': 'delta_w', 'delta_o_w_s': 'delta_w', 'delta_o_b_s': 'delta_w', 'delta_o_w_out': 'delta_w', 'new_m_norm_g': 'new_m', 'new_m_final_norm_g': 'new_m', 'new_m_e_w_in': 'new_m', 'new_m_e_shift_mu': 'new_m', 'new_m_e_w_decay_up': 'new_m', 'new_m_e_w0': 'new_m', 'new_m_e_a_up': 'new_m', 'new_m_e_a0': 'new_m', 'new_m_e_k_k': 'new_m', 'new_m_e_k_a': 'new_m', 'new_m_e_r_k': 'new_m', 'new_m_e_gn_g': 'new_m', 'new_m_e_gn_b': 'new_m', 'new_m_e_w_out': 'new_m', 'new_m_o_w_in': 'new_m', 'new_m_o_ln_g': 'new_m', 'new_m_o_ln_b': 'new_m', 'new_m_o_w_s': 'new_m', 'new_m_o_b_s': 'new_m', 'new_m_o_w_out': 'new_m', 'new_v_norm_g': 'new_v', 'new_v_final_norm_g': 'new_v', 'new_v_e_w_in': 'new_v', 'new_v_e_shift_mu': 'new_v', 'new_v_e_w_decay_up': 'new_v', 'new_v_e_w0': 'new_v', 'new_v_e_a_up': 'new_v', 'new_v_e_a0': 'new_v', 'new_v_e_k_k': 'new_v', 'new_v_e_k_a': 'new_v', 'new_v_e_r_k': 'new_v', 'new_v_e_gn_g': 'new_v', 'new_v_e_gn_b': 'new_v', 'new_v_e_w_out': 'new_v', 'new_v_o_w_in': 'new_v', 'new_v_o_ln_g': 'new_v', 'new_v_o_ln_b': 'new_v', 'new_v_o_w_s': 'new_v', 'new_v_o_b_s': 'new_v', 'new_v_o_w_out': 'new_v'}


def _forward(args):
    return _fwd_reference(*[args[k] for k in FWD_PARAMS])


def _output_shape():
    def fwd():
        inp = _fwd_setup_inputs(0)
        return _fwd_reference(*[inp[k] for k in FWD_PARAMS])
    out = _jax.eval_shape(fwd)
    return out.shape, out.dtype

N_MICROBATCH = 1
ADAM_LR = 0.001
ADAM_B1 = 0.9
ADAM_B2 = 0.999
ADAM_EPS = 1e-08
ADAM_WD = 0.01
ADAM_STEP = 10
PER_EXAMPLE_BATCH_AXIS = {'x': 0, 'loss_target': 0}
SHARED_INPUTS = []
_WEIGHT_DTYPES = {'norm_g': _jnp.float32, 'final_norm_g': _jnp.float32, 'e_w_in': _jnp.float32, 'e_shift_mu': _jnp.float32, 'e_w_decay_up': _jnp.float32, 'e_w0': _jnp.float32, 'e_a_up': _jnp.float32, 'e_a0': _jnp.float32, 'e_k_k': _jnp.float32, 'e_k_a': _jnp.float32, 'e_r_k': _jnp.float32, 'e_gn_g': _jnp.float32, 'e_gn_b': _jnp.float32, 'e_w_out': _jnp.float32, 'o_w_in': _jnp.float32, 'o_ln_g': _jnp.float32, 'o_ln_b': _jnp.float32, 'o_w_s': _jnp.float32, 'o_b_s': _jnp.float32, 'o_w_out': _jnp.float32}
MOMENT_SCALE = {'norm_g': 3.490821e-02, 'final_norm_g': 8.001042e+00, 'e_w_in': 2.155176e-02, 'e_shift_mu': 4.500497e-02, 'e_w_decay_up': 1.209722e-03, 'e_w0': 9.946254e-03, 'e_a_up': 1.063345e-02, 'e_a0': 1.180187e-02, 'e_k_k': 2.722279e-02, 'e_k_a': 2.854102e-02, 'e_r_k': 5.601139e-02, 'e_gn_g': 2.456676e-02, 'e_gn_b': 2.525823e-02, 'e_w_out': 2.161501e-02, 'o_w_in': 1.338993e-02, 'o_ln_g': 5.071636e-03, 'o_ln_b': 5.136324e-03, 'o_w_s': 1.450897e-02, 'o_b_s': 2.132191e-02, 'o_w_out': 1.538142e-02}


def _to_microbatches(a, axis):
    t = _jnp.moveaxis(a, axis, 0)
    t = t.reshape((N_MICROBATCH, t.shape[0] // N_MICROBATCH) + t.shape[1:])
    return _jnp.moveaxis(t, 1, axis + 1)


def setup_inputs(seed: int = 0) -> dict:
    inp = _fwd_setup_inputs(seed)
    key = _jax.random.fold_in(_jax.random.key(seed), 7919)
    shape, _ = _output_shape()
    out = dict(inp)
    out["loss_target"] = _jax.random.normal(_jax.random.fold_in(key, 0), shape, _jnp.float32)
    for i, name in enumerate(TWIN_WEIGHTS):
        w = inp[name].astype(_jnp.float32)
        if MOMENT_SCALE is None:
            s = _jnp.sqrt(_jnp.mean(_jnp.square(w)) + 1e-30)
        else:
            s = MOMENT_SCALE[name]
        km, kv = _jax.random.split(_jax.random.fold_in(key, i + 1))
        out[name] = w
        out["m_" + name] = s * _jax.random.normal(km, w.shape, _jnp.float32)
        out["v_" + name] = (s * s) * _jax.random.uniform(kv, w.shape, _jnp.float32, 0.5, 1.5)
    if N_MICROBATCH > 1:
        for name, axis in PER_EXAMPLE_BATCH_AXIS.items():
            out[name] = _to_microbatches(out[name], axis)
    return {'x': out['x'], 'norm_g': out['norm_g'], 'final_norm_g': out['final_norm_g'], 'e_w_in': out['e_w_in'], 'e_shift_mu': out['e_shift_mu'], 'e_w_decay_up': out['e_w_decay_up'], 'e_w0': out['e_w0'], 'e_a_up': out['e_a_up'], 'e_a0': out['e_a0'], 'e_k_k': out['e_k_k'], 'e_k_a': out['e_k_a'], 'e_r_k': out['e_r_k'], 'e_gn_g': out['e_gn_g'], 'e_gn_b': out['e_gn_b'], 'e_w_out': out['e_w_out'], 'o_w_in': out['o_w_in'], 'o_ln_g': out['o_ln_g'], 'o_ln_b': out['o_ln_b'], 'o_w_s': out['o_w_s'], 'o_b_s': out['o_b_s'], 'o_w_out': out['o_w_out'], 'loss_target': out['loss_target'], 'm_norm_g': out['m_norm_g'], 'm_final_norm_g': out['m_final_norm_g'], 'm_e_w_in': out['m_e_w_in'], 'm_e_shift_mu': out['m_e_shift_mu'], 'm_e_w_decay_up': out['m_e_w_decay_up'], 'm_e_w0': out['m_e_w0'], 'm_e_a_up': out['m_e_a_up'], 'm_e_a0': out['m_e_a0'], 'm_e_k_k': out['m_e_k_k'], 'm_e_k_a': out['m_e_k_a'], 'm_e_r_k': out['m_e_r_k'], 'm_e_gn_g': out['m_e_gn_g'], 'm_e_gn_b': out['m_e_gn_b'], 'm_e_w_out': out['m_e_w_out'], 'm_o_w_in': out['m_o_w_in'], 'm_o_ln_g': out['m_o_ln_g'], 'm_o_ln_b': out['m_o_ln_b'], 'm_o_w_s': out['m_o_w_s'], 'm_o_b_s': out['m_o_b_s'], 'm_o_w_out': out['m_o_w_out'], 'v_norm_g': out['v_norm_g'], 'v_final_norm_g': out['v_final_norm_g'], 'v_e_w_in': out['v_e_w_in'], 'v_e_shift_mu': out['v_e_shift_mu'], 'v_e_w_decay_up': out['v_e_w_decay_up'], 'v_e_w0': out['v_e_w0'], 'v_e_a_up': out['v_e_a_up'], 'v_e_a0': out['v_e_a0'], 'v_e_k_k': out['v_e_k_k'], 'v_e_k_a': out['v_e_k_a'], 'v_e_r_k': out['v_e_r_k'], 'v_e_gn_g': out['v_e_gn_g'], 'v_e_gn_b': out['v_e_gn_b'], 'v_e_w_out': out['v_e_w_out'], 'v_o_w_in': out['v_o_w_in'], 'v_o_ln_g': out['v_o_ln_g'], 'v_o_ln_b': out['v_o_ln_b'], 'v_o_w_s': out['v_o_w_s'], 'v_o_b_s': out['v_o_b_s'], 'v_o_w_out': out['v_o_w_out']}


def _loss(weights, diff, rest, loss_target):
    with _jax.named_scope("forward"):
        args = {**rest, TWIN_DIFF_INPUT: diff, **{k: w.astype(_WEIGHT_DTYPES[k]) for k, w in weights.items()}}
        y = _forward(args)
    with _jax.named_scope("loss_head"):
        err = _jnp.square(y.astype(_jnp.float32) - loss_target)
        return 0.5 * _jnp.sum(_jnp.mean(err, axis=-1)) if err.ndim else 0.5 * err


def _adamw(w, g, m, v):
    m = ADAM_B1 * m + (1.0 - ADAM_B1) * g
    v = ADAM_B2 * v + (1.0 - ADAM_B2) * _jnp.square(g)
    m_hat = m / (1.0 - ADAM_B1 ** ADAM_STEP)
    v_hat = v / (1.0 - ADAM_B2 ** ADAM_STEP)
    delta = -ADAM_LR * (m_hat / (_jnp.sqrt(v_hat) + ADAM_EPS) + ADAM_WD * w)
    return delta, m, v


def reference(x, norm_g, final_norm_g, e_w_in, e_shift_mu, e_w_decay_up, e_w0, e_a_up, e_a0, e_k_k, e_k_a, e_r_k, e_gn_g, e_gn_b, e_w_out, o_w_in, o_ln_g, o_ln_b, o_w_s, o_b_s, o_w_out, loss_target, m_norm_g, m_final_norm_g, m_e_w_in, m_e_shift_mu, m_e_w_decay_up, m_e_w0, m_e_a_up, m_e_a0, m_e_k_k, m_e_k_a, m_e_r_k, m_e_gn_g, m_e_gn_b, m_e_w_out, m_o_w_in, m_o_ln_g, m_o_ln_b, m_o_w_s, m_o_b_s, m_o_w_out, v_norm_g, v_final_norm_g, v_e_w_in, v_e_shift_mu, v_e_w_decay_up, v_e_w0, v_e_a_up, v_e_a0, v_e_k_k, v_e_k_a, v_e_r_k, v_e_gn_g, v_e_gn_b, v_e_w_out, v_o_w_in, v_o_ln_g, v_o_ln_b, v_o_w_s, v_o_b_s, v_o_w_out):
    given = dict(x=x, norm_g=norm_g, final_norm_g=final_norm_g, e_w_in=e_w_in, e_shift_mu=e_shift_mu, e_w_decay_up=e_w_decay_up, e_w0=e_w0, e_a_up=e_a_up, e_a0=e_a0, e_k_k=e_k_k, e_k_a=e_k_a, e_r_k=e_r_k, e_gn_g=e_gn_g, e_gn_b=e_gn_b, e_w_out=e_w_out, o_w_in=o_w_in, o_ln_g=o_ln_g, o_ln_b=o_ln_b, o_w_s=o_w_s, o_b_s=o_b_s, o_w_out=o_w_out, loss_target=loss_target, m_norm_g=m_norm_g, m_final_norm_g=m_final_norm_g, m_e_w_in=m_e_w_in, m_e_shift_mu=m_e_shift_mu, m_e_w_decay_up=m_e_w_decay_up, m_e_w0=m_e_w0, m_e_a_up=m_e_a_up, m_e_a0=m_e_a0, m_e_k_k=m_e_k_k, m_e_k_a=m_e_k_a, m_e_r_k=m_e_r_k, m_e_gn_g=m_e_gn_g, m_e_gn_b=m_e_gn_b, m_e_w_out=m_e_w_out, m_o_w_in=m_o_w_in, m_o_ln_g=m_o_ln_g, m_o_ln_b=m_o_ln_b, m_o_w_s=m_o_w_s, m_o_b_s=m_o_b_s, m_o_w_out=m_o_w_out, v_norm_g=v_norm_g, v_final_norm_g=v_final_norm_g, v_e_w_in=v_e_w_in, v_e_shift_mu=v_e_shift_mu, v_e_w_decay_up=v_e_w_decay_up, v_e_w0=v_e_w0, v_e_a_up=v_e_a_up, v_e_a0=v_e_a0, v_e_k_k=v_e_k_k, v_e_k_a=v_e_k_a, v_e_r_k=v_e_r_k, v_e_gn_g=v_e_gn_g, v_e_gn_b=v_e_gn_b, v_e_w_out=v_e_w_out, v_o_w_in=v_o_w_in, v_o_ln_g=v_o_ln_g, v_o_ln_b=v_o_ln_b, v_o_w_s=v_o_w_s, v_o_b_s=v_o_b_s, v_o_w_out=v_o_w_out)
    weights = {n: given[n] for n in TWIN_WEIGHTS}
    shared = {n: given[n] for n in SHARED_INPUTS}
    per_example = {n: given[n] for n in ['x']}
    grad_fn = _jax.value_and_grad(_loss, argnums=(0, 1))

    def one_microbatch(ex, loss_target):
        ex = dict(ex)
        diff = ex.pop(TWIN_DIFF_INPUT)
        return grad_fn(weights, diff, {**shared, **ex}, loss_target)

    if N_MICROBATCH == 1:
        loss, (grad_w, grad_x) = one_microbatch(per_example, given["loss_target"])
    else:
        def body(carry, xs):
            loss_sum, grad_sum = carry
            l_k, (gw_k, gx_k) = one_microbatch(xs[0], xs[1])
            with _jax.named_scope("update"):
                return (loss_sum + l_k, _jax.tree.map(_jnp.add, grad_sum, gw_k)), gx_k

        init = (_jnp.zeros((), _jnp.float32), _jax.tree.map(_jnp.zeros_like, weights))
        (loss, grad_w), grad_x = _jax.lax.scan(body, init, (per_example, given["loss_target"]))
    with _jax.named_scope("update"):
        delta_w, new_m, new_v = {}, {}, {}
        for n in TWIN_WEIGHTS:
            delta_w[n], new_m[n], new_v[n] = _adamw(weights[n], grad_w[n], given["m_" + n], given["v_" + n])
    return (loss, grad_x, *[grad_w[n] for n in TWIN_WEIGHTS], *[delta_w[n] for n in TWIN_WEIGHTS],
            *[new_m[n] for n in TWIN_WEIGHTS], *[new_v[n] for n in TWIN_WEIGHTS])
```

```python
import functools

import jax
import jax.numpy as jnp
from jax import lax
from jax.experimental import pallas as pl
from jax.experimental.pallas import tpu as pltpu

F32 = jnp.float32
BF16 = jnp.bfloat16
HIGHEST = lax.Precision.HIGHEST

N_DEV = 8
RWKV_HEAD = 64
SB_HEAD = 128
SB_BLOCK = 128
SGU_CHUNK = 128
SGU_GROUPS = 16
RMS_EPS = 1e-6
GN_EPS = 64e-5
LN_EPS = 1e-5
L2_EPS = 1e-12
ADAM_LR = 0.001
ADAM_B1 = 0.9
ADAM_B2 = 0.999
ADAM_EPS = 1e-08
ADAM_WD = 0.01
ADAM_STEP = 10

VMEM_LIMIT_V7X = 56 * 1024 * 1024
LANES = 128
SCAN_STEPS_PER_BLOCK = 8


def _params(*sem):
    return pltpu.CompilerParams(dimension_semantics=sem, vmem_limit_bytes=VMEM_LIMIT_V7X)


def _tile(n, target, mult):
    best = None
    d = mult
    while d <= min(n, target):
        if n % d == 0:
            best = d
        d += mult
    return n if best is None else best


def _exchange(src, scatter, name):
    shard = src.shape[1:] if scatter else src.shape

    def body(src_ref, out_ref, send_sems, recv_sems, local_sem):
        x, y, c = lax.axis_index("x"), lax.axis_index("y"), lax.axis_index("c")
        me = 4 * x + 2 * y + c

        def part_for(dev):
            return src_ref.at[dev] if scatter else src_ref

        local = pltpu.make_async_copy(part_for(me), out_ref.at[me], local_sem)
        local.start()
        sends, recvs = [], []
        for k in range(1, N_DEV):
            px = 1 - x if k & 4 else x
            py = 1 - y if k & 2 else y
            pc = 1 - c if k & 1 else c
            peer = 4 * px + 2 * py + pc
            sends.append(pltpu.make_async_remote_copy(
                src_ref=part_for(peer), dst_ref=out_ref.at[me], send_sem=send_sems.at[k - 1],
                recv_sem=recv_sems.at[k - 1], device_id=(px, py, pc), device_id_type=pl.DeviceIdType.MESH))
            recvs.append(pltpu.make_async_remote_copy(
                src_ref=part_for(peer), dst_ref=out_ref.at[peer], send_sem=send_sems.at[k - 1],
                recv_sem=recv_sems.at[k - 1], device_id=(px, py, pc), device_id_type=pl.DeviceIdType.MESH))
        for cp in sends:
            cp.start()
        for cp in recvs:
            cp.wait_recv()
        for cp in sends:
            cp.wait_send()
        local.wait()

    return pl.pallas_call(
        body, name=name,
        out_shape=jax.ShapeDtypeStruct((N_DEV,) + tuple(shard), src.dtype),
        in_specs=[pl.BlockSpec(memory_space=pl.ANY)],
        out_specs=pl.BlockSpec(memory_space=pl.ANY),
        scratch_shapes=[pltpu.SemaphoreType.DMA((N_DEV - 1,)), pltpu.SemaphoreType.DMA((N_DEV - 1,)),
                        pltpu.SemaphoreType.DMA(())],
    )(src)


def _matmul(a, b, mode, out_dtype, name, res=None, tm=1024, tn=1024, tk=1024):
    if mode == "nn":
        (m, k), (k2, n) = a.shape, b.shape
    elif mode == "nt":
        (m, k), (n, k2) = a.shape, b.shape
    else:
        (k, m), (k2, n) = a.shape, b.shape
    assert k == k2, (a.shape, b.shape, mode)
    tm, tn, tk = _tile(m, tm, 128), _tile(n, tn, 128), _tile(k, tk, 128)
    nk = k // tk
    if mode == "nn":
        a_spec = pl.BlockSpec((tm, tk), lambda i, j, kk: (i, kk))
        b_spec = pl.BlockSpec((tk, tn), lambda i, j, kk: (kk, j))
        dims = (((1,), (0,)), ((), ()))
    elif mode == "nt":
        a_spec = pl.BlockSpec((tm, tk), lambda i, j, kk: (i, kk))
        b_spec = pl.BlockSpec((tn, tk), lambda i, j, kk: (j, kk))
        dims = (((1,), (1,)), ((), ()))
    else:
        a_spec = pl.BlockSpec((tk, tm), lambda i, j, kk: (kk, i))
        b_spec = pl.BlockSpec((tk, tn), lambda i, j, kk: (kk, j))
        dims = (((0,), (0,)), ((), ()))
    o_spec = pl.BlockSpec((tm, tn), lambda i, j, kk: (i, j))
    has_res = res is not None

    def body(*refs):
        if has_res:
            a_ref, b_ref, r_ref, o_ref, acc_ref = refs
        else:
            a_ref, b_ref, o_ref, acc_ref = refs
        kk = pl.program_id(2)

        @pl.when(kk == 0)
        def _():
            acc_ref[...] = jnp.zeros_like(acc_ref)

        acc_ref[...] += lax.dot_general(a_ref[...].astype(BF16), b_ref[...].astype(BF16), dims,
                                        preferred_element_type=F32)

        @pl.when(kk == nk - 1)
        def _():
            out = acc_ref[...]
            if has_res:
                out = out + r_ref[...]
            o_ref[...] = out.astype(out_dtype)

    ins = [a, b] + ([res] if has_res else [])
    specs = [a_spec, b_spec] + ([o_spec] if has_res else [])
    return pl.pallas_call(
        body, name=name, grid=(m // tm, n // tn, nk),
        out_shape=jax.ShapeDtypeStruct((m, n), out_dtype),
        in_specs=specs, out_specs=o_spec,
        scratch_shapes=[pltpu.VMEM((tm, tn), F32)],
        compiler_params=_params("parallel", "parallel", "arbitrary"),
    )(*ins)


def _rms_fwd(x, g, name):
    t, d = x.shape
    tm = _tile(t, 256, 8)

    def body(x_ref, g_ref, h_ref):
        xv = x_ref[...]
        rstd = lax.rsqrt(jnp.mean(xv * xv, axis=-1, keepdims=True) + RMS_EPS)
        h_ref[...] = (xv * rstd * g_ref[...]).astype(BF16)

    return pl.pallas_call(
        body, name=name, grid=(t // tm,),
        out_shape=jax.ShapeDtypeStruct((t, d), BF16),
        in_specs=[pl.BlockSpec((tm, d), lambda i: (i, 0)), pl.BlockSpec((1, d), lambda i: (0, 0))],
        out_specs=pl.BlockSpec((tm, d), lambda i: (i, 0)),
        compiler_params=_params("parallel"),
    )(x, g)


def _rms_bwd(x, g, dh, dres, name):
    t, d = x.shape
    tm = _tile(t, 256, 8)

    def body(x_ref, g_ref, dh_ref, dres_ref, dx_ref, dg_ref):
        @pl.when(pl.program_id(0) == 0)
        def _():
            dg_ref[...] = jnp.zeros_like(dg_ref)

        xv = x_ref[...]
        rstd = lax.rsqrt(jnp.mean(xv * xv, axis=-1, keepdims=True) + RMS_EPS)
        xhat = xv * rstd
        dh_v = dh_ref[...]
        dg_ref[...] += jnp.sum(dh_v * xhat, axis=0, keepdims=True)
        dxh = dh_v * g_ref[...]
        dx_ref[...] = dres_ref[...] + rstd * (dxh - xhat * jnp.mean(dxh * xhat, axis=-1, keepdims=True))

    row = pl.BlockSpec((tm, d), lambda i: (i, 0))
    vec = pl.BlockSpec((1, d), lambda i: (0, 0))
    return pl.pallas_call(
        body, name=name, grid=(t // tm,),
        out_shape=(jax.ShapeDtypeStruct((t, d), F32), jax.ShapeDtypeStruct((1, d), F32)),
        in_specs=[row, vec, row, row], out_specs=(row, vec),
        compiler_params=_params("arbitrary"),
    )(x, g, dh, dres)


def _final_loss(x, g, target, name):
    t, d = x.shape
    tm = _tile(t, 256, 8)

    def body(x_ref, g_ref, t_ref, dx_ref, dg_ref, loss_ref):
        @pl.when(pl.program_id(0) == 0)
        def _():
            dg_ref[...] = jnp.zeros_like(dg_ref)
            loss_ref[...] = jnp.zeros_like(loss_ref)

        xv = x_ref[...]
        rstd = lax.rsqrt(jnp.mean(xv * xv, axis=-1, keepdims=True) + RMS_EPS)
        xhat = xv * rstd
        gv = g_ref[...]
        err = xhat * gv - t_ref[...]
        loss_ref[...] += 0.5 * jnp.sum(jnp.mean(err * err, axis=-1, keepdims=True), axis=0, keepdims=True)
        dout = err * (1.0 / d)
        dg_ref[...] += jnp.sum(dout * xhat, axis=0, keepdims=True)
        dxh = dout * gv
        dx_ref[...] = rstd * (dxh - xhat * jnp.mean(dxh * xhat, axis=-1, keepdims=True))

    row = pl.BlockSpec((tm, d), lambda i: (i, 0))
    vec = pl.BlockSpec((1, d), lambda i: (0, 0))
    return pl.pallas_call(
        body, name=name, grid=(t // tm,),
        out_shape=(jax.ShapeDtypeStruct((t, d), F32), jax.ShapeDtypeStruct((1, d), F32),
                   jax.ShapeDtypeStruct((1, 1), F32)),
        in_specs=[row, vec, row], out_specs=(row, vec, pl.BlockSpec((1, 1), lambda i: (0, 0))),
        compiler_params=_params("arbitrary"),
    )(x, g, target)


def _dot32(a, b):
    return jnp.dot(a, b, precision=HIGHEST, preferred_element_type=F32)


def _head_sum(v, e, et):
    return _dot32(_dot32(v, e), et)


def _log_sigmoid(z):
    return jnp.minimum(z, 0.0) - jnp.log1p(jnp.exp(-jnp.abs(z)))


def _silu(g):
    return g * jax.nn.sigmoid(g)


def _prep_math(k, wlo, alo, wd, w0, wa, a0, k_k, k_a, e, et):
    wl = w0 + _dot32(jnp.tanh(wlo), wd)
    w_log = _log_sigmoid(wl) - 0.5
    w = jnp.exp(-jnp.exp(w_log))
    a = jax.nn.sigmoid(a0 + _dot32(alo, wa))
    kk0 = k * k_k
    kk = kk0 * lax.rsqrt(jnp.maximum(_head_sum(kk0 * kk0, e, et), L2_EPS * L2_EPS))
    kx = k * (1.0 + (a - 1.0) * k_a)
    return w, kx, kk, kk * a


def _post_math(ys, r, kx, v, g, gn_g, gn_b, r_k, e, et):
    inv = 1.0 / RWKV_HEAD
    mu = _head_sum(ys, e, et) * inv
    dlt = ys - mu
    var = _head_sum(dlt * dlt, e, et) * inv
    y = dlt * lax.rsqrt(var + GN_EPS) * gn_g + gn_b
    bonus = _head_sum(r * kx * r_k, e, et) * v
    return (y + bonus) * _silu(g)


def _shifted(p, prev_row, first):
    rows = lax.broadcasted_iota(jnp.int32, p.shape, 0)
    prev = jnp.where(first, 0.0, prev_row)
    return jnp.where(rows == 0, prev, pltpu.roll(p, 1, 0))


def _head_indicator(width):
    ch = lax.broadcasted_iota(jnp.int32, (width, width // RWKV_HEAD), 0) // RWKV_HEAD
    hd = lax.broadcasted_iota(jnp.int32, (width, width // RWKV_HEAD), 1)
    e = (ch == hd).astype(F32)
    return e, e.T


def _rwkv_prep(p, seq, dims, mu, wd, w0, wa, a0, k_k, k_a, e, et):
    t = p.shape[0]
    width, lora = dims
    cols = 3 * width + 2 * lora
    tm = 128
    per_seq = seq // tm

    def body(p_ref, prev_ref, mu_ref, wd_ref, w0_ref, wa_ref, a0_ref, kk_ref, ka_ref, e_ref, et_ref,
             r_out, w_out, kx_out, v_out, kkn_out, b_out):
        i = pl.program_id(0)
        pv = p_ref[...]
        psh = _shifted(pv, prev_ref[7:8, :], i % per_seq == 0)
        ps = pv + mu_ref[...] * (psh - pv)
        r, k, v = ps[:, :width], ps[:, width:2 * width], ps[:, 2 * width:3 * width]
        wlo, alo = ps[:, 3 * width:3 * width + lora], ps[:, 3 * width + lora:]
        w, kx, kk, b = _prep_math(k, wlo, alo, wd_ref[...], w0_ref[...], wa_ref[...], a0_ref[...],
                                  kk_ref[...], ka_ref[...], e_ref[...], et_ref[...])
        r_out[...] = r
        w_out[...] = w
        kx_out[...] = kx
        v_out[...] = v
        kkn_out[...] = kk
        b_out[...] = b

    full = lambda a: pl.BlockSpec(a.shape, lambda i: (0,) * a.ndim)
    out = pl.BlockSpec((tm, width), lambda i: (i, 0))
    return pl.pallas_call(
        body, name="rwkv_prep", grid=(t // tm,),
        out_shape=tuple(jax.ShapeDtypeStruct((t, width), F32) for _ in range(6)),
        in_specs=[pl.BlockSpec((tm, cols), lambda i: (i, 0)),
                  pl.BlockSpec((8, cols), lambda i: (jnp.maximum(i * (tm // 8) - 1, 0), 0)),
                  full(mu), full(wd), full(w0), full(wa), full(a0), full(k_k), full(k_a), full(e), full(et)],
        out_specs=tuple(out for _ in range(6)),
        compiler_params=_params("parallel"),
    )(p, p, mu, wd, w0, wa, a0, k_k, k_a, e, et)


def _rwkv_prep_bwd(p, seq, dims, mu, wd, w0, wa, a0, k_k, k_a, e, et, dr, dw, dkx, dv, dkk, db):
    t = p.shape[0]
    width, lora = dims
    cols = 3 * width + 2 * lora
    tm = 128
    n_tiles = t // tm
    per_seq = seq // tm

    def body(p_ref, prev_ref, mu_ref, wd_ref, w0_ref, wa_ref, a0_ref, kk_ref, ka_ref, e_ref, et_ref,
             dr_ref, dw_ref, dkx_ref, dv_ref, dkk_ref, db_ref,
             dp_out, dmu_out, dwd_out, dw0_out, dwa_out, da0_out, dkk_out, dka_out, carry):
        step = pl.program_id(0)
        i = n_tiles - 1 - step

        @pl.when(step == 0)
        def _():
            for ref in (dmu_out, dwd_out, dw0_out, dwa_out, da0_out, dkk_out, dka_out, carry):
                ref[...] = jnp.zeros_like(ref)

        pv = p_ref[...]
        first = i % per_seq == 0
        psh = _shifted(pv, prev_ref[7:8, :], first)
        muv = mu_ref[...]
        ps = pv + muv * (psh - pv)
        k = ps[:, width:2 * width]
        wlo, alo = ps[:, 3 * width:3 * width + lora], ps[:, 3 * width + lora:]
        ev, etv = e_ref[...], et_ref[...]
        _, vjp = jax.vjp(lambda *a: _prep_math(*a, ev, etv), k, wlo, alo, wd_ref[...], w0_ref[...],
                         wa_ref[...], a0_ref[...], kk_ref[...], ka_ref[...])
        dk, dwlo, dalo, dwd, dw0, dwa, da0, dk_k, dk_a = vjp(
            (dw_ref[...], dkx_ref[...], dkk_ref[...], db_ref[...]))
        dps = jnp.concatenate([dr_ref[...], dk, dv_ref[...], dwlo, dalo], axis=1)
        dmu_out[...] += jnp.sum(dps * (psh - pv), axis=0, keepdims=True)
        dwd_out[...] += dwd
        dw0_out[...] += dw0
        dwa_out[...] += dwa
        da0_out[...] += da0
        dkk_out[...] += dk_k
        dka_out[...] += dk_a
        dsh = dps * muv
        rows = lax.broadcasted_iota(jnp.int32, dsh.shape, 0)
        nxt = jnp.where(rows == tm - 1, carry[...], pltpu.roll(dsh, tm - 1, 0))
        dp_out[...] = (dps * (1.0 - muv) + nxt).astype(BF16)
        carry[...] = jnp.where(first, 0.0, dsh[0:1, :])

    full = lambda a: pl.BlockSpec(a.shape, lambda s: (0,) * a.ndim)
    tok = pl.BlockSpec((tm, width), lambda s: (n_tiles - 1 - s, 0))
    vec = lambda n: jax.ShapeDtypeStruct((1, n), F32)
    outs = (jax.ShapeDtypeStruct((t, cols), BF16), vec(cols), jax.ShapeDtypeStruct(wd.shape, F32), vec(width),
            jax.ShapeDtypeStruct(wa.shape, F32), vec(width), vec(width), vec(width))
    return pl.pallas_call(
        body, name="rwkv_prep_bwd", grid=(n_tiles,),
        out_shape=outs,
        in_specs=[pl.BlockSpec((tm, cols), lambda s: (n_tiles - 1 - s, 0)),
                  pl.BlockSpec((8, cols), lambda s: (jnp.maximum((n_tiles - 1 - s) * (tm // 8) - 1, 0), 0)),
                  full(mu), full(wd), full(w0), full(wa), full(a0), full(k_k), full(k_a), full(e), full(et),
                  tok, tok, tok, tok, tok, tok],
        out_specs=(pl.BlockSpec((tm, cols), lambda s: (n_tiles - 1 - s, 0)),) + tuple(
            pl.BlockSpec(o.shape, lambda s: (0, 0)) for o in outs[1:]),
        scratch_shapes=[pltpu.VMEM((1, cols), F32)],
        compiler_params=_params("arbitrary"),
    )(p, p, mu, wd, w0, wa, a0, k_k, k_a, e, et, dr, dw, dkx, dv, dkk, db)


def _rwkv_post(ys, r, kx, v, p, g_col, gn_g, gn_b, r_k, e, et):
    t, width = ys.shape
    tm = 256

    def body(ys_ref, r_ref, kx_ref, v_ref, g_ref, gg_ref, gb_ref, rk_ref, e_ref, et_ref, out_ref):
        out_ref[...] = _post_math(ys_ref[...], r_ref[...], kx_ref[...], v_ref[...], g_ref[...], gg_ref[...],
                                  gb_ref[...], rk_ref[...], e_ref[...], et_ref[...]).astype(BF16)

    tok = pl.BlockSpec((tm, width), lambda i: (i, 0))
    full = lambda a: pl.BlockSpec(a.shape, lambda i: (0,) * a.ndim)
    return pl.pallas_call(
        body, name="rwkv_post", grid=(t // tm,),
        out_shape=jax.ShapeDtypeStruct((t, width), BF16),
        in_specs=[tok, tok, tok, tok, pl.BlockSpec((tm, width), lambda i: (i, g_col)),
                  full(gn_g), full(gn_b), full(r_k), full(e), full(et)],
        out_specs=tok,
        compiler_params=_params("parallel"),
    )(ys, r, kx, v, p, gn_g, gn_b, r_k, e, et)


def _rwkv_post_bwd(ys, r, kx, v, p, g_col, gn_g, gn_b, r_k, e, et, dy, dy_col):
    t, width = ys.shape
    tm = 128

    def body(ys_ref, r_ref, kx_ref, v_ref, g_ref, gg_ref, gb_ref, rk_ref, e_ref, et_ref, dy_ref,
             dys_out, dr_out, dkx_out, dv_out, dg_out, dgg_out, dgb_out, drk_out):
        @pl.when(pl.program_id(0) == 0)
        def _():
            for ref in (dgg_out, dgb_out, drk_out):
                ref[...] = jnp.zeros_like(ref)

        ev, etv = e_ref[...], et_ref[...]
        _, vjp = jax.vjp(lambda *a: _post_math(*a, ev, etv), ys_ref[...], r_ref[...], kx_ref[...], v_ref[...],
                         g_ref[...], gg_ref[...], gb_ref[...], rk_ref[...])
        dys, dr, dkx, dv, dg, dgg, dgb, drk = vjp(dy_ref[...])
        dys_out[...] = dys
        dr_out[...] = dr
        dkx_out[...] = dkx
        dv_out[...] = dv
        dg_out[...] = dg.astype(BF16)
        dgg_out[...] += dgg
        dgb_out[...] += dgb
        drk_out[...] += drk

    tok = pl.BlockSpec((tm, width), lambda i: (i, 0))
    full = lambda a: pl.BlockSpec(a.shape, lambda i: (0,) * a.ndim)
    big = jax.ShapeDtypeStruct((t, width), F32)
    vec = jax.ShapeDtypeStruct((1, width), F32)
    vspec = pl.BlockSpec((1, width), lambda i: (0, 0))
    return pl.pallas_call(
        body, name="rwkv_post_bwd", grid=(t // tm,),
        out_shape=(big, big, big, big, jax.ShapeDtypeStruct((t, width), BF16), vec, vec, vec),
        in_specs=[tok, tok, tok, tok, pl.BlockSpec((tm, width), lambda i: (i, g_col)),
                  full(gn_g), full(gn_b), full(r_k), full(e), full(et),
                  pl.BlockSpec((tm, width), lambda i: (i, dy_col))],
        out_specs=(tok, tok, tok, tok, tok, vspec, vspec, vspec),
        compiler_params=_params("arbitrary"),
    )(ys, r, kx, v, p, gn_g, gn_b, r_k, e, et, dy)


def _to_scan_k(a, bl, seq):
    h = a.shape[1] // RWKV_HEAD
    a = a.reshape(bl, seq, h, RWKV_HEAD).transpose(1, 3, 0, 2).reshape(seq, RWKV_HEAD, bl * h)
    return jnp.concatenate([a, a], axis=-1)


def _to_scan_v(a, bl, seq):
    h = a.shape[1] // RWKV_HEAD
    half = RWKV_HEAD // 2
    return a.reshape(bl, seq, h, 2, half).transpose(1, 4, 3, 0, 2).reshape(seq, half, 2 * bl * h)


def _from_scan_k(a, bl, seq):
    h = a.shape[2] // (2 * bl)
    a = a[:, :, :bl * h].reshape(seq, RWKV_HEAD, bl, h).transpose(2, 0, 3, 1)
    return a.reshape(bl * seq, h * RWKV_HEAD)


def _from_scan_v(a, bl, seq):
    half = RWKV_HEAD // 2
    h = a.shape[2] // (2 * bl)
    a = a.reshape(seq, half, 2, bl, h).transpose(3, 0, 4, 2, 1)
    return a.reshape(bl * seq, h * RWKV_HEAD)


def _scan_fwd(kk, w, b, kx, r, v):
    seq, nk, lanes = kk.shape
    nv = v.shape[1]
    tt = SCAN_STEPS_PER_BLOCK

    def body(kk_ref, w_ref, b_ref, kx_ref, r_ref, v_ref, y_ref, st_ref, s_ref):
        @pl.when(pl.program_id(0) == 0)
        def _():
            s_ref[...] = jnp.zeros_like(s_ref)

        def step(i, carry):
            kkv, wv, bv, kxv, rv = kk_ref[i], w_ref[i], b_ref[i], kx_ref[i], r_ref[i]
            for j in range(nv):
                s_old = s_ref[j]
                st_ref[i, j] = s_old
                sa = -jnp.sum(s_old * kkv, axis=0, keepdims=True)
                s_new = s_old * wv + sa * bv + v_ref[i, j:j + 1, :] * kxv
                s_ref[j] = s_new
                y_ref[i, j:j + 1, :] = jnp.sum(s_new * rv, axis=0, keepdims=True)
            return carry

        lax.fori_loop(0, tt, step, 0)

    krow = pl.BlockSpec((tt, nk, lanes), lambda i: (i, 0, 0))
    vrow = pl.BlockSpec((tt, nv, lanes), lambda i: (i, 0, 0))
    return pl.pallas_call(
        body, name="rwkv_scan_fwd", grid=(seq // tt,),
        out_shape=(jax.ShapeDtypeStruct((seq, nv, lanes), F32), jax.ShapeDtypeStruct((seq, nv, nk, lanes), F32)),
        in_specs=[krow, krow, krow, krow, krow, vrow],
        out_specs=(vrow, pl.BlockSpec((tt, nv, nk, lanes), lambda i: (i, 0, 0, 0))),
        scratch_shapes=[pltpu.VMEM((nv, nk, lanes), F32)],
        compiler_params=_params("arbitrary"),
    )(kk, w, b, kx, r, v)


def _scan_bwd(kk, w, b, kx, r, v, states, dy):
    seq, nk, lanes = kk.shape
    nv = v.shape[1]
    tt = SCAN_STEPS_PER_BLOCK
    nblk = seq // tt

    def both_halves(a):
        return a + pltpu.roll(a, lanes // 2, 1)

    def body(kk_ref, w_ref, b_ref, kx_ref, r_ref, v_ref, st_ref, dy_ref,
             dkk_ref, dw_ref, db_ref, dkx_ref, dr_ref, dv_ref, g_ref):
        @pl.when(pl.program_id(0) == 0)
        def _():
            g_ref[...] = jnp.zeros_like(g_ref)

        def step(n, carry):
            i = tt - 1 - n
            kkv, wv, bv, kxv, rv = kk_ref[i], w_ref[i], b_ref[i], kx_ref[i], r_ref[i]
            zero = jnp.zeros((nk, lanes), F32)
            a_r, a_w, a_b, a_kx, a_kk = zero, zero, zero, zero, zero
            for j in range(nv):
                s_old = st_ref[i, j]
                vv = v_ref[i, j:j + 1, :]
                dyv = dy_ref[i, j:j + 1, :]
                sa = -jnp.sum(s_old * kkv, axis=0, keepdims=True)
                s_new = s_old * wv + sa * bv + vv * kxv
                g = g_ref[j] + dyv * rv
                a_r = a_r + s_new * dyv
                a_w = a_w + g * s_old
                dsa = jnp.sum(g * bv, axis=0, keepdims=True)
                a_b = a_b + g * sa
                dv_ref[i, j:j + 1, :] = jnp.sum(g * kxv, axis=0, keepdims=True)
                a_kx = a_kx + g * vv
                a_kk = a_kk + s_old * dsa
                g_ref[j] = g * wv - dsa * kkv
            dr_ref[i] = both_halves(a_r)
            dw_ref[i] = both_halves(a_w)
            db_ref[i] = both_halves(a_b)
            dkx_ref[i] = both_halves(a_kx)
            dkk_ref[i] = -both_halves(a_kk)
            return carry

        lax.fori_loop(0, tt, step, 0)

    rev = lambda i: nblk - 1 - i
    krow = pl.BlockSpec((tt, nk, lanes), lambda i: (rev(i), 0, 0))
    vrow = pl.BlockSpec((tt, nv, lanes), lambda i: (rev(i), 0, 0))
    kshape = jax.ShapeDtypeStruct((seq, nk, lanes), F32)
    return pl.pallas_call(
        body, name="rwkv_scan_bwd", grid=(nblk,),
        out_shape=(kshape, kshape, kshape, kshape, kshape, jax.ShapeDtypeStruct((seq, nv, lanes), F32)),
        in_specs=[krow, krow, krow, krow, krow, vrow,
                  pl.BlockSpec((tt, nv, nk, lanes), lambda i: (rev(i), 0, 0, 0)), vrow],
        out_specs=(krow, krow, krow, krow, krow, vrow),
        scratch_shapes=[pltpu.VMEM((nv, nk, lanes), F32)],
        compiler_params=_params("arbitrary"),
    )(kk, w, b, kx, r, v, states, dy)


def _sb_strip(q, ks, vs, g, qi):
    scale = 1.0 / (SB_HEAD ** 0.5)
    blk = SB_BLOCK
    t_pos = qi * blk + lax.broadcasted_iota(jnp.int32, (blk, blk), 0)
    s_in = lax.broadcasted_iota(jnp.int32, (blk, blk), 1)
    j_in = lax.broadcasted_iota(jnp.int32, (blk, blk), 0)
    after = (j_in > s_in).astype(F32)
    qb = q.astype(BF16)
    out = jnp.zeros((blk, SB_HEAD), F32)
    later = jnp.zeros((blk, 1), F32)
    for j in reversed(range(len(ks))):
        causal = (j * blk + s_in) < t_pos
        z = lax.dot_general(qb, ks[j].astype(BF16), (((1,), (1,)), ((), ())), preferred_element_type=F32) * scale
        lsz = _log_sigmoid(z)
        keep = jnp.where(causal, lsz - z, 0.0)
        att = jnp.where(causal, jnp.exp(lsz + _dot32(keep, after) + later), 0.0)
        out = out + jnp.dot(att.astype(BF16), vs[j].astype(BF16), preferred_element_type=F32)
        later = later + jnp.sum(keep, axis=1, keepdims=True)
    return out * _silu(g)


def _sb_fwd(p, bl, seq, cols, n_heads):
    t = p.shape[0]
    nq = seq // SB_BLOCK
    q_col, k_col, v_col, g_col = cols

    def body(q_ref, k_ref, v_ref, g_ref, out_ref):
        ks = [k_ref[j * SB_BLOCK:(j + 1) * SB_BLOCK, :] for j in range(nq)]
        vs = [v_ref[j * SB_BLOCK:(j + 1) * SB_BLOCK, :] for j in range(nq)]
        out_ref[...] = _sb_strip(q_ref[...], ks, vs, g_ref[...], pl.program_id(2)).astype(BF16)

    qspec = lambda col: pl.BlockSpec((SB_BLOCK, SB_HEAD), lambda b, h, i: (b * nq + i, col + h))
    kspec = lambda col: pl.BlockSpec((seq, SB_HEAD), lambda b, h, i: (b, col + h))
    return pl.pallas_call(
        body, name="sb_attn_fwd", grid=(bl, n_heads, nq),
        out_shape=jax.ShapeDtypeStruct((t, n_heads * SB_HEAD), BF16),
        in_specs=[qspec(q_col), kspec(k_col), kspec(v_col), qspec(g_col)],
        out_specs=pl.BlockSpec((SB_BLOCK, SB_HEAD), lambda b, h, i: (b * nq + i, h)),
        compiler_params=_params("parallel", "parallel", "arbitrary"),
    )(p, p, p, p)


def _sb_bwd(p, bl, seq, cols, n_heads, dy, dy_col):
    t = p.shape[0]
    nq = seq // SB_BLOCK
    q_col, k_col, v_col, g_col = cols

    def body(q_ref, k_ref, v_ref, g_ref, dy_ref, dq_out, dk_out, dv_out, dg_out, dk_acc, dv_acc):
        i = pl.program_id(2)

        @pl.when(i == 0)
        def _():
            dk_acc[...] = jnp.zeros_like(dk_acc)
            dv_acc[...] = jnp.zeros_like(dv_acc)

        ks = [k_ref[j * SB_BLOCK:(j + 1) * SB_BLOCK, :] for j in range(nq)]
        vs = [v_ref[j * SB_BLOCK:(j + 1) * SB_BLOCK, :] for j in range(nq)]
        _, vjp = jax.vjp(lambda q, kl, vl, g: _sb_strip(q, kl, vl, g, i), q_ref[...], ks, vs, g_ref[...])
        dq, dks, dvs, dg = vjp(dy_ref[...].astype(F32))
        dq_out[...] = dq.astype(BF16)
        dg_out[...] = dg.astype(BF16)
        for j in range(nq):
            rows = slice(j * SB_BLOCK, (j + 1) * SB_BLOCK)
            dk_acc[rows, :] += dks[j]
            dv_acc[rows, :] += dvs[j]

        @pl.when(i == nq - 1)
        def _():
            dk_out[...] = dk_acc[...].astype(BF16)
            dv_out[...] = dv_acc[...].astype(BF16)

    qspec = lambda col: pl.BlockSpec((SB_BLOCK, SB_HEAD), lambda b, h, i: (b * nq + i, col + h))
    kspec = lambda col: pl.BlockSpec((seq, SB_HEAD), lambda b, h, i: (b, col + h))
    width = n_heads * SB_HEAD
    shape = jax.ShapeDtypeStruct((t, width), BF16)
    return pl.pallas_call(
        body, name="sb_attn_bwd", grid=(bl, n_heads, nq),
        out_shape=(shape, shape, shape, shape),
        in_specs=[qspec(q_col), kspec(k_col), kspec(v_col), qspec(g_col), qspec(dy_col)],
        out_specs=(qspec(0), kspec(0), kspec(0), qspec(0)),
        scratch_shapes=[pltpu.VMEM((seq, SB_HEAD), F32), pltpu.VMEM((seq, SB_HEAD), F32)],
        compiler_params=_params("parallel", "parallel", "arbitrary"),
    )(p, p, p, p, dy)


def _gelu(x):
    return 0.5 * x * (1.0 + lax.erf(x * (2.0 ** -0.5)))


def _sgu_math(us, vs, gs, ln_g, ln_b, ws, bs):
    width = sum(v.shape[1] for v in vs)
    vg = [_gelu(v) for v in vs]
    mu = sum(jnp.sum(v, axis=1, keepdims=True) for v in vg) * (1.0 / width)
    dl = [v - mu for v in vg]
    var = sum(jnp.sum(d * d, axis=1, keepdims=True) for d in dl) * (1.0 / width)
    rstd = lax.rsqrt(var + LN_EPS)
    n = ws[0].shape[0]
    tri = lax.broadcasted_iota(jnp.int32, (n, n), 0) >= lax.broadcasted_iota(jnp.int32, (n, n), 1)
    outs = []
    for i in range(len(vs)):
        vn = dl[i] * rstd * ln_g[i] + ln_b[i]
        mixed = _dot32(jnp.where(tri, ws[i], 0.0), vn) + bs[i]
        outs.append(_gelu(us[i]) * mixed * _silu(gs[i]))
    return outs


def _sgu_load(p_ref, lng_ref, lnb_ref, ws_ref, bs_ref, width):
    gd = width // SGU_GROUPS
    grp = lambda ref, base, i: ref[:, base + i * gd:base + (i + 1) * gd]
    idx = range(SGU_GROUPS)
    return ([grp(p_ref, 0, i) for i in idx], [grp(p_ref, width, i) for i in idx],
            [grp(p_ref, 2 * width, i) for i in idx], [grp(lng_ref, 0, i) for i in idx],
            [grp(lnb_ref, 0, i) for i in idx], [ws_ref[i] for i in idx], [bs_ref[i] for i in idx])


def _sgu_fwd(p, ln_g, ln_b, w_s, b_s):
    t = p.shape[0]
    width = p.shape[1] // 3
    gd = width // SGU_GROUPS
    tm = SGU_CHUNK

    def body(p_ref, lng_ref, lnb_ref, ws_ref, bs_ref, y_ref):
        outs = _sgu_math(*_sgu_load(p_ref, lng_ref, lnb_ref, ws_ref, bs_ref, width))
        for i in range(SGU_GROUPS):
            y_ref[:, i * gd:(i + 1) * gd] = outs[i].astype(BF16)

    full = lambda a: pl.BlockSpec(a.shape, lambda i: (0,) * a.ndim)
    return pl.pallas_call(
        body, name="sgu_fwd", grid=(t // tm,),
        out_shape=jax.ShapeDtypeStruct((t, width), BF16),
        in_specs=[pl.BlockSpec((tm, 3 * width), lambda i: (i, 0)), full(ln_g), full(ln_b), full(w_s), full(b_s)],
        out_specs=pl.BlockSpec((tm, width), lambda i: (i, 0)),
        compiler_params=_params("parallel"),
    )(p, ln_g, ln_b, w_s, b_s)


def _sgu_bwd(p, ln_g, ln_b, w_s, b_s, dy):
    t = p.shape[0]
    width = p.shape[1] // 3
    gd = width // SGU_GROUPS
    tm = SGU_CHUNK

    def body(p_ref, lng_ref, lnb_ref, ws_ref, bs_ref, dy_ref, dp_out, dlng_out, dlnb_out, dws_out, dbs_out):
        @pl.when(pl.program_id(0) == 0)
        def _():
            for ref in (dlng_out, dlnb_out, dws_out, dbs_out):
                ref[...] = jnp.zeros_like(ref)

        _, vjp = jax.vjp(_sgu_math, *_sgu_load(p_ref, lng_ref, lnb_ref, ws_ref, bs_ref, width))
        dus, dvs, dgs, dlng, dlnb, dws, dbs = vjp(
            [dy_ref[:, i * gd:(i + 1) * gd] for i in range(SGU_GROUPS)])
        for i in range(SGU_GROUPS):
            cols = slice(i * gd, (i + 1) * gd)
            dp_out[:, i * gd:(i + 1) * gd] = dus[i].astype(BF16)
            dp_out[:, width + i * gd:width + (i + 1) * gd] = dvs[i].astype(BF16)
            dp_out[:, 2 * width + i * gd:2 * width + (i + 1) * gd] = dgs[i].astype(BF16)
            dlng_out[:, cols] += dlng[i]
            dlnb_out[:, cols] += dlnb[i]
            dws_out[i] += dws[i]
            dbs_out[i] += dbs[i]

    full = lambda a: pl.BlockSpec(a.shape, lambda i: (0,) * a.ndim)
    like = lambda a: jax.ShapeDtypeStruct(a.shape, F32)
    return pl.pallas_call(
        body, name="sgu_bwd", grid=(t // tm,),
        out_shape=(jax.ShapeDtypeStruct((t, 3 * width), BF16), like(ln_g), like(ln_b), like(w_s), like(b_s)),
        in_specs=[pl.BlockSpec((tm, 3 * width), lambda i: (i, 0)), full(ln_g), full(ln_b), full(w_s), full(b_s),
                  pl.BlockSpec((tm, width), lambda i: (i, 0))],
        out_specs=(pl.BlockSpec((tm, 3 * width), lambda i: (i, 0)), full(ln_g), full(ln_b), full(w_s), full(b_s)),
        compiler_params=_params("arbitrary"),
    )(p, ln_g, ln_b, w_s, b_s, dy)


def _sum_slabs(parts, name):
    _, rows, cols = parts.shape
    tr = _tile(rows, max(8, (1 << 18) // cols), 8)

    def body(p_ref, o_ref):
        acc = p_ref[0].astype(F32)
        for d in range(1, N_DEV):
            acc = acc + p_ref[d].astype(F32)
        o_ref[...] = acc

    return pl.pallas_call(
        body, name=name, grid=(rows // tr,),
        out_shape=jax.ShapeDtypeStruct((rows, cols), F32),
        in_specs=[pl.BlockSpec((N_DEV, tr, cols), lambda i: (0, i, 0))],
        out_specs=pl.BlockSpec((tr, cols), lambda i: (i, 0)),
        compiler_params=_params("parallel"),
    )(parts)


def _adamw(w, g, m, v, name):
    rows, cols = w.shape
    tr = _tile(rows, max(8, (1 << 18) // cols), 8)

    def body(w_ref, g_ref, m_ref, v_ref, d_out, m_out, v_out):
        gv = g_ref[...]
        mn = ADAM_B1 * m_ref[...] + (1.0 - ADAM_B1) * gv
        vn = ADAM_B2 * v_ref[...] + (1.0 - ADAM_B2) * (gv * gv)
        m_hat = mn / (1.0 - ADAM_B1 ** ADAM_STEP)
        v_hat = vn / (1.0 - ADAM_B2 ** ADAM_STEP)
        d_out[...] = -ADAM_LR * (m_hat / (jnp.sqrt(v_hat) + ADAM_EPS) + ADAM_WD * w_ref[...])
        m_out[...] = mn
        v_out[...] = vn

    blk = pl.BlockSpec((tr, cols), lambda i: (i, 0))
    shape = jax.ShapeDtypeStruct((rows, cols), F32)
    return pl.pallas_call(
        body, name=name, grid=(rows // tr,),
        out_shape=(shape, shape, shape),
        in_specs=[blk, blk, blk, blk], out_specs=(blk, blk, blk),
        compiler_params=_params("parallel"),
    )(w, g, m, v)


PACK_COLS = 1024


def _pack(arrays):
    flat = jnp.concatenate([a.reshape(-1).astype(F32) for a in arrays])
    rows = -(-flat.shape[0] // (8 * PACK_COLS)) * 8
    return jnp.pad(flat, (0, rows * PACK_COLS - flat.shape[0])).reshape(rows, PACK_COLS)


def _unpack(packed, shapes):
    flat = packed.reshape(-1)
    out, at = [], 0
    for s in shapes:
        n = 1
        for d in s:
            n *= d
        out.append(flat[at:at + n].reshape(s))
        at += n
    return out


def kernel(x, norm_g, final_norm_g, e_w_in, e_shift_mu, e_w_decay_up, e_w0, e_a_up, e_a0, e_k_k, e_k_a, e_r_k, e_gn_g, e_gn_b, e_w_out, o_w_in, o_ln_g, o_ln_b, o_w_s, o_b_s, o_w_out, loss_target, m_norm_g, m_final_norm_g, m_e_w_in, m_e_shift_mu, m_e_w_decay_up, m_e_w0, m_e_a_up, m_e_a0, m_e_k_k, m_e_k_a, m_e_r_k, m_e_gn_g, m_e_gn_b, m_e_w_out, m_o_w_in, m_o_ln_g, m_o_ln_b, m_o_w_s, m_o_b_s, m_o_w_out, v_norm_g, v_final_norm_g, v_e_w_in, v_e_shift_mu, v_e_w_decay_up, v_e_w0, v_e_a_up, v_e_a0, v_e_k_k, v_e_k_a, v_e_r_k, v_e_gn_g, v_e_gn_b, v_e_w_out, v_o_w_in, v_o_ln_g, v_o_ln_b, v_o_w_s, v_o_b_s, v_o_w_out):
    weights = dict(norm_g=norm_g, final_norm_g=final_norm_g, e_w_in=e_w_in, e_shift_mu=e_shift_mu,
                   e_w_decay_up=e_w_decay_up, e_w0=e_w0, e_a_up=e_a_up, e_a0=e_a0, e_k_k=e_k_k, e_k_a=e_k_a,
                   e_r_k=e_r_k, e_gn_g=e_gn_g, e_gn_b=e_gn_b, e_w_out=e_w_out, o_w_in=o_w_in, o_ln_g=o_ln_g,
                   o_ln_b=o_ln_b, o_w_s=o_w_s, o_b_s=o_b_s, o_w_out=o_w_out)
    mom1 = dict(norm_g=m_norm_g, final_norm_g=m_final_norm_g, e_w_in=m_e_w_in, e_shift_mu=m_e_shift_mu,
                e_w_decay_up=m_e_w_decay_up, e_w0=m_e_w0, e_a_up=m_e_a_up, e_a0=m_e_a0, e_k_k=m_e_k_k,
                e_k_a=m_e_k_a, e_r_k=m_e_r_k, e_gn_g=m_e_gn_g, e_gn_b=m_e_gn_b, e_w_out=m_e_w_out,
                o_w_in=m_o_w_in, o_ln_g=m_o_ln_g, o_ln_b=m_o_ln_b, o_w_s=m_o_w_s, o_b_s=m_o_b_s,
                o_w_out=m_o_w_out)
    mom2 = dict(norm_g=v_norm_g, final_norm_g=v_final_norm_g, e_w_in=v_e_w_in, e_shift_mu=v_e_shift_mu,
                e_w_decay_up=v_e_w_decay_up, e_w0=v_e_w0, e_a_up=v_e_a_up, e_a0=v_e_a0, e_k_k=v_e_k_k,
                e_k_a=v_e_k_a, e_r_k=v_e_r_k, e_gn_g=v_e_gn_g, e_gn_b=v_e_gn_b, e_w_out=v_e_w_out,
                o_w_in=v_o_w_in, o_ln_g=v_o_ln_g, o_ln_b=v_o_ln_b, o_w_s=v_o_w_s, o_b_s=v_o_b_s,
                o_w_out=v_o_w_out)
    names = list(weights)
    big = ("e_w_in", "e_w_out", "o_w_in", "o_w_out")

    bl, seq, d = x.shape
    t = bl * seq
    width = e_w0.shape[1]
    lora = e_w_decay_up.shape[1]
    n_sb = width // SB_HEAD
    me = 4 * lax.axis_index("x") + 2 * lax.axis_index("y") + lax.axis_index("c")

    e_win_t = _exchange(e_w_in[0].T.astype(BF16), False, "gather_e_w_in").reshape(-1, d)
    e_wout = _exchange(e_w_out[0].astype(BF16), False, "gather_e_w_out").reshape(-1, d)
    o_win_t = _exchange(o_w_in[0].T.astype(BF16), False, "gather_o_w_in").reshape(-1, d)
    o_wout = _exchange(o_w_out[0].astype(BF16), False, "gather_o_w_out").reshape(-1, d)
    sharded_small = ("e_w_decay_up", "e_a_up", "o_ln_g", "o_ln_b")
    small_shapes = [weights[n][0].shape for n in sharded_small]
    got = _exchange(_pack([weights[n][0] for n in sharded_small]), False, "gather_small")
    per_dev = [_unpack(got[dev], small_shapes) for dev in range(N_DEV)]
    wd, wa, ln_g, ln_b = [jnp.concatenate([per_dev[dev][i] for dev in range(N_DEV)], axis=-1).reshape(
        small_shapes[i][:-1] + (-1,)) for i in range(4)]
    ln_g, ln_b = ln_g.reshape(1, -1), ln_b.reshape(1, -1)
    e_ind, e_ind_t = _head_indicator(width)
    b_s3 = o_b_s[0][:, :, None]

    x2d = x.reshape(t, d)
    target = loss_target.reshape(t, d)
    cols_rwkv = 3 * width + 2 * lora
    assert cols_rwkv % LANES == 0 and width % LANES == 0
    g_rwkv_col = cols_rwkv // width if cols_rwkv % width == 0 else None
    sb0 = (cols_rwkv + width) // SB_HEAD
    sb_cols = (sb0, sb0 + n_sb, sb0 + 2 * n_sb, sb0 + 3 * n_sb)

    h0 = _rms_fwd(x2d, norm_g[0:1], "rms0_fwd")
    p = _matmul(h0, e_win_t, "nt", F32, "e_in_fwd", tn=1280)
    p_rwkv = p[:, :cols_rwkv]
    g_rwkv = p[:, cols_rwkv:cols_rwkv + width]
    r, w, kx, v, kk, b = _rwkv_prep(p_rwkv, seq, (width, lora), e_shift_mu, wd, e_w0, wa, e_a0, e_k_k, e_k_a,
                                    e_ind, e_ind_t)
    sk = [_to_scan_k(a, bl, seq) for a in (kk, w, b, kx, r)]
    sv = _to_scan_v(v, bl, seq)
    ys_scan, states = _scan_fwd(*sk, sv)
    ys = _from_scan_v(ys_scan, bl, seq)
    ya = _rwkv_post(ys, r, kx, v, g_rwkv, 0, e_gn_g, e_gn_b, e_r_k, e_ind, e_ind_t)
    yb = _sb_fwd(p, bl, seq, sb_cols, n_sb)
    y = jnp.concatenate([ya, yb], axis=1)
    x1 = _matmul(y, e_wout, "nn", F32, "e_out_fwd", res=x2d)
    h1 = _rms_fwd(x1, norm_g[1:2], "rms1_fwd")
    p2 = _matmul(h1, o_win_t, "nt", F32, "o_in_fwd")
    y2 = _sgu_fwd(p2, ln_g, ln_b, o_w_s[0], b_s3)
    x2 = _matmul(y2, o_wout, "nn", F32, "o_out_fwd", res=x1)
    dx2, d_final_g, loss_part = _final_loss(x2, final_norm_g.reshape(1, d), target, "final_loss")

    dy2 = _matmul(dx2, o_wout, "nt", F32, "o_out_bwd_x")
    d_o_wout = _matmul(y2, dx2, "tn", F32, "o_out_bwd_w")
    dp2, d_ln_g, d_ln_b, d_w_s, d_b_s3 = _sgu_bwd(p2, ln_g, ln_b, o_w_s[0], b_s3, dy2)
    dh1 = _matmul(dp2, o_win_t, "nn", F32, "o_in_bwd_x", tk=1536)
    d_o_win_t = _matmul(dp2, h1, "tn", F32, "o_in_bwd_w")
    dx1, d_g1 = _rms_bwd(x1, norm_g[1:2], dh1, dx2, "rms1_bwd")
    dy = _matmul(dx1, e_wout, "nt", F32, "e_out_bwd_x")
    d_e_wout = _matmul(y, dx1, "tn", F32, "e_out_bwd_w")
    dq, dk, dv_sb, dg_sb = _sb_bwd(p, bl, seq, sb_cols, n_sb, dy, n_sb)
    dys, dr1, dkx1, dv1, dg_rwkv, d_gn_g, d_gn_b, d_r_k = _rwkv_post_bwd(
        ys, r, kx, v, g_rwkv, 0, e_gn_g, e_gn_b, e_r_k, e_ind, e_ind_t, dy, 0)
    dkk_s, dw_s, db_s, dkx_s, dr_s, dv_s = _scan_bwd(*sk, sv, states, _to_scan_v(dys, bl, seq))
    dkk, dw, db, dkx2, dr2 = [_from_scan_k(a, bl, seq) for a in (dkk_s, dw_s, db_s, dkx_s, dr_s)]
    dv2 = _from_scan_v(dv_s, bl, seq)
    dp_rwkv, d_mu, d_wd, d_w0, d_wa, d_a0, d_k_k, d_k_a = _rwkv_prep_bwd(
        p_rwkv, seq, (width, lora), e_shift_mu, wd, e_w0, wa, e_a0, e_k_k, e_k_a, e_ind, e_ind_t,
        dr1 + dr2, dw, dkx1 + dkx2, dv1 + dv2, dkk, db)
    dp = jnp.concatenate([dp_rwkv, dg_rwkv, dq, dk, dv_sb, dg_sb], axis=1)
    dh0 = _matmul(dp, e_win_t, "nn", F32, "e_in_bwd_x", tk=1280)
    d_e_win_t = _matmul(dp, h0, "tn", F32, "e_in_bwd_w", tm=1280)
    grad_x, d_g0 = _rms_bwd(x2d, norm_g[0:1], dh0, dx1, "rms0_bwd")

    def scattered(full, name):
        parts = _exchange(full.reshape((N_DEV, full.shape[0] // N_DEV, full.shape[1])), True, "scatter_" + name)
        return _sum_slabs(parts, "sum_" + name)

    grads = {
        "e_w_in": scattered(d_e_win_t, "e_w_in").T[None],
        "e_w_out": scattered(d_e_wout, "e_w_out")[None],
        "o_w_in": scattered(d_o_win_t, "o_w_in").T[None],
        "o_w_out": scattered(d_o_wout, "o_w_out")[None],
    }
    small_full = {
        "norm_g": jnp.concatenate([d_g0, d_g1], axis=0), "final_norm_g": d_final_g.reshape(-1),
        "e_shift_mu": d_mu, "e_w_decay_up": d_wd[None], "e_w0": d_w0, "e_a_up": d_wa[None], "e_a0": d_a0,
        "e_k_k": d_k_k, "e_k_a": d_k_a, "e_r_k": d_r_k, "e_gn_g": d_gn_g, "e_gn_b": d_gn_b,
        "o_ln_g": d_ln_g, "o_ln_b": d_ln_b, "o_w_s": d_w_s[None], "o_b_s": d_b_s3[:, :, 0][None],
    }
    small = [n for n in names if n not in big]
    parts = _exchange(_pack([small_full[n] for n in small]), False, "gather_small_grads")
    totals = _unpack(_sum_slabs(parts, "sum_small_grads"), [small_full[n].shape for n in small])
    for n, g in zip(small, totals):
        if n in sharded_small:
            size = weights[n].shape[-1]
            g = lax.dynamic_slice_in_dim(g, me * size, size, axis=g.ndim - 1)
        grads[n] = g.reshape(weights[n].shape)

    delta, new_m, new_v = {}, {}, {}
    for n in big:
        shp = weights[n].shape
        flat = lambda a: a.reshape(shp[-2], shp[-1])
        dl, mn, vn = _adamw(flat(weights[n]), flat(grads[n]), flat(mom1[n]), flat(mom2[n]), "adamw_" + n)
        delta[n], new_m[n], new_v[n] = dl.reshape(shp), mn.reshape(shp), vn.reshape(shp)
    packed = [_pack([src[n] for n in small]) for src in (weights, grads, mom1, mom2)]
    outs = _adamw(*packed, "adamw_small")
    shapes = [weights[n].shape for n in small]
    for dst, arr in zip((delta, new_m, new_v), outs):
        for n, a in zip(small, _unpack(arr, shapes)):
            dst[n] = a

    loss = lax.psum(loss_part[0, 0], ("x", "y", "c"))
    return (loss, grad_x.reshape(bl, seq, d), *[grads[n] for n in names], *[delta[n] for n in names],
            *[new_m[n] for n in names], *[new_v[n] for n in names])
```

```python
import functools

import jax
import jax.numpy as jnp
from jax import lax
from jax.experimental import pallas as pl
from jax.experimental.pallas import tpu as pltpu

F32 = jnp.float32
BF16 = jnp.bfloat16
HIGHEST = lax.Precision.HIGHEST

N_DEV = 8
RWKV_HEAD = 64
SB_HEAD = 128
SB_BLOCK = 128
SGU_CHUNK = 128
SGU_GROUPS = 16
RMS_EPS = 1e-6
GN_EPS = 64e-5
LN_EPS = 1e-5
L2_EPS = 1e-12
ADAM_LR = 0.001
ADAM_B1 = 0.9
ADAM_B2 = 0.999
ADAM_EPS = 1e-08
ADAM_WD = 0.01
ADAM_STEP = 10

VMEM_LIMIT_V7X = 56 * 1024 * 1024
LANES = 128
SCAN_STEPS_PER_BLOCK = 8


def _params(*sem):
    return pltpu.CompilerParams(dimension_semantics=sem, vmem_limit_bytes=VMEM_LIMIT_V7X)


def _tile(n, target, mult):
    best = None
    d = mult
    while d <= min(n, target):
        if n % d == 0:
            best = d
        d += mult
    return n if best is None else best


def _remote(src, dst, send_sems, recv_sems, k, dev):
    return pltpu.make_async_remote_copy(src_ref=src, dst_ref=dst, send_sem=send_sems.at[k], recv_sem=recv_sems.at[k],
                                        device_id=dev, device_id_type=pl.DeviceIdType.MESH)


_HBM = pl.BlockSpec(memory_space=pl.ANY)


def _all_gather(src, name):
    def body(src_ref, out_ref, send_sems, recv_sems, local_sem):
        x, y, c = lax.axis_index("x"), lax.axis_index("y"), lax.axis_index("c")
        me, sibling = (x, y, c), (x, y, 1 - c)
        chips = [(1 - x, y), (x, 1 - y), (1 - x, 1 - y)]

        def slot(px, py, pc):
            return out_ref.at[4 * px + 2 * py + pc]

        def copy(k, block, to, own=False):
            return _remote(src_ref if own else slot(*block), slot(*block), send_sems, recv_sems, k, to)

        mine = pltpu.make_async_copy(src_ref, slot(*me), local_sem)
        mine.start()
        first = [copy(0, me, sibling, True)] + [copy(1 + j, me, (*chip, c), True) for j, chip in enumerate(chips)]
        for cp in first:
            cp.start()
        passed = [copy(4 + j, (*chip, c), sibling) for j, chip in enumerate(chips)]
        for j, chip in enumerate(chips):
            copy(1 + j, (*chip, c), me).wait_recv()
            passed[j].start()
        copy(0, sibling, me).wait_recv()
        for j, chip in enumerate(chips):
            copy(4 + j, (*chip, 1 - c), me).wait_recv()
        for cp in first + passed:
            cp.wait_send()
        mine.wait()

    return pl.pallas_call(
        body, name=name,
        out_shape=jax.ShapeDtypeStruct((N_DEV,) + tuple(src.shape), src.dtype),
        in_specs=[_HBM], out_specs=_HBM,
        scratch_shapes=[pltpu.SemaphoreType.DMA((7,)), pltpu.SemaphoreType.DMA((7,)), pltpu.SemaphoreType.DMA(())],
    )(src)


def _pair_swap(full, name):
    n_chips = full.shape[0]

    def body(src_ref, out_ref, send_sems, recv_sems):
        x, y, c = lax.axis_index("x"), lax.axis_index("y"), lax.axis_index("c")
        copies = [_remote(src_ref.at[q, 1 - c], out_ref.at[q], send_sems, recv_sems, q, (x, y, 1 - c))
                  for q in range(n_chips)]
        for cp in copies:
            cp.start()
        for cp in copies:
            cp.wait_recv()
        for cp in copies:
            cp.wait_send()

    return pl.pallas_call(
        body, name=name,
        out_shape=jax.ShapeDtypeStruct((n_chips,) + tuple(full.shape[2:]), full.dtype),
        in_specs=[_HBM], out_specs=_HBM,
        scratch_shapes=[pltpu.SemaphoreType.DMA((n_chips,)), pltpu.SemaphoreType.DMA((n_chips,))],
    )(full)


def _pair_sum(full, got, core, name):
    n_chips, _, rows, cols = full.shape
    tr = _tile(rows, max(16, (1 << 19) // cols), 16)

    def body(core_ref, a_ref, b_ref, o_ref):
        o_ref[...] = (a_ref[...] + b_ref[...]).astype(BF16)

    return pl.pallas_call(
        body, name=name,
        grid_spec=pltpu.PrefetchScalarGridSpec(
            num_scalar_prefetch=1, grid=(n_chips, rows // tr),
            in_specs=[pl.BlockSpec((None, None, tr, cols), lambda q, i, s: (q, s[0], i, 0)),
                      pl.BlockSpec((None, tr, cols), lambda q, i, s: (q, i, 0))],
            out_specs=pl.BlockSpec((None, tr, cols), lambda q, i, s: (q, i, 0))),
        out_shape=jax.ShapeDtypeStruct((n_chips, rows, cols), BF16),
        compiler_params=_params("parallel", "parallel"),
    )(core, full, got)


def _chip_scatter(parts, name):
    def body(src_ref, out_ref, send_sems, recv_sems, local_sem):
        x, y, c = lax.axis_index("x"), lax.axis_index("y"), lax.axis_index("c")
        here = 2 * x + y
        local = pltpu.make_async_copy(src_ref.at[here], out_ref.at[here], local_sem)
        local.start()
        chips = [(1 - x, y), (x, 1 - y), (1 - x, 1 - y)]
        sends = [_remote(src_ref.at[2 * px + py], out_ref.at[here], send_sems, recv_sems, j, (px, py, c))
                 for j, (px, py) in enumerate(chips)]
        recvs = [_remote(src_ref.at[2 * px + py], out_ref.at[2 * px + py], send_sems, recv_sems, j, (px, py, c))
                 for j, (px, py) in enumerate(chips)]
        for cp in sends:
            cp.start()
        for cp in recvs:
            cp.wait_recv()
        for cp in sends:
            cp.wait_send()
        local.wait()

    return pl.pallas_call(
        body, name=name,
        out_shape=jax.ShapeDtypeStruct(parts.shape, parts.dtype),
        in_specs=[_HBM], out_specs=_HBM,
        scratch_shapes=[pltpu.SemaphoreType.DMA((3,)), pltpu.SemaphoreType.DMA((3,)), pltpu.SemaphoreType.DMA(())],
    )(parts)


def _matmul(a, b, mode, out_dtype, name, res=None, tm=1024, tn=1024, tk=1024):
    if mode == "nn":
        (m, k), (k2, n) = a.shape, b.shape
    elif mode == "nt":
        (m, k), (n, k2) = a.shape, b.shape
    else:
        (k, m), (k2, n) = a.shape, b.shape
    assert k == k2, (a.shape, b.shape, mode)
    tm, tn, tk = _tile(m, tm, 128), _tile(n, tn, 128), _tile(k, tk, 128)
    nk = k // tk
    if mode == "nn":
        a_spec = pl.BlockSpec((tm, tk), lambda i, j, kk: (i, kk))
        b_spec = pl.BlockSpec((tk, tn), lambda i, j, kk: (kk, j))
        dims = (((1,), (0,)), ((), ()))
    elif mode == "nt":
        a_spec = pl.BlockSpec((tm, tk), lambda i, j, kk: (i, kk))
        b_spec = pl.BlockSpec((tn, tk), lambda i, j, kk: (j, kk))
        dims = (((1,), (1,)), ((), ()))
    else:
        a_spec = pl.BlockSpec((tk, tm), lambda i, j, kk: (kk, i))
        b_spec = pl.BlockSpec((tk, tn), lambda i, j, kk: (kk, j))
        dims = (((0,), (0,)), ((), ()))
    o_spec = pl.BlockSpec((tm, tn), lambda i, j, kk: (i, j))
    has_res = res is not None

    def body(*refs):
        if has_res:
            a_ref, b_ref, r_ref, o_ref, acc_ref = refs
        else:
            a_ref, b_ref, o_ref, acc_ref = refs
        kk = pl.program_id(2)

        @pl.when(kk == 0)
        def _():
            acc_ref[...] = jnp.zeros_like(acc_ref)

        acc_ref[...] += lax.dot_general(a_ref[...].astype(BF16), b_ref[...].astype(BF16), dims,
                                        preferred_element_type=F32)

        @pl.when(kk == nk - 1)
        def _():
            out = acc_ref[...]
            if has_res:
                out = out + r_ref[...]
            o_ref[...] = out.astype(out_dtype)

    ins = [a, b] + ([res] if has_res else [])
    specs = [a_spec, b_spec] + ([o_spec] if has_res else [])
    return pl.pallas_call(
        body, name=name, grid=(m // tm, n // tn, nk),
        out_shape=jax.ShapeDtypeStruct((m, n), out_dtype),
        in_specs=specs, out_specs=o_spec,
        scratch_shapes=[pltpu.VMEM((tm, tn), F32)],
        compiler_params=_params("parallel", "parallel", "arbitrary"),
    )(*ins)


def _rms_fwd(x, g, name):
    t, d = x.shape
    tm = _tile(t, 256, 8)

    def body(x_ref, g_ref, h_ref):
        xv = x_ref[...]
        rstd = lax.rsqrt(jnp.mean(xv * xv, axis=-1, keepdims=True) + RMS_EPS)
        h_ref[...] = (xv * rstd * g_ref[...]).astype(BF16)

    return pl.pallas_call(
        body, name=name, grid=(t // tm,),
        out_shape=jax.ShapeDtypeStruct((t, d), BF16),
        in_specs=[pl.BlockSpec((tm, d), lambda i: (i, 0)), pl.BlockSpec((1, d), lambda i: (0, 0))],
        out_specs=pl.BlockSpec((tm, d), lambda i: (i, 0)),
        compiler_params=_params("parallel"),
    )(x, g)


def _rms_bwd(x, g, dh, dres, name):
    t, d = x.shape
    tm = _tile(t, 256, 8)

    def body(x_ref, g_ref, dh_ref, dres_ref, dx_ref, dg_ref):
        @pl.when(pl.program_id(0) == 0)
        def _():
            dg_ref[...] = jnp.zeros_like(dg_ref)

        xv = x_ref[...]
        rstd = lax.rsqrt(jnp.mean(xv * xv, axis=-1, keepdims=True) + RMS_EPS)
        xhat = xv * rstd
        dh_v = dh_ref[...]
        dg_ref[...] += jnp.sum(dh_v * xhat, axis=0, keepdims=True)
        dxh = dh_v * g_ref[...]
        dx_ref[...] = dres_ref[...] + rstd * (dxh - xhat * jnp.mean(dxh * xhat, axis=-1, keepdims=True))

    row = pl.BlockSpec((tm, d), lambda i: (i, 0))
    vec = pl.BlockSpec((1, d), lambda i: (0, 0))
    return pl.pallas_call(
        body, name=name, grid=(t // tm,),
        out_shape=(jax.ShapeDtypeStruct((t, d), F32), jax.ShapeDtypeStruct((1, d), F32)),
        in_specs=[row, vec, row, row], out_specs=(row, vec),
        compiler_params=_params("arbitrary"),
    )(x, g, dh, dres)


def _final_loss(x, g, target, name):
    t, d = x.shape
    tm = _tile(t, 256, 8)

    def body(x_ref, g_ref, t_ref, dx_ref, dg_ref, loss_ref):
        @pl.when(pl.program_id(0) == 0)
        def _():
            dg_ref[...] = jnp.zeros_like(dg_ref)
            loss_ref[...] = jnp.zeros_like(loss_ref)

        xv = x_ref[...]
        rstd = lax.rsqrt(jnp.mean(xv * xv, axis=-1, keepdims=True) + RMS_EPS)
        xhat = xv * rstd
        gv = g_ref[...]
        err = xhat * gv - t_ref[...]
        loss_ref[...] += 0.5 * jnp.sum(jnp.mean(err * err, axis=-1, keepdims=True), axis=0, keepdims=True)
        dout = err * (1.0 / d)
        dg_ref[...] += jnp.sum(dout * xhat, axis=0, keepdims=True)
        dxh = dout * gv
        dx_ref[...] = rstd * (dxh - xhat * jnp.mean(dxh * xhat, axis=-1, keepdims=True))

    row = pl.BlockSpec((tm, d), lambda i: (i, 0))
    vec = pl.BlockSpec((1, d), lambda i: (0, 0))
    return pl.pallas_call(
        body, name=name, grid=(t // tm,),
        out_shape=(jax.ShapeDtypeStruct((t, d), F32), jax.ShapeDtypeStruct((1, d), F32),
                   jax.ShapeDtypeStruct((1, 1), F32)),
        in_specs=[row, vec, row], out_specs=(row, vec, pl.BlockSpec((1, 1), lambda i: (0, 0))),
        compiler_params=_params("arbitrary"),
    )(x, g, target)


def _dot32(a, b):
    return jnp.dot(a, b, precision=HIGHEST, preferred_element_type=F32)


def _head_sum(v, e, et):
    return _dot32(_dot32(v, e), et)


def _log_sigmoid(z):
    return jnp.minimum(z, 0.0) - jnp.log1p(jnp.exp(-jnp.abs(z)))


def _silu(g):
    return g * jax.nn.sigmoid(g)


def _prep_math(k, wlo, alo, wd, w0, wa, a0, k_k, k_a, e, et):
    wl = w0 + _dot32(jnp.tanh(wlo), wd)
    w_log = _log_sigmoid(wl) - 0.5
    w = jnp.exp(-jnp.exp(w_log))
    a = jax.nn.sigmoid(a0 + _dot32(alo, wa))
    kk0 = k * k_k
    kk = kk0 * lax.rsqrt(jnp.maximum(_head_sum(kk0 * kk0, e, et), L2_EPS * L2_EPS))
    kx = k * (1.0 + (a - 1.0) * k_a)
    return w, kx, kk, kk * a


def _post_math(ys, r, kx, v, g, gn_g, gn_b, r_k, e, et):
    inv = 1.0 / RWKV_HEAD
    mu = _head_sum(ys, e, et) * inv
    dlt = ys - mu
    var = _head_sum(dlt * dlt, e, et) * inv
    y = dlt * lax.rsqrt(var + GN_EPS) * gn_g + gn_b
    bonus = _head_sum(r * kx * r_k, e, et) * v
    return (y + bonus) * _silu(g)


def _shifted(p, prev_row, first):
    rows = lax.broadcasted_iota(jnp.int32, p.shape, 0)
    prev = jnp.where(first, 0.0, prev_row)
    return jnp.where(rows == 0, prev, pltpu.roll(p, 1, 0))


def _head_indicator(width):
    ch = lax.broadcasted_iota(jnp.int32, (width, width // RWKV_HEAD), 0) // RWKV_HEAD
    hd = lax.broadcasted_iota(jnp.int32, (width, width // RWKV_HEAD), 1)
    e = (ch == hd).astype(F32)
    return e, e.T


def _rwkv_prep(p, seq, dims, mu, wd, w0, wa, a0, k_k, k_a, e, et):
    t = p.shape[0]
    width, lora = dims
    cols = 3 * width + 2 * lora
    tm = 128
    per_seq = seq // tm

    def body(p_ref, prev_ref, mu_ref, wd_ref, w0_ref, wa_ref, a0_ref, kk_ref, ka_ref, e_ref, et_ref,
             r_out, w_out, kx_out, v_out, kkn_out, b_out):
        i = pl.program_id(0)
        pv = p_ref[...]
        psh = _shifted(pv, prev_ref[7:8, :], i % per_seq == 0)
        ps = pv + mu_ref[...] * (psh - pv)
        r, k, v = ps[:, :width], ps[:, width:2 * width], ps[:, 2 * width:3 * width]
        wlo, alo = ps[:, 3 * width:3 * width + lora], ps[:, 3 * width + lora:]
        w, kx, kk, b = _prep_math(k, wlo, alo, wd_ref[...], w0_ref[...], wa_ref[...], a0_ref[...],
                                  kk_ref[...], ka_ref[...], e_ref[...], et_ref[...])
        r_out[...] = r
        w_out[...] = w
        kx_out[...] = kx
        v_out[...] = v
        kkn_out[...] = kk
        b_out[...] = b

    full = lambda a: pl.BlockSpec(a.shape, lambda i: (0,) * a.ndim)
    out = pl.BlockSpec((tm, width), lambda i: (i, 0))
    return pl.pallas_call(
        body, name="rwkv_prep", grid=(t // tm,),
        out_shape=tuple(jax.ShapeDtypeStruct((t, width), F32) for _ in range(6)),
        in_specs=[pl.BlockSpec((tm, cols), lambda i: (i, 0)),
                  pl.BlockSpec((8, cols), lambda i: (jnp.maximum(i * (tm // 8) - 1, 0), 0)),
                  full(mu), full(wd), full(w0), full(wa), full(a0), full(k_k), full(k_a), full(e), full(et)],
        out_specs=tuple(out for _ in range(6)),
        compiler_params=_params("parallel"),
    )(p, p, mu, wd, w0, wa, a0, k_k, k_a, e, et)


def _rwkv_prep_bwd(p, seq, dims, mu, wd, w0, wa, a0, k_k, k_a, e, et, dr, dw, dkx, dv, dkk, db):
    t = p.shape[0]
    width, lora = dims
    cols = 3 * width + 2 * lora
    tm = 128
    n_tiles = t // tm
    per_seq = seq // tm

    def body(p_ref, prev_ref, mu_ref, wd_ref, w0_ref, wa_ref, a0_ref, kk_ref, ka_ref, e_ref, et_ref,
             dr_ref, dw_ref, dkx_ref, dv_ref, dkk_ref, db_ref,
             dp_out, dmu_out, dwd_out, dw0_out, dwa_out, da0_out, dkk_out, dka_out, carry):
        step = pl.program_id(0)
        i = n_tiles - 1 - step

        @pl.when(step == 0)
        def _():
            for ref in (dmu_out, dwd_out, dw0_out, dwa_out, da0_out, dkk_out, dka_out, carry):
                ref[...] = jnp.zeros_like(ref)

        pv = p_ref[...]
        first = i % per_seq == 0
        psh = _shifted(pv, prev_ref[7:8, :], first)
        muv = mu_ref[...]
        ps = pv + muv * (psh - pv)
        k = ps[:, width:2 * width]
        wlo, alo = ps[:, 3 * width:3 * width + lora], ps[:, 3 * width + lora:]
        ev, etv = e_ref[...], et_ref[...]
        _, vjp = jax.vjp(lambda *a: _prep_math(*a, ev, etv), k, wlo, alo, wd_ref[...], w0_ref[...],
                         wa_ref[...], a0_ref[...], kk_ref[...], ka_ref[...])
        dk, dwlo, dalo, dwd, dw0, dwa, da0, dk_k, dk_a = vjp(
            (dw_ref[...], dkx_ref[...], dkk_ref[...], db_ref[...]))
        dps = jnp.concatenate([dr_ref[...], dk, dv_ref[...], dwlo, dalo], axis=1)
        dmu_out[...] += jnp.sum(dps * (psh - pv), axis=0, keepdims=True)
        dwd_out[...] += dwd
        dw0_out[...] += dw0
        dwa_out[...] += dwa
        da0_out[...] += da0
        dkk_out[...] += dk_k
        dka_out[...] += dk_a
        dsh = dps * muv
        rows = lax.broadcasted_iota(jnp.int32, dsh.shape, 0)
        nxt = jnp.where(rows == tm - 1, carry[...], pltpu.roll(dsh, tm - 1, 0))
        dp_out[...] = (dps * (1.0 - muv) + nxt).astype(BF16)
        carry[...] = jnp.where(first, 0.0, dsh[0:1, :])

    full = lambda a: pl.BlockSpec(a.shape, lambda s: (0,) * a.ndim)
    tok = pl.BlockSpec((tm, width), lambda s: (n_tiles - 1 - s, 0))
    vec = lambda n: jax.ShapeDtypeStruct((1, n), F32)
    outs = (jax.ShapeDtypeStruct((t, cols), BF16), vec(cols), jax.ShapeDtypeStruct(wd.shape, F32), vec(width),
            jax.ShapeDtypeStruct(wa.shape, F32), vec(width), vec(width), vec(width))
    return pl.pallas_call(
        body, name="rwkv_prep_bwd", grid=(n_tiles,),
        out_shape=outs,
        in_specs=[pl.BlockSpec((tm, cols), lambda s: (n_tiles - 1 - s, 0)),
                  pl.BlockSpec((8, cols), lambda s: (jnp.maximum((n_tiles - 1 - s) * (tm // 8) - 1, 0), 0)),
                  full(mu), full(wd), full(w0), full(wa), full(a0), full(k_k), full(k_a), full(e), full(et),
                  tok, tok, tok, tok, tok, tok],
        out_specs=(pl.BlockSpec((tm, cols), lambda s: (n_tiles - 1 - s, 0)),) + tuple(
            pl.BlockSpec(o.shape, lambda s: (0, 0)) for o in outs[1:]),
        scratch_shapes=[pltpu.VMEM((1, cols), F32)],
        compiler_params=_params("arbitrary"),
    )(p, p, mu, wd, w0, wa, a0, k_k, k_a, e, et, dr, dw, dkx, dv, dkk, db)


def _rwkv_post(ys, r, kx, v, p, g_col, gn_g, gn_b, r_k, e, et):
    t, width = ys.shape
    tm = 256

    def body(ys_ref, r_ref, kx_ref, v_ref, g_ref, gg_ref, gb_ref, rk_ref, e_ref, et_ref, out_ref):
        out_ref[...] = _post_math(ys_ref[...], r_ref[...], kx_ref[...], v_ref[...], g_ref[...], gg_ref[...],
                                  gb_ref[...], rk_ref[...], e_ref[...], et_ref[...]).astype(BF16)

    tok = pl.BlockSpec((tm, width), lambda i: (i, 0))
    full = lambda a: pl.BlockSpec(a.shape, lambda i: (0,) * a.ndim)
    return pl.pallas_call(
        body, name="rwkv_post", grid=(t // tm,),
        out_shape=jax.ShapeDtypeStruct((t, width), BF16),
        in_specs=[tok, tok, tok, tok, pl.BlockSpec((tm, width), lambda i: (i, g_col)),
                  full(gn_g), full(gn_b), full(r_k), full(e), full(et)],
        out_specs=tok,
        compiler_params=_params("parallel"),
    )(ys, r, kx, v, p, gn_g, gn_b, r_k, e, et)


def _rwkv_post_bwd(ys, r, kx, v, p, g_col, gn_g, gn_b, r_k, e, et, dy, dy_col):
    t, width = ys.shape
    tm = 128

    def body(ys_ref, r_ref, kx_ref, v_ref, g_ref, gg_ref, gb_ref, rk_ref, e_ref, et_ref, dy_ref,
             dys_out, dr_out, dkx_out, dv_out, dg_out, dgg_out, dgb_out, drk_out):
        @pl.when(pl.program_id(0) == 0)
        def _():
            for ref in (dgg_out, dgb_out, drk_out):
                ref[...] = jnp.zeros_like(ref)

        ev, etv = e_ref[...], et_ref[...]
        _, vjp = jax.vjp(lambda *a: _post_math(*a, ev, etv), ys_ref[...], r_ref[...], kx_ref[...], v_ref[...],
                         g_ref[...], gg_ref[...], gb_ref[...], rk_ref[...])
        dys, dr, dkx, dv, dg, dgg, dgb, drk = vjp(dy_ref[...])
        dys_out[...] = dys
        dr_out[...] = dr
        dkx_out[...] = dkx
        dv_out[...] = dv
        dg_out[...] = dg.astype(BF16)
        dgg_out[...] += dgg
        dgb_out[...] += dgb
        drk_out[...] += drk

    tok = pl.BlockSpec((tm, width), lambda i: (i, 0))
    full = lambda a: pl.BlockSpec(a.shape, lambda i: (0,) * a.ndim)
    big = jax.ShapeDtypeStruct((t, width), F32)
    vec = jax.ShapeDtypeStruct((1, width), F32)
    vspec = pl.BlockSpec((1, width), lambda i: (0, 0))
    return pl.pallas_call(
        body, name="rwkv_post_bwd", grid=(t // tm,),
        out_shape=(big, big, big, big, jax.ShapeDtypeStruct((t, width), BF16), vec, vec, vec),
        in_specs=[tok, tok, tok, tok, pl.BlockSpec((tm, width), lambda i: (i, g_col)),
                  full(gn_g), full(gn_b), full(r_k), full(e), full(et),
                  pl.BlockSpec((tm, width), lambda i: (i, dy_col))],
        out_specs=(tok, tok, tok, tok, tok, vspec, vspec, vspec),
        compiler_params=_params("arbitrary"),
    )(ys, r, kx, v, p, gn_g, gn_b, r_k, e, et, dy)


def _to_scan_k(a, bl, seq):
    h = a.shape[1] // RWKV_HEAD
    a = a.reshape(bl, seq, h, RWKV_HEAD).transpose(1, 3, 0, 2).reshape(seq, RWKV_HEAD, bl * h)
    return jnp.concatenate([a, a], axis=-1)


def _to_scan_v(a, bl, seq):
    h = a.shape[1] // RWKV_HEAD
    half = RWKV_HEAD // 2
    return a.reshape(bl, seq, h, 2, half).transpose(1, 4, 3, 0, 2).reshape(seq, half, 2 * bl * h)


def _from_scan_k(a, bl, seq):
    h = a.shape[2] // (2 * bl)
    a = a[:, :, :bl * h].reshape(seq, RWKV_HEAD, bl, h).transpose(2, 0, 3, 1)
    return a.reshape(bl * seq, h * RWKV_HEAD)


def _from_scan_v(a, bl, seq):
    half = RWKV_HEAD // 2
    h = a.shape[2] // (2 * bl)
    a = a.reshape(seq, half, 2, bl, h).transpose(3, 0, 4, 2, 1)
    return a.reshape(bl * seq, h * RWKV_HEAD)


def _scan_fwd(kk, w, b, kx, r, v):
    seq, nk, lanes = kk.shape
    nv = v.shape[1]
    tt = SCAN_STEPS_PER_BLOCK

    def body(kk_ref, w_ref, b_ref, kx_ref, r_ref, v_ref, y_ref, st_ref, s_ref):
        @pl.when(pl.program_id(0) == 0)
        def _():
            s_ref[...] = jnp.zeros_like(s_ref)

        def step(i, carry):
            kkv, wv, bv, kxv, rv = kk_ref[i], w_ref[i], b_ref[i], kx_ref[i], r_ref[i]
            for j in range(nv):
                s_old = s_ref[j]
                st_ref[i, j] = s_old
                sa = -jnp.sum(s_old * kkv, axis=0, keepdims=True)
                s_new = s_old * wv + sa * bv + v_ref[i, j:j + 1, :] * kxv
                s_ref[j] = s_new
                y_ref[i, j:j + 1, :] = jnp.sum(s_new * rv, axis=0, keepdims=True)
            return carry

        lax.fori_loop(0, tt, step, 0)

    krow = pl.BlockSpec((tt, nk, lanes), lambda i: (i, 0, 0))
    vrow = pl.BlockSpec((tt, nv, lanes), lambda i: (i, 0, 0))
    return pl.pallas_call(
        body, name="rwkv_scan_fwd", grid=(seq // tt,),
        out_shape=(jax.ShapeDtypeStruct((seq, nv, lanes), F32), jax.ShapeDtypeStruct((seq, nv, nk, lanes), F32)),
        in_specs=[krow, krow, krow, krow, krow, vrow],
        out_specs=(vrow, pl.BlockSpec((tt, nv, nk, lanes), lambda i: (i, 0, 0, 0))),
        scratch_shapes=[pltpu.VMEM((nv, nk, lanes), F32)],
        compiler_params=_params("arbitrary"),
    )(kk, w, b, kx, r, v)


def _scan_bwd(kk, w, b, kx, r, v, states, dy):
    seq, nk, lanes = kk.shape
    nv = v.shape[1]
    tt = SCAN_STEPS_PER_BLOCK
    nblk = seq // tt

    def both_halves(a):
        return a + pltpu.roll(a, lanes // 2, 1)

    def body(kk_ref, w_ref, b_ref, kx_ref, r_ref, v_ref, st_ref, dy_ref,
             dkk_ref, dw_ref, db_ref, dkx_ref, dr_ref, dv_ref, g_ref):
        @pl.when(pl.program_id(0) == 0)
        def _():
            g_ref[...] = jnp.zeros_like(g_ref)

        def step(n, carry):
            i = tt - 1 - n
            kkv, wv, bv, kxv, rv = kk_ref[i], w_ref[i], b_ref[i], kx_ref[i], r_ref[i]
            zero = jnp.zeros((nk, lanes), F32)
            a_r, a_w, a_b, a_kx, a_kk = zero, zero, zero, zero, zero
            for j in range(nv):
                s_old = st_ref[i, j]
                vv = v_ref[i, j:j + 1, :]
                dyv = dy_ref[i, j:j + 1, :]
                sa = -jnp.sum(s_old * kkv, axis=0, keepdims=True)
                s_new = s_old * wv + sa * bv + vv * kxv
                g = g_ref[j] + dyv * rv
                a_r = a_r + s_new * dyv
                a_w = a_w + g * s_old
                dsa = jnp.sum(g * bv, axis=0, keepdims=True)
                a_b = a_b + g * sa
                dv_ref[i, j:j + 1, :] = jnp.sum(g * kxv, axis=0, keepdims=True)
                a_kx = a_kx + g * vv
                a_kk = a_kk + s_old * dsa
                g_ref[j] = g * wv - dsa * kkv
            dr_ref[i] = both_halves(a_r)
            dw_ref[i] = both_halves(a_w)
            db_ref[i] = both_halves(a_b)
            dkx_ref[i] = both_halves(a_kx)
            dkk_ref[i] = -both_halves(a_kk)
            return carry

        lax.fori_loop(0, tt, step, 0)

    rev = lambda i: nblk - 1 - i
    krow = pl.BlockSpec((tt, nk, lanes), lambda i: (rev(i), 0, 0))
    vrow = pl.BlockSpec((tt, nv, lanes), lambda i: (rev(i), 0, 0))
    kshape = jax.ShapeDtypeStruct((seq, nk, lanes), F32)
    return pl.pallas_call(
        body, name="rwkv_scan_bwd", grid=(nblk,),
        out_shape=(kshape, kshape, kshape, kshape, kshape, jax.ShapeDtypeStruct((seq, nv, lanes), F32)),
        in_specs=[krow, krow, krow, krow, krow, vrow,
                  pl.BlockSpec((tt, nv, nk, lanes), lambda i: (rev(i), 0, 0, 0)), vrow],
        out_specs=(krow, krow, krow, krow, krow, vrow),
        scratch_shapes=[pltpu.VMEM((nv, nk, lanes), F32)],
        compiler_params=_params("arbitrary"),
    )(kk, w, b, kx, r, v, states, dy)


_NT = (((1,), (1,)), ((), ()))
_TN = (((0,), (0,)), ((), ()))
SB_SCALE = 1.0 / (SB_HEAD ** 0.5)


def _split_dot(a, b):
    hi = a.astype(BF16)
    lo = (a - hi.astype(F32)).astype(BF16)
    return jnp.dot(hi, b, preferred_element_type=F32) + jnp.dot(lo, b, preferred_element_type=F32)


def _sb_masks():
    blk = SB_BLOCK
    row = lax.broadcasted_iota(jnp.int32, (blk, blk), 0)
    col = lax.broadcasted_iota(jnp.int32, (blk, blk), 1)
    ones = jnp.ones((blk, blk), BF16)
    fwd = jnp.concatenate([(row > col).astype(BF16), ones], axis=1)
    bwd = jnp.concatenate([(col > row).astype(BF16), ones], axis=1)
    return row, col, fwd, bwd


def _sb_block(qb, k_ref, j, qi, row, col, mix):
    rows = pl.ds(pl.multiple_of(j * SB_BLOCK, SB_BLOCK), SB_BLOCK)
    z = lax.dot_general(qb, k_ref[rows, :].astype(BF16), _NT, preferred_element_type=F32) * SB_SCALE
    causal = (j * SB_BLOCK + col) < (qi * SB_BLOCK + row)
    lsz = _log_sigmoid(z)
    keep = jnp.where(causal, lsz - z, 0.0)
    return rows, causal, lsz, z, _split_dot(keep, mix)


def _sb_fwd(p, bl, seq, cols, n_heads):
    t = p.shape[0]
    nq = seq // SB_BLOCK
    q_col, k_col, v_col, g_col = cols
    blk = SB_BLOCK

    def body(q_ref, k_ref, v_ref, g_ref, out_ref, o_ref, tot_ref):
        qi = pl.program_id(2)
        row, col, mix, _ = _sb_masks()
        qb = q_ref[...].astype(BF16)

        def step(n, carry):
            out, later = carry
            rows, causal, lsz, _, both = _sb_block(qb, k_ref, qi - n, qi, row, col, mix)
            att = jnp.where(causal, jnp.exp(lsz + both[:, :blk] + later), 0.0)
            out = out + jnp.dot(att.astype(BF16), v_ref[rows, :].astype(BF16), preferred_element_type=F32)
            return out, later + both[:, blk:]

        zero = jnp.zeros((blk, SB_HEAD), F32)
        out, total = lax.fori_loop(0, qi + 1, step, (zero, zero))
        o_ref[...] = out
        tot_ref[...] = total
        out_ref[...] = (out * _silu(g_ref[...])).astype(BF16)

    qspec = lambda col: pl.BlockSpec((blk, SB_HEAD), lambda b, h, i: (b * nq + i, col + h))
    kspec = lambda col: pl.BlockSpec((seq, SB_HEAD), lambda b, h, i: (b, col + h))
    f32 = jax.ShapeDtypeStruct((t, n_heads * SB_HEAD), F32)
    return pl.pallas_call(
        body, name="sb_attn_fwd", grid=(bl, n_heads, nq),
        out_shape=(jax.ShapeDtypeStruct((t, n_heads * SB_HEAD), BF16), f32, f32),
        in_specs=[qspec(q_col), kspec(k_col), kspec(v_col), qspec(g_col)],
        out_specs=(qspec(0), qspec(0), qspec(0)),
        compiler_params=_params("parallel", "parallel", "arbitrary"),
    )(p, p, p, p)


def _sb_bwd(p, bl, seq, cols, n_heads, dy, dy_col, o, tot):
    t = p.shape[0]
    nq = seq // SB_BLOCK
    q_col, k_col, v_col, g_col = cols
    blk = SB_BLOCK

    def body(q_ref, k_ref, v_ref, g_ref, dy_ref, o_ref, tot_ref, dq_out, dk_out, dv_out, dg_out, dk_acc, dv_acc):
        qi = pl.program_id(2)

        @pl.when(qi == 0)
        def _():
            dk_acc[...] = jnp.zeros_like(dk_acc)
            dv_acc[...] = jnp.zeros_like(dv_acc)

        row, col, mix, mix_t = _sb_masks()
        gate = g_ref[...]
        sg = jax.nn.sigmoid(gate)
        dyv = dy_ref[...]
        dg_out[...] = (dyv * o_ref[...] * (sg * (1.0 + gate * (1.0 - sg)))).astype(BF16)
        dob = (dyv * (gate * sg)).astype(BF16)
        qb = q_ref[...].astype(BF16)
        total = tot_ref[...]

        def step(j, carry):
            dq, seen, dl_before = carry
            rows, causal, lsz, z, both = _sb_block(qb, k_ref, j, qi, row, col, mix)
            seen = seen + both[:, blk:]
            att = jnp.where(causal, jnp.exp(lsz + both[:, :blk] + (total - seen)), 0.0)
            datt = lax.dot_general(dob, v_ref[rows, :].astype(BF16), _NT, preferred_element_type=F32)
            dl = att * datt
            dv_acc[rows, :] += lax.dot_general(att.astype(BF16), dob, _TN, preferred_element_type=F32)
            both_t = _split_dot(dl, mix_t)
            dkeep = jnp.where(causal, both_t[:, :blk] + dl_before, 0.0)
            dzb = (((dl + dkeep) * jax.nn.sigmoid(-z) - dkeep) * SB_SCALE).astype(BF16)
            dq = dq + jnp.dot(dzb, k_ref[rows, :].astype(BF16), preferred_element_type=F32)
            dk_acc[rows, :] += lax.dot_general(dzb, qb, _TN, preferred_element_type=F32)
            return dq, seen, dl_before + both_t[:, blk:]

        zero = jnp.zeros((blk, SB_HEAD), F32)
        dq, _, _ = lax.fori_loop(0, qi + 1, step, (zero, zero, zero))
        dq_out[...] = dq.astype(BF16)

        @pl.when(qi == nq - 1)
        def _():
            dk_out[...] = dk_acc[...].astype(BF16)
            dv_out[...] = dv_acc[...].astype(BF16)

    qspec = lambda col: pl.BlockSpec((blk, SB_HEAD), lambda b, h, i: (b * nq + i, col + h))
    kspec = lambda col: pl.BlockSpec((seq, SB_HEAD), lambda b, h, i: (b, col + h))
    width = n_heads * SB_HEAD
    shape = jax.ShapeDtypeStruct((t, width), BF16)
    return pl.pallas_call(
        body, name="sb_attn_bwd", grid=(bl, n_heads, nq),
        out_shape=(shape, shape, shape, shape),
        in_specs=[qspec(q_col), kspec(k_col), kspec(v_col), qspec(g_col), qspec(dy_col), qspec(0), qspec(0)],
        out_specs=(qspec(0), kspec(0), kspec(0), qspec(0)),
        scratch_shapes=[pltpu.VMEM((seq, SB_HEAD), F32), pltpu.VMEM((seq, SB_HEAD), F32)],
        compiler_params=_params("parallel", "parallel", "arbitrary"),
    )(p, p, p, p, dy, o, tot)


def _gelu(x):
    return 0.5 * x * (1.0 + lax.erf(x * (2.0 ** -0.5)))


def _sgu_math(us, vs, gs, ln_g, ln_b, ws, bs):
    width = sum(v.shape[1] for v in vs)
    vg = [_gelu(v) for v in vs]
    mu = sum(jnp.sum(v, axis=1, keepdims=True) for v in vg) * (1.0 / width)
    dl = [v - mu for v in vg]
    var = sum(jnp.sum(d * d, axis=1, keepdims=True) for d in dl) * (1.0 / width)
    rstd = lax.rsqrt(var + LN_EPS)
    n = ws[0].shape[0]
    tri = lax.broadcasted_iota(jnp.int32, (n, n), 0) >= lax.broadcasted_iota(jnp.int32, (n, n), 1)
    outs = []
    for i in range(len(vs)):
        vn = dl[i] * rstd * ln_g[i] + ln_b[i]
        mixed = _dot32(jnp.where(tri, ws[i], 0.0), vn) + bs[i]
        outs.append(_gelu(us[i]) * mixed * _silu(gs[i]))
    return outs


def _sgu_load(p_ref, lng_ref, lnb_ref, ws_ref, bs_ref, width):
    gd = width // SGU_GROUPS
    grp = lambda ref, base, i: ref[:, base + i * gd:base + (i + 1) * gd]
    idx = range(SGU_GROUPS)
    return ([grp(p_ref, 0, i) for i in idx], [grp(p_ref, width, i) for i in idx],
            [grp(p_ref, 2 * width, i) for i in idx], [grp(lng_ref, 0, i) for i in idx],
            [grp(lnb_ref, 0, i) for i in idx], [ws_ref[i] for i in idx], [bs_ref[i] for i in idx])


def _sgu_fwd(p, ln_g, ln_b, w_s, b_s):
    t = p.shape[0]
    width = p.shape[1] // 3
    gd = width // SGU_GROUPS
    tm = SGU_CHUNK

    def body(p_ref, lng_ref, lnb_ref, ws_ref, bs_ref, y_ref):
        outs = _sgu_math(*_sgu_load(p_ref, lng_ref, lnb_ref, ws_ref, bs_ref, width))
        for i in range(SGU_GROUPS):
            y_ref[:, i * gd:(i + 1) * gd] = outs[i].astype(BF16)

    full = lambda a: pl.BlockSpec(a.shape, lambda i: (0,) * a.ndim)
    return pl.pallas_call(
        body, name="sgu_fwd", grid=(t // tm,),
        out_shape=jax.ShapeDtypeStruct((t, width), BF16),
        in_specs=[pl.BlockSpec((tm, 3 * width), lambda i: (i, 0)), full(ln_g), full(ln_b), full(w_s), full(b_s)],
        out_specs=pl.BlockSpec((tm, width), lambda i: (i, 0)),
        compiler_params=_params("parallel"),
    )(p, ln_g, ln_b, w_s, b_s)


def _sgu_bwd(p, ln_g, ln_b, w_s, b_s, dy):
    t = p.shape[0]
    width = p.shape[1] // 3
    gd = width // SGU_GROUPS
    tm = SGU_CHUNK

    def body(p_ref, lng_ref, lnb_ref, ws_ref, bs_ref, dy_ref, dp_out, dlng_out, dlnb_out, dws_out, dbs_out):
        @pl.when(pl.program_id(0) == 0)
        def _():
            for ref in (dlng_out, dlnb_out, dws_out, dbs_out):
                ref[...] = jnp.zeros_like(ref)

        _, vjp = jax.vjp(_sgu_math, *_sgu_load(p_ref, lng_ref, lnb_ref, ws_ref, bs_ref, width))
        dus, dvs, dgs, dlng, dlnb, dws, dbs = vjp(
            [dy_ref[:, i * gd:(i + 1) * gd] for i in range(SGU_GROUPS)])
        for i in range(SGU_GROUPS):
            cols = slice(i * gd, (i + 1) * gd)
            dp_out[:, i * gd:(i + 1) * gd] = dus[i].astype(BF16)
            dp_out[:, width + i * gd:width + (i + 1) * gd] = dvs[i].astype(BF16)
            dp_out[:, 2 * width + i * gd:2 * width + (i + 1) * gd] = dgs[i].astype(BF16)
            dlng_out[:, cols] += dlng[i]
            dlnb_out[:, cols] += dlnb[i]
            dws_out[i] += dws[i]
            dbs_out[i] += dbs[i]

    full = lambda a: pl.BlockSpec(a.shape, lambda i: (0,) * a.ndim)
    like = lambda a: jax.ShapeDtypeStruct(a.shape, F32)
    return pl.pallas_call(
        body, name="sgu_bwd", grid=(t // tm,),
        out_shape=(jax.ShapeDtypeStruct((t, 3 * width), BF16), like(ln_g), like(ln_b), like(w_s), like(b_s)),
        in_specs=[pl.BlockSpec((tm, 3 * width), lambda i: (i, 0)), full(ln_g), full(ln_b), full(w_s), full(b_s),
                  pl.BlockSpec((tm, width), lambda i: (i, 0))],
        out_specs=(pl.BlockSpec((tm, 3 * width), lambda i: (i, 0)), full(ln_g), full(ln_b), full(w_s), full(b_s)),
        compiler_params=_params("arbitrary"),
    )(p, ln_g, ln_b, w_s, b_s, dy)


def _sum_slabs(parts, name):
    n_parts, rows, cols = parts.shape
    tr = _tile(rows, max(16, (1 << 18) // cols), 16)

    def body(p_ref, o_ref):
        acc = p_ref[0].astype(F32)
        for d in range(1, n_parts):
            acc = acc + p_ref[d].astype(F32)
        o_ref[...] = acc

    return pl.pallas_call(
        body, name=name, grid=(rows // tr,),
        out_shape=jax.ShapeDtypeStruct((rows, cols), F32),
        in_specs=[pl.BlockSpec((n_parts, tr, cols), lambda i: (0, i, 0))],
        out_specs=pl.BlockSpec((tr, cols), lambda i: (i, 0)),
        compiler_params=_params("parallel"),
    )(parts)


def _adamw(w, g, m, v, name):
    rows, cols = w.shape
    tr = _tile(rows, max(8, (1 << 18) // cols), 8)

    def body(w_ref, g_ref, m_ref, v_ref, d_out, m_out, v_out):
        gv = g_ref[...]
        mn = ADAM_B1 * m_ref[...] + (1.0 - ADAM_B1) * gv
        vn = ADAM_B2 * v_ref[...] + (1.0 - ADAM_B2) * (gv * gv)
        m_hat = mn / (1.0 - ADAM_B1 ** ADAM_STEP)
        v_hat = vn / (1.0 - ADAM_B2 ** ADAM_STEP)
        d_out[...] = -ADAM_LR * (m_hat / (jnp.sqrt(v_hat) + ADAM_EPS) + ADAM_WD * w_ref[...])
        m_out[...] = mn
        v_out[...] = vn

    blk = pl.BlockSpec((tr, cols), lambda i: (i, 0))
    shape = jax.ShapeDtypeStruct((rows, cols), F32)
    return pl.pallas_call(
        body, name=name, grid=(rows // tr,),
        out_shape=(shape, shape, shape),
        in_specs=[blk, blk, blk, blk], out_specs=(blk, blk, blk),
        compiler_params=_params("parallel"),
    )(w, g, m, v)


PACK_COLS = 1024


def _pack(arrays):
    flat = jnp.concatenate([a.reshape(-1).astype(F32) for a in arrays])
    rows = -(-flat.shape[0] // (8 * PACK_COLS)) * 8
    return jnp.pad(flat, (0, rows * PACK_COLS - flat.shape[0])).reshape(rows, PACK_COLS)


def _unpack(packed, shapes):
    flat = packed.reshape(-1)
    out, at = [], 0
    for s in shapes:
        n = 1
        for d in s:
            n *= d
        out.append(flat[at:at + n].reshape(s))
        at += n
    return out


def kernel(x, norm_g, final_norm_g, e_w_in, e_shift_mu, e_w_decay_up, e_w0, e_a_up, e_a0, e_k_k, e_k_a, e_r_k, e_gn_g, e_gn_b, e_w_out, o_w_in, o_ln_g, o_ln_b, o_w_s, o_b_s, o_w_out, loss_target, m_norm_g, m_final_norm_g, m_e_w_in, m_e_shift_mu, m_e_w_decay_up, m_e_w0, m_e_a_up, m_e_a0, m_e_k_k, m_e_k_a, m_e_r_k, m_e_gn_g, m_e_gn_b, m_e_w_out, m_o_w_in, m_o_ln_g, m_o_ln_b, m_o_w_s, m_o_b_s, m_o_w_out, v_norm_g, v_final_norm_g, v_e_w_in, v_e_shift_mu, v_e_w_decay_up, v_e_w0, v_e_a_up, v_e_a0, v_e_k_k, v_e_k_a, v_e_r_k, v_e_gn_g, v_e_gn_b, v_e_w_out, v_o_w_in, v_o_ln_g, v_o_ln_b, v_o_w_s, v_o_b_s, v_o_w_out):
    weights = dict(norm_g=norm_g, final_norm_g=final_norm_g, e_w_in=e_w_in, e_shift_mu=e_shift_mu,
                   e_w_decay_up=e_w_decay_up, e_w0=e_w0, e_a_up=e_a_up, e_a0=e_a0, e_k_k=e_k_k, e_k_a=e_k_a,
                   e_r_k=e_r_k, e_gn_g=e_gn_g, e_gn_b=e_gn_b, e_w_out=e_w_out, o_w_in=o_w_in, o_ln_g=o_ln_g,
                   o_ln_b=o_ln_b, o_w_s=o_w_s, o_b_s=o_b_s, o_w_out=o_w_out)
    mom1 = dict(norm_g=m_norm_g, final_norm_g=m_final_norm_g, e_w_in=m_e_w_in, e_shift_mu=m_e_shift_mu,
                e_w_decay_up=m_e_w_decay_up, e_w0=m_e_w0, e_a_up=m_e_a_up, e_a0=m_e_a0, e_k_k=m_e_k_k,
                e_k_a=m_e_k_a, e_r_k=m_e_r_k, e_gn_g=m_e_gn_g, e_gn_b=m_e_gn_b, e_w_out=m_e_w_out,
                o_w_in=m_o_w_in, o_ln_g=m_o_ln_g, o_ln_b=m_o_ln_b, o_w_s=m_o_w_s, o_b_s=m_o_b_s,
                o_w_out=m_o_w_out)
    mom2 = dict(norm_g=v_norm_g, final_norm_g=v_final_norm_g, e_w_in=v_e_w_in, e_shift_mu=v_e_shift_mu,
                e_w_decay_up=v_e_w_decay_up, e_w0=v_e_w0, e_a_up=v_e_a_up, e_a0=v_e_a0, e_k_k=v_e_k_k,
                e_k_a=v_e_k_a, e_r_k=v_e_r_k, e_gn_g=v_e_gn_g, e_gn_b=v_e_gn_b, e_w_out=v_e_w_out,
                o_w_in=v_o_w_in, o_ln_g=v_o_ln_g, o_ln_b=v_o_ln_b, o_w_s=v_o_w_s, o_b_s=v_o_b_s,
                o_w_out=v_o_w_out)
    names = list(weights)
    big = ("e_w_in", "e_w_out", "o_w_in", "o_w_out")

    bl, seq, d = x.shape
    t = bl * seq
    width = e_w0.shape[1]
    lora = e_w_decay_up.shape[1]
    n_sb = width // SB_HEAD
    me = 4 * lax.axis_index("x") + 2 * lax.axis_index("y") + lax.axis_index("c")

    e_win_t = _all_gather(e_w_in[0].T.astype(BF16), "gather_e_w_in").reshape(-1, d)
    e_wout = _all_gather(e_w_out[0].astype(BF16), "gather_e_w_out").reshape(-1, d)
    o_win_t = _all_gather(o_w_in[0].T.astype(BF16), "gather_o_w_in").reshape(-1, d)
    o_wout = _all_gather(o_w_out[0].astype(BF16), "gather_o_w_out").reshape(-1, d)
    sharded_small = ("e_w_decay_up", "e_a_up", "o_ln_g", "o_ln_b")
    small_shapes = [weights[n][0].shape for n in sharded_small]
    got = _all_gather(_pack([weights[n][0] for n in sharded_small]), "gather_small")
    per_dev = [_unpack(got[dev], small_shapes) for dev in range(N_DEV)]
    wd, wa, ln_g, ln_b = [jnp.concatenate([per_dev[dev][i] for dev in range(N_DEV)], axis=-1).reshape(
        small_shapes[i][:-1] + (-1,)) for i in range(4)]
    ln_g, ln_b = ln_g.reshape(1, -1), ln_b.reshape(1, -1)
    e_ind, e_ind_t = _head_indicator(width)
    b_s3 = o_b_s[0][:, :, None]

    x2d = x.reshape(t, d)
    target = loss_target.reshape(t, d)
    cols_rwkv = 3 * width + 2 * lora
    assert cols_rwkv % LANES == 0 and width % LANES == 0
    sb0 = (cols_rwkv + width) // SB_HEAD
    sb_cols = (sb0, sb0 + n_sb, sb0 + 2 * n_sb, sb0 + 3 * n_sb)

    h0 = _rms_fwd(x2d, norm_g[0:1], "rms0_fwd")
    p = _matmul(h0, e_win_t, "nt", F32, "e_in_fwd", tn=1280)
    g_rwkv = p[:, cols_rwkv:cols_rwkv + width]
    r, w, kx, v, kk, b = _rwkv_prep(p, seq, (width, lora), e_shift_mu, wd, e_w0, wa, e_a0, e_k_k, e_k_a,
                                    e_ind, e_ind_t)
    sk = [_to_scan_k(a, bl, seq) for a in (kk, w, b, kx, r)]
    sv = _to_scan_v(v, bl, seq)
    ys_scan, states = _scan_fwd(*sk, sv)
    ys = _from_scan_v(ys_scan, bl, seq)
    ya = _rwkv_post(ys, r, kx, v, g_rwkv, 0, e_gn_g, e_gn_b, e_r_k, e_ind, e_ind_t)
    yb, sb_o, sb_tot = _sb_fwd(p, bl, seq, sb_cols, n_sb)
    y = jnp.concatenate([ya, yb], axis=1)
    x1 = _matmul(y, e_wout, "nn", F32, "e_out_fwd", res=x2d)
    h1 = _rms_fwd(x1, norm_g[1:2], "rms1_fwd")
    p2 = _matmul(h1, o_win_t, "nt", F32, "o_in_fwd")
    y2 = _sgu_fwd(p2, ln_g, ln_b, o_w_s[0], b_s3)
    x2 = _matmul(y2, o_wout, "nn", F32, "o_out_fwd", res=x1)
    dx2, d_final_g, loss_part = _final_loss(x2, final_norm_g.reshape(1, d), target, "final_loss")

    dy2 = _matmul(dx2, o_wout, "nt", F32, "o_out_bwd_x")
    d_o_wout = _matmul(y2, dx2, "tn", F32, "o_out_bwd_w")
    dp2, d_ln_g, d_ln_b, d_w_s, d_b_s3 = _sgu_bwd(p2, ln_g, ln_b, o_w_s[0], b_s3, dy2)
    dh1 = _matmul(dp2, o_win_t, "nn", F32, "o_in_bwd_x", tk=1536)
    d_o_win_t = _matmul(dp2, h1, "tn", F32, "o_in_bwd_w")
    dx1, d_g1 = _rms_bwd(x1, norm_g[1:2], dh1, dx2, "rms1_bwd")
    dy = _matmul(dx1, e_wout, "nt", F32, "e_out_bwd_x")
    d_e_wout = _matmul(y, dx1, "tn", F32, "e_out_bwd_w")
    dq, dk, dv_sb, dg_sb = _sb_bwd(p, bl, seq, sb_cols, n_sb, dy, n_sb, sb_o, sb_tot)
    dys, dr1, dkx1, dv1, dg_rwkv, d_gn_g, d_gn_b, d_r_k = _rwkv_post_bwd(
        ys, r, kx, v, g_rwkv, 0, e_gn_g, e_gn_b, e_r_k, e_ind, e_ind_t, dy, 0)
    dkk_s, dw_s, db_s, dkx_s, dr_s, dv_s = _scan_bwd(*sk, sv, states, _to_scan_v(dys, bl, seq))
    dkk, dw, db, dkx2, dr2 = [_from_scan_k(a, bl, seq) for a in (dkk_s, dw_s, db_s, dkx_s, dr_s)]
    dv2 = _from_scan_v(dv_s, bl, seq)
    dp_rwkv, d_mu, d_wd, d_w0, d_wa, d_a0, d_k_k, d_k_a = _rwkv_prep_bwd(
        p, seq, (width, lora), e_shift_mu, wd, e_w0, wa, e_a0, e_k_k, e_k_a, e_ind, e_ind_t,
        dr1 + dr2, dw, dkx1 + dkx2, dv1 + dv2, dkk, db)
    dp = jnp.concatenate([dp_rwkv, dg_rwkv, dq, dk, dv_sb, dg_sb], axis=1)
    dh0 = _matmul(dp, e_win_t, "nn", F32, "e_in_bwd_x", tk=1280)
    d_e_win_t = _matmul(dp, h0, "tn", F32, "e_in_bwd_w", tm=1280)
    grad_x, d_g0 = _rms_bwd(x2d, norm_g[0:1], dh0, dx1, "rms0_bwd")

    core = lax.axis_index("c").astype(jnp.int32).reshape(1)

    def scattered(full, name):
        by_owner = full.reshape((N_DEV // 2, 2, full.shape[0] // N_DEV, full.shape[1]))
        got = _pair_swap(by_owner, "swap_" + name)
        partials = _pair_sum(by_owner, got, core, "pairsum_" + name)
        return _sum_slabs(_chip_scatter(partials, "scatter_" + name), "sum_" + name)

    grads = {
        "e_w_in": scattered(d_e_win_t, "e_w_in").T[None],
        "e_w_out": scattered(d_e_wout, "e_w_out")[None],
        "o_w_in": scattered(d_o_win_t, "o_w_in").T[None],
        "o_w_out": scattered(d_o_wout, "o_w_out")[None],
    }
    small_full = {
        "norm_g": jnp.concatenate([d_g0, d_g1], axis=0), "final_norm_g": d_final_g.reshape(-1),
        "e_shift_mu": d_mu, "e_w_decay_up": d_wd[None], "e_w0": d_w0, "e_a_up": d_wa[None], "e_a0": d_a0,
        "e_k_k": d_k_k, "e_k_a": d_k_a, "e_r_k": d_r_k, "e_gn_g": d_gn_g, "e_gn_b": d_gn_b,
        "o_ln_g": d_ln_g, "o_ln_b": d_ln_b, "o_w_s": d_w_s[None], "o_b_s": d_b_s3[:, :, 0][None],
    }
    small = [n for n in names if n not in big]
    parts = _all_gather(_pack([small_full[n] for n in small]), "gather_small_grads")
    totals = _unpack(_sum_slabs(parts, "sum_small_grads"), [small_full[n].shape for n in small])
    for n, g in zip(small, totals):
        if n in sharded_small:
            size = weights[n].shape[-1]
            g = lax.dynamic_slice_in_dim(g, me * size, size, axis=g.ndim - 1)
        grads[n] = g.reshape(weights[n].shape)

    delta, new_m, new_v = {}, {}, {}
    for n in big:
        shp = weights[n].shape
        flat = lambda a: a.reshape(shp[-2], shp[-1])
        dl, mn, vn = _adamw(flat(weights[n]), flat(grads[n]), flat(mom1[n]), flat(mom2[n]), "adamw_" + n)
        delta[n], new_m[n], new_v[n] = dl.reshape(shp), mn.reshape(shp), vn.reshape(shp)
    packed = [_pack([src[n] for n in small]) for src in (weights, grads, mom1, mom2)]
    outs = _adamw(*packed, "adamw_small")
    shapes = [weights[n].shape for n in small]
    for dst, arr in zip((delta, new_m, new_v), outs):
        for n, a in zip(small, _unpack(arr, shapes)):
            dst[n] = a

    loss = lax.psum(loss_part[0, 0], ("x", "y", "c"))
    return (loss, grad_x.reshape(bl, seq, d), *[grads[n] for n in names], *[delta[n] for n in names],
            *[new_m[n] for n in names], *[new_v[n] for n in names])
```

```python
import functools

import jax
import jax.numpy as jnp
from jax import lax
from jax.experimental import pallas as pl
from jax.experimental.pallas import tpu as pltpu

F32 = jnp.float32
BF16 = jnp.bfloat16
HIGHEST = lax.Precision.HIGHEST

N_DEV = 8
RWKV_HEAD = 64
SB_HEAD = 128
SB_BLOCK = 128
SGU_CHUNK = 128
SGU_GROUPS = 16
RMS_EPS = 1e-6
GN_EPS = 64e-5
LN_EPS = 1e-5
L2_EPS = 1e-12
ADAM_LR = 0.001
ADAM_B1 = 0.9
ADAM_B2 = 0.999
ADAM_EPS = 1e-08
ADAM_WD = 0.01
ADAM_STEP = 10

VMEM_LIMIT_V7X = 56 * 1024 * 1024
LANES = 128
SCAN_STEPS_PER_BLOCK = 8


def _params(*sem):
    return pltpu.CompilerParams(dimension_semantics=sem, vmem_limit_bytes=VMEM_LIMIT_V7X)


def _tile(n, target, mult):
    best = None
    d = mult
    while d <= min(n, target):
        if n % d == 0:
            best = d
        d += mult
    return n if best is None else best


def _remote(src, dst, send_sems, recv_sems, k, dev):
    return pltpu.make_async_remote_copy(src_ref=src, dst_ref=dst, send_sem=send_sems.at[k], recv_sem=recv_sems.at[k],
                                        device_id=dev, device_id_type=pl.DeviceIdType.MESH)


_HBM = pl.BlockSpec(memory_space=pl.ANY)


GATHER_COPIES = 7


def _gather_phases(src_ref, out_ref, send_sems, recv_sems, local_sems, n):
    x, y, c = lax.axis_index("x"), lax.axis_index("y"), lax.axis_index("c")
    me, sibling = (x, y, c), (x, y, 1 - c)
    chips = [(1 - x, y), (x, 1 - y), (1 - x, 1 - y)]

    def slot(px, py, pc):
        return out_ref.at[4 * px + 2 * py + pc]

    def copy(k, block, to, own=False):
        return _remote(src_ref if own else slot(*block), slot(*block), send_sems, recv_sems,
                       GATHER_COPIES * n + k, to)

    mine = pltpu.make_async_copy(src_ref, slot(*me), local_sems.at[n])
    first = [copy(0, me, sibling, True)] + [copy(1 + j, me, (*chip, c), True) for j, chip in enumerate(chips)]
    passed = [copy(4 + j, (*chip, c), sibling) for j, chip in enumerate(chips)]

    def start():
        mine.start()
        for cp in first:
            cp.start()

    def relay():
        for j, chip in enumerate(chips):
            copy(1 + j, (*chip, c), me).wait_recv()
            passed[j].start()

    def finish():
        copy(0, sibling, me).wait_recv()
        for j, chip in enumerate(chips):
            copy(4 + j, (*chip, 1 - c), me).wait_recv()
        for cp in first + passed:
            cp.wait_send()
        mine.wait()

    return start, relay, finish


def _gather_scratch(n):
    return [pltpu.SemaphoreType.DMA((GATHER_COPIES * n,)), pltpu.SemaphoreType.DMA((GATHER_COPIES * n,)),
            pltpu.SemaphoreType.DMA((n,))]


def _gathered_shape(src):
    return jax.ShapeDtypeStruct((N_DEV,) + tuple(src.shape), src.dtype)


def _all_gather(src, name):
    def body(src_ref, out_ref, send_sems, recv_sems, local_sems):
        for phase in _gather_phases(src_ref, out_ref, send_sems, recv_sems, local_sems, 0):
            phase()

    return pl.pallas_call(
        body, name=name, out_shape=_gathered_shape(src), in_specs=[_HBM], out_specs=_HBM,
        scratch_shapes=_gather_scratch(1),
    )(src)


def _pair_swap(full, name):
    n_chips = full.shape[0]

    def body(src_ref, out_ref, send_sems, recv_sems):
        x, y, c = lax.axis_index("x"), lax.axis_index("y"), lax.axis_index("c")
        copies = [_remote(src_ref.at[q, 1 - c], out_ref.at[q], send_sems, recv_sems, q, (x, y, 1 - c))
                  for q in range(n_chips)]
        for cp in copies:
            cp.start()
        for cp in copies:
            cp.wait_recv()
        for cp in copies:
            cp.wait_send()

    return pl.pallas_call(
        body, name=name,
        out_shape=jax.ShapeDtypeStruct((n_chips,) + tuple(full.shape[2:]), full.dtype),
        in_specs=[_HBM], out_specs=_HBM,
        scratch_shapes=[pltpu.SemaphoreType.DMA((n_chips,)), pltpu.SemaphoreType.DMA((n_chips,))],
    )(full)


def _pair_sum(full, got, core, name):
    n_chips, _, rows, cols = full.shape
    tr = _tile(rows, max(16, (1 << 19) // cols), 16)

    def body(core_ref, a_ref, b_ref, o_ref):
        o_ref[...] = (a_ref[...] + b_ref[...]).astype(BF16)

    return pl.pallas_call(
        body, name=name,
        grid_spec=pltpu.PrefetchScalarGridSpec(
            num_scalar_prefetch=1, grid=(n_chips, rows // tr),
            in_specs=[pl.BlockSpec((None, None, tr, cols), lambda q, i, s: (q, s[0], i, 0)),
                      pl.BlockSpec((None, tr, cols), lambda q, i, s: (q, i, 0))],
            out_specs=pl.BlockSpec((None, tr, cols), lambda q, i, s: (q, i, 0))),
        out_shape=jax.ShapeDtypeStruct((n_chips, rows, cols), BF16),
        compiler_params=_params("parallel", "parallel"),
    )(core, full, got)


def _chip_scatter(parts, name):
    def body(src_ref, out_ref, send_sems, recv_sems, local_sem):
        x, y, c = lax.axis_index("x"), lax.axis_index("y"), lax.axis_index("c")
        here = 2 * x + y
        local = pltpu.make_async_copy(src_ref.at[here], out_ref.at[here], local_sem)
        local.start()
        chips = [(1 - x, y), (x, 1 - y), (1 - x, 1 - y)]
        sends = [_remote(src_ref.at[2 * px + py], out_ref.at[here], send_sems, recv_sems, j, (px, py, c))
                 for j, (px, py) in enumerate(chips)]
        recvs = [_remote(src_ref.at[2 * px + py], out_ref.at[2 * px + py], send_sems, recv_sems, j, (px, py, c))
                 for j, (px, py) in enumerate(chips)]
        for cp in sends:
            cp.start()
        for cp in recvs:
            cp.wait_recv()
        for cp in sends:
            cp.wait_send()
        local.wait()

    return pl.pallas_call(
        body, name=name,
        out_shape=jax.ShapeDtypeStruct(parts.shape, parts.dtype),
        in_specs=[_HBM], out_specs=_HBM,
        scratch_shapes=[pltpu.SemaphoreType.DMA((3,)), pltpu.SemaphoreType.DMA((3,)), pltpu.SemaphoreType.DMA(())],
    )(parts)


def _matmul(a, b, mode, out_dtype, name, res=None, tm=1024, tn=1024, tk=1024):
    if mode == "nn":
        (m, k), (k2, n) = a.shape, b.shape
    elif mode == "nt":
        (m, k), (n, k2) = a.shape, b.shape
    else:
        (k, m), (k2, n) = a.shape, b.shape
    assert k == k2, (a.shape, b.shape, mode)
    tm, tn, tk = _tile(m, tm, 128), _tile(n, tn, 128), _tile(k, tk, 128)
    nk = k // tk
    if mode == "nn":
        a_spec = pl.BlockSpec((tm, tk), lambda i, j, kk: (i, kk))
        b_spec = pl.BlockSpec((tk, tn), lambda i, j, kk: (kk, j))
        dims = (((1,), (0,)), ((), ()))
    elif mode == "nt":
        a_spec = pl.BlockSpec((tm, tk), lambda i, j, kk: (i, kk))
        b_spec = pl.BlockSpec((tn, tk), lambda i, j, kk: (j, kk))
        dims = (((1,), (1,)), ((), ()))
    else:
        a_spec = pl.BlockSpec((tk, tm), lambda i, j, kk: (kk, i))
        b_spec = pl.BlockSpec((tk, tn), lambda i, j, kk: (kk, j))
        dims = (((0,), (0,)), ((), ()))
    o_spec = pl.BlockSpec((tm, tn), lambda i, j, kk: (i, j))
    has_res = res is not None

    def body(*refs):
        if has_res:
            a_ref, b_ref, r_ref, o_ref, acc_ref = refs
        else:
            a_ref, b_ref, o_ref, acc_ref = refs
        kk = pl.program_id(2)

        @pl.when(kk == 0)
        def _():
            acc_ref[...] = jnp.zeros_like(acc_ref)

        acc_ref[...] += lax.dot_general(a_ref[...].astype(BF16), b_ref[...].astype(BF16), dims,
                                        preferred_element_type=F32)

        @pl.when(kk == nk - 1)
        def _():
            out = acc_ref[...]
            if has_res:
                out = out + r_ref[...]
            o_ref[...] = out.astype(out_dtype)

    ins = [a, b] + ([res] if has_res else [])
    specs = [a_spec, b_spec] + ([o_spec] if has_res else [])
    return pl.pallas_call(
        body, name=name, grid=(m // tm, n // tn, nk),
        out_shape=jax.ShapeDtypeStruct((m, n), out_dtype),
        in_specs=specs, out_specs=o_spec,
        scratch_shapes=[pltpu.VMEM((tm, tn), F32)],
        compiler_params=_params("parallel", "parallel", "arbitrary"),
    )(*ins)


def _rms_fwd(x, g, name):
    t, d = x.shape
    tm = _tile(t, 256, 8)

    def body(x_ref, g_ref, h_ref):
        xv = x_ref[...]
        rstd = lax.rsqrt(jnp.mean(xv * xv, axis=-1, keepdims=True) + RMS_EPS)
        h_ref[...] = (xv * rstd * g_ref[...]).astype(BF16)

    return pl.pallas_call(
        body, name=name, grid=(t // tm,),
        out_shape=jax.ShapeDtypeStruct((t, d), BF16),
        in_specs=[pl.BlockSpec((tm, d), lambda i: (i, 0)), pl.BlockSpec((1, d), lambda i: (0, 0))],
        out_specs=pl.BlockSpec((tm, d), lambda i: (i, 0)),
        compiler_params=_params("parallel"),
    )(x, g)


def _rms_bwd(x, g, dh, dres, name):
    t, d = x.shape
    tm = _tile(t, 256, 8)

    def body(x_ref, g_ref, dh_ref, dres_ref, dx_ref, dg_ref):
        @pl.when(pl.program_id(0) == 0)
        def _():
            dg_ref[...] = jnp.zeros_like(dg_ref)

        xv = x_ref[...]
        rstd = lax.rsqrt(jnp.mean(xv * xv, axis=-1, keepdims=True) + RMS_EPS)
        xhat = xv * rstd
        dh_v = dh_ref[...]
        dg_ref[...] += jnp.sum(dh_v * xhat, axis=0, keepdims=True)
        dxh = dh_v * g_ref[...]
        dx_ref[...] = dres_ref[...] + rstd * (dxh - xhat * jnp.mean(dxh * xhat, axis=-1, keepdims=True))

    row = pl.BlockSpec((tm, d), lambda i: (i, 0))
    vec = pl.BlockSpec((1, d), lambda i: (0, 0))
    return pl.pallas_call(
        body, name=name, grid=(t // tm,),
        out_shape=(jax.ShapeDtypeStruct((t, d), F32), jax.ShapeDtypeStruct((1, d), F32)),
        in_specs=[row, vec, row, row], out_specs=(row, vec),
        compiler_params=_params("arbitrary"),
    )(x, g, dh, dres)


def _final_loss(x, g, target, name):
    t, d = x.shape
    tm = _tile(t, 256, 8)

    def body(x_ref, g_ref, t_ref, dx_ref, dg_ref, loss_ref):
        @pl.when(pl.program_id(0) == 0)
        def _():
            dg_ref[...] = jnp.zeros_like(dg_ref)
            loss_ref[...] = jnp.zeros_like(loss_ref)

        xv = x_ref[...]
        rstd = lax.rsqrt(jnp.mean(xv * xv, axis=-1, keepdims=True) + RMS_EPS)
        xhat = xv * rstd
        gv = g_ref[...]
        err = xhat * gv - t_ref[...]
        loss_ref[...] += 0.5 * jnp.sum(jnp.mean(err * err, axis=-1, keepdims=True), axis=0, keepdims=True)
        dout = err * (1.0 / d)
        dg_ref[...] += jnp.sum(dout * xhat, axis=0, keepdims=True)
        dxh = dout * gv
        dx_ref[...] = rstd * (dxh - xhat * jnp.mean(dxh * xhat, axis=-1, keepdims=True))

    row = pl.BlockSpec((tm, d), lambda i: (i, 0))
    vec = pl.BlockSpec((1, d), lambda i: (0, 0))
    return pl.pallas_call(
        body, name=name, grid=(t // tm,),
        out_shape=(jax.ShapeDtypeStruct((t, d), F32), jax.ShapeDtypeStruct((1, d), F32),
                   jax.ShapeDtypeStruct((1, 1), F32)),
        in_specs=[row, vec, row], out_specs=(row, vec, pl.BlockSpec((1, 1), lambda i: (0, 0))),
        compiler_params=_params("arbitrary"),
    )(x, g, target)


def _dot32(a, b):
    return jnp.dot(a, b, precision=HIGHEST, preferred_element_type=F32)


def _head_sum(v, e, et):
    return _dot32(_dot32(v, e), et)


def _log_sigmoid(z):
    return jnp.minimum(z, 0.0) - jnp.log1p(jnp.exp(-jnp.abs(z)))


def _silu(g):
    return g * jax.nn.sigmoid(g)


def _prep_math(k, wlo, alo, wd, w0, wa, a0, k_k, k_a, e, et):
    wl = w0 + _dot32(jnp.tanh(wlo), wd)
    w_log = _log_sigmoid(wl) - 0.5
    w = jnp.exp(-jnp.exp(w_log))
    a = jax.nn.sigmoid(a0 + _dot32(alo, wa))
    kk0 = k * k_k
    kk = kk0 * lax.rsqrt(jnp.maximum(_head_sum(kk0 * kk0, e, et), L2_EPS * L2_EPS))
    kx = k * (1.0 + (a - 1.0) * k_a)
    return w, kx, kk, kk * a


def _post_math(ys, r, kx, v, g, gn_g, gn_b, r_k, e, et):
    inv = 1.0 / RWKV_HEAD
    mu = _head_sum(ys, e, et) * inv
    dlt = ys - mu
    var = _head_sum(dlt * dlt, e, et) * inv
    y = dlt * lax.rsqrt(var + GN_EPS) * gn_g + gn_b
    bonus = _head_sum(r * kx * r_k, e, et) * v
    return (y + bonus) * _silu(g)


def _shifted(p, prev_row, first):
    rows = lax.broadcasted_iota(jnp.int32, p.shape, 0)
    prev = jnp.where(first, 0.0, prev_row)
    return jnp.where(rows == 0, prev, pltpu.roll(p, 1, 0))


def _head_indicator(width):
    ch = lax.broadcasted_iota(jnp.int32, (width, width // RWKV_HEAD), 0) // RWKV_HEAD
    hd = lax.broadcasted_iota(jnp.int32, (width, width // RWKV_HEAD), 1)
    e = (ch == hd).astype(F32)
    return e, e.T


def _rwkv_prep(p, seq, dims, mu, wd, w0, wa, a0, k_k, k_a, e, et):
    t = p.shape[0]
    width, lora = dims
    cols = 3 * width + 2 * lora
    tm = 128
    per_seq = seq // tm

    def body(p_ref, prev_ref, mu_ref, wd_ref, w0_ref, wa_ref, a0_ref, kk_ref, ka_ref, e_ref, et_ref,
             r_out, w_out, kx_out, v_out, kkn_out, b_out):
        i = pl.program_id(0)
        pv = p_ref[...]
        psh = _shifted(pv, prev_ref[7:8, :], i % per_seq == 0)
        ps = pv + mu_ref[...] * (psh - pv)
        r, k, v = ps[:, :width], ps[:, width:2 * width], ps[:, 2 * width:3 * width]
        wlo, alo = ps[:, 3 * width:3 * width + lora], ps[:, 3 * width + lora:]
        w, kx, kk, b = _prep_math(k, wlo, alo, wd_ref[...], w0_ref[...], wa_ref[...], a0_ref[...],
                                  kk_ref[...], ka_ref[...], e_ref[...], et_ref[...])
        r_out[...] = r
        w_out[...] = w
        kx_out[...] = kx
        v_out[...] = v
        kkn_out[...] = kk
        b_out[...] = b

    full = lambda a: pl.BlockSpec(a.shape, lambda i: (0,) * a.ndim)
    out = pl.BlockSpec((tm, width), lambda i: (i, 0))
    return pl.pallas_call(
        body, name="rwkv_prep", grid=(t // tm,),
        out_shape=tuple(jax.ShapeDtypeStruct((t, width), F32) for _ in range(6)),
        in_specs=[pl.BlockSpec((tm, cols), lambda i: (i, 0)),
                  pl.BlockSpec((8, cols), lambda i: (jnp.maximum(i * (tm // 8) - 1, 0), 0)),
                  full(mu), full(wd), full(w0), full(wa), full(a0), full(k_k), full(k_a), full(e), full(et)],
        out_specs=tuple(out for _ in range(6)),
        compiler_params=_params("parallel"),
    )(p, p, mu, wd, w0, wa, a0, k_k, k_a, e, et)


def _rwkv_prep_bwd(p, seq, dims, mu, wd, w0, wa, a0, k_k, k_a, e, et, dr, dw, dkx, dv, dkk, db):
    t = p.shape[0]
    width, lora = dims
    cols = 3 * width + 2 * lora
    tm = 128
    n_tiles = t // tm
    per_seq = seq // tm

    def body(p_ref, prev_ref, mu_ref, wd_ref, w0_ref, wa_ref, a0_ref, kk_ref, ka_ref, e_ref, et_ref,
             dr_ref, dw_ref, dkx_ref, dv_ref, dkk_ref, db_ref,
             dp_out, dmu_out, dwd_out, dw0_out, dwa_out, da0_out, dkk_out, dka_out, carry):
        step = pl.program_id(0)
        i = n_tiles - 1 - step

        @pl.when(step == 0)
        def _():
            for ref in (dmu_out, dwd_out, dw0_out, dwa_out, da0_out, dkk_out, dka_out, carry):
                ref[...] = jnp.zeros_like(ref)

        pv = p_ref[...]
        first = i % per_seq == 0
        psh = _shifted(pv, prev_ref[7:8, :], first)
        muv = mu_ref[...]
        ps = pv + muv * (psh - pv)
        k = ps[:, width:2 * width]
        wlo, alo = ps[:, 3 * width:3 * width + lora], ps[:, 3 * width + lora:]
        ev, etv = e_ref[...], et_ref[...]
        _, vjp = jax.vjp(lambda *a: _prep_math(*a, ev, etv), k, wlo, alo, wd_ref[...], w0_ref[...],
                         wa_ref[...], a0_ref[...], kk_ref[...], ka_ref[...])
        dk, dwlo, dalo, dwd, dw0, dwa, da0, dk_k, dk_a = vjp(
            (dw_ref[...], dkx_ref[...], dkk_ref[...], db_ref[...]))
        dps = jnp.concatenate([dr_ref[...], dk, dv_ref[...], dwlo, dalo], axis=1)
        dmu_out[...] += jnp.sum(dps * (psh - pv), axis=0, keepdims=True)
        dwd_out[...] += dwd
        dw0_out[...] += dw0
        dwa_out[...] += dwa
        da0_out[...] += da0
        dkk_out[...] += dk_k
        dka_out[...] += dk_a
        dsh = dps * muv
        rows = lax.broadcasted_iota(jnp.int32, dsh.shape, 0)
        nxt = jnp.where(rows == tm - 1, carry[...], pltpu.roll(dsh, tm - 1, 0))
        dp_out[...] = (dps * (1.0 - muv) + nxt).astype(BF16)
        carry[...] = jnp.where(first, 0.0, dsh[0:1, :])

    full = lambda a: pl.BlockSpec(a.shape, lambda s: (0,) * a.ndim)
    tok = pl.BlockSpec((tm, width), lambda s: (n_tiles - 1 - s, 0))
    vec = lambda n: jax.ShapeDtypeStruct((1, n), F32)
    outs = (jax.ShapeDtypeStruct((t, cols), BF16), vec(cols), jax.ShapeDtypeStruct(wd.shape, F32), vec(width),
            jax.ShapeDtypeStruct(wa.shape, F32), vec(width), vec(width), vec(width))
    return pl.pallas_call(
        body, name="rwkv_prep_bwd", grid=(n_tiles,),
        out_shape=outs,
        in_specs=[pl.BlockSpec((tm, cols), lambda s: (n_tiles - 1 - s, 0)),
                  pl.BlockSpec((8, cols), lambda s: (jnp.maximum((n_tiles - 1 - s) * (tm // 8) - 1, 0), 0)),
                  full(mu), full(wd), full(w0), full(wa), full(a0), full(k_k), full(k_a), full(e), full(et),
                  tok, tok, tok, tok, tok, tok],
        out_specs=(pl.BlockSpec((tm, cols), lambda s: (n_tiles - 1 - s, 0)),) + tuple(
            pl.BlockSpec(o.shape, lambda s: (0, 0)) for o in outs[1:]),
        scratch_shapes=[pltpu.VMEM((1, cols), F32)],
        compiler_params=_params("arbitrary"),
    )(p, p, mu, wd, w0, wa, a0, k_k, k_a, e, et, dr, dw, dkx, dv, dkk, db)


def _rwkv_post(ys, r, kx, v, p, g_col, gn_g, gn_b, r_k, e, et):
    t, width = ys.shape
    tm = 256

    def body(ys_ref, r_ref, kx_ref, v_ref, g_ref, gg_ref, gb_ref, rk_ref, e_ref, et_ref, out_ref):
        out_ref[...] = _post_math(ys_ref[...], r_ref[...], kx_ref[...], v_ref[...], g_ref[...], gg_ref[...],
                                  gb_ref[...], rk_ref[...], e_ref[...], et_ref[...]).astype(BF16)

    tok = pl.BlockSpec((tm, width), lambda i: (i, 0))
    full = lambda a: pl.BlockSpec(a.shape, lambda i: (0,) * a.ndim)
    return pl.pallas_call(
        body, name="rwkv_post", grid=(t // tm,),
        out_shape=jax.ShapeDtypeStruct((t, width), BF16),
        in_specs=[tok, tok, tok, tok, pl.BlockSpec((tm, width), lambda i: (i, g_col)),
                  full(gn_g), full(gn_b), full(r_k), full(e), full(et)],
        out_specs=tok,
        compiler_params=_params("parallel"),
    )(ys, r, kx, v, p, gn_g, gn_b, r_k, e, et)


def _rwkv_post_bwd(ys, r, kx, v, p, g_col, gn_g, gn_b, r_k, e, et, dy, dy_col):
    t, width = ys.shape
    tm = 128

    def body(ys_ref, r_ref, kx_ref, v_ref, g_ref, gg_ref, gb_ref, rk_ref, e_ref, et_ref, dy_ref,
             dys_out, dr_out, dkx_out, dv_out, dg_out, dgg_out, dgb_out, drk_out):
        @pl.when(pl.program_id(0) == 0)
        def _():
            for ref in (dgg_out, dgb_out, drk_out):
                ref[...] = jnp.zeros_like(ref)

        ev, etv = e_ref[...], et_ref[...]
        _, vjp = jax.vjp(lambda *a: _post_math(*a, ev, etv), ys_ref[...], r_ref[...], kx_ref[...], v_ref[...],
                         g_ref[...], gg_ref[...], gb_ref[...], rk_ref[...])
        dys, dr, dkx, dv, dg, dgg, dgb, drk = vjp(dy_ref[...])
        dys_out[...] = dys
        dr_out[...] = dr
        dkx_out[...] = dkx
        dv_out[...] = dv
        dg_out[...] = dg.astype(BF16)
        dgg_out[...] += dgg
        dgb_out[...] += dgb
        drk_out[...] += drk

    tok = pl.BlockSpec((tm, width), lambda i: (i, 0))
    full = lambda a: pl.BlockSpec(a.shape, lambda i: (0,) * a.ndim)
    big = jax.ShapeDtypeStruct((t, width), F32)
    vec = jax.ShapeDtypeStruct((1, width), F32)
    vspec = pl.BlockSpec((1, width), lambda i: (0, 0))
    return pl.pallas_call(
        body, name="rwkv_post_bwd", grid=(t // tm,),
        out_shape=(big, big, big, big, jax.ShapeDtypeStruct((t, width), BF16), vec, vec, vec),
        in_specs=[tok, tok, tok, tok, pl.BlockSpec((tm, width), lambda i: (i, g_col)),
                  full(gn_g), full(gn_b), full(r_k), full(e), full(et),
                  pl.BlockSpec((tm, width), lambda i: (i, dy_col))],
        out_specs=(tok, tok, tok, tok, tok, vspec, vspec, vspec),
        compiler_params=_params("arbitrary"),
    )(ys, r, kx, v, p, gn_g, gn_b, r_k, e, et, dy)


def _to_scan_k(a, bl, seq):
    h = a.shape[1] // RWKV_HEAD
    a = a.reshape(bl, seq, h, RWKV_HEAD).transpose(1, 3, 0, 2).reshape(seq, RWKV_HEAD, bl * h)
    return jnp.concatenate([a, a], axis=-1)


def _to_scan_v(a, bl, seq):
    h = a.shape[1] // RWKV_HEAD
    half = RWKV_HEAD // 2
    return a.reshape(bl, seq, h, 2, half).transpose(1, 4, 3, 0, 2).reshape(seq, half, 2 * bl * h)


def _from_scan_k(a, bl, seq):
    h = a.shape[2] // (2 * bl)
    a = a[:, :, :bl * h].reshape(seq, RWKV_HEAD, bl, h).transpose(2, 0, 3, 1)
    return a.reshape(bl * seq, h * RWKV_HEAD)


def _from_scan_v(a, bl, seq):
    half = RWKV_HEAD // 2
    h = a.shape[2] // (2 * bl)
    a = a.reshape(seq, half, 2, bl, h).transpose(3, 0, 4, 2, 1)
    return a.reshape(bl * seq, h * RWKV_HEAD)


def _scan_fwd(kk, w, b, kx, r, v, gathers=()):
    seq, nk, lanes = kk.shape
    nv = v.shape[1]
    tt = SCAN_STEPS_PER_BLOCK
    nblk = seq // tt
    ng = len(gathers)

    def body(*refs):
        kk_ref, w_ref, b_ref, kx_ref, r_ref, v_ref = refs[:6]
        g_src = refs[6:6 + ng]
        y_ref, st_ref, sa_ref = refs[6 + ng:9 + ng]
        g_out = refs[9 + ng:9 + 2 * ng]
        s_ref = refs[9 + 2 * ng]
        pid = pl.program_id(0)

        @pl.when(pid == 0)
        def _():
            s_ref[...] = jnp.zeros_like(s_ref)

        if ng:
            send_sems, recv_sems, local_sems = refs[10 + 2 * ng:]
            phases = [_gather_phases(g_src[n], g_out[n], send_sems, recv_sems, local_sems, n) for n in range(ng)]
            for k, at in enumerate((0, (3 * nblk) // 4, nblk - 1)):
                @pl.when(pid == at)
                def _(k=k):
                    for ph in phases:
                        ph[k]()

        def step(i, carry):
            kkv, wv, bv, kxv, rv = kk_ref[i], w_ref[i], b_ref[i], kx_ref[i], r_ref[i]
            for j in range(nv):
                s_old = s_ref[j]
                sa = -jnp.sum(s_old * kkv, axis=0, keepdims=True)
                s_new = s_old * wv + sa * bv + v_ref[i, j:j + 1, :] * kxv
                s_ref[j] = s_new
                st_ref[i, j] = s_new
                sa_ref[i, j:j + 1, :] = sa
                y_ref[i, j:j + 1, :] = jnp.sum(s_new * rv, axis=0, keepdims=True)
            return carry

        lax.fori_loop(0, tt, step, 0)

    krow = pl.BlockSpec((tt, nk, lanes), lambda i: (i, 0, 0))
    vrow = pl.BlockSpec((tt, nv, lanes), lambda i: (i, 0, 0))
    vshape = jax.ShapeDtypeStruct((seq, nv, lanes), F32)
    return pl.pallas_call(
        body, name="rwkv_scan_fwd", grid=(nblk,),
        out_shape=(vshape, jax.ShapeDtypeStruct((seq, nv, nk, lanes), F32), vshape,
                   *[_gathered_shape(g) for g in gathers]),
        in_specs=[krow, krow, krow, krow, krow, vrow] + [_HBM] * ng,
        out_specs=(vrow, pl.BlockSpec((tt, nv, nk, lanes), lambda i: (i, 0, 0, 0)), vrow, *[_HBM] * ng),
        scratch_shapes=[pltpu.VMEM((nv, nk, lanes), F32)] + (_gather_scratch(ng) if ng else []),
        compiler_params=_params("arbitrary"),
    )(kk, w, b, kx, r, v, *gathers)


def _scan_bwd(kk, w, b, kx, r, v, states, sa_all, dy):
    seq, nk, lanes = kk.shape
    nv = v.shape[1]
    tt = SCAN_STEPS_PER_BLOCK
    nblk = seq // tt

    def both_halves(a):
        return a + pltpu.roll(a, lanes // 2, 1)

    def body(kk_ref, w_ref, b_ref, kx_ref, r_ref, v_ref, st_ref, before_ref, sa_ref, dy_ref,
             dkk_ref, dw_ref, db_ref, dkx_ref, dr_ref, dv_ref, g_ref):
        @pl.when(pl.program_id(0) == 0)
        def _():
            g_ref[...] = jnp.zeros_like(g_ref)

        def one_step(i, state_before):
            kkv, wv, bv, kxv, rv = kk_ref[i], w_ref[i], b_ref[i], kx_ref[i], r_ref[i]
            zero = jnp.zeros((nk, lanes), F32)
            a_r, a_w, a_b, a_kx, a_kk = zero, zero, zero, zero, zero
            for j in range(nv):
                s_old = state_before(j)
                s_new = st_ref[i, j]
                vv = v_ref[i, j:j + 1, :]
                dyv = dy_ref[i, j:j + 1, :]
                sa = sa_ref[i, j:j + 1, :]
                g = g_ref[j] + dyv * rv
                a_r = a_r + s_new * dyv
                a_w = a_w + g * s_old
                dsa = jnp.sum(g * bv, axis=0, keepdims=True)
                a_b = a_b + g * sa
                dv_ref[i, j:j + 1, :] = jnp.sum(g * kxv, axis=0, keepdims=True)
                a_kx = a_kx + g * vv
                a_kk = a_kk + s_old * dsa
                g_ref[j] = g * wv - dsa * kkv
            dr_ref[i] = both_halves(a_r)
            dw_ref[i] = both_halves(a_w)
            db_ref[i] = both_halves(a_b)
            dkx_ref[i] = both_halves(a_kx)
            dkk_ref[i] = -both_halves(a_kk)

        def step(n, carry):
            i = tt - 1 - n
            one_step(i, lambda j: st_ref[i - 1, j])
            return carry

        lax.fori_loop(0, tt - 1, step, 0)
        at_start = pl.program_id(0) == nblk - 1
        one_step(0, lambda j: jnp.where(at_start, 0.0, before_ref[0, j]))

    rev = lambda i: nblk - 1 - i
    krow = pl.BlockSpec((tt, nk, lanes), lambda i: (rev(i), 0, 0))
    vrow = pl.BlockSpec((tt, nv, lanes), lambda i: (rev(i), 0, 0))
    kshape = jax.ShapeDtypeStruct((seq, nk, lanes), F32)
    return pl.pallas_call(
        body, name="rwkv_scan_bwd", grid=(nblk,),
        out_shape=(kshape, kshape, kshape, kshape, kshape, jax.ShapeDtypeStruct((seq, nv, lanes), F32)),
        in_specs=[krow, krow, krow, krow, krow, vrow,
                  pl.BlockSpec((tt, nv, nk, lanes), lambda i: (rev(i), 0, 0, 0)),
                  pl.BlockSpec((1, nv, nk, lanes), lambda i: (jnp.maximum(rev(i) * tt - 1, 0), 0, 0, 0)),
                  vrow, vrow],
        out_specs=(krow, krow, krow, krow, krow, vrow),
        scratch_shapes=[pltpu.VMEM((nv, nk, lanes), F32)],
        compiler_params=_params("arbitrary"),
    )(kk, w, b, kx, r, v, states, states, sa_all, dy)


_NT = (((1,), (1,)), ((), ()))
_TN = (((0,), (0,)), ((), ()))
SB_SCALE = 1.0 / (SB_HEAD ** 0.5)


def _split_dot(a, b):
    hi = a.astype(BF16)
    lo = (a - hi.astype(F32)).astype(BF16)
    return jnp.dot(hi, b, preferred_element_type=F32) + jnp.dot(lo, b, preferred_element_type=F32)


def _sb_masks():
    blk = SB_BLOCK
    row = lax.broadcasted_iota(jnp.int32, (blk, blk), 0)
    col = lax.broadcasted_iota(jnp.int32, (blk, blk), 1)
    ones = jnp.ones((blk, blk), BF16)
    fwd = jnp.concatenate([(row > col).astype(BF16), ones], axis=1)
    bwd = jnp.concatenate([(col > row).astype(BF16), ones], axis=1)
    return row, col, fwd, bwd


def _sb_block(qb, kb, j, qi, row, col, mix):
    z = lax.dot_general(qb, kb, _NT, preferred_element_type=F32) * SB_SCALE
    causal = (j * SB_BLOCK + col) < (qi * SB_BLOCK + row)
    lsz = _log_sigmoid(z)
    keep = jnp.where(causal, lsz - z, 0.0)
    return causal, lsz, z, _split_dot(keep, mix)


SB_HEADS_PER_STEP = 2


def _sb_chains(bl):
    return [(b, slice(hh * SB_HEAD, (hh + 1) * SB_HEAD)) for b in range(bl) for hh in range(SB_HEADS_PER_STEP)]


def _sb_specs(bl, seq, cols, n_heads):
    hp = SB_HEADS_PER_STEP
    assert n_heads % hp == 0 and all(c % hp == 0 for c in cols)
    qspec = lambda col: pl.BlockSpec((bl, SB_BLOCK, hp * SB_HEAD), lambda h, i: (0, i, col // hp + h))
    kspec = lambda col: pl.BlockSpec((bl, seq, hp * SB_HEAD), lambda h, i: (0, 0, col // hp + h))
    return qspec, kspec


def _sb_fwd(p, bl, seq, cols, n_heads):
    t = p.shape[0]
    nq = seq // SB_BLOCK
    q_col, k_col, v_col, g_col = cols
    blk = SB_BLOCK
    chains = _sb_chains(bl)

    def body(q_ref, k_ref, v_ref, g_ref, out_ref, o_ref, tot_ref):
        qi = pl.program_id(1)
        row, col, mix, _ = _sb_masks()
        qbs = [q_ref[b, :, ln].astype(BF16) for b, ln in chains]

        def step(n, carry):
            j = qi - n
            rows = pl.ds(pl.multiple_of(j * blk, blk), blk)
            new = []
            for (b, ln), qb, (out, later) in zip(chains, qbs, carry):
                causal, lsz, _, both = _sb_block(qb, k_ref[b, rows, ln].astype(BF16), j, qi, row, col, mix)
                att = jnp.where(causal, jnp.exp(lsz + both[:, :blk] + later), 0.0)
                out = out + jnp.dot(att.astype(BF16), v_ref[b, rows, ln].astype(BF16), preferred_element_type=F32)
                new.append((out, later + both[:, blk:]))
            return tuple(new)

        zero = jnp.zeros((blk, SB_HEAD), F32)
        done = lax.fori_loop(0, qi + 1, step, tuple((zero, zero) for _ in chains))
        for (b, ln), (out, total) in zip(chains, done):
            o_ref[b, :, ln] = out
            tot_ref[b, :, ln] = total
            out_ref[b, :, ln] = (out * _silu(g_ref[b, :, ln])).astype(BF16)

    qspec, kspec = _sb_specs(bl, seq, cols, n_heads)
    width = n_heads * SB_HEAD
    p3 = p.reshape(bl, seq, p.shape[1])
    f32 = jax.ShapeDtypeStruct((bl, seq, width), F32)
    outs = pl.pallas_call(
        body, name="sb_attn_fwd", grid=(n_heads // SB_HEADS_PER_STEP, nq),
        out_shape=(jax.ShapeDtypeStruct((bl, seq, width), BF16), f32, f32),
        in_specs=[qspec(q_col), kspec(k_col), kspec(v_col), qspec(g_col)],
        out_specs=(qspec(0), qspec(0), qspec(0)),
        compiler_params=_params("parallel", "arbitrary"),
    )(p3, p3, p3, p3)
    return tuple(a.reshape(t, width) for a in outs)


def _sb_bwd(p, bl, seq, cols, n_heads, dy, dy_col, o, tot):
    t = p.shape[0]
    nq = seq // SB_BLOCK
    q_col, k_col, v_col, g_col = cols
    blk = SB_BLOCK

    chains = _sb_chains(bl)

    def body(q_ref, k_ref, v_ref, g_ref, dy_ref, o_ref, tot_ref, dq_out, dk_out, dv_out, dg_out, dk_acc, dv_acc):
        qi = pl.program_id(1)

        @pl.when(qi == 0)
        def _():
            dk_acc[...] = jnp.zeros_like(dk_acc)
            dv_acc[...] = jnp.zeros_like(dv_acc)

        row, col, mix, mix_t = _sb_masks()
        qbs, dobs, totals = [], [], []
        for b, ln in chains:
            gate = g_ref[b, :, ln]
            sg = jax.nn.sigmoid(gate)
            dyv = dy_ref[b, :, ln]
            dg_out[b, :, ln] = (dyv * o_ref[b, :, ln] * (sg * (1.0 + gate * (1.0 - sg)))).astype(BF16)
            dobs.append((dyv * (gate * sg)).astype(BF16))
            qbs.append(q_ref[b, :, ln].astype(BF16))
            totals.append(tot_ref[b, :, ln])

        def step(j, carry):
            rows = pl.ds(pl.multiple_of(j * blk, blk), blk)
            new = []
            for (b, ln), qb, dob, total, (dq, seen, dl_before) in zip(chains, qbs, dobs, totals, carry):
                kb = k_ref[b, rows, ln].astype(BF16)
                causal, lsz, z, both = _sb_block(qb, kb, j, qi, row, col, mix)
                seen = seen + both[:, blk:]
                att = jnp.where(causal, jnp.exp(lsz + both[:, :blk] + (total - seen)), 0.0)
                datt = lax.dot_general(dob, v_ref[b, rows, ln].astype(BF16), _NT, preferred_element_type=F32)
                dl = att * datt
                dv_acc[b, rows, ln] += lax.dot_general(att.astype(BF16), dob, _TN, preferred_element_type=F32)
                both_t = _split_dot(dl, mix_t)
                dkeep = jnp.where(causal, both_t[:, :blk] + dl_before, 0.0)
                dzb = (((dl + dkeep) * jax.nn.sigmoid(-z) - dkeep) * SB_SCALE).astype(BF16)
                dq = dq + jnp.dot(dzb, kb, preferred_element_type=F32)
                dk_acc[b, rows, ln] += lax.dot_general(dzb, qb, _TN, preferred_element_type=F32)
                new.append((dq, seen, dl_before + both_t[:, blk:]))
            return tuple(new)

        zero = jnp.zeros((blk, SB_HEAD), F32)
        done = lax.fori_loop(0, qi + 1, step, tuple((zero, zero, zero) for _ in chains))
        for (b, ln), (dq, _, _) in zip(chains, done):
            dq_out[b, :, ln] = dq.astype(BF16)

        @pl.when(qi == nq - 1)
        def _():
            dk_out[...] = dk_acc[...].astype(BF16)
            dv_out[...] = dv_acc[...].astype(BF16)

    qspec, kspec = _sb_specs(bl, seq, cols + (dy_col,), n_heads)
    width = n_heads * SB_HEAD
    shape = jax.ShapeDtypeStruct((bl, seq, width), BF16)
    as3 = lambda a: a.reshape(bl, seq, a.shape[1])
    p3 = as3(p)
    acc = pltpu.VMEM((bl, seq, SB_HEADS_PER_STEP * SB_HEAD), F32)
    outs = pl.pallas_call(
        body, name="sb_attn_bwd", grid=(n_heads // SB_HEADS_PER_STEP, nq),
        out_shape=(shape, shape, shape, shape),
        in_specs=[qspec(q_col), kspec(k_col), kspec(v_col), qspec(g_col), qspec(dy_col), qspec(0), qspec(0)],
        out_specs=(qspec(0), kspec(0), kspec(0), qspec(0)),
        scratch_shapes=[acc, acc],
        compiler_params=_params("parallel", "arbitrary"),
    )(p3, p3, p3, p3, as3(dy), as3(o), as3(tot))
    return tuple(a.reshape(t, width) for a in outs)


def _gelu(x):
    return 0.5 * x * (1.0 + lax.erf(x * (2.0 ** -0.5)))


def _sgu_math(us, vs, gs, ln_g, ln_b, ws, bs):
    width = sum(v.shape[1] for v in vs)
    vg = [_gelu(v) for v in vs]
    mu = sum(jnp.sum(v, axis=1, keepdims=True) for v in vg) * (1.0 / width)
    dl = [v - mu for v in vg]
    var = sum(jnp.sum(d * d, axis=1, keepdims=True) for d in dl) * (1.0 / width)
    rstd = lax.rsqrt(var + LN_EPS)
    n = ws[0].shape[0]
    tri = lax.broadcasted_iota(jnp.int32, (n, n), 0) >= lax.broadcasted_iota(jnp.int32, (n, n), 1)
    outs = []
    for i in range(len(vs)):
        vn = dl[i] * rstd * ln_g[i] + ln_b[i]
        mixed = _dot32(jnp.where(tri, ws[i], 0.0), vn) + bs[i]
        outs.append(_gelu(us[i]) * mixed * _silu(gs[i]))
    return outs


def _sgu_load(p_ref, lng_ref, lnb_ref, ws_ref, bs_ref, width):
    gd = width // SGU_GROUPS
    grp = lambda ref, base, i: ref[:, base + i * gd:base + (i + 1) * gd]
    idx = range(SGU_GROUPS)
    return ([grp(p_ref, 0, i) for i in idx], [grp(p_ref, width, i) for i in idx],
            [grp(p_ref, 2 * width, i) for i in idx], [grp(lng_ref, 0, i) for i in idx],
            [grp(lnb_ref, 0, i) for i in idx], [ws_ref[i] for i in idx], [bs_ref[i] for i in idx])


def _sgu_fwd(p, ln_g, ln_b, w_s, b_s):
    t = p.shape[0]
    width = p.shape[1] // 3
    gd = width // SGU_GROUPS
    tm = SGU_CHUNK

    def body(p_ref, lng_ref, lnb_ref, ws_ref, bs_ref, y_ref):
        outs = _sgu_math(*_sgu_load(p_ref, lng_ref, lnb_ref, ws_ref, bs_ref, width))
        for i in range(SGU_GROUPS):
            y_ref[:, i * gd:(i + 1) * gd] = outs[i].astype(BF16)

    full = lambda a: pl.BlockSpec(a.shape, lambda i: (0,) * a.ndim)
    return pl.pallas_call(
        body, name="sgu_fwd", grid=(t // tm,),
        out_shape=jax.ShapeDtypeStruct((t, width), BF16),
        in_specs=[pl.BlockSpec((tm, 3 * width), lambda i: (i, 0)), full(ln_g), full(ln_b), full(w_s), full(b_s)],
        out_specs=pl.BlockSpec((tm, width), lambda i: (i, 0)),
        compiler_params=_params("parallel"),
    )(p, ln_g, ln_b, w_s, b_s)


def _sgu_bwd(p, ln_g, ln_b, w_s, b_s, dy):
    t = p.shape[0]
    width = p.shape[1] // 3
    gd = width // SGU_GROUPS
    tm = SGU_CHUNK

    def body(p_ref, lng_ref, lnb_ref, ws_ref, bs_ref, dy_ref, dp_out, dlng_out, dlnb_out, dws_out, dbs_out):
        @pl.when(pl.program_id(0) == 0)
        def _():
            for ref in (dlng_out, dlnb_out, dws_out, dbs_out):
                ref[...] = jnp.zeros_like(ref)

        _, vjp = jax.vjp(_sgu_math, *_sgu_load(p_ref, lng_ref, lnb_ref, ws_ref, bs_ref, width))
        dus, dvs, dgs, dlng, dlnb, dws, dbs = vjp(
            [dy_ref[:, i * gd:(i + 1) * gd] for i in range(SGU_GROUPS)])
        for i in range(SGU_GROUPS):
            cols = slice(i * gd, (i + 1) * gd)
            dp_out[:, i * gd:(i + 1) * gd] = dus[i].astype(BF16)
            dp_out[:, width + i * gd:width + (i + 1) * gd] = dvs[i].astype(BF16)
            dp_out[:, 2 * width + i * gd:2 * width + (i + 1) * gd] = dgs[i].astype(BF16)
            dlng_out[:, cols] += dlng[i]
            dlnb_out[:, cols] += dlnb[i]
            dws_out[i] += dws[i]
            dbs_out[i] += dbs[i]

    full = lambda a: pl.BlockSpec(a.shape, lambda i: (0,) * a.ndim)
    like = lambda a: jax.ShapeDtypeStruct(a.shape, F32)
    return pl.pallas_call(
        body, name="sgu_bwd", grid=(t // tm,),
        out_shape=(jax.ShapeDtypeStruct((t, 3 * width), BF16), like(ln_g), like(ln_b), like(w_s), like(b_s)),
        in_specs=[pl.BlockSpec((tm, 3 * width), lambda i: (i, 0)), full(ln_g), full(ln_b), full(w_s), full(b_s),
                  pl.BlockSpec((tm, width), lambda i: (i, 0))],
        out_specs=(pl.BlockSpec((tm, 3 * width), lambda i: (i, 0)), full(ln_g), full(ln_b), full(w_s), full(b_s)),
        compiler_params=_params("arbitrary"),
    )(p, ln_g, ln_b, w_s, b_s, dy)


def _sum_slabs(parts, name):
    n_parts, rows, cols = parts.shape
    tr = _tile(rows, max(16, (1 << 18) // cols), 16)

    def body(p_ref, o_ref):
        acc = p_ref[0].astype(F32)
        for d in range(1, n_parts):
            acc = acc + p_ref[d].astype(F32)
        o_ref[...] = acc

    return pl.pallas_call(
        body, name=name, grid=(rows // tr,),
        out_shape=jax.ShapeDtypeStruct((rows, cols), F32),
        in_specs=[pl.BlockSpec((n_parts, tr, cols), lambda i: (0, i, 0))],
        out_specs=pl.BlockSpec((tr, cols), lambda i: (i, 0)),
        compiler_params=_params("parallel"),
    )(parts)


def _adamw(w, g, m, v, name):
    rows, cols = w.shape
    tr = _tile(rows, max(8, (1 << 18) // cols), 8)

    def body(w_ref, g_ref, m_ref, v_ref, d_out, m_out, v_out):
        gv = g_ref[...]
        mn = ADAM_B1 * m_ref[...] + (1.0 - ADAM_B1) * gv
        vn = ADAM_B2 * v_ref[...] + (1.0 - ADAM_B2) * (gv * gv)
        m_hat = mn / (1.0 - ADAM_B1 ** ADAM_STEP)
        v_hat = vn / (1.0 - ADAM_B2 ** ADAM_STEP)
        d_out[...] = -ADAM_LR * (m_hat / (jnp.sqrt(v_hat) + ADAM_EPS) + ADAM_WD * w_ref[...])
        m_out[...] = mn
        v_out[...] = vn

    blk = pl.BlockSpec((tr, cols), lambda i: (i, 0))
    shape = jax.ShapeDtypeStruct((rows, cols), F32)
    return pl.pallas_call(
        body, name=name, grid=(rows // tr,),
        out_shape=(shape, shape, shape),
        in_specs=[blk, blk, blk, blk], out_specs=(blk, blk, blk),
        compiler_params=_params("parallel"),
    )(w, g, m, v)


PACK_COLS = 1024


def _pack(arrays):
    flat = jnp.concatenate([a.reshape(-1).astype(F32) for a in arrays])
    rows = -(-flat.shape[0] // (8 * PACK_COLS)) * 8
    return jnp.pad(flat, (0, rows * PACK_COLS - flat.shape[0])).reshape(rows, PACK_COLS)


def _unpack(packed, shapes):
    flat = packed.reshape(-1)
    out, at = [], 0
    for s in shapes:
        n = 1
        for d in s:
            n *= d
        out.append(flat[at:at + n].reshape(s))
        at += n
    return out


def kernel(x, norm_g, final_norm_g, e_w_in, e_shift_mu, e_w_decay_up, e_w0, e_a_up, e_a0, e_k_k, e_k_a, e_r_k, e_gn_g, e_gn_b, e_w_out, o_w_in, o_ln_g, o_ln_b, o_w_s, o_b_s, o_w_out, loss_target, m_norm_g, m_final_norm_g, m_e_w_in, m_e_shift_mu, m_e_w_decay_up, m_e_w0, m_e_a_up, m_e_a0, m_e_k_k, m_e_k_a, m_e_r_k, m_e_gn_g, m_e_gn_b, m_e_w_out, m_o_w_in, m_o_ln_g, m_o_ln_b, m_o_w_s, m_o_b_s, m_o_w_out, v_norm_g, v_final_norm_g, v_e_w_in, v_e_shift_mu, v_e_w_decay_up, v_e_w0, v_e_a_up, v_e_a0, v_e_k_k, v_e_k_a, v_e_r_k, v_e_gn_g, v_e_gn_b, v_e_w_out, v_o_w_in, v_o_ln_g, v_o_ln_b, v_o_w_s, v_o_b_s, v_o_w_out):
    weights = dict(norm_g=norm_g, final_norm_g=final_norm_g, e_w_in=e_w_in, e_shift_mu=e_shift_mu,
                   e_w_decay_up=e_w_decay_up, e_w0=e_w0, e_a_up=e_a_up, e_a0=e_a0, e_k_k=e_k_k, e_k_a=e_k_a,
                   e_r_k=e_r_k, e_gn_g=e_gn_g, e_gn_b=e_gn_b, e_w_out=e_w_out, o_w_in=o_w_in, o_ln_g=o_ln_g,
                   o_ln_b=o_ln_b, o_w_s=o_w_s, o_b_s=o_b_s, o_w_out=o_w_out)
    mom1 = dict(norm_g=m_norm_g, final_norm_g=m_final_norm_g, e_w_in=m_e_w_in, e_shift_mu=m_e_shift_mu,
                e_w_decay_up=m_e_w_decay_up, e_w0=m_e_w0, e_a_up=m_e_a_up, e_a0=m_e_a0, e_k_k=m_e_k_k,
                e_k_a=m_e_k_a, e_r_k=m_e_r_k, e_gn_g=m_e_gn_g, e_gn_b=m_e_gn_b, e_w_out=m_e_w_out,
                o_w_in=m_o_w_in, o_ln_g=m_o_ln_g, o_ln_b=m_o_ln_b, o_w_s=m_o_w_s, o_b_s=m_o_b_s,
                o_w_out=m_o_w_out)
    mom2 = dict(norm_g=v_norm_g, final_norm_g=v_final_norm_g, e_w_in=v_e_w_in, e_shift_mu=v_e_shift_mu,
                e_w_decay_up=v_e_w_decay_up, e_w0=v_e_w0, e_a_up=v_e_a_up, e_a0=v_e_a0, e_k_k=v_e_k_k,
                e_k_a=v_e_k_a, e_r_k=v_e_r_k, e_gn_g=v_e_gn_g, e_gn_b=v_e_gn_b, e_w_out=v_e_w_out,
                o_w_in=v_o_w_in, o_ln_g=v_o_ln_g, o_ln_b=v_o_ln_b, o_w_s=v_o_w_s, o_b_s=v_o_b_s,
                o_w_out=v_o_w_out)
    names = list(weights)
    big = ("e_w_in", "e_w_out", "o_w_in", "o_w_out")

    bl, seq, d = x.shape
    t = bl * seq
    width = e_w0.shape[1]
    lora = e_w_decay_up.shape[1]
    n_sb = width // SB_HEAD
    me = 4 * lax.axis_index("x") + 2 * lax.axis_index("y") + lax.axis_index("c")

    e_win_t = _all_gather(e_w_in[0].T.astype(BF16), "gather_e_w_in").reshape(-1, d)
    later_shards = (e_w_out[0].astype(BF16), o_w_in[0].T.astype(BF16), o_w_out[0].astype(BF16))
    sharded_small = ("e_w_decay_up", "e_a_up", "o_ln_g", "o_ln_b")
    small_shapes = [weights[n][0].shape for n in sharded_small]
    got = _all_gather(_pack([weights[n][0] for n in sharded_small]), "gather_small")
    per_dev = [_unpack(got[dev], small_shapes) for dev in range(N_DEV)]
    wd, wa, ln_g, ln_b = [jnp.concatenate([per_dev[dev][i] for dev in range(N_DEV)], axis=-1).reshape(
        small_shapes[i][:-1] + (-1,)) for i in range(4)]
    ln_g, ln_b = ln_g.reshape(1, -1), ln_b.reshape(1, -1)
    e_ind, e_ind_t = _head_indicator(width)
    b_s3 = o_b_s[0][:, :, None]

    x2d = x.reshape(t, d)
    target = loss_target.reshape(t, d)
    cols_rwkv = 3 * width + 2 * lora
    assert cols_rwkv % LANES == 0 and width % LANES == 0
    sb0 = (cols_rwkv + width) // SB_HEAD
    sb_cols = (sb0, sb0 + n_sb, sb0 + 2 * n_sb, sb0 + 3 * n_sb)

    h0 = _rms_fwd(x2d, norm_g[0:1], "rms0_fwd")
    p = _matmul(h0, e_win_t, "nt", F32, "e_in_fwd", tn=1280)
    g_rwkv = p[:, cols_rwkv:cols_rwkv + width]
    r, w, kx, v, kk, b = _rwkv_prep(p, seq, (width, lora), e_shift_mu, wd, e_w0, wa, e_a0, e_k_k, e_k_a,
                                    e_ind, e_ind_t)
    sk = [_to_scan_k(a, bl, seq) for a in (kk, w, b, kx, r)]
    sv = _to_scan_v(v, bl, seq)
    ys_scan, states, sa_all, e_wout, o_win_t, o_wout = _scan_fwd(*sk, sv, gathers=later_shards)
    e_wout, o_win_t, o_wout = (a.reshape(-1, d) for a in (e_wout, o_win_t, o_wout))
    ys = _from_scan_v(ys_scan, bl, seq)
    ya = _rwkv_post(ys, r, kx, v, g_rwkv, 0, e_gn_g, e_gn_b, e_r_k, e_ind, e_ind_t)
    yb, sb_o, sb_tot = _sb_fwd(p, bl, seq, sb_cols, n_sb)
    y = jnp.concatenate([ya, yb], axis=1)
    x1 = _matmul(y, e_wout, "nn", F32, "e_out_fwd", res=x2d)
    h1 = _rms_fwd(x1, norm_g[1:2], "rms1_fwd")
    p2 = _matmul(h1, o_win_t, "nt", F32, "o_in_fwd")
    y2 = _sgu_fwd(p2, ln_g, ln_b, o_w_s[0], b_s3)
    x2 = _matmul(y2, o_wout, "nn", F32, "o_out_fwd", res=x1)
    dx2, d_final_g, loss_part = _final_loss(x2, final_norm_g.reshape(1, d), target, "final_loss")

    dy2 = _matmul(dx2, o_wout, "nt", F32, "o_out_bwd_x")
    d_o_wout = _matmul(y2, dx2, "tn", F32, "o_out_bwd_w")
    dp2, d_ln_g, d_ln_b, d_w_s, d_b_s3 = _sgu_bwd(p2, ln_g, ln_b, o_w_s[0], b_s3, dy2)
    dh1 = _matmul(dp2, o_win_t, "nn", F32, "o_in_bwd_x", tk=1536)
    d_o_win_t = _matmul(dp2, h1, "tn", F32, "o_in_bwd_w")
    dx1, d_g1 = _rms_bwd(x1, norm_g[1:2], dh1, dx2, "rms1_bwd")
    dy = _matmul(dx1, e_wout, "nt", F32, "e_out_bwd_x")
    d_e_wout = _matmul(y, dx1, "tn", F32, "e_out_bwd_w")
    dq, dk, dv_sb, dg_sb = _sb_bwd(p, bl, seq, sb_cols, n_sb, dy, n_sb, sb_o, sb_tot)
    dys, dr1, dkx1, dv1, dg_rwkv, d_gn_g, d_gn_b, d_r_k = _rwkv_post_bwd(
        ys, r, kx, v, g_rwkv, 0, e_gn_g, e_gn_b, e_r_k, e_ind, e_ind_t, dy, 0)
    dkk_s, dw_s, db_s, dkx_s, dr_s, dv_s = _scan_bwd(*sk, sv, states, sa_all, _to_scan_v(dys, bl, seq))
    dkk, dw, db, dkx2, dr2 = [_from_scan_k(a, bl, seq) for a in (dkk_s, dw_s, db_s, dkx_s, dr_s)]
    dv2 = _from_scan_v(dv_s, bl, seq)
    dp_rwkv, d_mu, d_wd, d_w0, d_wa, d_a0, d_k_k, d_k_a = _rwkv_prep_bwd(
        p, seq, (width, lora), e_shift_mu, wd, e_w0, wa, e_a0, e_k_k, e_k_a, e_ind, e_ind_t,
        dr1 + dr2, dw, dkx1 + dkx2, dv1 + dv2, dkk, db)
    dp = jnp.concatenate([dp_rwkv, dg_rwkv, dq, dk, dv_sb, dg_sb], axis=1)
    dh0 = _matmul(dp, e_win_t, "nn", F32, "e_in_bwd_x", tk=1280)
    d_e_win_t = _matmul(dp, h0, "tn", F32, "e_in_bwd_w", tm=1280)
    grad_x, d_g0 = _rms_bwd(x2d, norm_g[0:1], dh0, dx1, "rms0_bwd")

    core = lax.axis_index("c").astype(jnp.int32).reshape(1)

    def scattered(full, name):
        by_owner = full.reshape((N_DEV // 2, 2, full.shape[0] // N_DEV, full.shape[1]))
        got = _pair_swap(by_owner, "swap_" + name)
        partials = _pair_sum(by_owner, got, core, "pairsum_" + name)
        return _sum_slabs(_chip_scatter(partials, "scatter_" + name), "sum_" + name)

    grads = {
        "e_w_in": scattered(d_e_win_t, "e_w_in").T[None],
        "e_w_out": scattered(d_e_wout, "e_w_out")[None],
        "o_w_in": scattered(d_o_win_t, "o_w_in").T[None],
        "o_w_out": scattered(d_o_wout, "o_w_out")[None],
    }
    small_full = {
        "norm_g": jnp.concatenate([d_g0, d_g1], axis=0), "final_norm_g": d_final_g.reshape(-1),
        "e_shift_mu": d_mu, "e_w_decay_up": d_wd[None], "e_w0": d_w0, "e_a_up": d_wa[None], "e_a0": d_a0,
        "e_k_k": d_k_k, "e_k_a": d_k_a, "e_r_k": d_r_k, "e_gn_g": d_gn_g, "e_gn_b": d_gn_b,
        "o_ln_g": d_ln_g, "o_ln_b": d_ln_b, "o_w_s": d_w_s[None], "o_b_s": d_b_s3[:, :, 0][None],
    }
    small = [n for n in names if n not in big]
    parts = _all_gather(_pack([small_full[n] for n in small]), "gather_small_grads")
    totals = _unpack(_sum_slabs(parts, "sum_small_grads"), [small_full[n].shape for n in small])
    for n, g in zip(small, totals):
        if n in sharded_small:
            size = weights[n].shape[-1]
            g = lax.dynamic_slice_in_dim(g, me * size, size, axis=g.ndim - 1)
        grads[n] = g.reshape(weights[n].shape)

    delta, new_m, new_v = {}, {}, {}
    for n in big:
        shp = weights[n].shape
        flat = lambda a: a.reshape(shp[-2], shp[-1])
        dl, mn, vn = _adamw(flat(weights[n]), flat(grads[n]), flat(mom1[n]), flat(mom2[n]), "adamw_" + n)
        delta[n], new_m[n], new_v[n] = dl.reshape(shp), mn.reshape(shp), vn.reshape(shp)
    packed = [_pack([src[n] for n in small]) for src in (weights, grads, mom1, mom2)]
    outs = _adamw(*packed, "adamw_small")
    shapes = [weights[n].shape for n in small]
    for dst, arr in zip((delta, new_m, new_v), outs):
        for n, a in zip(small, _unpack(arr, shapes)):
            dst[n] = a

    loss = lax.psum(loss_part[0, 0], ("x", "y", "c"))
    return (loss, grad_x.reshape(bl, seq, d), *[grads[n] for n in names], *[delta[n] for n in names],
            *[new_m[n] for n in names], *[new_v[n] for n in names])
```

```python
import functools

import jax
import jax.numpy as jnp
from jax import lax
from jax.experimental import pallas as pl
from jax.experimental.pallas import tpu as pltpu

F32 = jnp.float32
BF16 = jnp.bfloat16
HIGHEST = lax.Precision.HIGHEST

N_DEV = 8
RWKV_HEAD = 64
SB_HEAD = 128
SB_BLOCK = 128
SGU_CHUNK = 128
SGU_GROUPS = 16
RMS_EPS = 1e-6
GN_EPS = 64e-5
LN_EPS = 1e-5
L2_EPS = 1e-12
ADAM_LR = 0.001
ADAM_B1 = 0.9
ADAM_B2 = 0.999
ADAM_EPS = 1e-08
ADAM_WD = 0.01
ADAM_STEP = 10

VMEM_LIMIT_V7X = 56 * 1024 * 1024
LANES = 128
SCAN_STEPS_PER_BLOCK = 8


def _params(*sem):
    return pltpu.CompilerParams(dimension_semantics=sem, vmem_limit_bytes=VMEM_LIMIT_V7X)


def _tile(n, target, mult):
    best = None
    d = mult
    while d <= min(n, target):
        if n % d == 0:
            best = d
        d += mult
    return n if best is None else best


def _remote(src, dst, send_sems, recv_sems, k, dev):
    return pltpu.make_async_remote_copy(src_ref=src, dst_ref=dst, send_sem=send_sems.at[k], recv_sem=recv_sems.at[k],
                                        device_id=dev, device_id_type=pl.DeviceIdType.MESH)


_HBM = pl.BlockSpec(memory_space=pl.ANY)


GATHER_COPIES = 7


def _gather_phases(src_ref, out_ref, send_sems, recv_sems, local_sems, n):
    x, y, c = lax.axis_index("x"), lax.axis_index("y"), lax.axis_index("c")
    me, sibling = (x, y, c), (x, y, 1 - c)
    chips = [(1 - x, y), (x, 1 - y), (1 - x, 1 - y)]

    def slot(px, py, pc):
        return out_ref.at[4 * px + 2 * py + pc]

    def copy(k, block, to, own=False):
        return _remote(src_ref if own else slot(*block), slot(*block), send_sems, recv_sems,
                       GATHER_COPIES * n + k, to)

    mine = pltpu.make_async_copy(src_ref, slot(*me), local_sems.at[n])
    first = [copy(0, me, sibling, True)] + [copy(1 + j, me, (*chip, c), True) for j, chip in enumerate(chips)]
    passed = [copy(4 + j, (*chip, c), sibling) for j, chip in enumerate(chips)]

    def start():
        mine.start()
        for cp in first:
            cp.start()

    def relay():
        for j, chip in enumerate(chips):
            copy(1 + j, (*chip, c), me).wait_recv()
            passed[j].start()

    def finish():
        copy(0, sibling, me).wait_recv()
        for j, chip in enumerate(chips):
            copy(4 + j, (*chip, 1 - c), me).wait_recv()
        for cp in first + passed:
            cp.wait_send()
        mine.wait()

    return start, relay, finish


def _gather_scratch(n):
    return [pltpu.SemaphoreType.DMA((GATHER_COPIES * n,)), pltpu.SemaphoreType.DMA((GATHER_COPIES * n,)),
            pltpu.SemaphoreType.DMA((n,))]


def _gathered_shape(src):
    return jax.ShapeDtypeStruct((N_DEV,) + tuple(src.shape), src.dtype)


def _all_gather(src, name):
    def body(src_ref, out_ref, send_sems, recv_sems, local_sems):
        for phase in _gather_phases(src_ref, out_ref, send_sems, recv_sems, local_sems, 0):
            phase()

    return pl.pallas_call(
        body, name=name, out_shape=_gathered_shape(src), in_specs=[_HBM], out_specs=_HBM,
        scratch_shapes=_gather_scratch(1),
    )(src)


N_CHIPS = N_DEV // 2
SCATTER_COPIES = N_CHIPS - 1


def _swap_phases(src_ref, out_ref, send_sems, recv_sems, n):
    x, y, c = lax.axis_index("x"), lax.axis_index("y"), lax.axis_index("c")
    copies = [_remote(src_ref.at[q, 1 - c], out_ref.at[q], send_sems, recv_sems, N_CHIPS * n + q, (x, y, 1 - c))
              for q in range(N_CHIPS)]

    def start():
        for cp in copies:
            cp.start()

    def finish():
        for cp in copies:
            cp.wait_recv()
        for cp in copies:
            cp.wait_send()

    return start, finish


def _swapped_shape(full):
    return jax.ShapeDtypeStruct((N_CHIPS,) + tuple(full.shape[2:]), full.dtype)


def _swap_scratch(n):
    return [pltpu.SemaphoreType.DMA((N_CHIPS * n,)), pltpu.SemaphoreType.DMA((N_CHIPS * n,))]


def _pair_swap(full, name):
    def body(src_ref, out_ref, send_sems, recv_sems):
        for phase in _swap_phases(src_ref, out_ref, send_sems, recv_sems, 0):
            phase()

    return pl.pallas_call(body, name=name, out_shape=_swapped_shape(full), in_specs=[_HBM], out_specs=_HBM,
                          scratch_shapes=_swap_scratch(1))(full)


def _pair_sum(full, got, core, name):
    n_chips, _, rows, cols = full.shape
    tr = _tile(rows, max(16, (1 << 19) // cols), 16)

    def body(core_ref, a_ref, b_ref, o_ref):
        o_ref[...] = (a_ref[...] + b_ref[...]).astype(BF16)

    return pl.pallas_call(
        body, name=name,
        grid_spec=pltpu.PrefetchScalarGridSpec(
            num_scalar_prefetch=1, grid=(n_chips, rows // tr),
            in_specs=[pl.BlockSpec((None, None, tr, cols), lambda q, i, s: (q, s[0], i, 0)),
                      pl.BlockSpec((None, tr, cols), lambda q, i, s: (q, i, 0))],
            out_specs=pl.BlockSpec((None, tr, cols), lambda q, i, s: (q, i, 0))),
        out_shape=jax.ShapeDtypeStruct((n_chips, rows, cols), BF16),
        compiler_params=_params("parallel", "parallel"),
    )(core, full, got)


def _scatter_phases(src_ref, out_ref, send_sems, recv_sems, local_sems, n):
    x, y, c = lax.axis_index("x"), lax.axis_index("y"), lax.axis_index("c")
    here = 2 * x + y
    chips = [(1 - x, y), (x, 1 - y), (1 - x, 1 - y)]
    local = pltpu.make_async_copy(src_ref.at[here], out_ref.at[here], local_sems.at[n])
    sends = [_remote(src_ref.at[2 * px + py], out_ref.at[here], send_sems, recv_sems, SCATTER_COPIES * n + j,
                     (px, py, c)) for j, (px, py) in enumerate(chips)]
    recvs = [_remote(src_ref.at[2 * px + py], out_ref.at[2 * px + py], send_sems, recv_sems,
                     SCATTER_COPIES * n + j, (px, py, c)) for j, (px, py) in enumerate(chips)]

    def start():
        local.start()
        for cp in sends:
            cp.start()

    def finish():
        for cp in recvs:
            cp.wait_recv()
        for cp in sends:
            cp.wait_send()
        local.wait()

    return start, finish


def _scatter_scratch(n):
    return [pltpu.SemaphoreType.DMA((SCATTER_COPIES * n,)), pltpu.SemaphoreType.DMA((SCATTER_COPIES * n,)),
            pltpu.SemaphoreType.DMA((n,))]


def _chip_scatter(parts, name):
    def body(src_ref, out_ref, send_sems, recv_sems, local_sems):
        for phase in _scatter_phases(src_ref, out_ref, send_sems, recv_sems, local_sems, 0):
            phase()

    return pl.pallas_call(body, name=name, out_shape=jax.ShapeDtypeStruct(parts.shape, parts.dtype),
                          in_specs=[_HBM], out_specs=_HBM, scratch_shapes=_scatter_scratch(1))(parts)


def _matmul(a, b, mode, out_dtype, name, res=None, tm=1024, tn=1024, tk=1024):
    if mode == "nn":
        (m, k), (k2, n) = a.shape, b.shape
    elif mode == "nt":
        (m, k), (n, k2) = a.shape, b.shape
    else:
        (k, m), (k2, n) = a.shape, b.shape
    assert k == k2, (a.shape, b.shape, mode)
    tm, tn, tk = _tile(m, tm, 128), _tile(n, tn, 128), _tile(k, tk, 128)
    nk = k // tk
    if mode == "nn":
        a_spec = pl.BlockSpec((tm, tk), lambda i, j, kk: (i, kk))
        b_spec = pl.BlockSpec((tk, tn), lambda i, j, kk: (kk, j))
        dims = (((1,), (0,)), ((), ()))
    elif mode == "nt":
        a_spec = pl.BlockSpec((tm, tk), lambda i, j, kk: (i, kk))
        b_spec = pl.BlockSpec((tn, tk), lambda i, j, kk: (j, kk))
        dims = (((1,), (1,)), ((), ()))
    else:
        a_spec = pl.BlockSpec((tk, tm), lambda i, j, kk: (kk, i))
        b_spec = pl.BlockSpec((tk, tn), lambda i, j, kk: (kk, j))
        dims = (((0,), (0,)), ((), ()))
    o_spec = pl.BlockSpec((tm, tn), lambda i, j, kk: (i, j))
    has_res = res is not None

    def body(*refs):
        if has_res:
            a_ref, b_ref, r_ref, o_ref, acc_ref = refs
        else:
            a_ref, b_ref, o_ref, acc_ref = refs
        kk = pl.program_id(2)

        @pl.when(kk == 0)
        def _():
            acc_ref[...] = jnp.zeros_like(acc_ref)

        acc_ref[...] += lax.dot_general(a_ref[...].astype(BF16), b_ref[...].astype(BF16), dims,
                                        preferred_element_type=F32)

        @pl.when(kk == nk - 1)
        def _():
            out = acc_ref[...]
            if has_res:
                out = out + r_ref[...]
            o_ref[...] = out.astype(out_dtype)

    ins = [a, b] + ([res] if has_res else [])
    specs = [a_spec, b_spec] + ([o_spec] if has_res else [])
    return pl.pallas_call(
        body, name=name, grid=(m // tm, n // tn, nk),
        out_shape=jax.ShapeDtypeStruct((m, n), out_dtype),
        in_specs=specs, out_specs=o_spec,
        scratch_shapes=[pltpu.VMEM((tm, tn), F32)],
        compiler_params=_params("parallel", "parallel", "arbitrary"),
    )(*ins)


def _rms_fwd(x, g, name):
    t, d = x.shape
    tm = _tile(t, 256, 8)

    def body(x_ref, g_ref, h_ref):
        xv = x_ref[...]
        rstd = lax.rsqrt(jnp.mean(xv * xv, axis=-1, keepdims=True) + RMS_EPS)
        h_ref[...] = (xv * rstd * g_ref[...]).astype(BF16)

    return pl.pallas_call(
        body, name=name, grid=(t // tm,),
        out_shape=jax.ShapeDtypeStruct((t, d), BF16),
        in_specs=[pl.BlockSpec((tm, d), lambda i: (i, 0)), pl.BlockSpec((1, d), lambda i: (0, 0))],
        out_specs=pl.BlockSpec((tm, d), lambda i: (i, 0)),
        compiler_params=_params("parallel"),
    )(x, g)


def _rms_bwd(x, g, dh, dres, name):
    t, d = x.shape
    tm = _tile(t, 256, 8)

    def body(x_ref, g_ref, dh_ref, dres_ref, dx_ref, dg_ref):
        @pl.when(pl.program_id(0) == 0)
        def _():
            dg_ref[...] = jnp.zeros_like(dg_ref)

        xv = x_ref[...]
        rstd = lax.rsqrt(jnp.mean(xv * xv, axis=-1, keepdims=True) + RMS_EPS)
        xhat = xv * rstd
        dh_v = dh_ref[...]
        dg_ref[...] += jnp.sum(dh_v * xhat, axis=0, keepdims=True)
        dxh = dh_v * g_ref[...]
        dx_ref[...] = dres_ref[...] + rstd * (dxh - xhat * jnp.mean(dxh * xhat, axis=-1, keepdims=True))

    row = pl.BlockSpec((tm, d), lambda i: (i, 0))
    vec = pl.BlockSpec((1, d), lambda i: (0, 0))
    return pl.pallas_call(
        body, name=name, grid=(t // tm,),
        out_shape=(jax.ShapeDtypeStruct((t, d), F32), jax.ShapeDtypeStruct((1, d), F32)),
        in_specs=[row, vec, row, row], out_specs=(row, vec),
        compiler_params=_params("arbitrary"),
    )(x, g, dh, dres)


def _final_loss(x, g, target, name):
    t, d = x.shape
    tm = _tile(t, 256, 8)

    def body(x_ref, g_ref, t_ref, dx_ref, dg_ref, loss_ref):
        @pl.when(pl.program_id(0) == 0)
        def _():
            dg_ref[...] = jnp.zeros_like(dg_ref)
            loss_ref[...] = jnp.zeros_like(loss_ref)

        xv = x_ref[...]
        rstd = lax.rsqrt(jnp.mean(xv * xv, axis=-1, keepdims=True) + RMS_EPS)
        xhat = xv * rstd
        gv = g_ref[...]
        err = xhat * gv - t_ref[...]
        loss_ref[...] += 0.5 * jnp.sum(jnp.mean(err * err, axis=-1, keepdims=True), axis=0, keepdims=True)
        dout = err * (1.0 / d)
        dg_ref[...] += jnp.sum(dout * xhat, axis=0, keepdims=True)
        dxh = dout * gv
        dx_ref[...] = rstd * (dxh - xhat * jnp.mean(dxh * xhat, axis=-1, keepdims=True))

    row = pl.BlockSpec((tm, d), lambda i: (i, 0))
    vec = pl.BlockSpec((1, d), lambda i: (0, 0))
    return pl.pallas_call(
        body, name=name, grid=(t // tm,),
        out_shape=(jax.ShapeDtypeStruct((t, d), F32), jax.ShapeDtypeStruct((1, d), F32),
                   jax.ShapeDtypeStruct((1, 1), F32)),
        in_specs=[row, vec, row], out_specs=(row, vec, pl.BlockSpec((1, 1), lambda i: (0, 0))),
        compiler_params=_params("arbitrary"),
    )(x, g, target)


def _dot32(a, b):
    return jnp.dot(a, b, precision=HIGHEST, preferred_element_type=F32)


def _head_sum(v, e, et):
    return _dot32(_dot32(v, e), et)


def _log_sigmoid(z):
    return jnp.minimum(z, 0.0) - jnp.log1p(jnp.exp(-jnp.abs(z)))


def _silu(g):
    return g * jax.nn.sigmoid(g)


def _prep_math(k, wlo, alo, wd, w0, wa, a0, k_k, k_a, e, et):
    wl = w0 + _dot32(jnp.tanh(wlo), wd)
    w_log = _log_sigmoid(wl) - 0.5
    w = jnp.exp(-jnp.exp(w_log))
    a = jax.nn.sigmoid(a0 + _dot32(alo, wa))
    kk0 = k * k_k
    kk = kk0 * lax.rsqrt(jnp.maximum(_head_sum(kk0 * kk0, e, et), L2_EPS * L2_EPS))
    kx = k * (1.0 + (a - 1.0) * k_a)
    return w, kx, kk, kk * a


def _post_math(ys, r, kx, v, g, gn_g, gn_b, r_k, e, et):
    inv = 1.0 / RWKV_HEAD
    mu = _head_sum(ys, e, et) * inv
    dlt = ys - mu
    var = _head_sum(dlt * dlt, e, et) * inv
    y = dlt * lax.rsqrt(var + GN_EPS) * gn_g + gn_b
    bonus = _head_sum(r * kx * r_k, e, et) * v
    return (y + bonus) * _silu(g)


def _shifted(p, prev_row, first):
    rows = lax.broadcasted_iota(jnp.int32, p.shape, 0)
    prev = jnp.where(first, 0.0, prev_row)
    return jnp.where(rows == 0, prev, pltpu.roll(p, 1, 0))


def _head_indicator(width):
    ch = lax.broadcasted_iota(jnp.int32, (width, width // RWKV_HEAD), 0) // RWKV_HEAD
    hd = lax.broadcasted_iota(jnp.int32, (width, width // RWKV_HEAD), 1)
    e = (ch == hd).astype(F32)
    return e, e.T


def _rwkv_prep(p, seq, dims, mu, wd, w0, wa, a0, k_k, k_a, e, et):
    t = p.shape[0]
    width, lora = dims
    cols = 3 * width + 2 * lora
    tm = 128
    per_seq = seq // tm

    def body(p_ref, prev_ref, mu_ref, wd_ref, w0_ref, wa_ref, a0_ref, kk_ref, ka_ref, e_ref, et_ref,
             r_out, w_out, kx_out, v_out, kkn_out, b_out):
        i = pl.program_id(0)
        pv = p_ref[...]
        psh = _shifted(pv, prev_ref[7:8, :], i % per_seq == 0)
        ps = pv + mu_ref[...] * (psh - pv)
        r, k, v = ps[:, :width], ps[:, width:2 * width], ps[:, 2 * width:3 * width]
        wlo, alo = ps[:, 3 * width:3 * width + lora], ps[:, 3 * width + lora:]
        w, kx, kk, b = _prep_math(k, wlo, alo, wd_ref[...], w0_ref[...], wa_ref[...], a0_ref[...],
                                  kk_ref[...], ka_ref[...], e_ref[...], et_ref[...])
        r_out[...] = r
        w_out[...] = w
        kx_out[...] = kx
        v_out[...] = v
        kkn_out[...] = kk
        b_out[...] = b

    full = lambda a: pl.BlockSpec(a.shape, lambda i: (0,) * a.ndim)
    out = pl.BlockSpec((tm, width), lambda i: (i, 0))
    return pl.pallas_call(
        body, name="rwkv_prep", grid=(t // tm,),
        out_shape=tuple(jax.ShapeDtypeStruct((t, width), F32) for _ in range(6)),
        in_specs=[pl.BlockSpec((tm, cols), lambda i: (i, 0)),
                  pl.BlockSpec((8, cols), lambda i: (jnp.maximum(i * (tm // 8) - 1, 0), 0)),
                  full(mu), full(wd), full(w0), full(wa), full(a0), full(k_k), full(k_a), full(e), full(et)],
        out_specs=tuple(out for _ in range(6)),
        compiler_params=_params("parallel"),
    )(p, p, mu, wd, w0, wa, a0, k_k, k_a, e, et)


def _rwkv_prep_bwd(p, seq, dims, mu, wd, w0, wa, a0, k_k, k_a, e, et, dr, dw, dkx, dv, dkk, db):
    t = p.shape[0]
    width, lora = dims
    cols = 3 * width + 2 * lora
    tm = 128
    n_tiles = t // tm
    per_seq = seq // tm

    def body(p_ref, prev_ref, mu_ref, wd_ref, w0_ref, wa_ref, a0_ref, kk_ref, ka_ref, e_ref, et_ref,
             dr_ref, dw_ref, dkx_ref, dv_ref, dkk_ref, db_ref,
             dp_out, dmu_out, dwd_out, dw0_out, dwa_out, da0_out, dkk_out, dka_out, carry):
        step = pl.program_id(0)
        i = n_tiles - 1 - step

        @pl.when(step == 0)
        def _():
            for ref in (dmu_out, dwd_out, dw0_out, dwa_out, da0_out, dkk_out, dka_out, carry):
                ref[...] = jnp.zeros_like(ref)

        pv = p_ref[...]
        first = i % per_seq == 0
        psh = _shifted(pv, prev_ref[7:8, :], first)
        muv = mu_ref[...]
        ps = pv + muv * (psh - pv)
        k = ps[:, width:2 * width]
        wlo, alo = ps[:, 3 * width:3 * width + lora], ps[:, 3 * width + lora:]
        ev, etv = e_ref[...], et_ref[...]
        _, vjp = jax.vjp(lambda *a: _prep_math(*a, ev, etv), k, wlo, alo, wd_ref[...], w0_ref[...],
                         wa_ref[...], a0_ref[...], kk_ref[...], ka_ref[...])
        dk, dwlo, dalo, dwd, dw0, dwa, da0, dk_k, dk_a = vjp(
            (dw_ref[...], dkx_ref[...], dkk_ref[...], db_ref[...]))
        dps = jnp.concatenate([dr_ref[...], dk, dv_ref[...], dwlo, dalo], axis=1)
        dmu_out[...] += jnp.sum(dps * (psh - pv), axis=0, keepdims=True)
        dwd_out[...] += dwd
        dw0_out[...] += dw0
        dwa_out[...] += dwa
        da0_out[...] += da0
        dkk_out[...] += dk_k
        dka_out[...] += dk_a
        dsh = dps * muv
        rows = lax.broadcasted_iota(jnp.int32, dsh.shape, 0)
        nxt = jnp.where(rows == tm - 1, carry[...], pltpu.roll(dsh, tm - 1, 0))
        dp_out[...] = (dps * (1.0 - muv) + nxt).astype(BF16)
        carry[...] = jnp.where(first, 0.0, dsh[0:1, :])

    full = lambda a: pl.BlockSpec(a.shape, lambda s: (0,) * a.ndim)
    tok = pl.BlockSpec((tm, width), lambda s: (n_tiles - 1 - s, 0))
    vec = lambda n: jax.ShapeDtypeStruct((1, n), F32)
    outs = (jax.ShapeDtypeStruct((t, cols), BF16), vec(cols), jax.ShapeDtypeStruct(wd.shape, F32), vec(width),
            jax.ShapeDtypeStruct(wa.shape, F32), vec(width), vec(width), vec(width))
    return pl.pallas_call(
        body, name="rwkv_prep_bwd", grid=(n_tiles,),
        out_shape=outs,
        in_specs=[pl.BlockSpec((tm, cols), lambda s: (n_tiles - 1 - s, 0)),
                  pl.BlockSpec((8, cols), lambda s: (jnp.maximum((n_tiles - 1 - s) * (tm // 8) - 1, 0), 0)),
                  full(mu), full(wd), full(w0), full(wa), full(a0), full(k_k), full(k_a), full(e), full(et),
                  tok, tok, tok, tok, tok, tok],
        out_specs=(pl.BlockSpec((tm, cols), lambda s: (n_tiles - 1 - s, 0)),) + tuple(
            pl.BlockSpec(o.shape, lambda s: (0, 0)) for o in outs[1:]),
        scratch_shapes=[pltpu.VMEM((1, cols), F32)],
        compiler_params=_params("arbitrary"),
    )(p, p, mu, wd, w0, wa, a0, k_k, k_a, e, et, dr, dw, dkx, dv, dkk, db)


def _rwkv_post(ys, r, kx, v, p, g_col, gn_g, gn_b, r_k, e, et):
    t, width = ys.shape
    tm = 256

    def body(ys_ref, r_ref, kx_ref, v_ref, g_ref, gg_ref, gb_ref, rk_ref, e_ref, et_ref, out_ref):
        out_ref[...] = _post_math(ys_ref[...], r_ref[...], kx_ref[...], v_ref[...], g_ref[...], gg_ref[...],
                                  gb_ref[...], rk_ref[...], e_ref[...], et_ref[...]).astype(BF16)

    tok = pl.BlockSpec((tm, width), lambda i: (i, 0))
    full = lambda a: pl.BlockSpec(a.shape, lambda i: (0,) * a.ndim)
    return pl.pallas_call(
        body, name="rwkv_post", grid=(t // tm,),
        out_shape=jax.ShapeDtypeStruct((t, width), BF16),
        in_specs=[tok, tok, tok, tok, pl.BlockSpec((tm, width), lambda i: (i, g_col)),
                  full(gn_g), full(gn_b), full(r_k), full(e), full(et)],
        out_specs=tok,
        compiler_params=_params("parallel"),
    )(ys, r, kx, v, p, gn_g, gn_b, r_k, e, et)


def _rwkv_post_bwd(ys, r, kx, v, p, g_col, gn_g, gn_b, r_k, e, et, dy, dy_col):
    t, width = ys.shape
    tm = 128

    def body(ys_ref, r_ref, kx_ref, v_ref, g_ref, gg_ref, gb_ref, rk_ref, e_ref, et_ref, dy_ref,
             dys_out, dr_out, dkx_out, dv_out, dg_out, dgg_out, dgb_out, drk_out):
        @pl.when(pl.program_id(0) == 0)
        def _():
            for ref in (dgg_out, dgb_out, drk_out):
                ref[...] = jnp.zeros_like(ref)

        ev, etv = e_ref[...], et_ref[...]
        _, vjp = jax.vjp(lambda *a: _post_math(*a, ev, etv), ys_ref[...], r_ref[...], kx_ref[...], v_ref[...],
                         g_ref[...], gg_ref[...], gb_ref[...], rk_ref[...])
        dys, dr, dkx, dv, dg, dgg, dgb, drk = vjp(dy_ref[...])
        dys_out[...] = dys
        dr_out[...] = dr
        dkx_out[...] = dkx
        dv_out[...] = dv
        dg_out[...] = dg.astype(BF16)
        dgg_out[...] += dgg
        dgb_out[...] += dgb
        drk_out[...] += drk

    tok = pl.BlockSpec((tm, width), lambda i: (i, 0))
    full = lambda a: pl.BlockSpec(a.shape, lambda i: (0,) * a.ndim)
    big = jax.ShapeDtypeStruct((t, width), F32)
    vec = jax.ShapeDtypeStruct((1, width), F32)
    vspec = pl.BlockSpec((1, width), lambda i: (0, 0))
    return pl.pallas_call(
        body, name="rwkv_post_bwd", grid=(t // tm,),
        out_shape=(big, big, big, big, jax.ShapeDtypeStruct((t, width), BF16), vec, vec, vec),
        in_specs=[tok, tok, tok, tok, pl.BlockSpec((tm, width), lambda i: (i, g_col)),
                  full(gn_g), full(gn_b), full(r_k), full(e), full(et),
                  pl.BlockSpec((tm, width), lambda i: (i, dy_col))],
        out_specs=(tok, tok, tok, tok, tok, vspec, vspec, vspec),
        compiler_params=_params("arbitrary"),
    )(ys, r, kx, v, p, gn_g, gn_b, r_k, e, et, dy)


def _to_scan_k(a, bl, seq):
    h = a.shape[1] // RWKV_HEAD
    a = a.reshape(bl, seq, h, RWKV_HEAD).transpose(1, 3, 0, 2).reshape(seq, RWKV_HEAD, bl * h)
    return jnp.concatenate([a, a], axis=-1)


def _to_scan_v(a, bl, seq):
    h = a.shape[1] // RWKV_HEAD
    half = RWKV_HEAD // 2
    return a.reshape(bl, seq, h, 2, half).transpose(1, 4, 3, 0, 2).reshape(seq, half, 2 * bl * h)


def _from_scan_k(a, bl, seq):
    h = a.shape[2] // (2 * bl)
    a = a[:, :, :bl * h].reshape(seq, RWKV_HEAD, bl, h).transpose(2, 0, 3, 1)
    return a.reshape(bl * seq, h * RWKV_HEAD)


def _from_scan_v(a, bl, seq):
    half = RWKV_HEAD // 2
    h = a.shape[2] // (2 * bl)
    a = a.reshape(seq, half, 2, bl, h).transpose(3, 0, 4, 2, 1)
    return a.reshape(bl * seq, h * RWKV_HEAD)


def _scan_fwd(kk, w, b, kx, r, v, gathers=()):
    seq, nk, lanes = kk.shape
    nv = v.shape[1]
    tt = SCAN_STEPS_PER_BLOCK
    nblk = seq // tt
    ng = len(gathers)

    def body(*refs):
        kk_ref, w_ref, b_ref, kx_ref, r_ref, v_ref = refs[:6]
        g_src = refs[6:6 + ng]
        y_ref, st_ref, sa_ref = refs[6 + ng:9 + ng]
        g_out = refs[9 + ng:9 + 2 * ng]
        s_ref = refs[9 + 2 * ng]
        pid = pl.program_id(0)

        @pl.when(pid == 0)
        def _():
            s_ref[...] = jnp.zeros_like(s_ref)

        if ng:
            send_sems, recv_sems, local_sems = refs[10 + 2 * ng:]
            phases = [_gather_phases(g_src[n], g_out[n], send_sems, recv_sems, local_sems, n) for n in range(ng)]
            for k, at in enumerate((0, (3 * nblk) // 4, nblk - 1)):
                @pl.when(pid == at)
                def _(k=k):
                    for ph in phases:
                        ph[k]()

        def step(i, carry):
            kkv, wv, bv, kxv, rv = kk_ref[i], w_ref[i], b_ref[i], kx_ref[i], r_ref[i]
            for j in range(nv):
                s_old = s_ref[j]
                sa = -jnp.sum(s_old * kkv, axis=0, keepdims=True)
                s_new = s_old * wv + sa * bv + v_ref[i, j:j + 1, :] * kxv
                s_ref[j] = s_new
                st_ref[i, j] = s_new
                sa_ref[i, j:j + 1, :] = sa
                y_ref[i, j:j + 1, :] = jnp.sum(s_new * rv, axis=0, keepdims=True)
            return carry

        lax.fori_loop(0, tt, step, 0)

    krow = pl.BlockSpec((tt, nk, lanes), lambda i: (i, 0, 0))
    vrow = pl.BlockSpec((tt, nv, lanes), lambda i: (i, 0, 0))
    vshape = jax.ShapeDtypeStruct((seq, nv, lanes), F32)
    return pl.pallas_call(
        body, name="rwkv_scan_fwd", grid=(nblk,),
        out_shape=(vshape, jax.ShapeDtypeStruct((seq, nv, nk, lanes), F32), vshape,
                   *[_gathered_shape(g) for g in gathers]),
        in_specs=[krow, krow, krow, krow, krow, vrow] + [_HBM] * ng,
        out_specs=(vrow, pl.BlockSpec((tt, nv, nk, lanes), lambda i: (i, 0, 0, 0)), vrow, *[_HBM] * ng),
        scratch_shapes=[pltpu.VMEM((nv, nk, lanes), F32)] + (_gather_scratch(ng) if ng else []),
        compiler_params=_params("arbitrary"),
    )(kk, w, b, kx, r, v, *gathers)


def _scan_bwd(kk, w, b, kx, r, v, states, sa_all, dy, scatters=()):
    seq, nk, lanes = kk.shape
    nv = v.shape[1]
    tt = SCAN_STEPS_PER_BLOCK
    nblk = seq // tt
    nsc = len(scatters)

    def both_halves(a):
        return a + pltpu.roll(a, lanes // 2, 1)

    def body(*refs):
        kk_ref, w_ref, b_ref, kx_ref, r_ref, v_ref, st_ref, before_ref, sa_ref, dy_ref = refs[:10]
        dkk_ref, dw_ref, db_ref, dkx_ref, dr_ref, dv_ref = refs[10 + nsc:16 + nsc]
        g_ref = refs[16 + 2 * nsc]

        if nsc:
            send_sems, recv_sems, local_sems = refs[17 + 2 * nsc:]
            phases = [_scatter_phases(refs[10 + n], refs[16 + nsc + n], send_sems, recv_sems, local_sems, n)
                      for n in range(nsc)]
            for k, at in enumerate((0, nblk - 1)):
                @pl.when(pl.program_id(0) == at)
                def _(k=k):
                    for ph in phases:
                        ph[k]()

        @pl.when(pl.program_id(0) == 0)
        def _():
            g_ref[...] = jnp.zeros_like(g_ref)

        def one_step(i, state_before):
            kkv, wv, bv, kxv, rv = kk_ref[i], w_ref[i], b_ref[i], kx_ref[i], r_ref[i]
            zero = jnp.zeros((nk, lanes), F32)
            a_r, a_w, a_b, a_kx, a_kk = zero, zero, zero, zero, zero
            for j in range(nv):
                s_old = state_before(j)
                s_new = st_ref[i, j]
                vv = v_ref[i, j:j + 1, :]
                dyv = dy_ref[i, j:j + 1, :]
                sa = sa_ref[i, j:j + 1, :]
                g = g_ref[j] + dyv * rv
                a_r = a_r + s_new * dyv
                a_w = a_w + g * s_old
                dsa = jnp.sum(g * bv, axis=0, keepdims=True)
                a_b = a_b + g * sa
                dv_ref[i, j:j + 1, :] = jnp.sum(g * kxv, axis=0, keepdims=True)
                a_kx = a_kx + g * vv
                a_kk = a_kk + s_old * dsa
                g_ref[j] = g * wv - dsa * kkv
            dr_ref[i] = both_halves(a_r)
            dw_ref[i] = both_halves(a_w)
            db_ref[i] = both_halves(a_b)
            dkx_ref[i] = both_halves(a_kx)
            dkk_ref[i] = -both_halves(a_kk)

        def step(n, carry):
            i = tt - 1 - n
            one_step(i, lambda j: st_ref[i - 1, j])
            return carry

        lax.fori_loop(0, tt - 1, step, 0)
        at_start = pl.program_id(0) == nblk - 1
        one_step(0, lambda j: jnp.where(at_start, 0.0, before_ref[0, j]))

    rev = lambda i: nblk - 1 - i
    krow = pl.BlockSpec((tt, nk, lanes), lambda i: (rev(i), 0, 0))
    vrow = pl.BlockSpec((tt, nv, lanes), lambda i: (rev(i), 0, 0))
    kshape = jax.ShapeDtypeStruct((seq, nk, lanes), F32)
    return pl.pallas_call(
        body, name="rwkv_scan_bwd", grid=(nblk,),
        out_shape=(kshape, kshape, kshape, kshape, kshape, jax.ShapeDtypeStruct((seq, nv, lanes), F32),
                   *[jax.ShapeDtypeStruct(s.shape, s.dtype) for s in scatters]),
        in_specs=[krow, krow, krow, krow, krow, vrow,
                  pl.BlockSpec((tt, nv, nk, lanes), lambda i: (rev(i), 0, 0, 0)),
                  pl.BlockSpec((1, nv, nk, lanes), lambda i: (jnp.maximum(rev(i) * tt - 1, 0), 0, 0, 0)),
                  vrow, vrow] + [_HBM] * nsc,
        out_specs=(krow, krow, krow, krow, krow, vrow, *[_HBM] * nsc),
        scratch_shapes=[pltpu.VMEM((nv, nk, lanes), F32)] + (_scatter_scratch(nsc) if nsc else []),
        compiler_params=_params("arbitrary"),
    )(kk, w, b, kx, r, v, states, states, sa_all, dy, *scatters)


_NT = (((1,), (1,)), ((), ()))
_TN = (((0,), (0,)), ((), ()))
SB_SCALE = 1.0 / (SB_HEAD ** 0.5)


def _split_dot(a, b):
    hi = a.astype(BF16)
    lo = (a - hi.astype(F32)).astype(BF16)
    return jnp.dot(hi, b, preferred_element_type=F32) + jnp.dot(lo, b, preferred_element_type=F32)


def _sb_masks():
    blk = SB_BLOCK
    row = lax.broadcasted_iota(jnp.int32, (blk, blk), 0)
    col = lax.broadcasted_iota(jnp.int32, (blk, blk), 1)
    ones = jnp.ones((blk, blk), BF16)
    fwd = jnp.concatenate([(row > col).astype(BF16), ones], axis=1)
    bwd = jnp.concatenate([(col > row).astype(BF16), ones], axis=1)
    return row, col, fwd, bwd


def _split_dots(xs, b):
    his = [a.astype(BF16) for a in xs]
    los = [(a - hi.astype(F32)).astype(BF16) for a, hi in zip(xs, his)]
    tops = [jnp.dot(hi, b, preferred_element_type=F32) for hi in his]
    return [top + jnp.dot(lo, b, preferred_element_type=F32) for top, lo in zip(tops, los)]


SB_HEADS_PER_STEP = 2


def _sb_chains(bl):
    return [(b, slice(hh * SB_HEAD, (hh + 1) * SB_HEAD)) for b in range(bl) for hh in range(SB_HEADS_PER_STEP)]


def _sb_specs(bl, seq, cols, n_heads):
    hp = SB_HEADS_PER_STEP
    assert n_heads % hp == 0 and all(c % hp == 0 for c in cols)
    qspec = lambda col: pl.BlockSpec((bl, SB_BLOCK, hp * SB_HEAD), lambda h, i: (0, i, col // hp + h))
    kspec = lambda col: pl.BlockSpec((bl, seq, hp * SB_HEAD), lambda h, i: (0, 0, col // hp + h))
    return qspec, kspec


def _sb_fwd(p, bl, seq, cols, n_heads):
    t = p.shape[0]
    nq = seq // SB_BLOCK
    q_col, k_col, v_col, g_col = cols
    blk = SB_BLOCK
    chains = _sb_chains(bl)

    def body(q_ref, k_ref, v_ref, g_ref, out_ref, o_ref, tot_ref):
        qi = pl.program_id(1)
        row, col, mix, _ = _sb_masks()
        qbs = [q_ref[b, :, ln].astype(BF16) for b, ln in chains]

        def step(n, carry):
            j = qi - n
            rows = pl.ds(pl.multiple_of(j * blk, blk), blk)
            causal = (j * blk + col) < (qi * blk + row)
            zs = [lax.dot_general(qb, k_ref[b, rows, ln].astype(BF16), _NT, preferred_element_type=F32) * SB_SCALE
                  for (b, ln), qb in zip(chains, qbs)]
            lszs = [_log_sigmoid(z) for z in zs]
            boths = _split_dots([jnp.where(causal, lsz - z, 0.0) for lsz, z in zip(lszs, zs)], mix)
            atts = [jnp.where(causal, jnp.exp(lsz + both[:, :blk] + later), 0.0).astype(BF16)
                    for lsz, both, (_, later) in zip(lszs, boths, carry)]
            outs = [out + jnp.dot(att, v_ref[b, rows, ln].astype(BF16), preferred_element_type=F32)
                    for (b, ln), att, (out, _) in zip(chains, atts, carry)]
            return tuple((out, later + both[:, blk:]) for out, both, (_, later) in zip(outs, boths, carry))

        zero = jnp.zeros((blk, SB_HEAD), F32)
        done = lax.fori_loop(0, qi + 1, step, tuple((zero, zero) for _ in chains))
        for (b, ln), (out, total) in zip(chains, done):
            o_ref[b, :, ln] = out
            tot_ref[b, :, ln] = total
            out_ref[b, :, ln] = (out * _silu(g_ref[b, :, ln])).astype(BF16)

    qspec, kspec = _sb_specs(bl, seq, cols, n_heads)
    width = n_heads * SB_HEAD
    p3 = p.reshape(bl, seq, p.shape[1])
    f32 = jax.ShapeDtypeStruct((bl, seq, width), F32)
    outs = pl.pallas_call(
        body, name="sb_attn_fwd", grid=(n_heads // SB_HEADS_PER_STEP, nq),
        out_shape=(jax.ShapeDtypeStruct((bl, seq, width), BF16), f32, f32),
        in_specs=[qspec(q_col), kspec(k_col), kspec(v_col), qspec(g_col)],
        out_specs=(qspec(0), qspec(0), qspec(0)),
        compiler_params=_params("parallel", "arbitrary"),
    )(p3, p3, p3, p3)
    return tuple(a.reshape(t, width) for a in outs)


def _sb_bwd(p, bl, seq, cols, n_heads, dy, dy_col, o, tot, swaps=()):
    t = p.shape[0]
    nq = seq // SB_BLOCK
    q_col, k_col, v_col, g_col = cols
    blk = SB_BLOCK
    ns = len(swaps)
    n_groups = n_heads // SB_HEADS_PER_STEP
    chains = _sb_chains(bl)

    def body(*refs):
        q_ref, k_ref, v_ref, g_ref, dy_ref, o_ref, tot_ref = refs[:7]
        dq_out, dk_out, dv_out, dg_out = refs[7 + ns:11 + ns]
        dk_acc, dv_acc = refs[11 + 2 * ns:13 + 2 * ns]
        group, qi = pl.program_id(0), pl.program_id(1)

        if ns:
            send_sems, recv_sems = refs[13 + 2 * ns:]
            phases = [_swap_phases(refs[7 + n], refs[11 + ns + n], send_sems, recv_sems, n) for n in range(ns)]
            for k, at in enumerate(((0, 0), (n_groups - 1, nq - 1))):
                @pl.when((group == at[0]) & (qi == at[1]))
                def _(k=k):
                    for ph in phases:
                        ph[k]()

        @pl.when(qi == 0)
        def _():
            dk_acc[...] = jnp.zeros_like(dk_acc)
            dv_acc[...] = jnp.zeros_like(dv_acc)

        row, col, mix, mix_t = _sb_masks()
        qbs, dobs, totals = [], [], []
        for b, ln in chains:
            gate = g_ref[b, :, ln]
            sg = jax.nn.sigmoid(gate)
            dyv = dy_ref[b, :, ln]
            dg_out[b, :, ln] = (dyv * o_ref[b, :, ln] * (sg * (1.0 + gate * (1.0 - sg)))).astype(BF16)
            dobs.append((dyv * (gate * sg)).astype(BF16))
            qbs.append(q_ref[b, :, ln].astype(BF16))
            totals.append(tot_ref[b, :, ln])

        def step(j, carry):
            rows = pl.ds(pl.multiple_of(j * blk, blk), blk)
            causal = (j * blk + col) < (qi * blk + row)
            kbs = [k_ref[b, rows, ln].astype(BF16) for b, ln in chains]
            zs = [lax.dot_general(qb, kb, _NT, preferred_element_type=F32) * SB_SCALE for qb, kb in zip(qbs, kbs)]
            datts = [lax.dot_general(dob, v_ref[b, rows, ln].astype(BF16), _NT, preferred_element_type=F32)
                     for (b, ln), dob in zip(chains, dobs)]
            lszs = [_log_sigmoid(z) for z in zs]
            boths = _split_dots([jnp.where(causal, lsz - z, 0.0) for lsz, z in zip(lszs, zs)], mix)
            seens = [seen + both[:, blk:] for both, (_, seen, _) in zip(boths, carry)]
            atts = [jnp.where(causal, jnp.exp(lsz + both[:, :blk] + (total - seen)), 0.0)
                    for lsz, both, total, seen in zip(lszs, boths, totals, seens)]
            dls = [att * datt for att, datt in zip(atts, datts)]
            for (b, ln), att, dob in zip(chains, atts, dobs):
                dv_acc[b, rows, ln] += lax.dot_general(att.astype(BF16), dob, _TN, preferred_element_type=F32)
            boths_t = _split_dots(dls, mix_t)
            dkeeps = [jnp.where(causal, both_t[:, :blk] + dl_before, 0.0)
                      for both_t, (_, _, dl_before) in zip(boths_t, carry)]
            dzbs = [(((dl + dkeep) * jax.nn.sigmoid(-z) - dkeep) * SB_SCALE).astype(BF16)
                    for dl, dkeep, z in zip(dls, dkeeps, zs)]
            dqs = [dq + jnp.dot(dzb, kb, preferred_element_type=F32) for dzb, kb, (dq, _, _) in zip(dzbs, kbs, carry)]
            for (b, ln), dzb, qb in zip(chains, dzbs, qbs):
                dk_acc[b, rows, ln] += lax.dot_general(dzb, qb, _TN, preferred_element_type=F32)
            return tuple((dq, seen, dl_before + both_t[:, blk:])
                         for dq, seen, both_t, (_, _, dl_before) in zip(dqs, seens, boths_t, carry))

        zero = jnp.zeros((blk, SB_HEAD), F32)
        done = lax.fori_loop(0, qi + 1, step, tuple((zero, zero, zero) for _ in chains))
        for (b, ln), (dq, _, _) in zip(chains, done):
            dq_out[b, :, ln] = dq.astype(BF16)

        @pl.when(qi == nq - 1)
        def _():
            dk_out[...] = dk_acc[...].astype(BF16)
            dv_out[...] = dv_acc[...].astype(BF16)

    qspec, kspec = _sb_specs(bl, seq, cols + (dy_col,), n_heads)
    width = n_heads * SB_HEAD
    shape = jax.ShapeDtypeStruct((bl, seq, width), BF16)
    as3 = lambda a: a.reshape(bl, seq, a.shape[1])
    p3 = as3(p)
    acc = pltpu.VMEM((bl, seq, SB_HEADS_PER_STEP * SB_HEAD), F32)
    outs = pl.pallas_call(
        body, name="sb_attn_bwd", grid=(n_groups, nq),
        out_shape=(shape, shape, shape, shape, *[_swapped_shape(s) for s in swaps]),
        in_specs=[qspec(q_col), kspec(k_col), kspec(v_col), qspec(g_col), qspec(dy_col), qspec(0), qspec(0)]
        + [_HBM] * ns,
        out_specs=(qspec(0), kspec(0), kspec(0), qspec(0), *[_HBM] * ns),
        scratch_shapes=[acc, acc] + (_swap_scratch(ns) if ns else []),
        compiler_params=_params("arbitrary", "arbitrary"),
    )(p3, p3, p3, p3, as3(dy), as3(o), as3(tot), *swaps)
    return tuple(a.reshape(t, width) for a in outs[:4]) + tuple(outs[4:])


def _gelu(x):
    return 0.5 * x * (1.0 + lax.erf(x * (2.0 ** -0.5)))


def _sgu_math(us, vs, gs, ln_g, ln_b, ws, bs):
    width = sum(v.shape[1] for v in vs)
    vg = [_gelu(v) for v in vs]
    mu = sum(jnp.sum(v, axis=1, keepdims=True) for v in vg) * (1.0 / width)
    dl = [v - mu for v in vg]
    var = sum(jnp.sum(d * d, axis=1, keepdims=True) for d in dl) * (1.0 / width)
    rstd = lax.rsqrt(var + LN_EPS)
    n = ws[0].shape[0]
    tri = lax.broadcasted_iota(jnp.int32, (n, n), 0) >= lax.broadcasted_iota(jnp.int32, (n, n), 1)
    outs = []
    for i in range(len(vs)):
        vn = dl[i] * rstd * ln_g[i] + ln_b[i]
        mixed = _dot32(jnp.where(tri, ws[i], 0.0), vn) + bs[i]
        outs.append(_gelu(us[i]) * mixed * _silu(gs[i]))
    return outs


def _sgu_load(p_ref, lng_ref, lnb_ref, ws_ref, bs_ref, width):
    gd = width // SGU_GROUPS
    grp = lambda ref, base, i: ref[:, base + i * gd:base + (i + 1) * gd]
    idx = range(SGU_GROUPS)
    return ([grp(p_ref, 0, i) for i in idx], [grp(p_ref, width, i) for i in idx],
            [grp(p_ref, 2 * width, i) for i in idx], [grp(lng_ref, 0, i) for i in idx],
            [grp(lnb_ref, 0, i) for i in idx], [ws_ref[i] for i in idx], [bs_ref[i] for i in idx])


def _sgu_fwd(p, ln_g, ln_b, w_s, b_s):
    t = p.shape[0]
    width = p.shape[1] // 3
    gd = width // SGU_GROUPS
    tm = SGU_CHUNK

    def body(p_ref, lng_ref, lnb_ref, ws_ref, bs_ref, y_ref):
        outs = _sgu_math(*_sgu_load(p_ref, lng_ref, lnb_ref, ws_ref, bs_ref, width))
        for i in range(SGU_GROUPS):
            y_ref[:, i * gd:(i + 1) * gd] = outs[i].astype(BF16)

    full = lambda a: pl.BlockSpec(a.shape, lambda i: (0,) * a.ndim)
    return pl.pallas_call(
        body, name="sgu_fwd", grid=(t // tm,),
        out_shape=jax.ShapeDtypeStruct((t, width), BF16),
        in_specs=[pl.BlockSpec((tm, 3 * width), lambda i: (i, 0)), full(ln_g), full(ln_b), full(w_s), full(b_s)],
        out_specs=pl.BlockSpec((tm, width), lambda i: (i, 0)),
        compiler_params=_params("parallel"),
    )(p, ln_g, ln_b, w_s, b_s)


def _sgu_bwd(p, ln_g, ln_b, w_s, b_s, dy):
    t = p.shape[0]
    width = p.shape[1] // 3
    gd = width // SGU_GROUPS
    tm = SGU_CHUNK

    def body(p_ref, lng_ref, lnb_ref, ws_ref, bs_ref, dy_ref, dp_out, dlng_out, dlnb_out, dws_out, dbs_out):
        @pl.when(pl.program_id(0) == 0)
        def _():
            for ref in (dlng_out, dlnb_out, dws_out, dbs_out):
                ref[...] = jnp.zeros_like(ref)

        _, vjp = jax.vjp(_sgu_math, *_sgu_load(p_ref, lng_ref, lnb_ref, ws_ref, bs_ref, width))
        dus, dvs, dgs, dlng, dlnb, dws, dbs = vjp(
            [dy_ref[:, i * gd:(i + 1) * gd] for i in range(SGU_GROUPS)])
        for i in range(SGU_GROUPS):
            cols = slice(i * gd, (i + 1) * gd)
            dp_out[:, i * gd:(i + 1) * gd] = dus[i].astype(BF16)
            dp_out[:, width + i * gd:width + (i + 1) * gd] = dvs[i].astype(BF16)
            dp_out[:, 2 * width + i * gd:2 * width + (i + 1) * gd] = dgs[i].astype(BF16)
            dlng_out[:, cols] += dlng[i]
            dlnb_out[:, cols] += dlnb[i]
            dws_out[i] += dws[i]
            dbs_out[i] += dbs[i]

    full = lambda a: pl.BlockSpec(a.shape, lambda i: (0,) * a.ndim)
    like = lambda a: jax.ShapeDtypeStruct(a.shape, F32)
    return pl.pallas_call(
        body, name="sgu_bwd", grid=(t // tm,),
        out_shape=(jax.ShapeDtypeStruct((t, 3 * width), BF16), like(ln_g), like(ln_b), like(w_s), like(b_s)),
        in_specs=[pl.BlockSpec((tm, 3 * width), lambda i: (i, 0)), full(ln_g), full(ln_b), full(w_s), full(b_s),
                  pl.BlockSpec((tm, width), lambda i: (i, 0))],
        out_specs=(pl.BlockSpec((tm, 3 * width), lambda i: (i, 0)), full(ln_g), full(ln_b), full(w_s), full(b_s)),
        compiler_params=_params("arbitrary"),
    )(p, ln_g, ln_b, w_s, b_s, dy)


def _sum_slabs(parts, name):
    n_parts, rows, cols = parts.shape
    tr = _tile(rows, max(16, (1 << 18) // cols), 16)

    def body(p_ref, o_ref):
        acc = p_ref[0].astype(F32)
        for d in range(1, n_parts):
            acc = acc + p_ref[d].astype(F32)
        o_ref[...] = acc

    return pl.pallas_call(
        body, name=name, grid=(rows // tr,),
        out_shape=jax.ShapeDtypeStruct((rows, cols), F32),
        in_specs=[pl.BlockSpec((n_parts, tr, cols), lambda i: (0, i, 0))],
        out_specs=pl.BlockSpec((tr, cols), lambda i: (i, 0)),
        compiler_params=_params("parallel"),
    )(parts)


def _adamw(w, g, m, v, name):
    rows, cols = w.shape
    tr = _tile(rows, max(8, (1 << 18) // cols), 8)

    def body(w_ref, g_ref, m_ref, v_ref, d_out, m_out, v_out):
        gv = g_ref[...]
        mn = ADAM_B1 * m_ref[...] + (1.0 - ADAM_B1) * gv
        vn = ADAM_B2 * v_ref[...] + (1.0 - ADAM_B2) * (gv * gv)
        m_hat = mn / (1.0 - ADAM_B1 ** ADAM_STEP)
        v_hat = vn / (1.0 - ADAM_B2 ** ADAM_STEP)
        d_out[...] = -ADAM_LR * (m_hat / (jnp.sqrt(v_hat) + ADAM_EPS) + ADAM_WD * w_ref[...])
        m_out[...] = mn
        v_out[...] = vn

    blk = pl.BlockSpec((tr, cols), lambda i: (i, 0))
    shape = jax.ShapeDtypeStruct((rows, cols), F32)
    return pl.pallas_call(
        body, name=name, grid=(rows // tr,),
        out_shape=(shape, shape, shape),
        in_specs=[blk, blk, blk, blk], out_specs=(blk, blk, blk),
        compiler_params=_params("parallel"),
    )(w, g, m, v)


PACK_COLS = 1024


def _pack(arrays):
    flat = jnp.concatenate([a.reshape(-1).astype(F32) for a in arrays])
    rows = -(-flat.shape[0] // (8 * PACK_COLS)) * 8
    return jnp.pad(flat, (0, rows * PACK_COLS - flat.shape[0])).reshape(rows, PACK_COLS)


def _unpack(packed, shapes):
    flat = packed.reshape(-1)
    out, at = [], 0
    for s in shapes:
        n = 1
        for d in s:
            n *= d
        out.append(flat[at:at + n].reshape(s))
        at += n
    return out


def kernel(x, norm_g, final_norm_g, e_w_in, e_shift_mu, e_w_decay_up, e_w0, e_a_up, e_a0, e_k_k, e_k_a, e_r_k, e_gn_g, e_gn_b, e_w_out, o_w_in, o_ln_g, o_ln_b, o_w_s, o_b_s, o_w_out, loss_target, m_norm_g, m_final_norm_g, m_e_w_in, m_e_shift_mu, m_e_w_decay_up, m_e_w0, m_e_a_up, m_e_a0, m_e_k_k, m_e_k_a, m_e_r_k, m_e_gn_g, m_e_gn_b, m_e_w_out, m_o_w_in, m_o_ln_g, m_o_ln_b, m_o_w_s, m_o_b_s, m_o_w_out, v_norm_g, v_final_norm_g, v_e_w_in, v_e_shift_mu, v_e_w_decay_up, v_e_w0, v_e_a_up, v_e_a0, v_e_k_k, v_e_k_a, v_e_r_k, v_e_gn_g, v_e_gn_b, v_e_w_out, v_o_w_in, v_o_ln_g, v_o_ln_b, v_o_w_s, v_o_b_s, v_o_w_out):
    weights = dict(norm_g=norm_g, final_norm_g=final_norm_g, e_w_in=e_w_in, e_shift_mu=e_shift_mu,
                   e_w_decay_up=e_w_decay_up, e_w0=e_w0, e_a_up=e_a_up, e_a0=e_a0, e_k_k=e_k_k, e_k_a=e_k_a,
                   e_r_k=e_r_k, e_gn_g=e_gn_g, e_gn_b=e_gn_b, e_w_out=e_w_out, o_w_in=o_w_in, o_ln_g=o_ln_g,
                   o_ln_b=o_ln_b, o_w_s=o_w_s, o_b_s=o_b_s, o_w_out=o_w_out)
    mom1 = dict(norm_g=m_norm_g, final_norm_g=m_final_norm_g, e_w_in=m_e_w_in, e_shift_mu=m_e_shift_mu,
                e_w_decay_up=m_e_w_decay_up, e_w0=m_e_w0, e_a_up=m_e_a_up, e_a0=m_e_a0, e_k_k=m_e_k_k,
                e_k_a=m_e_k_a, e_r_k=m_e_r_k, e_gn_g=m_e_gn_g, e_gn_b=m_e_gn_b, e_w_out=m_e_w_out,
                o_w_in=m_o_w_in, o_ln_g=m_o_ln_g, o_ln_b=m_o_ln_b, o_w_s=m_o_w_s, o_b_s=m_o_b_s,
                o_w_out=m_o_w_out)
    mom2 = dict(norm_g=v_norm_g, final_norm_g=v_final_norm_g, e_w_in=v_e_w_in, e_shift_mu=v_e_shift_mu,
                e_w_decay_up=v_e_w_decay_up, e_w0=v_e_w0, e_a_up=v_e_a_up, e_a0=v_e_a0, e_k_k=v_e_k_k,
                e_k_a=v_e_k_a, e_r_k=v_e_r_k, e_gn_g=v_e_gn_g, e_gn_b=v_e_gn_b, e_w_out=v_e_w_out,
                o_w_in=v_o_w_in, o_ln_g=v_o_ln_g, o_ln_b=v_o_ln_b, o_w_s=v_o_w_s, o_b_s=v_o_b_s,
                o_w_out=v_o_w_out)
    names = list(weights)
    big = ("e_w_in", "e_w_out", "o_w_in", "o_w_out")

    bl, seq, d = x.shape
    t = bl * seq
    width = e_w0.shape[1]
    lora = e_w_decay_up.shape[1]
    n_sb = width // SB_HEAD
    me = 4 * lax.axis_index("x") + 2 * lax.axis_index("y") + lax.axis_index("c")

    e_win_t = _all_gather(e_w_in[0].T.astype(BF16), "gather_e_w_in").reshape(-1, d)
    later_shards = (e_w_out[0].astype(BF16), o_w_in[0].T.astype(BF16), o_w_out[0].astype(BF16))
    sharded_small = ("e_w_decay_up", "e_a_up", "o_ln_g", "o_ln_b")
    small_shapes = [weights[n][0].shape for n in sharded_small]
    got = _all_gather(_pack([weights[n][0] for n in sharded_small]), "gather_small")
    per_dev = [_unpack(got[dev], small_shapes) for dev in range(N_DEV)]
    wd, wa, ln_g, ln_b = [jnp.concatenate([per_dev[dev][i] for dev in range(N_DEV)], axis=-1).reshape(
        small_shapes[i][:-1] + (-1,)) for i in range(4)]
    ln_g, ln_b = ln_g.reshape(1, -1), ln_b.reshape(1, -1)
    e_ind, e_ind_t = _head_indicator(width)
    b_s3 = o_b_s[0][:, :, None]

    x2d = x.reshape(t, d)
    target = loss_target.reshape(t, d)
    cols_rwkv = 3 * width + 2 * lora
    assert cols_rwkv % LANES == 0 and width % LANES == 0
    sb0 = (cols_rwkv + width) // SB_HEAD
    sb_cols = (sb0, sb0 + n_sb, sb0 + 2 * n_sb, sb0 + 3 * n_sb)

    h0 = _rms_fwd(x2d, norm_g[0:1], "rms0_fwd")
    p = _matmul(h0, e_win_t, "nt", F32, "e_in_fwd", tn=1280)
    g_rwkv = p[:, cols_rwkv:cols_rwkv + width]
    r, w, kx, v, kk, b = _rwkv_prep(p, seq, (width, lora), e_shift_mu, wd, e_w0, wa, e_a0, e_k_k, e_k_a,
                                    e_ind, e_ind_t)
    sk = [_to_scan_k(a, bl, seq) for a in (kk, w, b, kx, r)]
    sv = _to_scan_v(v, bl, seq)
    ys_scan, states, sa_all, e_wout, o_win_t, o_wout = _scan_fwd(*sk, sv, gathers=later_shards)
    e_wout, o_win_t, o_wout = (a.reshape(-1, d) for a in (e_wout, o_win_t, o_wout))
    ys = _from_scan_v(ys_scan, bl, seq)
    ya = _rwkv_post(ys, r, kx, v, g_rwkv, 0, e_gn_g, e_gn_b, e_r_k, e_ind, e_ind_t)
    yb, sb_o, sb_tot = _sb_fwd(p, bl, seq, sb_cols, n_sb)
    y = jnp.concatenate([ya, yb], axis=1)
    x1 = _matmul(y, e_wout, "nn", F32, "e_out_fwd", res=x2d)
    h1 = _rms_fwd(x1, norm_g[1:2], "rms1_fwd")
    p2 = _matmul(h1, o_win_t, "nt", F32, "o_in_fwd")
    y2 = _sgu_fwd(p2, ln_g, ln_b, o_w_s[0], b_s3)
    x2 = _matmul(y2, o_wout, "nn", F32, "o_out_fwd", res=x1)
    dx2, d_final_g, loss_part = _final_loss(x2, final_norm_g.reshape(1, d), target, "final_loss")

    dy2 = _matmul(dx2, o_wout, "nt", F32, "o_out_bwd_x")
    d_o_wout = _matmul(y2, dx2, "tn", F32, "o_out_bwd_w")
    dp2, d_ln_g, d_ln_b, d_w_s, d_b_s3 = _sgu_bwd(p2, ln_g, ln_b, o_w_s[0], b_s3, dy2)
    dh1 = _matmul(dp2, o_win_t, "nn", F32, "o_in_bwd_x", tk=1536)
    d_o_win_t = _matmul(dp2, h1, "tn", F32, "o_in_bwd_w")
    dx1, d_g1 = _rms_bwd(x1, norm_g[1:2], dh1, dx2, "rms1_bwd")
    dy = _matmul(dx1, e_wout, "nt", F32, "e_out_bwd_x")
    d_e_wout = _matmul(y, dx1, "tn", F32, "e_out_bwd_w")
    core = lax.axis_index("c").astype(jnp.int32).reshape(1)
    by_owner = lambda full: full.reshape((N_CHIPS, 2, full.shape[0] // N_DEV, full.shape[1]))
    early = {"e_w_out": by_owner(d_e_wout), "o_w_in": by_owner(d_o_win_t), "o_w_out": by_owner(d_o_wout)}
    dq, dk, dv_sb, dg_sb, *swapped = _sb_bwd(p, bl, seq, sb_cols, n_sb, dy, n_sb, sb_o, sb_tot,
                                             swaps=tuple(early.values()))
    partials = [_pair_sum(full, got, core, "pairsum_" + n) for (n, full), got in zip(early.items(), swapped)]
    dys, dr1, dkx1, dv1, dg_rwkv, d_gn_g, d_gn_b, d_r_k = _rwkv_post_bwd(
        ys, r, kx, v, g_rwkv, 0, e_gn_g, e_gn_b, e_r_k, e_ind, e_ind_t, dy, 0)
    dkk_s, dw_s, db_s, dkx_s, dr_s, dv_s, *landed = _scan_bwd(*sk, sv, states, sa_all, _to_scan_v(dys, bl, seq),
                                                              scatters=tuple(partials))
    early_sums = {n: _sum_slabs(parts, "sum_" + n) for n, parts in zip(early, landed)}
    dkk, dw, db, dkx2, dr2 = [_from_scan_k(a, bl, seq) for a in (dkk_s, dw_s, db_s, dkx_s, dr_s)]
    dv2 = _from_scan_v(dv_s, bl, seq)
    dp_rwkv, d_mu, d_wd, d_w0, d_wa, d_a0, d_k_k, d_k_a = _rwkv_prep_bwd(
        p, seq, (width, lora), e_shift_mu, wd, e_w0, wa, e_a0, e_k_k, e_k_a, e_ind, e_ind_t,
        dr1 + dr2, dw, dkx1 + dkx2, dv1 + dv2, dkk, db)
    dp = jnp.concatenate([dp_rwkv, dg_rwkv, dq, dk, dv_sb, dg_sb], axis=1)
    dh0 = _matmul(dp, e_win_t, "nn", F32, "e_in_bwd_x", tk=1280)
    d_e_win_t = _matmul(dp, h0, "tn", F32, "e_in_bwd_w", tm=1280)
    grad_x, d_g0 = _rms_bwd(x2d, norm_g[0:1], dh0, dx1, "rms0_bwd")

    last = by_owner(d_e_win_t)
    last = _pair_sum(last, _pair_swap(last, "swap_e_w_in"), core, "pairsum_e_w_in")
    grads = {
        "e_w_in": _sum_slabs(_chip_scatter(last, "scatter_e_w_in"), "sum_e_w_in").T[None],
        "e_w_out": early_sums["e_w_out"][None],
        "o_w_in": early_sums["o_w_in"].T[None],
        "o_w_out": early_sums["o_w_out"][None],
    }
    small_full = {
        "norm_g": jnp.concatenate([d_g0, d_g1], axis=0), "final_norm_g": d_final_g.reshape(-1),
        "e_shift_mu": d_mu, "e_w_decay_up": d_wd[None], "e_w0": d_w0, "e_a_up": d_wa[None], "e_a0": d_a0,
        "e_k_k": d_k_k, "e_k_a": d_k_a, "e_r_k": d_r_k, "e_gn_g": d_gn_g, "e_gn_b": d_gn_b,
        "o_ln_g": d_ln_g, "o_ln_b": d_ln_b, "o_w_s": d_w_s[None], "o_b_s": d_b_s3[:, :, 0][None],
    }
    small = [n for n in names if n not in big]
    parts = _all_gather(_pack([small_full[n] for n in small]), "gather_small_grads")
    totals = _unpack(_sum_slabs(parts, "sum_small_grads"), [small_full[n].shape for n in small])
    for n, g in zip(small, totals):
        if n in sharded_small:
            size = weights[n].shape[-1]
            g = lax.dynamic_slice_in_dim(g, me * size, size, axis=g.ndim - 1)
        grads[n] = g.reshape(weights[n].shape)

    delta, new_m, new_v = {}, {}, {}
    for n in big:
        shp = weights[n].shape
        flat = lambda a: a.reshape(shp[-2], shp[-1])
        dl, mn, vn = _adamw(flat(weights[n]), flat(grads[n]), flat(mom1[n]), flat(mom2[n]), "adamw_" + n)
        delta[n], new_m[n], new_v[n] = dl.reshape(shp), mn.reshape(shp), vn.reshape(shp)
    packed = [_pack([src[n] for n in small]) for src in (weights, grads, mom1, mom2)]
    outs = _adamw(*packed, "adamw_small")
    shapes = [weights[n].shape for n in small]
    for dst, arr in zip((delta, new_m, new_v), outs):
        for n, a in zip(small, _unpack(arr, shapes)):
            dst[n] = a

    loss = lax.psum(loss_part[0, 0], ("x", "y", "c"))
    return (loss, grad_x.reshape(bl, seq, d), *[grads[n] for n in names], *[delta[n] for n in names],
            *[new_m[n] for n in names], *[new_v[n] for n in names])
```

```python
import functools

import jax
import jax.numpy as jnp
from jax import lax
from jax.experimental import pallas as pl
from jax.experimental.pallas import tpu as pltpu

F32 = jnp.float32
BF16 = jnp.bfloat16

N_DEV = 8
RWKV_HEAD = 64
SB_HEAD = 128
SB_BLOCK = 128
SGU_CHUNK = 128
SGU_GROUPS = 16
RMS_EPS = 1e-6
GN_EPS = 64e-5
LN_EPS = 1e-5
L2_EPS = 1e-12
ADAM_LR = 0.001
ADAM_B1 = 0.9
ADAM_B2 = 0.999
ADAM_EPS = 1e-08
ADAM_WD = 0.01
ADAM_STEP = 10

VMEM_LIMIT_V7X = 56 * 1024 * 1024
LANES = 128
SCAN_STEPS_PER_BLOCK = 8


def _params(*sem):
    return pltpu.CompilerParams(dimension_semantics=sem, vmem_limit_bytes=VMEM_LIMIT_V7X)


def _tile(n, target, mult):
    best = None
    d = mult
    while d <= min(n, target):
        if n % d == 0:
            best = d
        d += mult
    return n if best is None else best


def _remote(src, dst, send_sems, recv_sems, k, dev):
    return pltpu.make_async_remote_copy(src_ref=src, dst_ref=dst, send_sem=send_sems.at[k], recv_sem=recv_sems.at[k],
                                        device_id=dev, device_id_type=pl.DeviceIdType.MESH)


_HBM = pl.BlockSpec(memory_space=pl.ANY)


GATHER_COPIES = 7


def _gather_phases(src_ref, out_ref, send_sems, recv_sems, local_sems, n):
    x, y, c = lax.axis_index("x"), lax.axis_index("y"), lax.axis_index("c")
    me, sibling = (x, y, c), (x, y, 1 - c)
    chips = [(1 - x, y), (x, 1 - y), (1 - x, 1 - y)]

    def slot(px, py, pc):
        return out_ref.at[4 * px + 2 * py + pc]

    def copy(k, block, to, own=False):
        return _remote(src_ref if own else slot(*block), slot(*block), send_sems, recv_sems,
                       GATHER_COPIES * n + k, to)

    mine = pltpu.make_async_copy(src_ref, slot(*me), local_sems.at[n])
    first = [copy(0, me, sibling, True)] + [copy(1 + j, me, (*chip, c), True) for j, chip in enumerate(chips)]
    passed = [copy(4 + j, (*chip, c), sibling) for j, chip in enumerate(chips)]

    def start():
        mine.start()
        for cp in first:
            cp.start()

    def relay():
        for j, chip in enumerate(chips):
            copy(1 + j, (*chip, c), me).wait_recv()
            passed[j].start()

    def finish():
        copy(0, sibling, me).wait_recv()
        for j, chip in enumerate(chips):
            copy(4 + j, (*chip, 1 - c), me).wait_recv()
        for cp in first + passed:
            cp.wait_send()
        mine.wait()

    return start, relay, finish


def _gather_scratch(n):
    return [pltpu.SemaphoreType.DMA((GATHER_COPIES * n,)), pltpu.SemaphoreType.DMA((GATHER_COPIES * n,)),
            pltpu.SemaphoreType.DMA((n,))]


def _gathered_shape(src):
    return jax.ShapeDtypeStruct((N_DEV,) + tuple(src.shape), src.dtype)


def _all_gather(src, name):
    def body(src_ref, out_ref, send_sems, recv_sems, local_sems):
        for phase in _gather_phases(src_ref, out_ref, send_sems, recv_sems, local_sems, 0):
            phase()

    return pl.pallas_call(
        body, name=name, out_shape=_gathered_shape(src), in_specs=[_HBM], out_specs=_HBM,
        scratch_shapes=_gather_scratch(1),
    )(src)


N_CHIPS = N_DEV // 2
SCATTER_COPIES = N_CHIPS - 1


def _swap_phases(src_ref, out_ref, send_sems, recv_sems, n):
    x, y, c = lax.axis_index("x"), lax.axis_index("y"), lax.axis_index("c")
    copies = [_remote(src_ref.at[q, 1 - c], out_ref.at[q], send_sems, recv_sems, N_CHIPS * n + q, (x, y, 1 - c))
              for q in range(N_CHIPS)]

    def start():
        for cp in copies:
            cp.start()

    def finish():
        for cp in copies:
            cp.wait_recv()
        for cp in copies:
            cp.wait_send()

    return start, finish


def _swapped_shape(full):
    return jax.ShapeDtypeStruct((N_CHIPS,) + tuple(full.shape[2:]), full.dtype)


def _swap_scratch(n):
    return [pltpu.SemaphoreType.DMA((N_CHIPS * n,)), pltpu.SemaphoreType.DMA((N_CHIPS * n,))]


def _pair_swap(full, name):
    def body(src_ref, out_ref, send_sems, recv_sems):
        for phase in _swap_phases(src_ref, out_ref, send_sems, recv_sems, 0):
            phase()

    return pl.pallas_call(body, name=name, out_shape=_swapped_shape(full), in_specs=[_HBM], out_specs=_HBM,
                          scratch_shapes=_swap_scratch(1))(full)


def _pair_sum(full, got, core, name):
    n_chips, _, rows, cols = full.shape
    tr = _tile(rows, max(16, (1 << 19) // cols), 16)

    def body(core_ref, a_ref, b_ref, o_ref):
        o_ref[...] = (a_ref[...] + b_ref[...]).astype(BF16)

    return pl.pallas_call(
        body, name=name,
        grid_spec=pltpu.PrefetchScalarGridSpec(
            num_scalar_prefetch=1, grid=(n_chips, rows // tr),
            in_specs=[pl.BlockSpec((None, None, tr, cols), lambda q, i, s: (q, s[0], i, 0)),
                      pl.BlockSpec((None, tr, cols), lambda q, i, s: (q, i, 0))],
            out_specs=pl.BlockSpec((None, tr, cols), lambda q, i, s: (q, i, 0))),
        out_shape=jax.ShapeDtypeStruct((n_chips, rows, cols), BF16),
        compiler_params=_params("parallel", "parallel"),
    )(core, full, got)


def _scatter_phases(src_ref, out_ref, send_sems, recv_sems, local_sems, n):
    x, y, c = lax.axis_index("x"), lax.axis_index("y"), lax.axis_index("c")
    here = 2 * x + y
    chips = [(1 - x, y), (x, 1 - y), (1 - x, 1 - y)]
    local = pltpu.make_async_copy(src_ref.at[here], out_ref.at[here], local_sems.at[n])
    sends = [_remote(src_ref.at[2 * px + py], out_ref.at[here], send_sems, recv_sems, SCATTER_COPIES * n + j,
                     (px, py, c)) for j, (px, py) in enumerate(chips)]
    recvs = [_remote(src_ref.at[2 * px + py], out_ref.at[2 * px + py], send_sems, recv_sems,
                     SCATTER_COPIES * n + j, (px, py, c)) for j, (px, py) in enumerate(chips)]

    def start():
        local.start()
        for cp in sends:
            cp.start()

    def finish():
        for cp in recvs:
            cp.wait_recv()
        for cp in sends:
            cp.wait_send()
        local.wait()

    return start, finish


def _scatter_scratch(n):
    return [pltpu.SemaphoreType.DMA((SCATTER_COPIES * n,)), pltpu.SemaphoreType.DMA((SCATTER_COPIES * n,)),
            pltpu.SemaphoreType.DMA((n,))]


def _chip_scatter(parts, name):
    def body(src_ref, out_ref, send_sems, recv_sems, local_sems):
        for phase in _scatter_phases(src_ref, out_ref, send_sems, recv_sems, local_sems, 0):
            phase()

    return pl.pallas_call(body, name=name, out_shape=jax.ShapeDtypeStruct(parts.shape, parts.dtype),
                          in_specs=[_HBM], out_specs=_HBM, scratch_shapes=_scatter_scratch(1))(parts)


def _matmul(a, b, mode, out_dtype, name, res=None, tm=1024, tn=1024, tk=1024):
    if mode == "nn":
        (m, k), (k2, n) = a.shape, b.shape
    elif mode == "nt":
        (m, k), (n, k2) = a.shape, b.shape
    else:
        (k, m), (k2, n) = a.shape, b.shape
    assert k == k2, (a.shape, b.shape, mode)
    tm, tn, tk = _tile(m, tm, 128), _tile(n, tn, 128), _tile(k, tk, 128)
    nk = k // tk
    if mode == "nn":
        a_spec = pl.BlockSpec((tm, tk), lambda i, j, kk: (i, kk))
        b_spec = pl.BlockSpec((tk, tn), lambda i, j, kk: (kk, j))
        dims = (((1,), (0,)), ((), ()))
    elif mode == "nt":
        a_spec = pl.BlockSpec((tm, tk), lambda i, j, kk: (i, kk))
        b_spec = pl.BlockSpec((tn, tk), lambda i, j, kk: (j, kk))
        dims = (((1,), (1,)), ((), ()))
    else:
        a_spec = pl.BlockSpec((tk, tm), lambda i, j, kk: (kk, i))
        b_spec = pl.BlockSpec((tk, tn), lambda i, j, kk: (kk, j))
        dims = (((0,), (0,)), ((), ()))
    o_spec = pl.BlockSpec((tm, tn), lambda i, j, kk: (i, j))
    has_res = res is not None

    def body(*refs):
        if has_res:
            a_ref, b_ref, r_ref, o_ref, acc_ref = refs
        else:
            a_ref, b_ref, o_ref, acc_ref = refs
        kk = pl.program_id(2)

        def product():
            return lax.dot_general(a_ref[...].astype(BF16), b_ref[...].astype(BF16), dims,
                                   preferred_element_type=F32)

        @pl.when(kk == 0)
        def _():
            acc_ref[...] = product()

        @pl.when(kk > 0)
        def _():
            acc_ref[...] += product()

        @pl.when(kk == nk - 1)
        def _():
            out = acc_ref[...]
            if has_res:
                out = out + r_ref[...]
            o_ref[...] = out.astype(out_dtype)

    ins = [a, b] + ([res] if has_res else [])
    specs = [a_spec, b_spec] + ([o_spec] if has_res else [])
    return pl.pallas_call(
        body, name=name, grid=(m // tm, n // tn, nk),
        out_shape=jax.ShapeDtypeStruct((m, n), out_dtype),
        in_specs=specs, out_specs=o_spec,
        scratch_shapes=[pltpu.VMEM((tm, tn), F32)],
        compiler_params=_params("parallel", "parallel", "arbitrary"),
    )(*ins)


def _rms_fwd(x, g, name):
    t, d = x.shape
    tm = _tile(t, 256, 8)

    def body(x_ref, g_ref, h_ref):
        xv = x_ref[...]
        rstd = lax.rsqrt(jnp.mean(xv * xv, axis=-1, keepdims=True) + RMS_EPS)
        h_ref[...] = (xv * rstd * g_ref[...]).astype(BF16)

    return pl.pallas_call(
        body, name=name, grid=(t // tm,),
        out_shape=jax.ShapeDtypeStruct((t, d), BF16),
        in_specs=[pl.BlockSpec((tm, d), lambda i: (i, 0)), pl.BlockSpec((1, d), lambda i: (0, 0))],
        out_specs=pl.BlockSpec((tm, d), lambda i: (i, 0)),
        compiler_params=_params("parallel"),
    )(x, g)


def _rms_bwd(x, g, dh, dres, name):
    t, d = x.shape
    tm = _tile(t, 256, 8)

    def body(x_ref, g_ref, dh_ref, dres_ref, dx_ref, dg_ref):
        @pl.when(pl.program_id(0) == 0)
        def _():
            dg_ref[...] = jnp.zeros_like(dg_ref)

        xv = x_ref[...]
        rstd = lax.rsqrt(jnp.mean(xv * xv, axis=-1, keepdims=True) + RMS_EPS)
        xhat = xv * rstd
        dh_v = dh_ref[...]
        dg_ref[...] += jnp.sum(dh_v * xhat, axis=0, keepdims=True)
        dxh = dh_v * g_ref[...]
        dx_ref[...] = dres_ref[...] + rstd * (dxh - xhat * jnp.mean(dxh * xhat, axis=-1, keepdims=True))

    row = pl.BlockSpec((tm, d), lambda i: (i, 0))
    vec = pl.BlockSpec((1, d), lambda i: (0, 0))
    return pl.pallas_call(
        body, name=name, grid=(t // tm,),
        out_shape=(jax.ShapeDtypeStruct((t, d), F32), jax.ShapeDtypeStruct((1, d), F32)),
        in_specs=[row, vec, row, row], out_specs=(row, vec),
        compiler_params=_params("arbitrary"),
    )(x, g, dh, dres)


def _final_loss(x, g, target, name):
    t, d = x.shape
    tm = _tile(t, 256, 8)

    def body(x_ref, g_ref, t_ref, dx_ref, dg_ref, loss_ref):
        @pl.when(pl.program_id(0) == 0)
        def _():
            dg_ref[...] = jnp.zeros_like(dg_ref)
            loss_ref[...] = jnp.zeros_like(loss_ref)

        xv = x_ref[...]
        rstd = lax.rsqrt(jnp.mean(xv * xv, axis=-1, keepdims=True) + RMS_EPS)
        xhat = xv * rstd
        gv = g_ref[...]
        err = xhat * gv - t_ref[...]
        loss_ref[...] += 0.5 * jnp.sum(jnp.mean(err * err, axis=-1, keepdims=True), axis=0, keepdims=True)
        dout = err * (1.0 / d)
        dg_ref[...] += jnp.sum(dout * xhat, axis=0, keepdims=True)
        dxh = dout * gv
        dx_ref[...] = rstd * (dxh - xhat * jnp.mean(dxh * xhat, axis=-1, keepdims=True))

    row = pl.BlockSpec((tm, d), lambda i: (i, 0))
    vec = pl.BlockSpec((1, d), lambda i: (0, 0))
    return pl.pallas_call(
        body, name=name, grid=(t // tm,),
        out_shape=(jax.ShapeDtypeStruct((t, d), F32), jax.ShapeDtypeStruct((1, d), F32),
                   jax.ShapeDtypeStruct((1, 1), F32)),
        in_specs=[row, vec, row], out_specs=(row, vec, pl.BlockSpec((1, 1), lambda i: (0, 0))),
        compiler_params=_params("arbitrary"),
    )(x, g, target)


def _dot32(a, b):
    return jnp.dot(a, b, precision=lax.Precision.HIGH, preferred_element_type=F32)


def _head_sum(v, e, et):
    return _dot32(_dot32(v, e), et)


def _log_sigmoid(z):
    return jnp.minimum(z, 0.0) - jnp.log1p(jnp.exp(-jnp.abs(z)))


def _silu(g):
    return g * jax.nn.sigmoid(g)


def _prep_math(k, wlo, alo, wd, w0, wa, a0, k_k, k_a, e, et):
    wl = w0 + _dot32(jnp.tanh(wlo), wd)
    w_log = _log_sigmoid(wl) - 0.5
    w = jnp.exp(-jnp.exp(w_log))
    a = jax.nn.sigmoid(a0 + _dot32(alo, wa))
    kk0 = k * k_k
    kk = kk0 * lax.rsqrt(jnp.maximum(_head_sum(kk0 * kk0, e, et), L2_EPS * L2_EPS))
    kx = k * (1.0 + (a - 1.0) * k_a)
    return w, kx, kk, kk * a


def _post_math(ys, r, kx, v, g, gn_g, gn_b, r_k, e, et):
    inv = 1.0 / RWKV_HEAD
    mu = _head_sum(ys, e, et) * inv
    dlt = ys - mu
    var = _head_sum(dlt * dlt, e, et) * inv
    y = dlt * lax.rsqrt(var + GN_EPS) * gn_g + gn_b
    bonus = _head_sum(r * kx * r_k, e, et) * v
    return (y + bonus) * _silu(g)


def _shifted(p, prev_row, first):
    rows = lax.broadcasted_iota(jnp.int32, p.shape, 0)
    prev = jnp.where(first, 0.0, prev_row)
    return jnp.where(rows == 0, prev, pltpu.roll(p, 1, 0))


def _head_indicator(width):
    ch = lax.broadcasted_iota(jnp.int32, (width, width // RWKV_HEAD), 0) // RWKV_HEAD
    hd = lax.broadcasted_iota(jnp.int32, (width, width // RWKV_HEAD), 1)
    e = (ch == hd).astype(F32)
    return e, e.T


def _rwkv_prep(p, seq, dims, mu, wd, w0, wa, a0, k_k, k_a, e, et):
    t = p.shape[0]
    width, lora = dims
    cols = 3 * width + 2 * lora
    tm = 128
    per_seq = seq // tm

    def body(p_ref, prev_ref, mu_ref, wd_ref, w0_ref, wa_ref, a0_ref, kk_ref, ka_ref, e_ref, et_ref,
             r_out, w_out, kx_out, v_out, kkn_out, b_out):
        i = pl.program_id(0)
        pv = p_ref[...]
        psh = _shifted(pv, prev_ref[7:8, :], i % per_seq == 0)
        ps = pv + mu_ref[...] * (psh - pv)
        r, k, v = ps[:, :width], ps[:, width:2 * width], ps[:, 2 * width:3 * width]
        wlo, alo = ps[:, 3 * width:3 * width + lora], ps[:, 3 * width + lora:]
        w, kx, kk, b = _prep_math(k, wlo, alo, wd_ref[...], w0_ref[...], wa_ref[...], a0_ref[...],
                                  kk_ref[...], ka_ref[...], e_ref[...], et_ref[...])
        r_out[...] = r
        w_out[...] = w
        kx_out[...] = kx
        v_out[...] = v
        kkn_out[...] = kk
        b_out[...] = b

    full = lambda a: pl.BlockSpec(a.shape, lambda i: (0,) * a.ndim)
    out = pl.BlockSpec((tm, width), lambda i: (i, 0))
    return pl.pallas_call(
        body, name="rwkv_prep", grid=(t // tm,),
        out_shape=tuple(jax.ShapeDtypeStruct((t, width), F32) for _ in range(6)),
        in_specs=[pl.BlockSpec((tm, cols), lambda i: (i, 0)),
                  pl.BlockSpec((8, cols), lambda i: (jnp.maximum(i * (tm // 8) - 1, 0), 0)),
                  full(mu), full(wd), full(w0), full(wa), full(a0), full(k_k), full(k_a), full(e), full(et)],
        out_specs=tuple(out for _ in range(6)),
        compiler_params=_params("parallel"),
    )(p, p, mu, wd, w0, wa, a0, k_k, k_a, e, et)


def _rwkv_prep_bwd(p, seq, dims, mu, wd, w0, wa, a0, k_k, k_a, e, et, dr, dw, dkx, dv, dkk, db):
    t = p.shape[0]
    width, lora = dims
    cols = 3 * width + 2 * lora
    tm = 128
    n_tiles = t // tm
    per_seq = seq // tm

    def body(p_ref, prev_ref, mu_ref, wd_ref, w0_ref, wa_ref, a0_ref, kk_ref, ka_ref, e_ref, et_ref,
             dr_ref, dw_ref, dkx_ref, dv_ref, dkk_ref, db_ref,
             dp_out, dmu_out, dwd_out, dw0_out, dwa_out, da0_out, dkk_out, dka_out, carry):
        step = pl.program_id(0)
        i = n_tiles - 1 - step

        @pl.when(step == 0)
        def _():
            for ref in (dmu_out, dwd_out, dw0_out, dwa_out, da0_out, dkk_out, dka_out, carry):
                ref[...] = jnp.zeros_like(ref)

        pv = p_ref[...]
        first = i % per_seq == 0
        psh = _shifted(pv, prev_ref[7:8, :], first)
        muv = mu_ref[...]
        ps = pv + muv * (psh - pv)
        k = ps[:, width:2 * width]
        wlo, alo = ps[:, 3 * width:3 * width + lora], ps[:, 3 * width + lora:]
        ev, etv = e_ref[...], et_ref[...]
        _, vjp = jax.vjp(lambda *a: _prep_math(*a, ev, etv), k, wlo, alo, wd_ref[...], w0_ref[...],
                         wa_ref[...], a0_ref[...], kk_ref[...], ka_ref[...])
        dk, dwlo, dalo, dwd, dw0, dwa, da0, dk_k, dk_a = vjp(
            (dw_ref[...], dkx_ref[...], dkk_ref[...], db_ref[...]))
        dps = jnp.concatenate([dr_ref[...], dk, dv_ref[...], dwlo, dalo], axis=1)
        dmu_out[...] += jnp.sum(dps * (psh - pv), axis=0, keepdims=True)
        dwd_out[...] += dwd
        dw0_out[...] += dw0
        dwa_out[...] += dwa
        da0_out[...] += da0
        dkk_out[...] += dk_k
        dka_out[...] += dk_a
        dsh = dps * muv
        rows = lax.broadcasted_iota(jnp.int32, dsh.shape, 0)
        nxt = jnp.where(rows == tm - 1, carry[...], pltpu.roll(dsh, tm - 1, 0))
        dp_out[...] = (dps * (1.0 - muv) + nxt).astype(BF16)
        carry[...] = jnp.where(first, 0.0, dsh[0:1, :])

    full = lambda a: pl.BlockSpec(a.shape, lambda s: (0,) * a.ndim)
    tok = pl.BlockSpec((tm, width), lambda s: (n_tiles - 1 - s, 0))
    vec = lambda n: jax.ShapeDtypeStruct((1, n), F32)
    outs = (jax.ShapeDtypeStruct((t, cols), BF16), vec(cols), jax.ShapeDtypeStruct(wd.shape, F32), vec(width),
            jax.ShapeDtypeStruct(wa.shape, F32), vec(width), vec(width), vec(width))
    return pl.pallas_call(
        body, name="rwkv_prep_bwd", grid=(n_tiles,),
        out_shape=outs,
        in_specs=[pl.BlockSpec((tm, cols), lambda s: (n_tiles - 1 - s, 0)),
                  pl.BlockSpec((8, cols), lambda s: (jnp.maximum((n_tiles - 1 - s) * (tm // 8) - 1, 0), 0)),
                  full(mu), full(wd), full(w0), full(wa), full(a0), full(k_k), full(k_a), full(e), full(et),
                  tok, tok, tok, tok, tok, tok],
        out_specs=(pl.BlockSpec((tm, cols), lambda s: (n_tiles - 1 - s, 0)),) + tuple(
            pl.BlockSpec(o.shape, lambda s: (0, 0)) for o in outs[1:]),
        scratch_shapes=[pltpu.VMEM((1, cols), F32)],
        compiler_params=_params("arbitrary"),
    )(p, p, mu, wd, w0, wa, a0, k_k, k_a, e, et, dr, dw, dkx, dv, dkk, db)


def _rwkv_post(ys, r, kx, v, p, g_col, gn_g, gn_b, r_k, e, et):
    t, width = ys.shape
    tm = 256

    def body(ys_ref, r_ref, kx_ref, v_ref, g_ref, gg_ref, gb_ref, rk_ref, e_ref, et_ref, out_ref):
        out_ref[...] = _post_math(ys_ref[...], r_ref[...], kx_ref[...], v_ref[...], g_ref[...], gg_ref[...],
                                  gb_ref[...], rk_ref[...], e_ref[...], et_ref[...]).astype(BF16)

    tok = pl.BlockSpec((tm, width), lambda i: (i, 0))
    full = lambda a: pl.BlockSpec(a.shape, lambda i: (0,) * a.ndim)
    return pl.pallas_call(
        body, name="rwkv_post", grid=(t // tm,),
        out_shape=jax.ShapeDtypeStruct((t, width), BF16),
        in_specs=[tok, tok, tok, tok, pl.BlockSpec((tm, width), lambda i: (i, g_col)),
                  full(gn_g), full(gn_b), full(r_k), full(e), full(et)],
        out_specs=tok,
        compiler_params=_params("parallel"),
    )(ys, r, kx, v, p, gn_g, gn_b, r_k, e, et)


def _rwkv_post_bwd(ys, r, kx, v, p, g_col, gn_g, gn_b, r_k, e, et, dy, dy_col):
    t, width = ys.shape
    tm = 128

    def body(ys_ref, r_ref, kx_ref, v_ref, g_ref, gg_ref, gb_ref, rk_ref, e_ref, et_ref, dy_ref,
             dys_out, dr_out, dkx_out, dv_out, dg_out, dgg_out, dgb_out, drk_out):
        @pl.when(pl.program_id(0) == 0)
        def _():
            for ref in (dgg_out, dgb_out, drk_out):
                ref[...] = jnp.zeros_like(ref)

        ev, etv = e_ref[...], et_ref[...]
        _, vjp = jax.vjp(lambda *a: _post_math(*a, ev, etv), ys_ref[...], r_ref[...], kx_ref[...], v_ref[...],
                         g_ref[...], gg_ref[...], gb_ref[...], rk_ref[...])
        dys, dr, dkx, dv, dg, dgg, dgb, drk = vjp(dy_ref[...])
        dys_out[...] = dys
        dr_out[...] = dr
        dkx_out[...] = dkx
        dv_out[...] = dv
        dg_out[...] = dg.astype(BF16)
        dgg_out[...] += dgg
        dgb_out[...] += dgb
        drk_out[...] += drk

    tok = pl.BlockSpec((tm, width), lambda i: (i, 0))
    full = lambda a: pl.BlockSpec(a.shape, lambda i: (0,) * a.ndim)
    big = jax.ShapeDtypeStruct((t, width), F32)
    vec = jax.ShapeDtypeStruct((1, width), F32)
    vspec = pl.BlockSpec((1, width), lambda i: (0, 0))
    return pl.pallas_call(
        body, name="rwkv_post_bwd", grid=(t // tm,),
        out_shape=(big, big, big, big, jax.ShapeDtypeStruct((t, width), BF16), vec, vec, vec),
        in_specs=[tok, tok, tok, tok, pl.BlockSpec((tm, width), lambda i: (i, g_col)),
                  full(gn_g), full(gn_b), full(r_k), full(e), full(et),
                  pl.BlockSpec((tm, width), lambda i: (i, dy_col))],
        out_specs=(tok, tok, tok, tok, tok, vspec, vspec, vspec),
        compiler_params=_params("arbitrary"),
    )(ys, r, kx, v, p, gn_g, gn_b, r_k, e, et, dy)


def _to_scan_k(a, bl, seq):
    h = a.shape[1] // RWKV_HEAD
    return a.reshape(bl, seq, h, RWKV_HEAD).transpose(1, 3, 0, 2).reshape(seq, RWKV_HEAD, bl * h)


def _both_halves(ref, i):
    row = ref[i]
    return jnp.concatenate([row, row], axis=1)


def _to_scan_v(a, bl, seq):
    h = a.shape[1] // RWKV_HEAD
    half = RWKV_HEAD // 2
    return a.reshape(bl, seq, h, 2, half).transpose(1, 4, 3, 0, 2).reshape(seq, half, 2 * bl * h)


def _from_scan_k(a, bl, seq):
    h = a.shape[2] // bl
    return a.reshape(seq, RWKV_HEAD, bl, h).transpose(2, 0, 3, 1).reshape(bl * seq, h * RWKV_HEAD)


def _from_scan_v(a, bl, seq):
    half = RWKV_HEAD // 2
    h = a.shape[2] // (2 * bl)
    a = a.reshape(seq, half, 2, bl, h).transpose(3, 0, 4, 2, 1)
    return a.reshape(bl * seq, h * RWKV_HEAD)


def _scan_fwd(kk, w, b, kx, r, v, gathers=()):
    seq, nk, half_lanes = kk.shape
    nv, lanes = v.shape[1:]
    assert lanes == 2 * half_lanes
    tt = SCAN_STEPS_PER_BLOCK
    nblk = seq // tt
    ng = len(gathers)

    def body(*refs):
        kk_ref, w_ref, b_ref, kx_ref, r_ref, v_ref = refs[:6]
        g_src = refs[6:6 + ng]
        y_ref, st_ref, sa_ref = refs[6 + ng:9 + ng]
        g_out = refs[9 + ng:9 + 2 * ng]
        s_ref = refs[9 + 2 * ng]
        pid = pl.program_id(0)

        @pl.when(pid == 0)
        def _():
            s_ref[...] = jnp.zeros_like(s_ref)

        if ng:
            send_sems, recv_sems, local_sems = refs[10 + 2 * ng:]
            phases = [_gather_phases(g_src[n], g_out[n], send_sems, recv_sems, local_sems, n) for n in range(ng)]
            for k, at in enumerate((0, (3 * nblk) // 4, nblk - 1)):
                @pl.when(pid == at)
                def _(k=k):
                    for ph in phases:
                        ph[k]()

        def step(i, carry):
            kkv, wv, bv, kxv, rv = (_both_halves(ref, i) for ref in (kk_ref, w_ref, b_ref, kx_ref, r_ref))
            for j in range(nv):
                s_old = s_ref[j]
                sa = -jnp.sum(s_old * kkv, axis=0, keepdims=True)
                s_new = s_old * wv + sa * bv + v_ref[i, j:j + 1, :] * kxv
                s_ref[j] = s_new
                st_ref[i, j] = s_new
                sa_ref[i, j:j + 1, :] = sa
                y_ref[i, j:j + 1, :] = jnp.sum(s_new * rv, axis=0, keepdims=True)
            return carry

        lax.fori_loop(0, tt, step, 0)

    krow = pl.BlockSpec((tt, nk, half_lanes), lambda i: (i, 0, 0))
    vrow = pl.BlockSpec((tt, nv, lanes), lambda i: (i, 0, 0))
    vshape = jax.ShapeDtypeStruct((seq, nv, lanes), F32)
    return pl.pallas_call(
        body, name="rwkv_scan_fwd", grid=(nblk,),
        out_shape=(vshape, jax.ShapeDtypeStruct((seq, nv, nk, lanes), F32), vshape,
                   *[_gathered_shape(g) for g in gathers]),
        in_specs=[krow, krow, krow, krow, krow, vrow] + [_HBM] * ng,
        out_specs=(vrow, pl.BlockSpec((tt, nv, nk, lanes), lambda i: (i, 0, 0, 0)), vrow, *[_HBM] * ng),
        scratch_shapes=[pltpu.VMEM((nv, nk, lanes), F32)] + (_gather_scratch(ng) if ng else []),
        compiler_params=_params("arbitrary"),
    )(kk, w, b, kx, r, v, *gathers)


def _scan_bwd(kk, w, b, kx, r, v, states, sa_all, dy, scatters=()):
    seq, nk, half_lanes = kk.shape
    nv, lanes = v.shape[1:]
    tt = SCAN_STEPS_PER_BLOCK
    nblk = seq // tt
    nsc = len(scatters)

    def both_halves(a):
        return (a + pltpu.roll(a, half_lanes, 1))[:, :half_lanes]

    def body(*refs):
        kk_ref, w_ref, b_ref, kx_ref, r_ref, v_ref, st_ref, before_ref, sa_ref, dy_ref = refs[:10]
        dkk_ref, dw_ref, db_ref, dkx_ref, dr_ref, dv_ref = refs[10 + nsc:16 + nsc]
        g_ref = refs[16 + 2 * nsc]

        if nsc:
            send_sems, recv_sems, local_sems = refs[17 + 2 * nsc:]
            phases = [_scatter_phases(refs[10 + n], refs[16 + nsc + n], send_sems, recv_sems, local_sems, n)
                      for n in range(nsc)]
            for k, at in enumerate((0, nblk - 1)):
                @pl.when(pl.program_id(0) == at)
                def _(k=k):
                    for ph in phases:
                        ph[k]()

        @pl.when(pl.program_id(0) == 0)
        def _():
            g_ref[...] = jnp.zeros_like(g_ref)

        def one_step(i, state_before):
            kkv, wv, bv, kxv, rv = (_both_halves(ref, i) for ref in (kk_ref, w_ref, b_ref, kx_ref, r_ref))
            zero = jnp.zeros((nk, lanes), F32)
            a_r, a_w, a_b, a_kx, a_kk = zero, zero, zero, zero, zero
            for j in range(nv):
                s_old = state_before(j)
                s_new = st_ref[i, j]
                vv = v_ref[i, j:j + 1, :]
                dyv = dy_ref[i, j:j + 1, :]
                sa = sa_ref[i, j:j + 1, :]
                g = g_ref[j] + dyv * rv
                a_r = a_r + s_new * dyv
                a_w = a_w + g * s_old
                dsa = jnp.sum(g * bv, axis=0, keepdims=True)
                a_b = a_b + g * sa
                dv_ref[i, j:j + 1, :] = jnp.sum(g * kxv, axis=0, keepdims=True)
                a_kx = a_kx + g * vv
                a_kk = a_kk + s_old * dsa
                g_ref[j] = g * wv - dsa * kkv
            dr_ref[i] = both_halves(a_r)
            dw_ref[i] = both_halves(a_w)
            db_ref[i] = both_halves(a_b)
            dkx_ref[i] = both_halves(a_kx)
            dkk_ref[i] = -both_halves(a_kk)

        def step(n, carry):
            i = tt - 1 - n
            one_step(i, lambda j: st_ref[i - 1, j])
            return carry

        lax.fori_loop(0, tt - 1, step, 0)
        at_start = pl.program_id(0) == nblk - 1
        one_step(0, lambda j: jnp.where(at_start, 0.0, before_ref[0, j]))

    rev = lambda i: nblk - 1 - i
    krow = pl.BlockSpec((tt, nk, half_lanes), lambda i: (rev(i), 0, 0))
    vrow = pl.BlockSpec((tt, nv, lanes), lambda i: (rev(i), 0, 0))
    kshape = jax.ShapeDtypeStruct((seq, nk, half_lanes), F32)
    return pl.pallas_call(
        body, name="rwkv_scan_bwd", grid=(nblk,),
        out_shape=(kshape, kshape, kshape, kshape, kshape, jax.ShapeDtypeStruct((seq, nv, lanes), F32),
                   *[jax.ShapeDtypeStruct(s.shape, s.dtype) for s in scatters]),
        in_specs=[krow, krow, krow, krow, krow, vrow,
                  pl.BlockSpec((tt, nv, nk, lanes), lambda i: (rev(i), 0, 0, 0)),
                  pl.BlockSpec((1, nv, nk, lanes), lambda i: (jnp.maximum(rev(i) * tt - 1, 0), 0, 0, 0)),
                  vrow, vrow] + [_HBM] * nsc,
        out_specs=(krow, krow, krow, krow, krow, vrow, *[_HBM] * nsc),
        scratch_shapes=[pltpu.VMEM((nv, nk, lanes), F32)] + (_scatter_scratch(nsc) if nsc else []),
        compiler_params=_params("arbitrary"),
    )(kk, w, b, kx, r, v, states, states, sa_all, dy, *scatters)


_NT = (((1,), (1,)), ((), ()))
_TN = (((0,), (0,)), ((), ()))
SB_SCALE = 1.0 / (SB_HEAD ** 0.5)


def _split_dot(a, b):
    hi = a.astype(BF16)
    lo = (a - hi.astype(F32)).astype(BF16)
    return jnp.dot(hi, b, preferred_element_type=F32) + jnp.dot(lo, b, preferred_element_type=F32)


def _sb_masks():
    blk = SB_BLOCK
    row = lax.broadcasted_iota(jnp.int32, (blk, blk), 0)
    col = lax.broadcasted_iota(jnp.int32, (blk, blk), 1)
    ones = jnp.ones((blk, blk), BF16)
    fwd = jnp.concatenate([(row > col).astype(BF16), ones], axis=1)
    bwd = jnp.concatenate([(col > row).astype(BF16), ones], axis=1)
    return row, col, fwd, bwd


def _split_dots(xs, b):
    his = [a.astype(BF16) for a in xs]
    los = [(a - hi.astype(F32)).astype(BF16) for a, hi in zip(xs, his)]
    tops = [jnp.dot(hi, b, preferred_element_type=F32) for hi in his]
    return [top + jnp.dot(lo, b, preferred_element_type=F32) for top, lo in zip(tops, los)]


SB_HEADS_PER_STEP = 2


def _sb_chains(bl):
    return [(b, slice(hh * SB_HEAD, (hh + 1) * SB_HEAD)) for b in range(bl) for hh in range(SB_HEADS_PER_STEP)]


def _sb_specs(bl, seq, cols, n_heads):
    hp = SB_HEADS_PER_STEP
    assert n_heads % hp == 0 and all(c % hp == 0 for c in cols)
    qspec = lambda col: pl.BlockSpec((bl, SB_BLOCK, hp * SB_HEAD), lambda h, i: (0, i, col // hp + h))
    kspec = lambda col: pl.BlockSpec((bl, seq, hp * SB_HEAD), lambda h, i: (0, 0, col // hp + h))
    return qspec, kspec


def _sb_fwd(p, bl, seq, cols, n_heads):
    t = p.shape[0]
    nq = seq // SB_BLOCK
    q_col, k_col, v_col, g_col = cols
    blk = SB_BLOCK
    chains = _sb_chains(bl)

    def body(q_ref, k_ref, v_ref, g_ref, out_ref, o_ref, tot_ref):
        qi = pl.program_id(1)
        row, col, mix, _ = _sb_masks()
        qbs = [q_ref[b, :, ln].astype(BF16) for b, ln in chains]

        def step(n, carry):
            j = qi - n
            rows = pl.ds(pl.multiple_of(j * blk, blk), blk)
            causal = (j * blk + col) < (qi * blk + row)
            zs = [lax.dot_general(qb, k_ref[b, rows, ln].astype(BF16), _NT, preferred_element_type=F32) * SB_SCALE
                  for (b, ln), qb in zip(chains, qbs)]
            lszs = [_log_sigmoid(z) for z in zs]
            boths = _split_dots([jnp.where(causal, lsz - z, 0.0) for lsz, z in zip(lszs, zs)], mix)
            atts = [jnp.where(causal, jnp.exp(lsz + both[:, :blk] + later), 0.0).astype(BF16)
                    for lsz, both, (_, later) in zip(lszs, boths, carry)]
            outs = [out + jnp.dot(att, v_ref[b, rows, ln].astype(BF16), preferred_element_type=F32)
                    for (b, ln), att, (out, _) in zip(chains, atts, carry)]
            return tuple((out, later + both[:, blk:]) for out, both, (_, later) in zip(outs, boths, carry))

        zero = jnp.zeros((blk, SB_HEAD), F32)
        done = lax.fori_loop(0, qi + 1, step, tuple((zero, zero) for _ in chains))
        for (b, ln), (out, total) in zip(chains, done):
            o_ref[b, :, ln] = out
            tot_ref[b, :, ln] = total
            out_ref[b, :, ln] = (out * _silu(g_ref[b, :, ln])).astype(BF16)

    qspec, kspec = _sb_specs(bl, seq, cols, n_heads)
    width = n_heads * SB_HEAD
    p3 = p.reshape(bl, seq, p.shape[1])
    f32 = jax.ShapeDtypeStruct((bl, seq, width), F32)
    outs = pl.pallas_call(
        body, name="sb_attn_fwd", grid=(n_heads // SB_HEADS_PER_STEP, nq),
        out_shape=(jax.ShapeDtypeStruct((bl, seq, width), BF16), f32, f32),
        in_specs=[qspec(q_col), kspec(k_col), kspec(v_col), qspec(g_col)],
        out_specs=(qspec(0), qspec(0), qspec(0)),
        compiler_params=_params("parallel", "arbitrary"),
    )(p3, p3, p3, p3)
    return tuple(a.reshape(t, width) for a in outs)


def _sb_bwd(p, bl, seq, cols, n_heads, dy, dy_col, o, tot, swaps=()):
    t = p.shape[0]
    nq = seq // SB_BLOCK
    q_col, k_col, v_col, g_col = cols
    blk = SB_BLOCK
    ns = len(swaps)
    n_groups = n_heads // SB_HEADS_PER_STEP
    chains = _sb_chains(bl)

    def body(*refs):
        q_ref, k_ref, v_ref, g_ref, dy_ref, o_ref, tot_ref = refs[:7]
        dq_out, dk_out, dv_out, dg_out = refs[7 + ns:11 + ns]
        dk_acc, dv_acc = refs[11 + 2 * ns:13 + 2 * ns]
        group, qi = pl.program_id(0), pl.program_id(1)

        if ns:
            send_sems, recv_sems = refs[13 + 2 * ns:]
            phases = [_swap_phases(refs[7 + n], refs[11 + ns + n], send_sems, recv_sems, n) for n in range(ns)]
            for k, at in enumerate(((0, 0), (n_groups - 1, nq - 1))):
                @pl.when((group == at[0]) & (qi == at[1]))
                def _(k=k):
                    for ph in phases:
                        ph[k]()

        @pl.when(qi == 0)
        def _():
            dk_acc[...] = jnp.zeros_like(dk_acc)
            dv_acc[...] = jnp.zeros_like(dv_acc)

        row, col, mix, mix_t = _sb_masks()
        qbs, dobs, totals = [], [], []
        for b, ln in chains:
            gate = g_ref[b, :, ln]
            sg = jax.nn.sigmoid(gate)
            dyv = dy_ref[b, :, ln]
            dg_out[b, :, ln] = (dyv * o_ref[b, :, ln] * (sg * (1.0 + gate * (1.0 - sg)))).astype(BF16)
            dobs.append((dyv * (gate * sg)).astype(BF16))
            qbs.append(q_ref[b, :, ln].astype(BF16))
            totals.append(tot_ref[b, :, ln])

        def step(j, carry):
            rows = pl.ds(pl.multiple_of(j * blk, blk), blk)
            causal = (j * blk + col) < (qi * blk + row)
            kbs = [k_ref[b, rows, ln].astype(BF16) for b, ln in chains]
            zs = [lax.dot_general(qb, kb, _NT, preferred_element_type=F32) * SB_SCALE for qb, kb in zip(qbs, kbs)]
            datts = [lax.dot_general(dob, v_ref[b, rows, ln].astype(BF16), _NT, preferred_element_type=F32)
                     for (b, ln), dob in zip(chains, dobs)]
            lszs = [_log_sigmoid(z) for z in zs]
            boths = _split_dots([jnp.where(causal, lsz - z, 0.0) for lsz, z in zip(lszs, zs)], mix)
            seens = [seen + both[:, blk:] for both, (_, seen, _) in zip(boths, carry)]
            atts = [jnp.where(causal, jnp.exp(lsz + both[:, :blk] + (total - seen)), 0.0)
                    for lsz, both, total, seen in zip(lszs, boths, totals, seens)]
            dls = [att * datt for att, datt in zip(atts, datts)]
            for (b, ln), att, dob in zip(chains, atts, dobs):
                dv_acc[b, rows, ln] += lax.dot_general(att.astype(BF16), dob, _TN, preferred_element_type=F32)
            boths_t = _split_dots(dls, mix_t)
            dkeeps = [jnp.where(causal, both_t[:, :blk] + dl_before, 0.0)
                      for both_t, (_, _, dl_before) in zip(boths_t, carry)]
            dzbs = [(((dl + dkeep) * jax.nn.sigmoid(-z) - dkeep) * SB_SCALE).astype(BF16)
                    for dl, dkeep, z in zip(dls, dkeeps, zs)]
            dqs = [dq + jnp.dot(dzb, kb, preferred_element_type=F32) for dzb, kb, (dq, _, _) in zip(dzbs, kbs, carry)]
            for (b, ln), dzb, qb in zip(chains, dzbs, qbs):
                dk_acc[b, rows, ln] += lax.dot_general(dzb, qb, _TN, preferred_element_type=F32)
            return tuple((dq, seen, dl_before + both_t[:, blk:])
                         for dq, seen, both_t, (_, _, dl_before) in zip(dqs, seens, boths_t, carry))

        zero = jnp.zeros((blk, SB_HEAD), F32)
        done = lax.fori_loop(0, qi + 1, step, tuple((zero, zero, zero) for _ in chains))
        for (b, ln), (dq, _, _) in zip(chains, done):
            dq_out[b, :, ln] = dq.astype(BF16)

        @pl.when(qi == nq - 1)
        def _():
            dk_out[...] = dk_acc[...].astype(BF16)
            dv_out[...] = dv_acc[...].astype(BF16)

    qspec, kspec = _sb_specs(bl, seq, cols + (dy_col,), n_heads)
    width = n_heads * SB_HEAD
    shape = jax.ShapeDtypeStruct((bl, seq, width), BF16)
    as3 = lambda a: a.reshape(bl, seq, a.shape[1])
    p3 = as3(p)
    acc = pltpu.VMEM((bl, seq, SB_HEADS_PER_STEP * SB_HEAD), F32)
    outs = pl.pallas_call(
        body, name="sb_attn_bwd", grid=(n_groups, nq),
        out_shape=(shape, shape, shape, shape, *[_swapped_shape(s) for s in swaps]),
        in_specs=[qspec(q_col), kspec(k_col), kspec(v_col), qspec(g_col), qspec(dy_col), qspec(0), qspec(0)]
        + [_HBM] * ns,
        out_specs=(qspec(0), kspec(0), kspec(0), qspec(0), *[_HBM] * ns),
        scratch_shapes=[acc, acc] + (_swap_scratch(ns) if ns else []),
        compiler_params=_params("arbitrary", "arbitrary"),
    )(p3, p3, p3, p3, as3(dy), as3(o), as3(tot), *swaps)
    return tuple(a.reshape(t, width) for a in outs[:4]) + tuple(outs[4:])


def _gelu(x):
    return 0.5 * x * (1.0 + lax.erf(x * (2.0 ** -0.5)))


def _sgu_math(us, vs, gs, ln_g, ln_b, ws, bs):
    width = sum(v.shape[1] for v in vs)
    vg = [_gelu(v) for v in vs]
    mu = sum(jnp.sum(v, axis=1, keepdims=True) for v in vg) * (1.0 / width)
    dl = [v - mu for v in vg]
    var = sum(jnp.sum(d * d, axis=1, keepdims=True) for d in dl) * (1.0 / width)
    rstd = lax.rsqrt(var + LN_EPS)
    n = ws[0].shape[0]
    tri = lax.broadcasted_iota(jnp.int32, (n, n), 0) >= lax.broadcasted_iota(jnp.int32, (n, n), 1)
    outs = []
    for i in range(len(vs)):
        vn = dl[i] * rstd * ln_g[i] + ln_b[i]
        mixed = _dot32(jnp.where(tri, ws[i], 0.0), vn) + bs[i]
        outs.append(_gelu(us[i]) * mixed * _silu(gs[i]))
    return outs


def _sgu_load(p_ref, lng_ref, lnb_ref, ws_ref, bs_ref, width):
    gd = width // SGU_GROUPS
    grp = lambda ref, base, i: ref[:, base + i * gd:base + (i + 1) * gd]
    idx = range(SGU_GROUPS)
    return ([grp(p_ref, 0, i) for i in idx], [grp(p_ref, width, i) for i in idx],
            [grp(p_ref, 2 * width, i) for i in idx], [grp(lng_ref, 0, i) for i in idx],
            [grp(lnb_ref, 0, i) for i in idx], [ws_ref[i] for i in idx], [bs_ref[i] for i in idx])


def _sgu_fwd(p, ln_g, ln_b, w_s, b_s):
    t = p.shape[0]
    width = p.shape[1] // 3
    gd = width // SGU_GROUPS
    tm = SGU_CHUNK

    def body(p_ref, lng_ref, lnb_ref, ws_ref, bs_ref, y_ref):
        outs = _sgu_math(*_sgu_load(p_ref, lng_ref, lnb_ref, ws_ref, bs_ref, width))
        for i in range(SGU_GROUPS):
            y_ref[:, i * gd:(i + 1) * gd] = outs[i].astype(BF16)

    full = lambda a: pl.BlockSpec(a.shape, lambda i: (0,) * a.ndim)
    return pl.pallas_call(
        body, name="sgu_fwd", grid=(t // tm,),
        out_shape=jax.ShapeDtypeStruct((t, width), BF16),
        in_specs=[pl.BlockSpec((tm, 3 * width), lambda i: (i, 0)), full(ln_g), full(ln_b), full(w_s), full(b_s)],
        out_specs=pl.BlockSpec((tm, width), lambda i: (i, 0)),
        compiler_params=_params("parallel"),
    )(p, ln_g, ln_b, w_s, b_s)


def _sgu_bwd(p, ln_g, ln_b, w_s, b_s, dy):
    t = p.shape[0]
    width = p.shape[1] // 3
    gd = width // SGU_GROUPS
    tm = SGU_CHUNK

    def body(p_ref, lng_ref, lnb_ref, ws_ref, bs_ref, dy_ref, dp_out, dlng_out, dlnb_out, dws_out, dbs_out):
        @pl.when(pl.program_id(0) == 0)
        def _():
            for ref in (dlng_out, dlnb_out, dws_out, dbs_out):
                ref[...] = jnp.zeros_like(ref)

        _, vjp = jax.vjp(_sgu_math, *_sgu_load(p_ref, lng_ref, lnb_ref, ws_ref, bs_ref, width))
        dus, dvs, dgs, dlng, dlnb, dws, dbs = vjp(
            [dy_ref[:, i * gd:(i + 1) * gd] for i in range(SGU_GROUPS)])
        for i in range(SGU_GROUPS):
            cols = slice(i * gd, (i + 1) * gd)
            dp_out[:, i * gd:(i + 1) * gd] = dus[i].astype(BF16)
            dp_out[:, width + i * gd:width + (i + 1) * gd] = dvs[i].astype(BF16)
            dp_out[:, 2 * width + i * gd:2 * width + (i + 1) * gd] = dgs[i].astype(BF16)
            dlng_out[:, cols] += dlng[i]
            dlnb_out[:, cols] += dlnb[i]
            dws_out[i] += dws[i]
            dbs_out[i] += dbs[i]

    full = lambda a: pl.BlockSpec(a.shape, lambda i: (0,) * a.ndim)
    like = lambda a: jax.ShapeDtypeStruct(a.shape, F32)
    return pl.pallas_call(
        body, name="sgu_bwd", grid=(t // tm,),
        out_shape=(jax.ShapeDtypeStruct((t, 3 * width), BF16), like(ln_g), like(ln_b), like(w_s), like(b_s)),
        in_specs=[pl.BlockSpec((tm, 3 * width), lambda i: (i, 0)), full(ln_g), full(ln_b), full(w_s), full(b_s),
                  pl.BlockSpec((tm, width), lambda i: (i, 0))],
        out_specs=(pl.BlockSpec((tm, 3 * width), lambda i: (i, 0)), full(ln_g), full(ln_b), full(w_s), full(b_s)),
        compiler_params=_params("arbitrary"),
    )(p, ln_g, ln_b, w_s, b_s, dy)


def _sum_slabs(parts, name):
    n_parts, rows, cols = parts.shape
    tr = _tile(rows, max(16, (1 << 18) // cols), 16)

    def body(p_ref, o_ref):
        acc = p_ref[0].astype(F32)
        for d in range(1, n_parts):
            acc = acc + p_ref[d].astype(F32)
        o_ref[...] = acc

    return pl.pallas_call(
        body, name=name, grid=(rows // tr,),
        out_shape=jax.ShapeDtypeStruct((rows, cols), F32),
        in_specs=[pl.BlockSpec((n_parts, tr, cols), lambda i: (0, i, 0))],
        out_specs=pl.BlockSpec((tr, cols), lambda i: (i, 0)),
        compiler_params=_params("parallel"),
    )(parts)


def _adamw(w, g, m, v, name):
    rows, cols = w.shape
    tr = _tile(rows, max(8, (1 << 18) // cols), 8)

    def body(w_ref, g_ref, m_ref, v_ref, d_out, m_out, v_out):
        gv = g_ref[...]
        mn = ADAM_B1 * m_ref[...] + (1.0 - ADAM_B1) * gv
        vn = ADAM_B2 * v_ref[...] + (1.0 - ADAM_B2) * (gv * gv)
        m_hat = mn / (1.0 - ADAM_B1 ** ADAM_STEP)
        v_hat = vn / (1.0 - ADAM_B2 ** ADAM_STEP)
        d_out[...] = -ADAM_LR * (m_hat / (jnp.sqrt(v_hat) + ADAM_EPS) + ADAM_WD * w_ref[...])
        m_out[...] = mn
        v_out[...] = vn

    blk = pl.BlockSpec((tr, cols), lambda i: (i, 0))
    shape = jax.ShapeDtypeStruct((rows, cols), F32)
    return pl.pallas_call(
        body, name=name, grid=(rows // tr,),
        out_shape=(shape, shape, shape),
        in_specs=[blk, blk, blk, blk], out_specs=(blk, blk, blk),
        compiler_params=_params("parallel"),
    )(w, g, m, v)


PACK_COLS = 1024


def _pack(arrays):
    flat = jnp.concatenate([a.reshape(-1).astype(F32) for a in arrays])
    rows = -(-flat.shape[0] // (8 * PACK_COLS)) * 8
    return jnp.pad(flat, (0, rows * PACK_COLS - flat.shape[0])).reshape(rows, PACK_COLS)


def _unpack(packed, shapes):
    flat = packed.reshape(-1)
    out, at = [], 0
    for s in shapes:
        n = 1
        for d in s:
            n *= d
        out.append(flat[at:at + n].reshape(s))
        at += n
    return out


def kernel(x, norm_g, final_norm_g, e_w_in, e_shift_mu, e_w_decay_up, e_w0, e_a_up, e_a0, e_k_k, e_k_a, e_r_k, e_gn_g, e_gn_b, e_w_out, o_w_in, o_ln_g, o_ln_b, o_w_s, o_b_s, o_w_out, loss_target, m_norm_g, m_final_norm_g, m_e_w_in, m_e_shift_mu, m_e_w_decay_up, m_e_w0, m_e_a_up, m_e_a0, m_e_k_k, m_e_k_a, m_e_r_k, m_e_gn_g, m_e_gn_b, m_e_w_out, m_o_w_in, m_o_ln_g, m_o_ln_b, m_o_w_s, m_o_b_s, m_o_w_out, v_norm_g, v_final_norm_g, v_e_w_in, v_e_shift_mu, v_e_w_decay_up, v_e_w0, v_e_a_up, v_e_a0, v_e_k_k, v_e_k_a, v_e_r_k, v_e_gn_g, v_e_gn_b, v_e_w_out, v_o_w_in, v_o_ln_g, v_o_ln_b, v_o_w_s, v_o_b_s, v_o_w_out):
    weights = dict(norm_g=norm_g, final_norm_g=final_norm_g, e_w_in=e_w_in, e_shift_mu=e_shift_mu,
                   e_w_decay_up=e_w_decay_up, e_w0=e_w0, e_a_up=e_a_up, e_a0=e_a0, e_k_k=e_k_k, e_k_a=e_k_a,
                   e_r_k=e_r_k, e_gn_g=e_gn_g, e_gn_b=e_gn_b, e_w_out=e_w_out, o_w_in=o_w_in, o_ln_g=o_ln_g,
                   o_ln_b=o_ln_b, o_w_s=o_w_s, o_b_s=o_b_s, o_w_out=o_w_out)
    mom1 = dict(norm_g=m_norm_g, final_norm_g=m_final_norm_g, e_w_in=m_e_w_in, e_shift_mu=m_e_shift_mu,
                e_w_decay_up=m_e_w_decay_up, e_w0=m_e_w0, e_a_up=m_e_a_up, e_a0=m_e_a0, e_k_k=m_e_k_k,
                e_k_a=m_e_k_a, e_r_k=m_e_r_k, e_gn_g=m_e_gn_g, e_gn_b=m_e_gn_b, e_w_out=m_e_w_out,
                o_w_in=m_o_w_in, o_ln_g=m_o_ln_g, o_ln_b=m_o_ln_b, o_w_s=m_o_w_s, o_b_s=m_o_b_s,
                o_w_out=m_o_w_out)
    mom2 = dict(norm_g=v_norm_g, final_norm_g=v_final_norm_g, e_w_in=v_e_w_in, e_shift_mu=v_e_shift_mu,
                e_w_decay_up=v_e_w_decay_up, e_w0=v_e_w0, e_a_up=v_e_a_up, e_a0=v_e_a0, e_k_k=v_e_k_k,
                e_k_a=v_e_k_a, e_r_k=v_e_r_k, e_gn_g=v_e_gn_g, e_gn_b=v_e_gn_b, e_w_out=v_e_w_out,
                o_w_in=v_o_w_in, o_ln_g=v_o_ln_g, o_ln_b=v_o_ln_b, o_w_s=v_o_w_s, o_b_s=v_o_b_s,
                o_w_out=v_o_w_out)
    names = list(weights)
    big = ("e_w_in", "e_w_out", "o_w_in", "o_w_out")

    bl, seq, d = x.shape
    t = bl * seq
    width = e_w0.shape[1]
    lora = e_w_decay_up.shape[1]
    n_sb = width // SB_HEAD
    me = 4 * lax.axis_index("x") + 2 * lax.axis_index("y") + lax.axis_index("c")

    e_win_t = _all_gather(e_w_in[0].T.astype(BF16), "gather_e_w_in").reshape(-1, d)
    later_shards = (e_w_out[0].astype(BF16), o_w_in[0].T.astype(BF16), o_w_out[0].astype(BF16))
    sharded_small = ("e_w_decay_up", "e_a_up", "o_ln_g", "o_ln_b")
    small_shapes = [weights[n][0].shape for n in sharded_small]
    got = _all_gather(_pack([weights[n][0] for n in sharded_small]), "gather_small")
    per_dev = [_unpack(got[dev], small_shapes) for dev in range(N_DEV)]
    wd, wa, ln_g, ln_b = [jnp.concatenate([per_dev[dev][i] for dev in range(N_DEV)], axis=-1).reshape(
        small_shapes[i][:-1] + (-1,)) for i in range(4)]
    ln_g, ln_b = ln_g.reshape(1, -1), ln_b.reshape(1, -1)
    e_ind, e_ind_t = _head_indicator(width)
    b_s3 = o_b_s[0][:, :, None]

    x2d = x.reshape(t, d)
    target = loss_target.reshape(t, d)
    cols_rwkv = 3 * width + 2 * lora
    assert cols_rwkv % LANES == 0 and width % LANES == 0
    sb0 = (cols_rwkv + width) // SB_HEAD
    sb_cols = (sb0, sb0 + n_sb, sb0 + 2 * n_sb, sb0 + 3 * n_sb)

    h0 = _rms_fwd(x2d, norm_g[0:1], "rms0_fwd")
    p = _matmul(h0, e_win_t, "nt", F32, "e_in_fwd", tm=2048, tn=1280)
    g_rwkv = p[:, cols_rwkv:cols_rwkv + width]
    r, w, kx, v, kk, b = _rwkv_prep(p, seq, (width, lora), e_shift_mu, wd, e_w0, wa, e_a0, e_k_k, e_k_a,
                                    e_ind, e_ind_t)
    sk = [_to_scan_k(a, bl, seq) for a in (kk, w, b, kx, r)]
    sv = _to_scan_v(v, bl, seq)
    ys_scan, states, sa_all, e_wout, o_win_t, o_wout = _scan_fwd(*sk, sv, gathers=later_shards)
    e_wout, o_win_t, o_wout = (a.reshape(-1, d) for a in (e_wout, o_win_t, o_wout))
    ys = _from_scan_v(ys_scan, bl, seq)
    ya = _rwkv_post(ys, r, kx, v, g_rwkv, 0, e_gn_g, e_gn_b, e_r_k, e_ind, e_ind_t)
    yb, sb_o, sb_tot = _sb_fwd(p, bl, seq, sb_cols, n_sb)
    y = jnp.concatenate([ya, yb], axis=1)
    x1 = _matmul(y, e_wout, "nn", F32, "e_out_fwd", res=x2d)
    h1 = _rms_fwd(x1, norm_g[1:2], "rms1_fwd")
    p2 = _matmul(h1, o_win_t, "nt", F32, "o_in_fwd", tm=2048)
    y2 = _sgu_fwd(p2, ln_g, ln_b, o_w_s[0], b_s3)
    x2 = _matmul(y2, o_wout, "nn", F32, "o_out_fwd", res=x1)
    dx2, d_final_g, loss_part = _final_loss(x2, final_norm_g.reshape(1, d), target, "final_loss")

    dy2 = _matmul(dx2, o_wout, "nt", F32, "o_out_bwd_x")
    d_o_wout = _matmul(y2, dx2, "tn", F32, "o_out_bwd_w")
    dp2, d_ln_g, d_ln_b, d_w_s, d_b_s3 = _sgu_bwd(p2, ln_g, ln_b, o_w_s[0], b_s3, dy2)
    dh1 = _matmul(dp2, o_win_t, "nn", F32, "o_in_bwd_x", tm=2048, tk=1536)
    d_o_win_t = _matmul(dp2, h1, "tn", F32, "o_in_bwd_w", tn=2048)
    dx1, d_g1 = _rms_bwd(x1, norm_g[1:2], dh1, dx2, "rms1_bwd")
    dy = _matmul(dx1, e_wout, "nt", F32, "e_out_bwd_x")
    d_e_wout = _matmul(y, dx1, "tn", F32, "e_out_bwd_w")
    core = lax.axis_index("c").astype(jnp.int32).reshape(1)
    by_owner = lambda full: full.reshape((N_CHIPS, 2, full.shape[0] // N_DEV, full.shape[1]))
    early = {"e_w_out": by_owner(d_e_wout), "o_w_in": by_owner(d_o_win_t), "o_w_out": by_owner(d_o_wout)}
    dq, dk, dv_sb, dg_sb, *swapped = _sb_bwd(p, bl, seq, sb_cols, n_sb, dy, n_sb, sb_o, sb_tot,
                                             swaps=tuple(early.values()))
    partials = [_pair_sum(full, got, core, "pairsum_" + n) for (n, full), got in zip(early.items(), swapped)]
    dys, dr1, dkx1, dv1, dg_rwkv, d_gn_g, d_gn_b, d_r_k = _rwkv_post_bwd(
        ys, r, kx, v, g_rwkv, 0, e_gn_g, e_gn_b, e_r_k, e_ind, e_ind_t, dy, 0)
    dkk_s, dw_s, db_s, dkx_s, dr_s, dv_s, *landed = _scan_bwd(*sk, sv, states, sa_all, _to_scan_v(dys, bl, seq),
                                                              scatters=tuple(partials))
    early_sums = {n: _sum_slabs(parts, "sum_" + n) for n, parts in zip(early, landed)}
    dkk, dw, db, dkx2, dr2 = [_from_scan_k(a, bl, seq) for a in (dkk_s, dw_s, db_s, dkx_s, dr_s)]
    dv2 = _from_scan_v(dv_s, bl, seq)
    dp_rwkv, d_mu, d_wd, d_w0, d_wa, d_a0, d_k_k, d_k_a = _rwkv_prep_bwd(
        p, seq, (width, lora), e_shift_mu, wd, e_w0, wa, e_a0, e_k_k, e_k_a, e_ind, e_ind_t,
        dr1 + dr2, dw, dkx1 + dkx2, dv1 + dv2, dkk, db)
    dp = jnp.concatenate([dp_rwkv, dg_rwkv, dq, dk, dv_sb, dg_sb], axis=1)
    dh0 = _matmul(dp, e_win_t, "nn", F32, "e_in_bwd_x", tm=2048, tk=1280)
    d_e_win_t = _matmul(dp, h0, "tn", F32, "e_in_bwd_w", tm=1280, tn=2048)
    grad_x, d_g0 = _rms_bwd(x2d, norm_g[0:1], dh0, dx1, "rms0_bwd")

    last = by_owner(d_e_win_t)
    last = _pair_sum(last, _pair_swap(last, "swap_e_w_in"), core, "pairsum_e_w_in")
    grads = {
        "e_w_in": _sum_slabs(_chip_scatter(last, "scatter_e_w_in"), "sum_e_w_in").T[None],
        "e_w_out": early_sums["e_w_out"][None],
        "o_w_in": early_sums["o_w_in"].T[None],
        "o_w_out": early_sums["o_w_out"][None],
    }
    small_full = {
        "norm_g": jnp.concatenate([d_g0, d_g1], axis=0), "final_norm_g": d_final_g.reshape(-1),
        "e_shift_mu": d_mu, "e_w_decay_up": d_wd[None], "e_w0": d_w0, "e_a_up": d_wa[None], "e_a0": d_a0,
        "e_k_k": d_k_k, "e_k_a": d_k_a, "e_r_k": d_r_k, "e_gn_g": d_gn_g, "e_gn_b": d_gn_b,
        "o_ln_g": d_ln_g, "o_ln_b": d_ln_b, "o_w_s": d_w_s[None], "o_b_s": d_b_s3[:, :, 0][None],
    }
    small = [n for n in names if n not in big]
    parts = _all_gather(_pack([small_full[n] for n in small]), "gather_small_grads")
    totals = _unpack(_sum_slabs(parts, "sum_small_grads"), [small_full[n].shape for n in small])
    for n, g in zip(small, totals):
        if n in sharded_small:
            size = weights[n].shape[-1]
            g = lax.dynamic_slice_in_dim(g, me * size, size, axis=g.ndim - 1)
        grads[n] = g.reshape(weights[n].shape)

    delta, new_m, new_v = {}, {}, {}
    for n in big:
        shp = weights[n].shape
        flat = lambda a: a.reshape(shp[-2], shp[-1])
        dl, mn, vn = _adamw(flat(weights[n]), flat(grads[n]), flat(mom1[n]), flat(mom2[n]), "adamw_" + n)
        delta[n], new_m[n], new_v[n] = dl.reshape(shp), mn.reshape(shp), vn.reshape(shp)
    packed = [_pack([src[n] for n in small]) for src in (weights, grads, mom1, mom2)]
    outs = _adamw(*packed, "adamw_small")
    shapes = [weights[n].shape for n in small]
    for dst, arr in zip((delta, new_m, new_v), outs):
        for n, a in zip(small, _unpack(arr, shapes)):
            dst[n] = a

    loss = lax.psum(loss_part[0, 0], ("x", "y", "c"))
    return (loss, grad_x.reshape(bl, seq, d), *[grads[n] for n in names], *[delta[n] for n in names],
            *[new_m[n] for n in names], *[new_v[n] for n in names])
```

```python
import functools

import jax
import jax.numpy as jnp
from jax import lax
from jax.experimental import pallas as pl
from jax.experimental.pallas import tpu as pltpu

F32 = jnp.float32
BF16 = jnp.bfloat16

N_DEV = 8
RWKV_HEAD = 64
SB_HEAD = 128
SB_BLOCK = 128
SGU_CHUNK = 128
SGU_GROUPS = 16
RMS_EPS = 1e-6
GN_EPS = 64e-5
LN_EPS = 1e-5
L2_EPS = 1e-12
ADAM_LR = 0.001
ADAM_B1 = 0.9
ADAM_B2 = 0.999
ADAM_EPS = 1e-08
ADAM_WD = 0.01
ADAM_STEP = 10

VMEM_LIMIT_V7X = 56 * 1024 * 1024
LANES = 128
SCAN_STEPS_PER_BLOCK = 8


def _params(*sem):
    return pltpu.CompilerParams(dimension_semantics=sem, vmem_limit_bytes=VMEM_LIMIT_V7X)


def _tile(n, target, mult):
    best = None
    d = mult
    while d <= min(n, target):
        if n % d == 0:
            best = d
        d += mult
    return n if best is None else best


def _remote(src, dst, send_sems, recv_sems, k, dev):
    return pltpu.make_async_remote_copy(src_ref=src, dst_ref=dst, send_sem=send_sems.at[k], recv_sem=recv_sems.at[k],
                                        device_id=dev, device_id_type=pl.DeviceIdType.MESH)


_HBM = pl.BlockSpec(memory_space=pl.ANY)


GATHER_COPIES = 7


def _gather_phases(src_ref, out_ref, send_sems, recv_sems, local_sems, n):
    x, y, c = lax.axis_index("x"), lax.axis_index("y"), lax.axis_index("c")
    me, sibling = (x, y, c), (x, y, 1 - c)
    chips = [(1 - x, y), (x, 1 - y), (1 - x, 1 - y)]

    def slot(px, py, pc):
        return out_ref.at[4 * px + 2 * py + pc]

    def copy(k, block, to, own=False):
        return _remote(src_ref if own else slot(*block), slot(*block), send_sems, recv_sems,
                       GATHER_COPIES * n + k, to)

    mine = pltpu.make_async_copy(src_ref, slot(*me), local_sems.at[n])
    first = [copy(0, me, sibling, True)] + [copy(1 + j, me, (*chip, c), True) for j, chip in enumerate(chips)]
    passed = [copy(4 + j, (*chip, c), sibling) for j, chip in enumerate(chips)]

    def start():
        mine.start()
        for cp in first:
            cp.start()

    def relay():
        for j, chip in enumerate(chips):
            copy(1 + j, (*chip, c), me).wait_recv()
            passed[j].start()

    def finish():
        copy(0, sibling, me).wait_recv()
        for j, chip in enumerate(chips):
            copy(4 + j, (*chip, 1 - c), me).wait_recv()
        for cp in first + passed:
            cp.wait_send()
        mine.wait()

    return start, relay, finish


def _gather_scratch(n):
    return [pltpu.SemaphoreType.DMA((GATHER_COPIES * n,)), pltpu.SemaphoreType.DMA((GATHER_COPIES * n,)),
            pltpu.SemaphoreType.DMA((n,))]


def _gathered_shape(src):
    return jax.ShapeDtypeStruct((N_DEV,) + tuple(src.shape), src.dtype)


def _all_gather(src, name):
    def body(src_ref, out_ref, send_sems, recv_sems, local_sems):
        for phase in _gather_phases(src_ref, out_ref, send_sems, recv_sems, local_sems, 0):
            phase()

    return pl.pallas_call(
        body, name=name, out_shape=_gathered_shape(src), in_specs=[_HBM], out_specs=_HBM,
        scratch_shapes=_gather_scratch(1),
    )(src)


N_CHIPS = N_DEV // 2
SCATTER_COPIES = N_CHIPS - 1


def _swap_phases(src_ref, out_ref, send_sems, recv_sems, n):
    x, y, c = lax.axis_index("x"), lax.axis_index("y"), lax.axis_index("c")
    copies = [_remote(src_ref.at[q, 1 - c], out_ref.at[q], send_sems, recv_sems, N_CHIPS * n + q, (x, y, 1 - c))
              for q in range(N_CHIPS)]

    def start():
        for cp in copies:
            cp.start()

    def finish():
        for cp in copies:
            cp.wait_recv()
        for cp in copies:
            cp.wait_send()

    return start, finish


def _swapped_shape(full):
    return jax.ShapeDtypeStruct((N_CHIPS,) + tuple(full.shape[2:]), full.dtype)


def _swap_scratch(n):
    return [pltpu.SemaphoreType.DMA((N_CHIPS * n,)), pltpu.SemaphoreType.DMA((N_CHIPS * n,))]


def _pair_swap(full, name):
    def body(src_ref, out_ref, send_sems, recv_sems):
        for phase in _swap_phases(src_ref, out_ref, send_sems, recv_sems, 0):
            phase()

    return pl.pallas_call(body, name=name, out_shape=_swapped_shape(full), in_specs=[_HBM], out_specs=_HBM,
                          scratch_shapes=_swap_scratch(1))(full)


def _pair_sum(full, got, core, name):
    n_chips, _, rows, cols = full.shape
    tr = _tile(rows, max(16, (1 << 19) // cols), 16)

    def body(core_ref, a_ref, b_ref, o_ref):
        o_ref[...] = (a_ref[...] + b_ref[...]).astype(BF16)

    return pl.pallas_call(
        body, name=name,
        grid_spec=pltpu.PrefetchScalarGridSpec(
            num_scalar_prefetch=1, grid=(n_chips, rows // tr),
            in_specs=[pl.BlockSpec((None, None, tr, cols), lambda q, i, s: (q, s[0], i, 0)),
                      pl.BlockSpec((None, tr, cols), lambda q, i, s: (q, i, 0))],
            out_specs=pl.BlockSpec((None, tr, cols), lambda q, i, s: (q, i, 0))),
        out_shape=jax.ShapeDtypeStruct((n_chips, rows, cols), BF16),
        compiler_params=_params("parallel", "parallel"),
    )(core, full, got)


def _scatter_phases(src_ref, out_ref, send_sems, recv_sems, local_sems, n):
    x, y, c = lax.axis_index("x"), lax.axis_index("y"), lax.axis_index("c")
    here = 2 * x + y
    chips = [(1 - x, y), (x, 1 - y), (1 - x, 1 - y)]
    local = pltpu.make_async_copy(src_ref.at[here], out_ref.at[here], local_sems.at[n])
    sends = [_remote(src_ref.at[2 * px + py], out_ref.at[here], send_sems, recv_sems, SCATTER_COPIES * n + j,
                     (px, py, c)) for j, (px, py) in enumerate(chips)]
    recvs = [_remote(src_ref.at[2 * px + py], out_ref.at[2 * px + py], send_sems, recv_sems,
                     SCATTER_COPIES * n + j, (px, py, c)) for j, (px, py) in enumerate(chips)]

    def start():
        local.start()
        for cp in sends:
            cp.start()

    def finish():
        for cp in recvs:
            cp.wait_recv()
        for cp in sends:
            cp.wait_send()
        local.wait()

    return start, finish


def _scatter_scratch(n):
    return [pltpu.SemaphoreType.DMA((SCATTER_COPIES * n,)), pltpu.SemaphoreType.DMA((SCATTER_COPIES * n,)),
            pltpu.SemaphoreType.DMA((n,))]


def _chip_scatter(parts, name):
    def body(src_ref, out_ref, send_sems, recv_sems, local_sems):
        for phase in _scatter_phases(src_ref, out_ref, send_sems, recv_sems, local_sems, 0):
            phase()

    return pl.pallas_call(body, name=name, out_shape=jax.ShapeDtypeStruct(parts.shape, parts.dtype),
                          in_specs=[_HBM], out_specs=_HBM, scratch_shapes=_scatter_scratch(1))(parts)


def _matmul(a, b, mode, out_dtype, name, res=None, tm=1024, tn=1024, tk=1024, scatters=()):
    if mode == "nn":
        (m, k), (k2, n) = a.shape, b.shape
    elif mode == "nt":
        (m, k), (n, k2) = a.shape, b.shape
    else:
        (k, m), (k2, n) = a.shape, b.shape
    assert k == k2, (a.shape, b.shape, mode)
    tm, tn, tk = _tile(m, tm, 128), _tile(n, tn, 128), _tile(k, tk, 128)
    nk = k // tk
    if mode == "nn":
        a_spec = pl.BlockSpec((tm, tk), lambda i, j, kk: (i, kk))
        b_spec = pl.BlockSpec((tk, tn), lambda i, j, kk: (kk, j))
        dims = (((1,), (0,)), ((), ()))
    elif mode == "nt":
        a_spec = pl.BlockSpec((tm, tk), lambda i, j, kk: (i, kk))
        b_spec = pl.BlockSpec((tn, tk), lambda i, j, kk: (j, kk))
        dims = (((1,), (1,)), ((), ()))
    else:
        a_spec = pl.BlockSpec((tk, tm), lambda i, j, kk: (kk, i))
        b_spec = pl.BlockSpec((tk, tn), lambda i, j, kk: (kk, j))
        dims = (((0,), (0,)), ((), ()))
    o_spec = pl.BlockSpec((tm, tn), lambda i, j, kk: (i, j))
    has_res = res is not None
    n_in = 3 if has_res else 2
    nsc = len(scatters)
    grid = (m // tm, n // tn, nk)

    def body(*refs):
        a_ref, b_ref = refs[:2]
        r_ref = refs[2] if has_res else None
        o_ref = refs[n_in + nsc]
        acc_ref = refs[n_in + 2 * nsc + 1]
        kk = pl.program_id(2)

        if nsc:
            send_sems, recv_sems, local_sems = refs[n_in + 2 * nsc + 2:]
            phases = [_scatter_phases(refs[n_in + s], refs[n_in + nsc + 1 + s], send_sems, recv_sems, local_sems, s)
                      for s in range(nsc)]
            for k, at in enumerate(((0, 0, 0), tuple(g - 1 for g in grid))):
                @pl.when((pl.program_id(0) == at[0]) & (pl.program_id(1) == at[1]) & (kk == at[2]))
                def _(k=k):
                    for ph in phases:
                        ph[k]()

        def product():
            return lax.dot_general(a_ref[...].astype(BF16), b_ref[...].astype(BF16), dims,
                                   preferred_element_type=F32)

        @pl.when(kk == 0)
        def _():
            acc_ref[...] = product()

        @pl.when(kk > 0)
        def _():
            acc_ref[...] += product()

        @pl.when(kk == nk - 1)
        def _():
            out = acc_ref[...]
            if has_res:
                out = out + r_ref[...]
            o_ref[...] = out.astype(out_dtype)

    ins = [a, b] + ([res] if has_res else [])
    specs = [a_spec, b_spec] + ([o_spec] if has_res else [])
    out = jax.ShapeDtypeStruct((m, n), out_dtype)
    if not nsc:
        return pl.pallas_call(
            body, name=name, grid=grid, out_shape=out, in_specs=specs, out_specs=o_spec,
            scratch_shapes=[pltpu.VMEM((tm, tn), F32)],
            compiler_params=_params("parallel", "parallel", "arbitrary"),
        )(*ins)
    return pl.pallas_call(
        body, name=name, grid=grid,
        out_shape=(out, *[jax.ShapeDtypeStruct(s.shape, s.dtype) for s in scatters]),
        in_specs=specs + [_HBM] * nsc, out_specs=(o_spec, *[_HBM] * nsc),
        scratch_shapes=[pltpu.VMEM((tm, tn), F32)] + _scatter_scratch(nsc),
        compiler_params=_params("arbitrary", "arbitrary", "arbitrary"),
    )(*ins, *scatters)


def _rms_fwd(x, g, name):
    t, d = x.shape
    tm = _tile(t, 256, 8)

    def body(x_ref, g_ref, h_ref):
        xv = x_ref[...]
        rstd = lax.rsqrt(jnp.mean(xv * xv, axis=-1, keepdims=True) + RMS_EPS)
        h_ref[...] = (xv * rstd * g_ref[...]).astype(BF16)

    return pl.pallas_call(
        body, name=name, grid=(t // tm,),
        out_shape=jax.ShapeDtypeStruct((t, d), BF16),
        in_specs=[pl.BlockSpec((tm, d), lambda i: (i, 0)), pl.BlockSpec((1, d), lambda i: (0, 0))],
        out_specs=pl.BlockSpec((tm, d), lambda i: (i, 0)),
        compiler_params=_params("parallel"),
    )(x, g)


def _rms_bwd(x, g, dh, dres, name):
    t, d = x.shape
    tm = _tile(t, 256, 8)

    def body(x_ref, g_ref, dh_ref, dres_ref, dx_ref, dg_ref):
        @pl.when(pl.program_id(0) == 0)
        def _():
            dg_ref[...] = jnp.zeros_like(dg_ref)

        xv = x_ref[...]
        rstd = lax.rsqrt(jnp.mean(xv * xv, axis=-1, keepdims=True) + RMS_EPS)
        xhat = xv * rstd
        dh_v = dh_ref[...]
        dg_ref[...] += jnp.sum(dh_v * xhat, axis=0, keepdims=True)
        dxh = dh_v * g_ref[...]
        dx_ref[...] = dres_ref[...] + rstd * (dxh - xhat * jnp.mean(dxh * xhat, axis=-1, keepdims=True))

    row = pl.BlockSpec((tm, d), lambda i: (i, 0))
    vec = pl.BlockSpec((1, d), lambda i: (0, 0))
    return pl.pallas_call(
        body, name=name, grid=(t // tm,),
        out_shape=(jax.ShapeDtypeStruct((t, d), F32), jax.ShapeDtypeStruct((1, d), F32)),
        in_specs=[row, vec, row, row], out_specs=(row, vec),
        compiler_params=_params("arbitrary"),
    )(x, g, dh, dres)


def _final_loss(x, g, target, name):
    t, d = x.shape
    tm = _tile(t, 256, 8)

    def body(x_ref, g_ref, t_ref, dx_ref, dg_ref, loss_ref):
        @pl.when(pl.program_id(0) == 0)
        def _():
            dg_ref[...] = jnp.zeros_like(dg_ref)
            loss_ref[...] = jnp.zeros_like(loss_ref)

        xv = x_ref[...]
        rstd = lax.rsqrt(jnp.mean(xv * xv, axis=-1, keepdims=True) + RMS_EPS)
        xhat = xv * rstd
        gv = g_ref[...]
        err = xhat * gv - t_ref[...]
        loss_ref[...] += 0.5 * jnp.sum(jnp.mean(err * err, axis=-1, keepdims=True), axis=0, keepdims=True)
        dout = err * (1.0 / d)
        dg_ref[...] += jnp.sum(dout * xhat, axis=0, keepdims=True)
        dxh = dout * gv
        dx_ref[...] = rstd * (dxh - xhat * jnp.mean(dxh * xhat, axis=-1, keepdims=True))

    row = pl.BlockSpec((tm, d), lambda i: (i, 0))
    vec = pl.BlockSpec((1, d), lambda i: (0, 0))
    return pl.pallas_call(
        body, name=name, grid=(t // tm,),
        out_shape=(jax.ShapeDtypeStruct((t, d), F32), jax.ShapeDtypeStruct((1, d), F32),
                   jax.ShapeDtypeStruct((1, 1), F32)),
        in_specs=[row, vec, row], out_specs=(row, vec, pl.BlockSpec((1, 1), lambda i: (0, 0))),
        compiler_params=_params("arbitrary"),
    )(x, g, target)


def _dot32(a, b):
    return jnp.dot(a, b, precision=lax.Precision.HIGH, preferred_element_type=F32)


def _head_sum(v, e, et):
    return _dot32(_dot32(v, e), et)


def _log_sigmoid(z):
    return jnp.minimum(z, 0.0) - jnp.log1p(jnp.exp(-jnp.abs(z)))


def _silu(g):
    return g * jax.nn.sigmoid(g)


def _prep_math(k, wlo, alo, wd, w0, wa, a0, k_k, k_a, e, et):
    wl = w0 + _dot32(jnp.tanh(wlo), wd)
    w_log = _log_sigmoid(wl) - 0.5
    w = jnp.exp(-jnp.exp(w_log))
    a = jax.nn.sigmoid(a0 + _dot32(alo, wa))
    kk0 = k * k_k
    kk = kk0 * lax.rsqrt(jnp.maximum(_head_sum(kk0 * kk0, e, et), L2_EPS * L2_EPS))
    kx = k * (1.0 + (a - 1.0) * k_a)
    return w, kx, kk, kk * a


def _post_math(ys, r, kx, v, g, gn_g, gn_b, r_k, e, et):
    inv = 1.0 / RWKV_HEAD
    mu = _head_sum(ys, e, et) * inv
    dlt = ys - mu
    var = _head_sum(dlt * dlt, e, et) * inv
    y = dlt * lax.rsqrt(var + GN_EPS) * gn_g + gn_b
    bonus = _head_sum(r * kx * r_k, e, et) * v
    return (y + bonus) * _silu(g)


def _shifted(p, prev_row, first):
    rows = lax.broadcasted_iota(jnp.int32, p.shape, 0)
    prev = jnp.where(first, 0.0, prev_row)
    return jnp.where(rows == 0, prev, pltpu.roll(p, 1, 0))


def _head_indicator(width):
    ch = lax.broadcasted_iota(jnp.int32, (width, width // RWKV_HEAD), 0) // RWKV_HEAD
    hd = lax.broadcasted_iota(jnp.int32, (width, width // RWKV_HEAD), 1)
    e = (ch == hd).astype(F32)
    return e, e.T


def _rwkv_prep(p, seq, dims, mu, wd, w0, wa, a0, k_k, k_a, e, et):
    t = p.shape[0]
    width, lora = dims
    cols = 3 * width + 2 * lora
    tm = 128
    per_seq = seq // tm

    def body(p_ref, prev_ref, mu_ref, wd_ref, w0_ref, wa_ref, a0_ref, kk_ref, ka_ref, e_ref, et_ref,
             r_out, w_out, kx_out, v_out, kkn_out, b_out):
        i = pl.program_id(0)
        pv = p_ref[...]
        psh = _shifted(pv, prev_ref[7:8, :], i % per_seq == 0)
        ps = pv + mu_ref[...] * (psh - pv)
        r, k, v = ps[:, :width], ps[:, width:2 * width], ps[:, 2 * width:3 * width]
        wlo, alo = ps[:, 3 * width:3 * width + lora], ps[:, 3 * width + lora:]
        w, kx, kk, b = _prep_math(k, wlo, alo, wd_ref[...], w0_ref[...], wa_ref[...], a0_ref[...],
                                  kk_ref[...], ka_ref[...], e_ref[...], et_ref[...])
        r_out[...] = r
        w_out[...] = w
        kx_out[...] = kx
        v_out[...] = v
        kkn_out[...] = kk
        b_out[...] = b

    full = lambda a: pl.BlockSpec(a.shape, lambda i: (0,) * a.ndim)
    out = pl.BlockSpec((tm, width), lambda i: (i, 0))
    return pl.pallas_call(
        body, name="rwkv_prep", grid=(t // tm,),
        out_shape=tuple(jax.ShapeDtypeStruct((t, width), F32) for _ in range(6)),
        in_specs=[pl.BlockSpec((tm, cols), lambda i: (i, 0)),
                  pl.BlockSpec((8, cols), lambda i: (jnp.maximum(i * (tm // 8) - 1, 0), 0)),
                  full(mu), full(wd), full(w0), full(wa), full(a0), full(k_k), full(k_a), full(e), full(et)],
        out_specs=tuple(out for _ in range(6)),
        compiler_params=_params("parallel"),
    )(p, p, mu, wd, w0, wa, a0, k_k, k_a, e, et)


def _rwkv_prep_bwd(p, seq, dims, mu, wd, w0, wa, a0, k_k, k_a, e, et, dr, dw, dkx, dv, dkk, db):
    t = p.shape[0]
    width, lora = dims
    cols = 3 * width + 2 * lora
    tm = 128
    n_tiles = t // tm
    per_seq = seq // tm

    def body(p_ref, prev_ref, mu_ref, wd_ref, w0_ref, wa_ref, a0_ref, kk_ref, ka_ref, e_ref, et_ref,
             dr_ref, dw_ref, dkx_ref, dv_ref, dkk_ref, db_ref,
             dp_out, dmu_out, dwd_out, dw0_out, dwa_out, da0_out, dkk_out, dka_out, carry):
        step = pl.program_id(0)
        i = n_tiles - 1 - step

        @pl.when(step == 0)
        def _():
            for ref in (dmu_out, dwd_out, dw0_out, dwa_out, da0_out, dkk_out, dka_out, carry):
                ref[...] = jnp.zeros_like(ref)

        pv = p_ref[...]
        first = i % per_seq == 0
        psh = _shifted(pv, prev_ref[7:8, :], first)
        muv = mu_ref[...]
        ps = pv + muv * (psh - pv)
        k = ps[:, width:2 * width]
        wlo, alo = ps[:, 3 * width:3 * width + lora], ps[:, 3 * width + lora:]
        ev, etv = e_ref[...], et_ref[...]
        _, vjp = jax.vjp(lambda *a: _prep_math(*a, ev, etv), k, wlo, alo, wd_ref[...], w0_ref[...],
                         wa_ref[...], a0_ref[...], kk_ref[...], ka_ref[...])
        dk, dwlo, dalo, dwd, dw0, dwa, da0, dk_k, dk_a = vjp(
            (dw_ref[...], dkx_ref[...], dkk_ref[...], db_ref[...]))
        dps = jnp.concatenate([dr_ref[...], dk, dv_ref[...], dwlo, dalo], axis=1)
        dmu_out[...] += jnp.sum(dps * (psh - pv), axis=0, keepdims=True)
        dwd_out[...] += dwd
        dw0_out[...] += dw0
        dwa_out[...] += dwa
        da0_out[...] += da0
        dkk_out[...] += dk_k
        dka_out[...] += dk_a
        dsh = dps * muv
        rows = lax.broadcasted_iota(jnp.int32, dsh.shape, 0)
        nxt = jnp.where(rows == tm - 1, carry[...], pltpu.roll(dsh, tm - 1, 0))
        dp_out[...] = (dps * (1.0 - muv) + nxt).astype(BF16)
        carry[...] = jnp.where(first, 0.0, dsh[0:1, :])

    full = lambda a: pl.BlockSpec(a.shape, lambda s: (0,) * a.ndim)
    tok = pl.BlockSpec((tm, width), lambda s: (n_tiles - 1 - s, 0))
    vec = lambda n: jax.ShapeDtypeStruct((1, n), F32)
    outs = (jax.ShapeDtypeStruct((t, cols), BF16), vec(cols), jax.ShapeDtypeStruct(wd.shape, F32), vec(width),
            jax.ShapeDtypeStruct(wa.shape, F32), vec(width), vec(width), vec(width))
    return pl.pallas_call(
        body, name="rwkv_prep_bwd", grid=(n_tiles,),
        out_shape=outs,
        in_specs=[pl.BlockSpec((tm, cols), lambda s: (n_tiles - 1 - s, 0)),
                  pl.BlockSpec((8, cols), lambda s: (jnp.maximum((n_tiles - 1 - s) * (tm // 8) - 1, 0), 0)),
                  full(mu), full(wd), full(w0), full(wa), full(a0), full(k_k), full(k_a), full(e), full(et),
                  tok, tok, tok, tok, tok, tok],
        out_specs=(pl.BlockSpec((tm, cols), lambda s: (n_tiles - 1 - s, 0)),) + tuple(
            pl.BlockSpec(o.shape, lambda s: (0, 0)) for o in outs[1:]),
        scratch_shapes=[pltpu.VMEM((1, cols), F32)],
        compiler_params=_params("arbitrary"),
    )(p, p, mu, wd, w0, wa, a0, k_k, k_a, e, et, dr, dw, dkx, dv, dkk, db)


def _rwkv_post(ys, r, kx, v, p, g_col, gn_g, gn_b, r_k, e, et):
    t, width = ys.shape
    tm = 256

    def body(ys_ref, r_ref, kx_ref, v_ref, g_ref, gg_ref, gb_ref, rk_ref, e_ref, et_ref, out_ref):
        out_ref[...] = _post_math(ys_ref[...], r_ref[...], kx_ref[...], v_ref[...], g_ref[...], gg_ref[...],
                                  gb_ref[...], rk_ref[...], e_ref[...], et_ref[...]).astype(BF16)

    tok = pl.BlockSpec((tm, width), lambda i: (i, 0))
    full = lambda a: pl.BlockSpec(a.shape, lambda i: (0,) * a.ndim)
    return pl.pallas_call(
        body, name="rwkv_post", grid=(t // tm,),
        out_shape=jax.ShapeDtypeStruct((t, width), BF16),
        in_specs=[tok, tok, tok, tok, pl.BlockSpec((tm, width), lambda i: (i, g_col)),
                  full(gn_g), full(gn_b), full(r_k), full(e), full(et)],
        out_specs=tok,
        compiler_params=_params("parallel"),
    )(ys, r, kx, v, p, gn_g, gn_b, r_k, e, et)


def _rwkv_post_bwd(ys, r, kx, v, p, g_col, gn_g, gn_b, r_k, e, et, dy, dy_col):
    t, width = ys.shape
    tm = 128

    def body(ys_ref, r_ref, kx_ref, v_ref, g_ref, gg_ref, gb_ref, rk_ref, e_ref, et_ref, dy_ref,
             dys_out, dr_out, dkx_out, dv_out, dg_out, dgg_out, dgb_out, drk_out):
        @pl.when(pl.program_id(0) == 0)
        def _():
            for ref in (dgg_out, dgb_out, drk_out):
                ref[...] = jnp.zeros_like(ref)

        ev, etv = e_ref[...], et_ref[...]
        _, vjp = jax.vjp(lambda *a: _post_math(*a, ev, etv), ys_ref[...], r_ref[...], kx_ref[...], v_ref[...],
                         g_ref[...], gg_ref[...], gb_ref[...], rk_ref[...])
        dys, dr, dkx, dv, dg, dgg, dgb, drk = vjp(dy_ref[...])
        dys_out[...] = dys
        dr_out[...] = dr
        dkx_out[...] = dkx
        dv_out[...] = dv
        dg_out[...] = dg.astype(BF16)
        dgg_out[...] += dgg
        dgb_out[...] += dgb
        drk_out[...] += drk

    tok = pl.BlockSpec((tm, width), lambda i: (i, 0))
    full = lambda a: pl.BlockSpec(a.shape, lambda i: (0,) * a.ndim)
    big = jax.ShapeDtypeStruct((t, width), F32)
    vec = jax.ShapeDtypeStruct((1, width), F32)
    vspec = pl.BlockSpec((1, width), lambda i: (0, 0))
    return pl.pallas_call(
        body, name="rwkv_post_bwd", grid=(t // tm,),
        out_shape=(big, big, big, big, jax.ShapeDtypeStruct((t, width), BF16), vec, vec, vec),
        in_specs=[tok, tok, tok, tok, pl.BlockSpec((tm, width), lambda i: (i, g_col)),
                  full(gn_g), full(gn_b), full(r_k), full(e), full(et),
                  pl.BlockSpec((tm, width), lambda i: (i, dy_col))],
        out_specs=(tok, tok, tok, tok, tok, vspec, vspec, vspec),
        compiler_params=_params("arbitrary"),
    )(ys, r, kx, v, p, gn_g, gn_b, r_k, e, et, dy)


def _to_scan_k(a, bl, seq):
    h = a.shape[1] // RWKV_HEAD
    twice = jnp.broadcast_to(a.reshape(1, bl, seq, h, RWKV_HEAD), (2, bl, seq, h, RWKV_HEAD))
    return twice.transpose(2, 4, 0, 1, 3).reshape(seq, RWKV_HEAD, 2 * bl * h)


def _to_scan_v(a, bl, seq):
    h = a.shape[1] // RWKV_HEAD
    half = RWKV_HEAD // 2
    return a.reshape(bl, seq, h, 2, half).transpose(1, 4, 3, 0, 2).reshape(seq, half, 2 * bl * h)


def _from_scan_k(a, bl, seq):
    h = a.shape[2] // (2 * bl)
    a = a[:, :, :bl * h].reshape(seq, RWKV_HEAD, bl, h).transpose(2, 0, 3, 1)
    return a.reshape(bl * seq, h * RWKV_HEAD)


def _from_scan_v(a, bl, seq):
    half = RWKV_HEAD // 2
    h = a.shape[2] // (2 * bl)
    a = a.reshape(seq, half, 2, bl, h).transpose(3, 0, 4, 2, 1)
    return a.reshape(bl * seq, h * RWKV_HEAD)


def _scan_fwd(kk, w, b, kx, r, v, gathers=()):
    seq, nk, lanes = kk.shape
    nv = v.shape[1]
    tt = SCAN_STEPS_PER_BLOCK
    nblk = seq // tt
    ng = len(gathers)

    def body(*refs):
        kk_ref, w_ref, b_ref, kx_ref, r_ref, v_ref = refs[:6]
        g_src = refs[6:6 + ng]
        y_ref, st_ref, sa_ref = refs[6 + ng:9 + ng]
        g_out = refs[9 + ng:9 + 2 * ng]
        s_ref = refs[9 + 2 * ng]
        pid = pl.program_id(0)

        @pl.when(pid == 0)
        def _():
            s_ref[...] = jnp.zeros_like(s_ref)

        if ng:
            send_sems, recv_sems, local_sems = refs[10 + 2 * ng:]
            phases = [_gather_phases(g_src[n], g_out[n], send_sems, recv_sems, local_sems, n) for n in range(ng)]
            for k, at in enumerate((0, (3 * nblk) // 4, nblk - 1)):
                @pl.when(pid == at)
                def _(k=k):
                    for ph in phases:
                        ph[k]()

        def step(i, carry):
            kkv, wv, bv, kxv, rv = kk_ref[i], w_ref[i], b_ref[i], kx_ref[i], r_ref[i]
            for j in range(nv):
                s_old = s_ref[j]
                sa = -jnp.sum(s_old * kkv, axis=0, keepdims=True)
                s_new = s_old * wv + sa * bv + v_ref[i, j:j + 1, :] * kxv
                s_ref[j] = s_new
                st_ref[i, j] = s_new
                sa_ref[i, j:j + 1, :] = sa
                y_ref[i, j:j + 1, :] = jnp.sum(s_new * rv, axis=0, keepdims=True)
            return carry

        lax.fori_loop(0, tt, step, 0)

    krow = pl.BlockSpec((tt, nk, lanes), lambda i: (i, 0, 0))
    vrow = pl.BlockSpec((tt, nv, lanes), lambda i: (i, 0, 0))
    vshape = jax.ShapeDtypeStruct((seq, nv, lanes), F32)
    return pl.pallas_call(
        body, name="rwkv_scan_fwd", grid=(nblk,),
        out_shape=(vshape, jax.ShapeDtypeStruct((seq, nv, nk, lanes), F32), vshape,
                   *[_gathered_shape(g) for g in gathers]),
        in_specs=[krow, krow, krow, krow, krow, vrow] + [_HBM] * ng,
        out_specs=(vrow, pl.BlockSpec((tt, nv, nk, lanes), lambda i: (i, 0, 0, 0)), vrow, *[_HBM] * ng),
        scratch_shapes=[pltpu.VMEM((nv, nk, lanes), F32)] + (_gather_scratch(ng) if ng else []),
        compiler_params=_params("arbitrary"),
    )(kk, w, b, kx, r, v, *gathers)


def _scan_bwd(kk, w, b, kx, r, v, states, sa_all, dy, scatters=()):
    seq, nk, lanes = kk.shape
    nv = v.shape[1]
    tt = SCAN_STEPS_PER_BLOCK
    nblk = seq // tt
    nsc = len(scatters)

    def both_halves(a):
        return a + pltpu.roll(a, lanes // 2, 1)

    def body(*refs):
        kk_ref, w_ref, b_ref, kx_ref, r_ref, v_ref, st_ref, before_ref, sa_ref, dy_ref = refs[:10]
        dkk_ref, dw_ref, db_ref, dkx_ref, dr_ref, dv_ref = refs[10 + nsc:16 + nsc]
        g_ref = refs[16 + 2 * nsc]

        if nsc:
            send_sems, recv_sems, local_sems = refs[17 + 2 * nsc:]
            phases = [_scatter_phases(refs[10 + n], refs[16 + nsc + n], send_sems, recv_sems, local_sems, n)
                      for n in range(nsc)]
            for k, at in enumerate((0, nblk - 1)):
                @pl.when(pl.program_id(0) == at)
                def _(k=k):
                    for ph in phases:
                        ph[k]()

        @pl.when(pl.program_id(0) == 0)
        def _():
            g_ref[...] = jnp.zeros_like(g_ref)

        def one_step(i, state_before):
            kkv, wv, bv, kxv, rv = kk_ref[i], w_ref[i], b_ref[i], kx_ref[i], r_ref[i]
            zero = jnp.zeros((nk, lanes), F32)
            a_r, a_w, a_b, a_kx, a_kk = zero, zero, zero, zero, zero
            for j in range(nv):
                s_old = state_before(j)
                s_new = st_ref[i, j]
                vv = v_ref[i, j:j + 1, :]
                dyv = dy_ref[i, j:j + 1, :]
                sa = sa_ref[i, j:j + 1, :]
                g = g_ref[j] + dyv * rv
                a_r = a_r + s_new * dyv
                a_w = a_w + g * s_old
                dsa = jnp.sum(g * bv, axis=0, keepdims=True)
                a_b = a_b + g * sa
                dv_ref[i, j:j + 1, :] = jnp.sum(g * kxv, axis=0, keepdims=True)
                a_kx = a_kx + g * vv
                a_kk = a_kk + s_old * dsa
                g_ref[j] = g * wv - dsa * kkv
            dr_ref[i] = both_halves(a_r)
            dw_ref[i] = both_halves(a_w)
            db_ref[i] = both_halves(a_b)
            dkx_ref[i] = both_halves(a_kx)
            dkk_ref[i] = -both_halves(a_kk)

        def step(n, carry):
            i = tt - 1 - n
            one_step(i, lambda j: st_ref[i - 1, j])
            return carry

        lax.fori_loop(0, tt - 1, step, 0)
        at_start = pl.program_id(0) == nblk - 1
        one_step(0, lambda j: jnp.where(at_start, 0.0, before_ref[0, j]))

    rev = lambda i: nblk - 1 - i
    krow = pl.BlockSpec((tt, nk, lanes), lambda i: (rev(i), 0, 0))
    vrow = pl.BlockSpec((tt, nv, lanes), lambda i: (rev(i), 0, 0))
    kshape = jax.ShapeDtypeStruct((seq, nk, lanes), F32)
    return pl.pallas_call(
        body, name="rwkv_scan_bwd", grid=(nblk,),
        out_shape=(kshape, kshape, kshape, kshape, kshape, jax.ShapeDtypeStruct((seq, nv, lanes), F32),
                   *[jax.ShapeDtypeStruct(s.shape, s.dtype) for s in scatters]),
        in_specs=[krow, krow, krow, krow, krow, vrow,
                  pl.BlockSpec((tt, nv, nk, lanes), lambda i: (rev(i), 0, 0, 0)),
                  pl.BlockSpec((1, nv, nk, lanes), lambda i: (jnp.maximum(rev(i) * tt - 1, 0), 0, 0, 0)),
                  vrow, vrow] + [_HBM] * nsc,
        out_specs=(krow, krow, krow, krow, krow, vrow, *[_HBM] * nsc),
        scratch_shapes=[pltpu.VMEM((nv, nk, lanes), F32)] + (_scatter_scratch(nsc) if nsc else []),
        compiler_params=_params("arbitrary"),
    )(kk, w, b, kx, r, v, states, states, sa_all, dy, *scatters)


_NT = (((1,), (1,)), ((), ()))
_TN = (((0,), (0,)), ((), ()))
SB_SCALE = 1.0 / (SB_HEAD ** 0.5)


def _split_dot(a, b):
    hi = a.astype(BF16)
    lo = (a - hi.astype(F32)).astype(BF16)
    return jnp.dot(hi, b, preferred_element_type=F32) + jnp.dot(lo, b, preferred_element_type=F32)


def _sb_masks():
    blk = SB_BLOCK
    row = lax.broadcasted_iota(jnp.int32, (blk, blk), 0)
    col = lax.broadcasted_iota(jnp.int32, (blk, blk), 1)
    ones = jnp.ones((blk, blk), BF16)
    fwd = jnp.concatenate([(row > col).astype(BF16), ones], axis=1)
    bwd = jnp.concatenate([(col > row).astype(BF16), ones], axis=1)
    return row, col, fwd, bwd


def _split_dots(xs, b):
    his = [a.astype(BF16) for a in xs]
    los = [(a - hi.astype(F32)).astype(BF16) for a, hi in zip(xs, his)]
    tops = [jnp.dot(hi, b, preferred_element_type=F32) for hi in his]
    return [top + jnp.dot(lo, b, preferred_element_type=F32) for top, lo in zip(tops, los)]


SB_HEADS_PER_STEP = 2


def _sb_chains(bl):
    return [(b, slice(hh * SB_HEAD, (hh + 1) * SB_HEAD)) for b in range(bl) for hh in range(SB_HEADS_PER_STEP)]


def _sb_specs(bl, seq, cols, n_heads):
    hp = SB_HEADS_PER_STEP
    assert n_heads % hp == 0 and all(c % hp == 0 for c in cols)
    qspec = lambda col: pl.BlockSpec((bl, SB_BLOCK, hp * SB_HEAD), lambda h, i: (0, i, col // hp + h))
    kspec = lambda col: pl.BlockSpec((bl, seq, hp * SB_HEAD), lambda h, i: (0, 0, col // hp + h))
    return qspec, kspec


def _sb_fwd(p, bl, seq, cols, n_heads):
    t = p.shape[0]
    nq = seq // SB_BLOCK
    q_col, k_col, v_col, g_col = cols
    blk = SB_BLOCK
    chains = _sb_chains(bl)

    def body(q_ref, k_ref, v_ref, g_ref, out_ref, o_ref, tot_ref):
        qi = pl.program_id(1)
        row, col, mix, _ = _sb_masks()
        qbs = [q_ref[b, :, ln].astype(BF16) for b, ln in chains]

        def step(n, carry):
            j = qi - n
            rows = pl.ds(pl.multiple_of(j * blk, blk), blk)
            causal = (j * blk + col) < (qi * blk + row)
            zs = [lax.dot_general(qb, k_ref[b, rows, ln].astype(BF16), _NT, preferred_element_type=F32) * SB_SCALE
                  for (b, ln), qb in zip(chains, qbs)]
            lszs = [_log_sigmoid(z) for z in zs]
            boths = _split_dots([jnp.where(causal, lsz - z, 0.0) for lsz, z in zip(lszs, zs)], mix)
            atts = [jnp.where(causal, jnp.exp(lsz + both[:, :blk] + later), 0.0).astype(BF16)
                    for lsz, both, (_, later) in zip(lszs, boths, carry)]
            outs = [out + jnp.dot(att, v_ref[b, rows, ln].astype(BF16), preferred_element_type=F32)
                    for (b, ln), att, (out, _) in zip(chains, atts, carry)]
            return tuple((out, later + both[:, blk:]) for out, both, (_, later) in zip(outs, boths, carry))

        zero = jnp.zeros((blk, SB_HEAD), F32)
        done = lax.fori_loop(0, qi + 1, step, tuple((zero, zero) for _ in chains))
        for (b, ln), (out, total) in zip(chains, done):
            o_ref[b, :, ln] = out
            tot_ref[b, :, ln] = total
            out_ref[b, :, ln] = (out * _silu(g_ref[b, :, ln])).astype(BF16)

    qspec, kspec = _sb_specs(bl, seq, cols, n_heads)
    width = n_heads * SB_HEAD
    p3 = p.reshape(bl, seq, p.shape[1])
    f32 = jax.ShapeDtypeStruct((bl, seq, width), F32)
    outs = pl.pallas_call(
        body, name="sb_attn_fwd", grid=(n_heads // SB_HEADS_PER_STEP, nq),
        out_shape=(jax.ShapeDtypeStruct((bl, seq, width), BF16), f32, f32),
        in_specs=[qspec(q_col), kspec(k_col), kspec(v_col), qspec(g_col)],
        out_specs=(qspec(0), qspec(0), qspec(0)),
        compiler_params=_params("parallel", "arbitrary"),
    )(p3, p3, p3, p3)
    return tuple(a.reshape(t, width) for a in outs)


def _sb_bwd(p, bl, seq, cols, n_heads, dy, dy_col, o, tot, swaps=()):
    t = p.shape[0]
    nq = seq // SB_BLOCK
    q_col, k_col, v_col, g_col = cols
    blk = SB_BLOCK
    ns = len(swaps)
    n_groups = n_heads // SB_HEADS_PER_STEP
    chains = _sb_chains(bl)

    def body(*refs):
        q_ref, k_ref, v_ref, g_ref, dy_ref, o_ref, tot_ref = refs[:7]
        dq_out, dk_out, dv_out, dg_out = refs[7 + ns:11 + ns]
        dk_acc, dv_acc = refs[11 + 2 * ns:13 + 2 * ns]
        group, qi = pl.program_id(0), pl.program_id(1)

        if ns:
            send_sems, recv_sems = refs[13 + 2 * ns:]
            phases = [_swap_phases(refs[7 + n], refs[11 + ns + n], send_sems, recv_sems, n) for n in range(ns)]
            for k, at in enumerate(((0, 0), (n_groups - 1, nq - 1))):
                @pl.when((group == at[0]) & (qi == at[1]))
                def _(k=k):
                    for ph in phases:
                        ph[k]()

        @pl.when(qi == 0)
        def _():
            dk_acc[...] = jnp.zeros_like(dk_acc)
            dv_acc[...] = jnp.zeros_like(dv_acc)

        row, col, mix, mix_t = _sb_masks()
        qbs, dobs, totals = [], [], []
        for b, ln in chains:
            gate = g_ref[b, :, ln]
            sg = jax.nn.sigmoid(gate)
            dyv = dy_ref[b, :, ln]
            dg_out[b, :, ln] = (dyv * o_ref[b, :, ln] * (sg * (1.0 + gate * (1.0 - sg)))).astype(BF16)
            dobs.append((dyv * (gate * sg)).astype(BF16))
            qbs.append(q_ref[b, :, ln].astype(BF16))
            totals.append(tot_ref[b, :, ln])

        def step(j, carry):
            rows = pl.ds(pl.multiple_of(j * blk, blk), blk)
            causal = (j * blk + col) < (qi * blk + row)
            kbs = [k_ref[b, rows, ln].astype(BF16) for b, ln in chains]
            zs = [lax.dot_general(qb, kb, _NT, preferred_element_type=F32) * SB_SCALE for qb, kb in zip(qbs, kbs)]
            datts = [lax.dot_general(dob, v_ref[b, rows, ln].astype(BF16), _NT, preferred_element_type=F32)
                     for (b, ln), dob in zip(chains, dobs)]
            lszs = [_log_sigmoid(z) for z in zs]
            boths = _split_dots([jnp.where(causal, lsz - z, 0.0) for lsz, z in zip(lszs, zs)], mix)
            seens = [seen + both[:, blk:] for both, (_, seen, _) in zip(boths, carry)]
            atts = [jnp.where(causal, jnp.exp(lsz + both[:, :blk] + (total - seen)), 0.0)
                    for lsz, both, total, seen in zip(lszs, boths, totals, seens)]
            dls = [att * datt for att, datt in zip(atts, datts)]
            for (b, ln), att, dob in zip(chains, atts, dobs):
                dv_acc[b, rows, ln] += lax.dot_general(att.astype(BF16), dob, _TN, preferred_element_type=F32)
            boths_t = _split_dots(dls, mix_t)
            dkeeps = [jnp.where(causal, both_t[:, :blk] + dl_before, 0.0)
                      for both_t, (_, _, dl_before) in zip(boths_t, carry)]
            dzbs = [(((dl + dkeep) * jax.nn.sigmoid(-z) - dkeep) * SB_SCALE).astype(BF16)
                    for dl, dkeep, z in zip(dls, dkeeps, zs)]
            dqs = [dq + jnp.dot(dzb, kb, preferred_element_type=F32) for dzb, kb, (dq, _, _) in zip(dzbs, kbs, carry)]
            for (b, ln), dzb, qb in zip(chains, dzbs, qbs):
                dk_acc[b, rows, ln] += lax.dot_general(dzb, qb, _TN, preferred_element_type=F32)
            return tuple((dq, seen, dl_before + both_t[:, blk:])
                         for dq, seen, both_t, (_, _, dl_before) in zip(dqs, seens, boths_t, carry))

        zero = jnp.zeros((blk, SB_HEAD), F32)
        done = lax.fori_loop(0, qi + 1, step, tuple((zero, zero, zero) for _ in chains))
        for (b, ln), (dq, _, _) in zip(chains, done):
            dq_out[b, :, ln] = dq.astype(BF16)

        @pl.when(qi == nq - 1)
        def _():
            dk_out[...] = dk_acc[...].astype(BF16)
            dv_out[...] = dv_acc[...].astype(BF16)

    qspec, kspec = _sb_specs(bl, seq, cols + (dy_col,), n_heads)
    width = n_heads * SB_HEAD
    shape = jax.ShapeDtypeStruct((bl, seq, width), BF16)
    as3 = lambda a: a.reshape(bl, seq, a.shape[1])
    p3 = as3(p)
    acc = pltpu.VMEM((bl, seq, SB_HEADS_PER_STEP * SB_HEAD), F32)
    outs = pl.pallas_call(
        body, name="sb_attn_bwd", grid=(n_groups, nq),
        out_shape=(shape, shape, shape, shape, *[_swapped_shape(s) for s in swaps]),
        in_specs=[qspec(q_col), kspec(k_col), kspec(v_col), qspec(g_col), qspec(dy_col), qspec(0), qspec(0)]
        + [_HBM] * ns,
        out_specs=(qspec(0), kspec(0), kspec(0), qspec(0), *[_HBM] * ns),
        scratch_shapes=[acc, acc] + (_swap_scratch(ns) if ns else []),
        compiler_params=_params("arbitrary", "arbitrary"),
    )(p3, p3, p3, p3, as3(dy), as3(o), as3(tot), *swaps)
    return tuple(a.reshape(t, width) for a in outs[:4]) + tuple(outs[4:])


def _gelu(x):
    return 0.5 * x * (1.0 + lax.erf(x * (2.0 ** -0.5)))


def _sgu_math(us, vs, gs, ln_g, ln_b, ws, bs):
    width = sum(v.shape[1] for v in vs)
    vg = [_gelu(v) for v in vs]
    mu = sum(jnp.sum(v, axis=1, keepdims=True) for v in vg) * (1.0 / width)
    dl = [v - mu for v in vg]
    var = sum(jnp.sum(d * d, axis=1, keepdims=True) for d in dl) * (1.0 / width)
    rstd = lax.rsqrt(var + LN_EPS)
    n = ws[0].shape[0]
    tri = lax.broadcasted_iota(jnp.int32, (n, n), 0) >= lax.broadcasted_iota(jnp.int32, (n, n), 1)
    outs = []
    for i in range(len(vs)):
        vn = dl[i] * rstd * ln_g[i] + ln_b[i]
        mixed = _dot32(jnp.where(tri, ws[i], 0.0), vn) + bs[i]
        outs.append(_gelu(us[i]) * mixed * _silu(gs[i]))
    return outs


def _sgu_load(p_ref, lng_ref, lnb_ref, ws_ref, bs_ref, width):
    gd = width // SGU_GROUPS
    grp = lambda ref, base, i: ref[:, base + i * gd:base + (i + 1) * gd]
    idx = range(SGU_GROUPS)
    return ([grp(p_ref, 0, i) for i in idx], [grp(p_ref, width, i) for i in idx],
            [grp(p_ref, 2 * width, i) for i in idx], [grp(lng_ref, 0, i) for i in idx],
            [grp(lnb_ref, 0, i) for i in idx], [ws_ref[i] for i in idx], [bs_ref[i] for i in idx])


def _sgu_fwd(p, ln_g, ln_b, w_s, b_s):
    t = p.shape[0]
    width = p.shape[1] // 3
    gd = width // SGU_GROUPS
    tm = SGU_CHUNK

    def body(p_ref, lng_ref, lnb_ref, ws_ref, bs_ref, y_ref):
        outs = _sgu_math(*_sgu_load(p_ref, lng_ref, lnb_ref, ws_ref, bs_ref, width))
        for i in range(SGU_GROUPS):
            y_ref[:, i * gd:(i + 1) * gd] = outs[i].astype(BF16)

    full = lambda a: pl.BlockSpec(a.shape, lambda i: (0,) * a.ndim)
    return pl.pallas_call(
        body, name="sgu_fwd", grid=(t // tm,),
        out_shape=jax.ShapeDtypeStruct((t, width), BF16),
        in_specs=[pl.BlockSpec((tm, 3 * width), lambda i: (i, 0)), full(ln_g), full(ln_b), full(w_s), full(b_s)],
        out_specs=pl.BlockSpec((tm, width), lambda i: (i, 0)),
        compiler_params=_params("parallel"),
    )(p, ln_g, ln_b, w_s, b_s)


def _sgu_bwd(p, ln_g, ln_b, w_s, b_s, dy):
    t = p.shape[0]
    width = p.shape[1] // 3
    gd = width // SGU_GROUPS
    tm = SGU_CHUNK

    def body(p_ref, lng_ref, lnb_ref, ws_ref, bs_ref, dy_ref, dp_out, dlng_out, dlnb_out, dws_out, dbs_out):
        @pl.when(pl.program_id(0) == 0)
        def _():
            for ref in (dlng_out, dlnb_out, dws_out, dbs_out):
                ref[...] = jnp.zeros_like(ref)

        _, vjp = jax.vjp(_sgu_math, *_sgu_load(p_ref, lng_ref, lnb_ref, ws_ref, bs_ref, width))
        dus, dvs, dgs, dlng, dlnb, dws, dbs = vjp(
            [dy_ref[:, i * gd:(i + 1) * gd] for i in range(SGU_GROUPS)])
        for i in range(SGU_GROUPS):
            cols = slice(i * gd, (i + 1) * gd)
            dp_out[:, i * gd:(i + 1) * gd] = dus[i].astype(BF16)
            dp_out[:, width + i * gd:width + (i + 1) * gd] = dvs[i].astype(BF16)
            dp_out[:, 2 * width + i * gd:2 * width + (i + 1) * gd] = dgs[i].astype(BF16)
            dlng_out[:, cols] += dlng[i]
            dlnb_out[:, cols] += dlnb[i]
            dws_out[i] += dws[i]
            dbs_out[i] += dbs[i]

    full = lambda a: pl.BlockSpec(a.shape, lambda i: (0,) * a.ndim)
    like = lambda a: jax.ShapeDtypeStruct(a.shape, F32)
    return pl.pallas_call(
        body, name="sgu_bwd", grid=(t // tm,),
        out_shape=(jax.ShapeDtypeStruct((t, 3 * width), BF16), like(ln_g), like(ln_b), like(w_s), like(b_s)),
        in_specs=[pl.BlockSpec((tm, 3 * width), lambda i: (i, 0)), full(ln_g), full(ln_b), full(w_s), full(b_s),
                  pl.BlockSpec((tm, width), lambda i: (i, 0))],
        out_specs=(pl.BlockSpec((tm, 3 * width), lambda i: (i, 0)), full(ln_g), full(ln_b), full(w_s), full(b_s)),
        compiler_params=_params("arbitrary"),
    )(p, ln_g, ln_b, w_s, b_s, dy)


def _sum_slabs(parts, name):
    n_parts, rows, cols = parts.shape
    tr = _tile(rows, max(16, (1 << 18) // cols), 16)

    def body(p_ref, o_ref):
        acc = p_ref[0].astype(F32)
        for d in range(1, n_parts):
            acc = acc + p_ref[d].astype(F32)
        o_ref[...] = acc

    return pl.pallas_call(
        body, name=name, grid=(rows // tr,),
        out_shape=jax.ShapeDtypeStruct((rows, cols), F32),
        in_specs=[pl.BlockSpec((n_parts, tr, cols), lambda i: (0, i, 0))],
        out_specs=pl.BlockSpec((tr, cols), lambda i: (i, 0)),
        compiler_params=_params("parallel"),
    )(parts)


def _adamw(w, g, m, v, name):
    rows, cols = w.shape
    tr = _tile(rows, max(8, (1 << 18) // cols), 8)

    def body(w_ref, g_ref, m_ref, v_ref, d_out, m_out, v_out):
        gv = g_ref[...]
        mn = ADAM_B1 * m_ref[...] + (1.0 - ADAM_B1) * gv
        vn = ADAM_B2 * v_ref[...] + (1.0 - ADAM_B2) * (gv * gv)
        m_hat = mn / (1.0 - ADAM_B1 ** ADAM_STEP)
        v_hat = vn / (1.0 - ADAM_B2 ** ADAM_STEP)
        d_out[...] = -ADAM_LR * (m_hat / (jnp.sqrt(v_hat) + ADAM_EPS) + ADAM_WD * w_ref[...])
        m_out[...] = mn
        v_out[...] = vn

    blk = pl.BlockSpec((tr, cols), lambda i: (i, 0))
    shape = jax.ShapeDtypeStruct((rows, cols), F32)
    return pl.pallas_call(
        body, name=name, grid=(rows // tr,),
        out_shape=(shape, shape, shape),
        in_specs=[blk, blk, blk, blk], out_specs=(blk, blk, blk),
        compiler_params=_params("parallel"),
    )(w, g, m, v)


PACK_COLS = 1024


def _pack(arrays):
    flat = jnp.concatenate([a.reshape(-1).astype(F32) for a in arrays])
    rows = -(-flat.shape[0] // (8 * PACK_COLS)) * 8
    return jnp.pad(flat, (0, rows * PACK_COLS - flat.shape[0])).reshape(rows, PACK_COLS)


def _unpack(packed, shapes):
    flat = packed.reshape(-1)
    out, at = [], 0
    for s in shapes:
        n = 1
        for d in s:
            n *= d
        out.append(flat[at:at + n].reshape(s))
        at += n
    return out


def kernel(x, norm_g, final_norm_g, e_w_in, e_shift_mu, e_w_decay_up, e_w0, e_a_up, e_a0, e_k_k, e_k_a, e_r_k, e_gn_g, e_gn_b, e_w_out, o_w_in, o_ln_g, o_ln_b, o_w_s, o_b_s, o_w_out, loss_target, m_norm_g, m_final_norm_g, m_e_w_in, m_e_shift_mu, m_e_w_decay_up, m_e_w0, m_e_a_up, m_e_a0, m_e_k_k, m_e_k_a, m_e_r_k, m_e_gn_g, m_e_gn_b, m_e_w_out, m_o_w_in, m_o_ln_g, m_o_ln_b, m_o_w_s, m_o_b_s, m_o_w_out, v_norm_g, v_final_norm_g, v_e_w_in, v_e_shift_mu, v_e_w_decay_up, v_e_w0, v_e_a_up, v_e_a0, v_e_k_k, v_e_k_a, v_e_r_k, v_e_gn_g, v_e_gn_b, v_e_w_out, v_o_w_in, v_o_ln_g, v_o_ln_b, v_o_w_s, v_o_b_s, v_o_w_out):
    weights = dict(norm_g=norm_g, final_norm_g=final_norm_g, e_w_in=e_w_in, e_shift_mu=e_shift_mu,
                   e_w_decay_up=e_w_decay_up, e_w0=e_w0, e_a_up=e_a_up, e_a0=e_a0, e_k_k=e_k_k, e_k_a=e_k_a,
                   e_r_k=e_r_k, e_gn_g=e_gn_g, e_gn_b=e_gn_b, e_w_out=e_w_out, o_w_in=o_w_in, o_ln_g=o_ln_g,
                   o_ln_b=o_ln_b, o_w_s=o_w_s, o_b_s=o_b_s, o_w_out=o_w_out)
    mom1 = dict(norm_g=m_norm_g, final_norm_g=m_final_norm_g, e_w_in=m_e_w_in, e_shift_mu=m_e_shift_mu,
                e_w_decay_up=m_e_w_decay_up, e_w0=m_e_w0, e_a_up=m_e_a_up, e_a0=m_e_a0, e_k_k=m_e_k_k,
                e_k_a=m_e_k_a, e_r_k=m_e_r_k, e_gn_g=m_e_gn_g, e_gn_b=m_e_gn_b, e_w_out=m_e_w_out,
                o_w_in=m_o_w_in, o_ln_g=m_o_ln_g, o_ln_b=m_o_ln_b, o_w_s=m_o_w_s, o_b_s=m_o_b_s,
                o_w_out=m_o_w_out)
    mom2 = dict(norm_g=v_norm_g, final_norm_g=v_final_norm_g, e_w_in=v_e_w_in, e_shift_mu=v_e_shift_mu,
                e_w_decay_up=v_e_w_decay_up, e_w0=v_e_w0, e_a_up=v_e_a_up, e_a0=v_e_a0, e_k_k=v_e_k_k,
                e_k_a=v_e_k_a, e_r_k=v_e_r_k, e_gn_g=v_e_gn_g, e_gn_b=v_e_gn_b, e_w_out=v_e_w_out,
                o_w_in=v_o_w_in, o_ln_g=v_o_ln_g, o_ln_b=v_o_ln_b, o_w_s=v_o_w_s, o_b_s=v_o_b_s,
                o_w_out=v_o_w_out)
    names = list(weights)
    big = ("e_w_in", "e_w_out", "o_w_in", "o_w_out")

    bl, seq, d = x.shape
    t = bl * seq
    width = e_w0.shape[1]
    lora = e_w_decay_up.shape[1]
    n_sb = width // SB_HEAD
    me = 4 * lax.axis_index("x") + 2 * lax.axis_index("y") + lax.axis_index("c")

    e_win_t = _all_gather(e_w_in[0].T.astype(BF16), "gather_e_w_in").reshape(-1, d)
    later_shards = (e_w_out[0].astype(BF16), o_w_in[0].T.astype(BF16), o_w_out[0].astype(BF16))
    sharded_small = ("e_w_decay_up", "e_a_up", "o_ln_g", "o_ln_b")
    small_shapes = [weights[n][0].shape for n in sharded_small]
    got = _all_gather(_pack([weights[n][0] for n in sharded_small]), "gather_small")
    per_dev = [_unpack(got[dev], small_shapes) for dev in range(N_DEV)]
    wd, wa, ln_g, ln_b = [jnp.concatenate([per_dev[dev][i] for dev in range(N_DEV)], axis=-1).reshape(
        small_shapes[i][:-1] + (-1,)) for i in range(4)]
    ln_g, ln_b = ln_g.reshape(1, -1), ln_b.reshape(1, -1)
    e_ind, e_ind_t = _head_indicator(width)
    b_s3 = o_b_s[0][:, :, None]

    x2d = x.reshape(t, d)
    target = loss_target.reshape(t, d)
    cols_rwkv = 3 * width + 2 * lora
    assert cols_rwkv % LANES == 0 and width % LANES == 0
    sb0 = (cols_rwkv + width) // SB_HEAD
    sb_cols = (sb0, sb0 + n_sb, sb0 + 2 * n_sb, sb0 + 3 * n_sb)

    h0 = _rms_fwd(x2d, norm_g[0:1], "rms0_fwd")
    p = _matmul(h0, e_win_t, "nt", F32, "e_in_fwd", tm=2048, tn=1280)
    g_rwkv = p[:, cols_rwkv:cols_rwkv + width]
    r, w, kx, v, kk, b = _rwkv_prep(p, seq, (width, lora), e_shift_mu, wd, e_w0, wa, e_a0, e_k_k, e_k_a,
                                    e_ind, e_ind_t)
    sk = [_to_scan_k(a, bl, seq) for a in (kk, w, b, kx, r)]
    sv = _to_scan_v(v, bl, seq)
    ys_scan, states, sa_all, e_wout, o_win_t, o_wout = _scan_fwd(*sk, sv, gathers=later_shards)
    e_wout, o_win_t, o_wout = (a.reshape(-1, d) for a in (e_wout, o_win_t, o_wout))
    ys = _from_scan_v(ys_scan, bl, seq)
    ya = _rwkv_post(ys, r, kx, v, g_rwkv, 0, e_gn_g, e_gn_b, e_r_k, e_ind, e_ind_t)
    yb, sb_o, sb_tot = _sb_fwd(p, bl, seq, sb_cols, n_sb)
    y = jnp.concatenate([ya, yb], axis=1)
    x1 = _matmul(y, e_wout, "nn", F32, "e_out_fwd", res=x2d)
    h1 = _rms_fwd(x1, norm_g[1:2], "rms1_fwd")
    p2 = _matmul(h1, o_win_t, "nt", F32, "o_in_fwd", tm=2048)
    y2 = _sgu_fwd(p2, ln_g, ln_b, o_w_s[0], b_s3)
    x2 = _matmul(y2, o_wout, "nn", F32, "o_out_fwd", res=x1)
    dx2, d_final_g, loss_part = _final_loss(x2, final_norm_g.reshape(1, d), target, "final_loss")

    dy2 = _matmul(dx2, o_wout, "nt", F32, "o_out_bwd_x")
    d_o_wout = _matmul(y2, dx2, "tn", F32, "o_out_bwd_w")
    dp2, d_ln_g, d_ln_b, d_w_s, d_b_s3 = _sgu_bwd(p2, ln_g, ln_b, o_w_s[0], b_s3, dy2)
    dh1 = _matmul(dp2, o_win_t, "nn", F32, "o_in_bwd_x", tm=2048, tk=1536)
    d_o_win_t = _matmul(dp2, h1, "tn", F32, "o_in_bwd_w", tn=2048)
    dx1, d_g1 = _rms_bwd(x1, norm_g[1:2], dh1, dx2, "rms1_bwd")
    dy = _matmul(dx1, e_wout, "nt", F32, "e_out_bwd_x")
    d_e_wout = _matmul(y, dx1, "tn", F32, "e_out_bwd_w")
    core = lax.axis_index("c").astype(jnp.int32).reshape(1)
    by_owner = lambda full: full.reshape((N_CHIPS, 2, full.shape[0] // N_DEV, full.shape[1]))
    early = {"e_w_out": by_owner(d_e_wout), "o_w_in": by_owner(d_o_win_t), "o_w_out": by_owner(d_o_wout)}
    dq, dk, dv_sb, dg_sb, *swapped = _sb_bwd(p, bl, seq, sb_cols, n_sb, dy, n_sb, sb_o, sb_tot,
                                             swaps=tuple(early.values()))
    partials = [_pair_sum(full, got, core, "pairsum_" + n) for (n, full), got in zip(early.items(), swapped)]
    dys, dr1, dkx1, dv1, dg_rwkv, d_gn_g, d_gn_b, d_r_k = _rwkv_post_bwd(
        ys, r, kx, v, g_rwkv, 0, e_gn_g, e_gn_b, e_r_k, e_ind, e_ind_t, dy, 0)
    dkk_s, dw_s, db_s, dkx_s, dr_s, dv_s, *landed = _scan_bwd(*sk, sv, states, sa_all, _to_scan_v(dys, bl, seq),
                                                              scatters=tuple(partials))
    early_sums = {n: _sum_slabs(parts, "sum_" + n) for n, parts in zip(early, landed)}
    dkk, dw, db, dkx2, dr2 = [_from_scan_k(a, bl, seq) for a in (dkk_s, dw_s, db_s, dkx_s, dr_s)]
    dv2 = _from_scan_v(dv_s, bl, seq)
    dp_rwkv, d_mu, d_wd, d_w0, d_wa, d_a0, d_k_k, d_k_a = _rwkv_prep_bwd(
        p, seq, (width, lora), e_shift_mu, wd, e_w0, wa, e_a0, e_k_k, e_k_a, e_ind, e_ind_t,
        dr1 + dr2, dw, dkx1 + dkx2, dv1 + dv2, dkk, db)
    dp = jnp.concatenate([dp_rwkv, dg_rwkv, dq, dk, dv_sb, dg_sb], axis=1)
    last = by_owner(_matmul(dp, h0, "tn", F32, "e_in_bwd_w", tm=1280, tn=2048))
    last = _pair_sum(last, _pair_swap(last, "swap_e_w_in"), core, "pairsum_e_w_in")
    dh0, last = _matmul(dp, e_win_t, "nn", F32, "e_in_bwd_x", tm=2048, tk=1280, scatters=(last,))
    grad_x, d_g0 = _rms_bwd(x2d, norm_g[0:1], dh0, dx1, "rms0_bwd")

    grads = {
        "e_w_in": _sum_slabs(last, "sum_e_w_in").T[None],
        "e_w_out": early_sums["e_w_out"][None],
        "o_w_in": early_sums["o_w_in"].T[None],
        "o_w_out": early_sums["o_w_out"][None],
    }
    small_full = {
        "norm_g": jnp.concatenate([d_g0, d_g1], axis=0), "final_norm_g": d_final_g.reshape(-1),
        "e_shift_mu": d_mu, "e_w_decay_up": d_wd[None], "e_w0": d_w0, "e_a_up": d_wa[None], "e_a0": d_a0,
        "e_k_k": d_k_k, "e_k_a": d_k_a, "e_r_k": d_r_k, "e_gn_g": d_gn_g, "e_gn_b": d_gn_b,
        "o_ln_g": d_ln_g, "o_ln_b": d_ln_b, "o_w_s": d_w_s[None], "o_b_s": d_b_s3[:, :, 0][None],
    }
    small = [n for n in names if n not in big]
    parts = _all_gather(_pack([small_full[n] for n in small]), "gather_small_grads")
    totals = _unpack(_sum_slabs(parts, "sum_small_grads"), [small_full[n].shape for n in small])
    for n, g in zip(small, totals):
        if n in sharded_small:
            size = weights[n].shape[-1]
            g = lax.dynamic_slice_in_dim(g, me * size, size, axis=g.ndim - 1)
        grads[n] = g.reshape(weights[n].shape)

    delta, new_m, new_v = {}, {}, {}
    for n in big:
        shp = weights[n].shape
        flat = lambda a: a.reshape(shp[-2], shp[-1])
        dl, mn, vn = _adamw(flat(weights[n]), flat(grads[n]), flat(mom1[n]), flat(mom2[n]), "adamw_" + n)
        delta[n], new_m[n], new_v[n] = dl.reshape(shp), mn.reshape(shp), vn.reshape(shp)
    packed = [_pack([src[n] for n in small]) for src in (weights, grads, mom1, mom2)]
    outs = _adamw(*packed, "adamw_small")
    shapes = [weights[n].shape for n in small]
    for dst, arr in zip((delta, new_m, new_v), outs):
        for n, a in zip(small, _unpack(arr, shapes)):
            dst[n] = a

    loss = lax.psum(loss_part[0, 0], ("x", "y", "c"))
    return (loss, grad_x.reshape(bl, seq, d), *[grads[n] for n in names], *[delta[n] for n in names],
            *[new_m[n] for n in names], *[new_v[n] for n in names])
```

```python
import functools

import jax
import jax.numpy as jnp
from jax import lax
from jax.experimental import pallas as pl
from jax.experimental.pallas import tpu as pltpu

F32 = jnp.float32
BF16 = jnp.bfloat16

N_DEV = 8
RWKV_HEAD = 64
SB_HEAD = 128
SB_BLOCK = 128
SGU_CHUNK = 128
SGU_GROUPS = 16
RMS_EPS = 1e-6
GN_EPS = 64e-5
LN_EPS = 1e-5
L2_EPS = 1e-12
ADAM_LR = 0.001
ADAM_B1 = 0.9
ADAM_B2 = 0.999
ADAM_EPS = 1e-08
ADAM_WD = 0.01
ADAM_STEP = 10

VMEM_LIMIT_V7X = 56 * 1024 * 1024
LANES = 128
SCAN_STEPS_PER_BLOCK = 8


def _params(*sem):
    return pltpu.CompilerParams(dimension_semantics=sem, vmem_limit_bytes=VMEM_LIMIT_V7X)


def _tile(n, target, mult):
    best = None
    d = mult
    while d <= min(n, target):
        if n % d == 0:
            best = d
        d += mult
    return n if best is None else best


def _remote(src, dst, send_sems, recv_sems, k, dev):
    return pltpu.make_async_remote_copy(src_ref=src, dst_ref=dst, send_sem=send_sems.at[k], recv_sem=recv_sems.at[k],
                                        device_id=dev, device_id_type=pl.DeviceIdType.MESH)


_HBM = pl.BlockSpec(memory_space=pl.ANY)


GATHER_COPIES = 7


def _gather_phases(src_ref, out_ref, send_sems, recv_sems, local_sems, n):
    x, y, c = lax.axis_index("x"), lax.axis_index("y"), lax.axis_index("c")
    me, sibling = (x, y, c), (x, y, 1 - c)
    chips = [(1 - x, y), (x, 1 - y), (1 - x, 1 - y)]

    def slot(px, py, pc):
        return out_ref.at[4 * px + 2 * py + pc]

    def copy(k, block, to, own=False):
        return _remote(src_ref if own else slot(*block), slot(*block), send_sems, recv_sems,
                       GATHER_COPIES * n + k, to)

    mine = pltpu.make_async_copy(src_ref, slot(*me), local_sems.at[n])
    first = [copy(0, me, sibling, True)] + [copy(1 + j, me, (*chip, c), True) for j, chip in enumerate(chips)]
    passed = [copy(4 + j, (*chip, c), sibling) for j, chip in enumerate(chips)]

    def start():
        mine.start()
        for cp in first:
            cp.start()

    def relay():
        for j, chip in enumerate(chips):
            copy(1 + j, (*chip, c), me).wait_recv()
            passed[j].start()

    def finish():
        copy(0, sibling, me).wait_recv()
        for j, chip in enumerate(chips):
            copy(4 + j, (*chip, 1 - c), me).wait_recv()
        for cp in first + passed:
            cp.wait_send()
        mine.wait()

    return start, relay, finish


def _gather_scratch(n):
    return [pltpu.SemaphoreType.DMA((GATHER_COPIES * n,)), pltpu.SemaphoreType.DMA((GATHER_COPIES * n,)),
            pltpu.SemaphoreType.DMA((n,))]


def _gathered_shape(src):
    return jax.ShapeDtypeStruct((N_DEV,) + tuple(src.shape), src.dtype)


def _all_gather(src, name):
    def body(src_ref, out_ref, send_sems, recv_sems, local_sems):
        for phase in _gather_phases(src_ref, out_ref, send_sems, recv_sems, local_sems, 0):
            phase()

    return pl.pallas_call(
        body, name=name, out_shape=_gathered_shape(src), in_specs=[_HBM], out_specs=_HBM,
        scratch_shapes=_gather_scratch(1),
    )(src)


N_CHIPS = N_DEV // 2
SCATTER_COPIES = N_CHIPS - 1


def _swap_phases(src_ref, out_ref, send_sems, recv_sems, n):
    x, y, c = lax.axis_index("x"), lax.axis_index("y"), lax.axis_index("c")
    copies = [_remote(src_ref.at[q, 1 - c], out_ref.at[q], send_sems, recv_sems, N_CHIPS * n + q, (x, y, 1 - c))
              for q in range(N_CHIPS)]

    def start():
        for cp in copies:
            cp.start()

    def finish():
        for cp in copies:
            cp.wait_recv()
        for cp in copies:
            cp.wait_send()

    return start, finish


def _swapped_shape(full):
    return jax.ShapeDtypeStruct((N_CHIPS,) + tuple(full.shape[2:]), full.dtype)


def _swap_scratch(n):
    return [pltpu.SemaphoreType.DMA((N_CHIPS * n,)), pltpu.SemaphoreType.DMA((N_CHIPS * n,))]


def _pair_swap(full, name):
    def body(src_ref, out_ref, send_sems, recv_sems):
        for phase in _swap_phases(src_ref, out_ref, send_sems, recv_sems, 0):
            phase()

    return pl.pallas_call(body, name=name, out_shape=_swapped_shape(full), in_specs=[_HBM], out_specs=_HBM,
                          scratch_shapes=_swap_scratch(1))(full)


def _pair_sum(full, got, core, name):
    n_chips, _, rows, cols = full.shape
    tr = _tile(rows, max(16, (1 << 19) // cols), 16)

    def body(core_ref, a_ref, b_ref, o_ref):
        o_ref[...] = (a_ref[...].astype(F32) + b_ref[...].astype(F32)).astype(BF16)

    return pl.pallas_call(
        body, name=name,
        grid_spec=pltpu.PrefetchScalarGridSpec(
            num_scalar_prefetch=1, grid=(n_chips, rows // tr),
            in_specs=[pl.BlockSpec((None, None, tr, cols), lambda q, i, s: (q, s[0], i, 0)),
                      pl.BlockSpec((None, tr, cols), lambda q, i, s: (q, i, 0))],
            out_specs=pl.BlockSpec((None, tr, cols), lambda q, i, s: (q, i, 0))),
        out_shape=jax.ShapeDtypeStruct((n_chips, rows, cols), BF16),
        compiler_params=_params("parallel", "parallel"),
    )(core, full, got)


def _scatter_phases(src_ref, out_ref, send_sems, recv_sems, local_sems, n):
    x, y, c = lax.axis_index("x"), lax.axis_index("y"), lax.axis_index("c")
    here = 2 * x + y
    chips = [(1 - x, y), (x, 1 - y), (1 - x, 1 - y)]
    local = pltpu.make_async_copy(src_ref.at[here], out_ref.at[here], local_sems.at[n])
    sends = [_remote(src_ref.at[2 * px + py], out_ref.at[here], send_sems, recv_sems, SCATTER_COPIES * n + j,
                     (px, py, c)) for j, (px, py) in enumerate(chips)]
    recvs = [_remote(src_ref.at[2 * px + py], out_ref.at[2 * px + py], send_sems, recv_sems,
                     SCATTER_COPIES * n + j, (px, py, c)) for j, (px, py) in enumerate(chips)]

    def start():
        local.start()
        for cp in sends:
            cp.start()

    def finish():
        for cp in recvs:
            cp.wait_recv()
        for cp in sends:
            cp.wait_send()
        local.wait()

    return start, finish


def _scatter_scratch(n):
    return [pltpu.SemaphoreType.DMA((SCATTER_COPIES * n,)), pltpu.SemaphoreType.DMA((SCATTER_COPIES * n,)),
            pltpu.SemaphoreType.DMA((n,))]


def _chip_scatter(parts, name):
    def body(src_ref, out_ref, send_sems, recv_sems, local_sems):
        for phase in _scatter_phases(src_ref, out_ref, send_sems, recv_sems, local_sems, 0):
            phase()

    return pl.pallas_call(body, name=name, out_shape=jax.ShapeDtypeStruct(parts.shape, parts.dtype),
                          in_specs=[_HBM], out_specs=_HBM, scratch_shapes=_scatter_scratch(1))(parts)


def _matmul(a, b, mode, out_dtype, name, res=None, tm=1024, tn=1024, tk=1024, scatters=()):
    if mode == "nn":
        (m, k), (k2, n) = a.shape, b.shape
    elif mode == "nt":
        (m, k), (n, k2) = a.shape, b.shape
    else:
        (k, m), (k2, n) = a.shape, b.shape
    assert k == k2, (a.shape, b.shape, mode)
    tm, tn, tk = _tile(m, tm, 128), _tile(n, tn, 128), _tile(k, tk, 128)
    nk = k // tk
    if mode == "nn":
        a_spec = pl.BlockSpec((tm, tk), lambda i, j, kk: (i, kk))
        b_spec = pl.BlockSpec((tk, tn), lambda i, j, kk: (kk, j))
        dims = (((1,), (0,)), ((), ()))
    elif mode == "nt":
        a_spec = pl.BlockSpec((tm, tk), lambda i, j, kk: (i, kk))
        b_spec = pl.BlockSpec((tn, tk), lambda i, j, kk: (j, kk))
        dims = (((1,), (1,)), ((), ()))
    else:
        a_spec = pl.BlockSpec((tk, tm), lambda i, j, kk: (kk, i))
        b_spec = pl.BlockSpec((tk, tn), lambda i, j, kk: (kk, j))
        dims = (((0,), (0,)), ((), ()))
    o_spec = pl.BlockSpec((tm, tn), lambda i, j, kk: (i, j))
    has_res = res is not None
    n_in = 3 if has_res else 2
    nsc = len(scatters)
    grid = (m // tm, n // tn, nk)

    def body(*refs):
        a_ref, b_ref = refs[:2]
        r_ref = refs[2] if has_res else None
        o_ref = refs[n_in + nsc]
        acc_ref = refs[n_in + 2 * nsc + 1]
        kk = pl.program_id(2)

        if nsc:
            send_sems, recv_sems, local_sems = refs[n_in + 2 * nsc + 2:]
            phases = [_scatter_phases(refs[n_in + s], refs[n_in + nsc + 1 + s], send_sems, recv_sems, local_sems, s)
                      for s in range(nsc)]
            for k, at in enumerate(((0, 0, 0), tuple(g - 1 for g in grid))):
                @pl.when((pl.program_id(0) == at[0]) & (pl.program_id(1) == at[1]) & (kk == at[2]))
                def _(k=k):
                    for ph in phases:
                        ph[k]()

        def product():
            return lax.dot_general(a_ref[...].astype(BF16), b_ref[...].astype(BF16), dims,
                                   preferred_element_type=F32)

        @pl.when(kk == 0)
        def _():
            acc_ref[...] = product()

        @pl.when(kk > 0)
        def _():
            acc_ref[...] += product()

        @pl.when(kk == nk - 1)
        def _():
            out = acc_ref[...]
            if has_res:
                out = out + r_ref[...]
            o_ref[...] = out.astype(out_dtype)

    ins = [a, b] + ([res] if has_res else [])
    specs = [a_spec, b_spec] + ([o_spec] if has_res else [])
    out = jax.ShapeDtypeStruct((m, n), out_dtype)
    if not nsc:
        return pl.pallas_call(
            body, name=name, grid=grid, out_shape=out, in_specs=specs, out_specs=o_spec,
            scratch_shapes=[pltpu.VMEM((tm, tn), F32)],
            compiler_params=_params("parallel", "parallel", "arbitrary"),
        )(*ins)
    return pl.pallas_call(
        body, name=name, grid=grid,
        out_shape=(out, *[jax.ShapeDtypeStruct(s.shape, s.dtype) for s in scatters]),
        in_specs=specs + [_HBM] * nsc, out_specs=(o_spec, *[_HBM] * nsc),
        scratch_shapes=[pltpu.VMEM((tm, tn), F32)] + _scatter_scratch(nsc),
        compiler_params=_params("arbitrary", "arbitrary", "arbitrary"),
    )(*ins, *scatters)


def _rms_fwd(x, g, name):
    t, d = x.shape
    tm = _tile(t, 256, 8)

    def body(x_ref, g_ref, h_ref):
        xv = x_ref[...]
        rstd = lax.rsqrt(jnp.mean(xv * xv, axis=-1, keepdims=True) + RMS_EPS)
        h_ref[...] = (xv * rstd * g_ref[...]).astype(BF16)

    return pl.pallas_call(
        body, name=name, grid=(t // tm,),
        out_shape=jax.ShapeDtypeStruct((t, d), BF16),
        in_specs=[pl.BlockSpec((tm, d), lambda i: (i, 0)), pl.BlockSpec((1, d), lambda i: (0, 0))],
        out_specs=pl.BlockSpec((tm, d), lambda i: (i, 0)),
        compiler_params=_params("parallel"),
    )(x, g)


def _rms_bwd(x, g, dh, dres, name):
    t, d = x.shape
    tm = _tile(t, 256, 8)

    def body(x_ref, g_ref, dh_ref, dres_ref, dx_ref, dg_ref):
        @pl.when(pl.program_id(0) == 0)
        def _():
            dg_ref[...] = jnp.zeros_like(dg_ref)

        xv = x_ref[...]
        rstd = lax.rsqrt(jnp.mean(xv * xv, axis=-1, keepdims=True) + RMS_EPS)
        xhat = xv * rstd
        dh_v = dh_ref[...]
        dg_ref[...] += jnp.sum(dh_v * xhat, axis=0, keepdims=True)
        dxh = dh_v * g_ref[...]
        dx_ref[...] = dres_ref[...] + rstd * (dxh - xhat * jnp.mean(dxh * xhat, axis=-1, keepdims=True))

    row = pl.BlockSpec((tm, d), lambda i: (i, 0))
    vec = pl.BlockSpec((1, d), lambda i: (0, 0))
    return pl.pallas_call(
        body, name=name, grid=(t // tm,),
        out_shape=(jax.ShapeDtypeStruct((t, d), F32), jax.ShapeDtypeStruct((1, d), F32)),
        in_specs=[row, vec, row, row], out_specs=(row, vec),
        compiler_params=_params("arbitrary"),
    )(x, g, dh, dres)


def _final_loss(x, g, target, name):
    t, d = x.shape
    tm = _tile(t, 256, 8)

    def body(x_ref, g_ref, t_ref, dx_ref, dg_ref, loss_ref):
        @pl.when(pl.program_id(0) == 0)
        def _():
            dg_ref[...] = jnp.zeros_like(dg_ref)
            loss_ref[...] = jnp.zeros_like(loss_ref)

        xv = x_ref[...]
        rstd = lax.rsqrt(jnp.mean(xv * xv, axis=-1, keepdims=True) + RMS_EPS)
        xhat = xv * rstd
        gv = g_ref[...]
        err = xhat * gv - t_ref[...]
        loss_ref[...] += 0.5 * jnp.sum(jnp.mean(err * err, axis=-1, keepdims=True), axis=0, keepdims=True)
        dout = err * (1.0 / d)
        dg_ref[...] += jnp.sum(dout * xhat, axis=0, keepdims=True)
        dxh = dout * gv
        dx_ref[...] = rstd * (dxh - xhat * jnp.mean(dxh * xhat, axis=-1, keepdims=True))

    row = pl.BlockSpec((tm, d), lambda i: (i, 0))
    vec = pl.BlockSpec((1, d), lambda i: (0, 0))
    return pl.pallas_call(
        body, name=name, grid=(t // tm,),
        out_shape=(jax.ShapeDtypeStruct((t, d), F32), jax.ShapeDtypeStruct((1, d), F32),
                   jax.ShapeDtypeStruct((1, 1), F32)),
        in_specs=[row, vec, row], out_specs=(row, vec, pl.BlockSpec((1, 1), lambda i: (0, 0))),
        compiler_params=_params("arbitrary"),
    )(x, g, target)


def _dot32(a, b):
    return jnp.dot(a, b, precision=lax.Precision.HIGH, preferred_element_type=F32)


def _head_sum(v, e, et):
    return _dot32(_dot32(v, e), et)


def _log_sigmoid(z):
    return jnp.minimum(z, 0.0) - jnp.log1p(jnp.exp(-jnp.abs(z)))


def _silu(g):
    return g * jax.nn.sigmoid(g)


def _prep_math(k, wlo, alo, wd, w0, wa, a0, k_k, k_a, e, et):
    wl = w0 + _dot32(jnp.tanh(wlo), wd)
    w_log = _log_sigmoid(wl) - 0.5
    w = jnp.exp(-jnp.exp(w_log))
    a = jax.nn.sigmoid(a0 + _dot32(alo, wa))
    kk0 = k * k_k
    kk = kk0 * lax.rsqrt(jnp.maximum(_head_sum(kk0 * kk0, e, et), L2_EPS * L2_EPS))
    kx = k * (1.0 + (a - 1.0) * k_a)
    return w, kx, kk, kk * a


def _post_math(ys, r, kx, v, g, gn_g, gn_b, r_k, e, et):
    inv = 1.0 / RWKV_HEAD
    mu = _head_sum(ys, e, et) * inv
    dlt = ys - mu
    var = _head_sum(dlt * dlt, e, et) * inv
    y = dlt * lax.rsqrt(var + GN_EPS) * gn_g + gn_b
    bonus = _head_sum(r * kx * r_k, e, et) * v
    return (y + bonus) * _silu(g)


def _shifted(p, prev_row, first):
    rows = lax.broadcasted_iota(jnp.int32, p.shape, 0)
    prev = jnp.where(first, 0.0, prev_row)
    return jnp.where(rows == 0, prev, pltpu.roll(p, 1, 0))


def _head_indicator(width):
    ch = lax.broadcasted_iota(jnp.int32, (width, width // RWKV_HEAD), 0) // RWKV_HEAD
    hd = lax.broadcasted_iota(jnp.int32, (width, width // RWKV_HEAD), 1)
    e = (ch == hd).astype(F32)
    return e, e.T


def _rwkv_prep(p, seq, dims, mu, wd, w0, wa, a0, k_k, k_a, e, et):
    t = p.shape[0]
    width, lora = dims
    cols = 3 * width + 2 * lora
    tm = 128
    per_seq = seq // tm

    def body(p_ref, prev_ref, mu_ref, wd_ref, w0_ref, wa_ref, a0_ref, kk_ref, ka_ref, e_ref, et_ref,
             r_out, w_out, kx_out, v_out, kkn_out, b_out):
        i = pl.program_id(0)
        pv = p_ref[...]
        psh = _shifted(pv, prev_ref[7:8, :], i % per_seq == 0)
        ps = pv + mu_ref[...] * (psh - pv)
        r, k, v = ps[:, :width], ps[:, width:2 * width], ps[:, 2 * width:3 * width]
        wlo, alo = ps[:, 3 * width:3 * width + lora], ps[:, 3 * width + lora:]
        w, kx, kk, b = _prep_math(k, wlo, alo, wd_ref[...], w0_ref[...], wa_ref[...], a0_ref[...],
                                  kk_ref[...], ka_ref[...], e_ref[...], et_ref[...])
        r_out[...] = r
        w_out[...] = w
        kx_out[...] = kx
        v_out[...] = v
        kkn_out[...] = kk
        b_out[...] = b

    full = lambda a: pl.BlockSpec(a.shape, lambda i: (0,) * a.ndim)
    out = pl.BlockSpec((tm, width), lambda i: (i, 0))
    return pl.pallas_call(
        body, name="rwkv_prep", grid=(t // tm,),
        out_shape=tuple(jax.ShapeDtypeStruct((t, width), F32) for _ in range(6)),
        in_specs=[pl.BlockSpec((tm, cols), lambda i: (i, 0)),
                  pl.BlockSpec((8, cols), lambda i: (jnp.maximum(i * (tm // 8) - 1, 0), 0)),
                  full(mu), full(wd), full(w0), full(wa), full(a0), full(k_k), full(k_a), full(e), full(et)],
        out_specs=tuple(out for _ in range(6)),
        compiler_params=_params("parallel"),
    )(p, p, mu, wd, w0, wa, a0, k_k, k_a, e, et)


def _rwkv_prep_bwd(p, seq, dims, mu, wd, w0, wa, a0, k_k, k_a, e, et, dr, dw, dkx, dv, dkk, db):
    t = p.shape[0]
    width, lora = dims
    cols = 3 * width + 2 * lora
    tm = 128
    n_tiles = t // tm
    per_seq = seq // tm

    def body(p_ref, prev_ref, mu_ref, wd_ref, w0_ref, wa_ref, a0_ref, kk_ref, ka_ref, e_ref, et_ref,
             dr_ref, dw_ref, dkx_ref, dv_ref, dkk_ref, db_ref,
             dp_out, dmu_out, dwd_out, dw0_out, dwa_out, da0_out, dkk_out, dka_out, carry):
        step = pl.program_id(0)
        i = n_tiles - 1 - step

        @pl.when(step == 0)
        def _():
            for ref in (dmu_out, dwd_out, dw0_out, dwa_out, da0_out, dkk_out, dka_out, carry):
                ref[...] = jnp.zeros_like(ref)

        pv = p_ref[...]
        first = i % per_seq == 0
        psh = _shifted(pv, prev_ref[7:8, :], first)
        muv = mu_ref[...]
        ps = pv + muv * (psh - pv)
        k = ps[:, width:2 * width]
        wlo, alo = ps[:, 3 * width:3 * width + lora], ps[:, 3 * width + lora:]
        ev, etv = e_ref[...], et_ref[...]
        _, vjp = jax.vjp(lambda *a: _prep_math(*a, ev, etv), k, wlo, alo, wd_ref[...], w0_ref[...],
                         wa_ref[...], a0_ref[...], kk_ref[...], ka_ref[...])
        dk, dwlo, dalo, dwd, dw0, dwa, da0, dk_k, dk_a = vjp(
            (dw_ref[...], dkx_ref[...], dkk_ref[...], db_ref[...]))
        dps = jnp.concatenate([dr_ref[...], dk, dv_ref[...], dwlo, dalo], axis=1)
        dmu_out[...] += jnp.sum(dps * (psh - pv), axis=0, keepdims=True)
        dwd_out[...] += dwd
        dw0_out[...] += dw0
        dwa_out[...] += dwa
        da0_out[...] += da0
        dkk_out[...] += dk_k
        dka_out[...] += dk_a
        dsh = dps * muv
        rows = lax.broadcasted_iota(jnp.int32, dsh.shape, 0)
        nxt = jnp.where(rows == tm - 1, carry[...], pltpu.roll(dsh, tm - 1, 0))
        dp_out[...] = (dps * (1.0 - muv) + nxt).astype(BF16)
        carry[...] = jnp.where(first, 0.0, dsh[0:1, :])

    full = lambda a: pl.BlockSpec(a.shape, lambda s: (0,) * a.ndim)
    tok = pl.BlockSpec((tm, width), lambda s: (n_tiles - 1 - s, 0))
    vec = lambda n: jax.ShapeDtypeStruct((1, n), F32)
    outs = (jax.ShapeDtypeStruct((t, cols), BF16), vec(cols), jax.ShapeDtypeStruct(wd.shape, F32), vec(width),
            jax.ShapeDtypeStruct(wa.shape, F32), vec(width), vec(width), vec(width))
    return pl.pallas_call(
        body, name="rwkv_prep_bwd", grid=(n_tiles,),
        out_shape=outs,
        in_specs=[pl.BlockSpec((tm, cols), lambda s: (n_tiles - 1 - s, 0)),
                  pl.BlockSpec((8, cols), lambda s: (jnp.maximum((n_tiles - 1 - s) * (tm // 8) - 1, 0), 0)),
                  full(mu), full(wd), full(w0), full(wa), full(a0), full(k_k), full(k_a), full(e), full(et),
                  tok, tok, tok, tok, tok, tok],
        out_specs=(pl.BlockSpec((tm, cols), lambda s: (n_tiles - 1 - s, 0)),) + tuple(
            pl.BlockSpec(o.shape, lambda s: (0, 0)) for o in outs[1:]),
        scratch_shapes=[pltpu.VMEM((1, cols), F32)],
        compiler_params=_params("arbitrary"),
    )(p, p, mu, wd, w0, wa, a0, k_k, k_a, e, et, dr, dw, dkx, dv, dkk, db)


def _rwkv_post(ys, r, kx, v, p, g_col, gn_g, gn_b, r_k, e, et):
    t, width = ys.shape
    tm = 256

    def body(ys_ref, r_ref, kx_ref, v_ref, g_ref, gg_ref, gb_ref, rk_ref, e_ref, et_ref, out_ref):
        out_ref[...] = _post_math(ys_ref[...], r_ref[...], kx_ref[...], v_ref[...], g_ref[...], gg_ref[...],
                                  gb_ref[...], rk_ref[...], e_ref[...], et_ref[...]).astype(BF16)

    tok = pl.BlockSpec((tm, width), lambda i: (i, 0))
    full = lambda a: pl.BlockSpec(a.shape, lambda i: (0,) * a.ndim)
    return pl.pallas_call(
        body, name="rwkv_post", grid=(t // tm,),
        out_shape=jax.ShapeDtypeStruct((t, width), BF16),
        in_specs=[tok, tok, tok, tok, pl.BlockSpec((tm, width), lambda i: (i, g_col)),
                  full(gn_g), full(gn_b), full(r_k), full(e), full(et)],
        out_specs=tok,
        compiler_params=_params("parallel"),
    )(ys, r, kx, v, p, gn_g, gn_b, r_k, e, et)


def _rwkv_post_bwd(ys, r, kx, v, p, g_col, gn_g, gn_b, r_k, e, et, dy, dy_col):
    t, width = ys.shape
    tm = 128

    def body(ys_ref, r_ref, kx_ref, v_ref, g_ref, gg_ref, gb_ref, rk_ref, e_ref, et_ref, dy_ref,
             dys_out, dr_out, dkx_out, dv_out, dg_out, dgg_out, dgb_out, drk_out):
        @pl.when(pl.program_id(0) == 0)
        def _():
            for ref in (dgg_out, dgb_out, drk_out):
                ref[...] = jnp.zeros_like(ref)

        ev, etv = e_ref[...], et_ref[...]
        _, vjp = jax.vjp(lambda *a: _post_math(*a, ev, etv), ys_ref[...], r_ref[...], kx_ref[...], v_ref[...],
                         g_ref[...], gg_ref[...], gb_ref[...], rk_ref[...])
        dys, dr, dkx, dv, dg, dgg, dgb, drk = vjp(dy_ref[...])
        dys_out[...] = dys
        dr_out[...] = dr
        dkx_out[...] = dkx
        dv_out[...] = dv
        dg_out[...] = dg.astype(BF16)
        dgg_out[...] += dgg
        dgb_out[...] += dgb
        drk_out[...] += drk

    tok = pl.BlockSpec((tm, width), lambda i: (i, 0))
    full = lambda a: pl.BlockSpec(a.shape, lambda i: (0,) * a.ndim)
    big = jax.ShapeDtypeStruct((t, width), F32)
    vec = jax.ShapeDtypeStruct((1, width), F32)
    vspec = pl.BlockSpec((1, width), lambda i: (0, 0))
    return pl.pallas_call(
        body, name="rwkv_post_bwd", grid=(t // tm,),
        out_shape=(big, big, big, big, jax.ShapeDtypeStruct((t, width), BF16), vec, vec, vec),
        in_specs=[tok, tok, tok, tok, pl.BlockSpec((tm, width), lambda i: (i, g_col)),
                  full(gn_g), full(gn_b), full(r_k), full(e), full(et),
                  pl.BlockSpec((tm, width), lambda i: (i, dy_col))],
        out_specs=(tok, tok, tok, tok, tok, vspec, vspec, vspec),
        compiler_params=_params("arbitrary"),
    )(ys, r, kx, v, p, gn_g, gn_b, r_k, e, et, dy)


def _to_scan_k(a, bl, seq):
    h = a.shape[1] // RWKV_HEAD
    twice = jnp.broadcast_to(a.reshape(1, bl, seq, h, RWKV_HEAD), (2, bl, seq, h, RWKV_HEAD))
    return twice.transpose(2, 4, 0, 1, 3).reshape(seq, RWKV_HEAD, 2 * bl * h)


def _to_scan_v(a, bl, seq):
    h = a.shape[1] // RWKV_HEAD
    half = RWKV_HEAD // 2
    return a.reshape(bl, seq, h, 2, half).transpose(1, 4, 3, 0, 2).reshape(seq, half, 2 * bl * h)


def _from_scan_k(a, bl, seq):
    h = a.shape[2] // (2 * bl)
    a = a[:, :, :bl * h].reshape(seq, RWKV_HEAD, bl, h).transpose(2, 0, 3, 1)
    return a.reshape(bl * seq, h * RWKV_HEAD)


def _from_scan_v(a, bl, seq):
    half = RWKV_HEAD // 2
    h = a.shape[2] // (2 * bl)
    a = a.reshape(seq, half, 2, bl, h).transpose(3, 0, 4, 2, 1)
    return a.reshape(bl * seq, h * RWKV_HEAD)


def _scan_fwd(kk, w, b, kx, r, v, gathers=()):
    seq, nk, lanes = kk.shape
    nv = v.shape[1]
    tt = SCAN_STEPS_PER_BLOCK
    nblk = seq // tt
    ng = len(gathers)

    def body(*refs):
        kk_ref, w_ref, b_ref, kx_ref, r_ref, v_ref = refs[:6]
        g_src = refs[6:6 + ng]
        y_ref, st_ref, sa_ref = refs[6 + ng:9 + ng]
        g_out = refs[9 + ng:9 + 2 * ng]
        s_ref = refs[9 + 2 * ng]
        pid = pl.program_id(0)

        @pl.when(pid == 0)
        def _():
            s_ref[...] = jnp.zeros_like(s_ref)

        if ng:
            send_sems, recv_sems, local_sems = refs[10 + 2 * ng:]
            phases = [_gather_phases(g_src[n], g_out[n], send_sems, recv_sems, local_sems, n) for n in range(ng)]
            for k, at in enumerate((0, (3 * nblk) // 4, nblk - 1)):
                @pl.when(pid == at)
                def _(k=k):
                    for ph in phases:
                        ph[k]()

        def step(i, carry):
            kkv, wv, bv, kxv, rv = kk_ref[i], w_ref[i], b_ref[i], kx_ref[i], r_ref[i]
            for j in range(nv):
                s_old = s_ref[j]
                sa = -jnp.sum(s_old * kkv, axis=0, keepdims=True)
                s_new = s_old * wv + sa * bv + v_ref[i, j:j + 1, :] * kxv
                s_ref[j] = s_new
                st_ref[i, j] = s_new
                sa_ref[i, j:j + 1, :] = sa
                y_ref[i, j:j + 1, :] = jnp.sum(s_new * rv, axis=0, keepdims=True)
            return carry

        lax.fori_loop(0, tt, step, 0)

    krow = pl.BlockSpec((tt, nk, lanes), lambda i: (i, 0, 0))
    vrow = pl.BlockSpec((tt, nv, lanes), lambda i: (i, 0, 0))
    vshape = jax.ShapeDtypeStruct((seq, nv, lanes), F32)
    return pl.pallas_call(
        body, name="rwkv_scan_fwd", grid=(nblk,),
        out_shape=(vshape, jax.ShapeDtypeStruct((seq, nv, nk, lanes), F32), vshape,
                   *[_gathered_shape(g) for g in gathers]),
        in_specs=[krow, krow, krow, krow, krow, vrow] + [_HBM] * ng,
        out_specs=(vrow, pl.BlockSpec((tt, nv, nk, lanes), lambda i: (i, 0, 0, 0)), vrow, *[_HBM] * ng),
        scratch_shapes=[pltpu.VMEM((nv, nk, lanes), F32)] + (_gather_scratch(ng) if ng else []),
        compiler_params=_params("arbitrary"),
    )(kk, w, b, kx, r, v, *gathers)


def _scan_bwd(kk, w, b, kx, r, v, states, sa_all, dy, scatters=()):
    seq, nk, lanes = kk.shape
    nv = v.shape[1]
    tt = SCAN_STEPS_PER_BLOCK
    nblk = seq // tt
    nsc = len(scatters)

    def both_halves(a):
        return a + pltpu.roll(a, lanes // 2, 1)

    def body(*refs):
        kk_ref, w_ref, b_ref, kx_ref, r_ref, v_ref, st_ref, before_ref, sa_ref, dy_ref = refs[:10]
        dkk_ref, dw_ref, db_ref, dkx_ref, dr_ref, dv_ref = refs[10 + nsc:16 + nsc]
        g_ref = refs[16 + 2 * nsc]

        if nsc:
            send_sems, recv_sems, local_sems = refs[17 + 2 * nsc:]
            phases = [_scatter_phases(refs[10 + n], refs[16 + nsc + n], send_sems, recv_sems, local_sems, n)
                      for n in range(nsc)]
            for k, at in enumerate((0, nblk - 1)):
                @pl.when(pl.program_id(0) == at)
                def _(k=k):
                    for ph in phases:
                        ph[k]()

        @pl.when(pl.program_id(0) == 0)
        def _():
            g_ref[...] = jnp.zeros_like(g_ref)

        def one_step(i, state_before):
            kkv, wv, bv, kxv, rv = kk_ref[i], w_ref[i], b_ref[i], kx_ref[i], r_ref[i]
            zero = jnp.zeros((nk, lanes), F32)
            a_r, a_w, a_b, a_kx, a_kk = zero, zero, zero, zero, zero
            for j in range(nv):
                s_old = state_before(j)
                s_new = st_ref[i, j]
                vv = v_ref[i, j:j + 1, :]
                dyv = dy_ref[i, j:j + 1, :]
                sa = sa_ref[i, j:j + 1, :]
                g = g_ref[j] + dyv * rv
                a_r = a_r + s_new * dyv
                a_w = a_w + g * s_old
                dsa = jnp.sum(g * bv, axis=0, keepdims=True)
                a_b = a_b + g * sa
                dv_ref[i, j:j + 1, :] = jnp.sum(g * kxv, axis=0, keepdims=True)
                a_kx = a_kx + g * vv
                a_kk = a_kk + s_old * dsa
                g_ref[j] = g * wv - dsa * kkv
            dr_ref[i] = both_halves(a_r)
            dw_ref[i] = both_halves(a_w)
            db_ref[i] = both_halves(a_b)
            dkx_ref[i] = both_halves(a_kx)
            dkk_ref[i] = -both_halves(a_kk)

        def step(n, carry):
            i = tt - 1 - n
            one_step(i, lambda j: st_ref[i - 1, j])
            return carry

        lax.fori_loop(0, tt - 1, step, 0)
        at_start = pl.program_id(0) == nblk - 1
        one_step(0, lambda j: jnp.where(at_start, 0.0, before_ref[0, j]))

    rev = lambda i: nblk - 1 - i
    krow = pl.BlockSpec((tt, nk, lanes), lambda i: (rev(i), 0, 0))
    vrow = pl.BlockSpec((tt, nv, lanes), lambda i: (rev(i), 0, 0))
    kshape = jax.ShapeDtypeStruct((seq, nk, lanes), F32)
    return pl.pallas_call(
        body, name="rwkv_scan_bwd", grid=(nblk,),
        out_shape=(kshape, kshape, kshape, kshape, kshape, jax.ShapeDtypeStruct((seq, nv, lanes), F32),
                   *[jax.ShapeDtypeStruct(s.shape, s.dtype) for s in scatters]),
        in_specs=[krow, krow, krow, krow, krow, vrow,
                  pl.BlockSpec((tt, nv, nk, lanes), lambda i: (rev(i), 0, 0, 0)),
                  pl.BlockSpec((1, nv, nk, lanes), lambda i: (jnp.maximum(rev(i) * tt - 1, 0), 0, 0, 0)),
                  vrow, vrow] + [_HBM] * nsc,
        out_specs=(krow, krow, krow, krow, krow, vrow, *[_HBM] * nsc),
        scratch_shapes=[pltpu.VMEM((nv, nk, lanes), F32)] + (_scatter_scratch(nsc) if nsc else []),
        compiler_params=_params("arbitrary"),
    )(kk, w, b, kx, r, v, states, states, sa_all, dy, *scatters)


_NT = (((1,), (1,)), ((), ()))
_TN = (((0,), (0,)), ((), ()))
SB_SCALE = 1.0 / (SB_HEAD ** 0.5)


SB_QUERY_ROWS = 256


def _sb_masks(qr):
    blk = SB_BLOCK
    row = lax.broadcasted_iota(jnp.int32, (qr, blk), 0)
    col = lax.broadcasted_iota(jnp.int32, (qr, blk), 1)
    j_in = lax.broadcasted_iota(jnp.int32, (blk, blk), 0)
    s_in = lax.broadcasted_iota(jnp.int32, (blk, blk), 1)
    ones = jnp.ones((blk, blk), BF16)
    fwd = jnp.concatenate([(j_in > s_in).astype(BF16), ones], axis=1)
    bwd = jnp.concatenate([(s_in > j_in).astype(BF16), ones], axis=1)
    return row, col, fwd, bwd


def _split_dots(xs, b):
    his = [a.astype(BF16) for a in xs]
    los = [(a - hi.astype(F32)).astype(BF16) for a, hi in zip(xs, his)]
    tops = [jnp.dot(hi, b, preferred_element_type=F32) for hi in his]
    return [top + jnp.dot(lo, b, preferred_element_type=F32) for top, lo in zip(tops, los)]


SB_HEADS_PER_STEP = 2


def _sb_chains(bl):
    return [(b, slice(hh * SB_HEAD, (hh + 1) * SB_HEAD)) for b in range(bl) for hh in range(SB_HEADS_PER_STEP)]


def _sb_specs(bl, seq, cols, n_heads):
    hp = SB_HEADS_PER_STEP
    qr = min(SB_QUERY_ROWS, seq)
    assert n_heads % hp == 0 and all(c % hp == 0 for c in cols) and seq % qr == 0 and qr % SB_BLOCK == 0
    qspec = lambda col: pl.BlockSpec((bl, qr, hp * SB_HEAD), lambda h, i: (0, i, col // hp + h))
    kspec = lambda col: pl.BlockSpec((bl, seq, hp * SB_HEAD), lambda h, i: (0, 0, col // hp + h))
    return qr, qspec, kspec


def _sb_fwd(p, bl, seq, cols, n_heads):
    t = p.shape[0]
    q_col, k_col, v_col, g_col = cols
    blk = SB_BLOCK
    chains = _sb_chains(bl)
    qr, qspec, kspec = _sb_specs(bl, seq, cols, n_heads)
    nq = seq // qr
    per_tile = qr // blk

    def body(q_ref, k_ref, v_ref, g_ref, out_ref, o_ref, tot_ref):
        qi = pl.program_id(1)
        row, col, mix, _ = _sb_masks(qr)
        qbs = [q_ref[b, :, ln].astype(BF16) for b, ln in chains]
        n_blocks = (qi + 1) * per_tile

        def step(n, carry):
            j = n_blocks - 1 - n
            rows = pl.ds(pl.multiple_of(j * blk, blk), blk)
            causal = (j * blk + col) < (qi * qr + row)
            zs = [lax.dot_general(qb, k_ref[b, rows, ln].astype(BF16), _NT, preferred_element_type=F32) * SB_SCALE
                  for (b, ln), qb in zip(chains, qbs)]
            lszs = [_log_sigmoid(z) for z in zs]
            boths = _split_dots([jnp.where(causal, lsz - z, 0.0) for lsz, z in zip(lszs, zs)], mix)
            atts = [jnp.where(causal, jnp.exp(lsz + both[:, :blk] + later), 0.0).astype(BF16)
                    for lsz, both, (_, later) in zip(lszs, boths, carry)]
            outs = [out + jnp.dot(att, v_ref[b, rows, ln].astype(BF16), preferred_element_type=F32)
                    for (b, ln), att, (out, _) in zip(chains, atts, carry)]
            return tuple((out, later + both[:, blk:]) for out, both, (_, later) in zip(outs, boths, carry))

        zero = jnp.zeros((qr, SB_HEAD), F32)
        done = lax.fori_loop(0, n_blocks, step, tuple((zero, zero) for _ in chains))
        for (b, ln), (out, total) in zip(chains, done):
            o_ref[b, :, ln] = out
            tot_ref[b, :, ln] = total
            out_ref[b, :, ln] = (out * _silu(g_ref[b, :, ln])).astype(BF16)

    width = n_heads * SB_HEAD
    p3 = p.reshape(bl, seq, p.shape[1])
    f32 = jax.ShapeDtypeStruct((bl, seq, width), F32)
    outs = pl.pallas_call(
        body, name="sb_attn_fwd", grid=(n_heads // SB_HEADS_PER_STEP, nq),
        out_shape=(jax.ShapeDtypeStruct((bl, seq, width), BF16), f32, f32),
        in_specs=[qspec(q_col), kspec(k_col), kspec(v_col), qspec(g_col)],
        out_specs=(qspec(0), qspec(0), qspec(0)),
        compiler_params=_params("parallel", "arbitrary"),
    )(p3, p3, p3, p3)
    return tuple(a.reshape(t, width) for a in outs)


def _sb_bwd(p, bl, seq, cols, n_heads, dy, dy_col, o, tot, swaps=()):
    t = p.shape[0]
    q_col, k_col, v_col, g_col = cols
    blk = SB_BLOCK
    ns = len(swaps)
    n_groups = n_heads // SB_HEADS_PER_STEP
    chains = _sb_chains(bl)
    qr, qspec, kspec = _sb_specs(bl, seq, cols + (dy_col,), n_heads)
    nq = seq // qr
    per_tile = qr // blk

    def body(*refs):
        q_ref, k_ref, v_ref, g_ref, dy_ref, o_ref, tot_ref = refs[:7]
        dq_out, dk_out, dv_out, dg_out = refs[7 + ns:11 + ns]
        dk_acc, dv_acc = refs[11 + 2 * ns:13 + 2 * ns]
        group, qi = pl.program_id(0), pl.program_id(1)

        if ns:
            send_sems, recv_sems = refs[13 + 2 * ns:]
            phases = [_swap_phases(refs[7 + n], refs[11 + ns + n], send_sems, recv_sems, n) for n in range(ns)]
            for k, at in enumerate(((0, 0), (n_groups - 1, nq - 1))):
                @pl.when((group == at[0]) & (qi == at[1]))
                def _(k=k):
                    for ph in phases:
                        ph[k]()

        @pl.when(qi == 0)
        def _():
            dk_acc[...] = jnp.zeros_like(dk_acc)
            dv_acc[...] = jnp.zeros_like(dv_acc)

        row, col, mix, mix_t = _sb_masks(qr)
        qbs, dobs, totals = [], [], []
        for b, ln in chains:
            gate = g_ref[b, :, ln]
            sg = jax.nn.sigmoid(gate)
            dyv = dy_ref[b, :, ln]
            dg_out[b, :, ln] = (dyv * o_ref[b, :, ln] * (sg * (1.0 + gate * (1.0 - sg)))).astype(BF16)
            dobs.append((dyv * (gate * sg)).astype(BF16))
            qbs.append(q_ref[b, :, ln].astype(BF16))
            totals.append(tot_ref[b, :, ln])

        def step(j, carry):
            rows = pl.ds(pl.multiple_of(j * blk, blk), blk)
            causal = (j * blk + col) < (qi * qr + row)
            kbs = [k_ref[b, rows, ln].astype(BF16) for b, ln in chains]
            zs = [lax.dot_general(qb, kb, _NT, preferred_element_type=F32) * SB_SCALE for qb, kb in zip(qbs, kbs)]
            datts = [lax.dot_general(dob, v_ref[b, rows, ln].astype(BF16), _NT, preferred_element_type=F32)
                     for (b, ln), dob in zip(chains, dobs)]
            lszs = [_log_sigmoid(z) for z in zs]
            boths = _split_dots([jnp.where(causal, lsz - z, 0.0) for lsz, z in zip(lszs, zs)], mix)
            seens = [seen + both[:, blk:] for both, (_, seen, _) in zip(boths, carry)]
            atts = [jnp.where(causal, jnp.exp(lsz + both[:, :blk] + (total - seen)), 0.0)
                    for lsz, both, total, seen in zip(lszs, boths, totals, seens)]
            dls = [att * datt for att, datt in zip(atts, datts)]
            for (b, ln), att, dob in zip(chains, atts, dobs):
                dv_acc[b, rows, ln] += lax.dot_general(att.astype(BF16), dob, _TN, preferred_element_type=F32)
            boths_t = _split_dots(dls, mix_t)
            dkeeps = [jnp.where(causal, both_t[:, :blk] + dl_before, 0.0)
                      for both_t, (_, _, dl_before) in zip(boths_t, carry)]
            dzbs = [(((dl + dkeep) * jax.nn.sigmoid(-z) - dkeep) * SB_SCALE).astype(BF16)
                    for dl, dkeep, z in zip(dls, dkeeps, zs)]
            dqs = [dq + jnp.dot(dzb, kb, preferred_element_type=F32) for dzb, kb, (dq, _, _) in zip(dzbs, kbs, carry)]
            for (b, ln), dzb, qb in zip(chains, dzbs, qbs):
                dk_acc[b, rows, ln] += lax.dot_general(dzb, qb, _TN, preferred_element_type=F32)
            return tuple((dq, seen, dl_before + both_t[:, blk:])
                         for dq, seen, both_t, (_, _, dl_before) in zip(dqs, seens, boths_t, carry))

        zero = jnp.zeros((qr, SB_HEAD), F32)
        done = lax.fori_loop(0, (qi + 1) * per_tile, step, tuple((zero, zero, zero) for _ in chains))
        for (b, ln), (dq, _, _) in zip(chains, done):
            dq_out[b, :, ln] = dq.astype(BF16)

        @pl.when(qi == nq - 1)
        def _():
            dk_out[...] = dk_acc[...].astype(BF16)
            dv_out[...] = dv_acc[...].astype(BF16)

    width = n_heads * SB_HEAD
    shape = jax.ShapeDtypeStruct((bl, seq, width), BF16)
    as3 = lambda a: a.reshape(bl, seq, a.shape[1])
    p3 = as3(p)
    acc = pltpu.VMEM((bl, seq, SB_HEADS_PER_STEP * SB_HEAD), F32)
    outs = pl.pallas_call(
        body, name="sb_attn_bwd", grid=(n_groups, nq),
        out_shape=(shape, shape, shape, shape, *[_swapped_shape(s) for s in swaps]),
        in_specs=[qspec(q_col), kspec(k_col), kspec(v_col), qspec(g_col), qspec(dy_col), qspec(0), qspec(0)]
        + [_HBM] * ns,
        out_specs=(qspec(0), kspec(0), kspec(0), qspec(0), *[_HBM] * ns),
        scratch_shapes=[acc, acc] + (_swap_scratch(ns) if ns else []),
        compiler_params=_params("arbitrary", "arbitrary"),
    )(p3, p3, p3, p3, as3(dy), as3(o), as3(tot), *swaps)
    return tuple(a.reshape(t, width) for a in outs[:4]) + tuple(outs[4:])


def _gelu(x):
    return 0.5 * x * (1.0 + lax.erf(x * (2.0 ** -0.5)))


def _sgu_math(us, vs, gs, ln_g, ln_b, ws, bs):
    width = sum(v.shape[1] for v in vs)
    vg = [_gelu(v) for v in vs]
    mu = sum(jnp.sum(v, axis=1, keepdims=True) for v in vg) * (1.0 / width)
    dl = [v - mu for v in vg]
    var = sum(jnp.sum(d * d, axis=1, keepdims=True) for d in dl) * (1.0 / width)
    rstd = lax.rsqrt(var + LN_EPS)
    n = ws[0].shape[0]
    tri = lax.broadcasted_iota(jnp.int32, (n, n), 0) >= lax.broadcasted_iota(jnp.int32, (n, n), 1)
    outs = []
    for i in range(len(vs)):
        vn = dl[i] * rstd * ln_g[i] + ln_b[i]
        mixed = _dot32(jnp.where(tri, ws[i], 0.0), vn) + bs[i]
        outs.append(_gelu(us[i]) * mixed * _silu(gs[i]))
    return outs


def _sgu_load(p_ref, lng_ref, lnb_ref, ws_ref, bs_ref, width):
    gd = width // SGU_GROUPS
    grp = lambda ref, base, i: ref[:, base + i * gd:base + (i + 1) * gd]
    idx = range(SGU_GROUPS)
    return ([grp(p_ref, 0, i) for i in idx], [grp(p_ref, width, i) for i in idx],
            [grp(p_ref, 2 * width, i) for i in idx], [grp(lng_ref, 0, i) for i in idx],
            [grp(lnb_ref, 0, i) for i in idx], [ws_ref[i] for i in idx], [bs_ref[i] for i in idx])


def _sgu_fwd(p, ln_g, ln_b, w_s, b_s):
    t = p.shape[0]
    width = p.shape[1] // 3
    gd = width // SGU_GROUPS
    tm = SGU_CHUNK

    def body(p_ref, lng_ref, lnb_ref, ws_ref, bs_ref, y_ref):
        outs = _sgu_math(*_sgu_load(p_ref, lng_ref, lnb_ref, ws_ref, bs_ref, width))
        for i in range(SGU_GROUPS):
            y_ref[:, i * gd:(i + 1) * gd] = outs[i].astype(BF16)

    full = lambda a: pl.BlockSpec(a.shape, lambda i: (0,) * a.ndim)
    return pl.pallas_call(
        body, name="sgu_fwd", grid=(t // tm,),
        out_shape=jax.ShapeDtypeStruct((t, width), BF16),
        in_specs=[pl.BlockSpec((tm, 3 * width), lambda i: (i, 0)), full(ln_g), full(ln_b), full(w_s), full(b_s)],
        out_specs=pl.BlockSpec((tm, width), lambda i: (i, 0)),
        compiler_params=_params("parallel"),
    )(p, ln_g, ln_b, w_s, b_s)


def _sgu_bwd(p, ln_g, ln_b, w_s, b_s, dy):
    t = p.shape[0]
    width = p.shape[1] // 3
    gd = width // SGU_GROUPS
    tm = SGU_CHUNK

    def body(p_ref, lng_ref, lnb_ref, ws_ref, bs_ref, dy_ref, dp_out, dlng_out, dlnb_out, dws_out, dbs_out):
        @pl.when(pl.program_id(0) == 0)
        def _():
            for ref in (dlng_out, dlnb_out, dws_out, dbs_out):
                ref[...] = jnp.zeros_like(ref)

        _, vjp = jax.vjp(_sgu_math, *_sgu_load(p_ref, lng_ref, lnb_ref, ws_ref, bs_ref, width))
        dus, dvs, dgs, dlng, dlnb, dws, dbs = vjp(
            [dy_ref[:, i * gd:(i + 1) * gd] for i in range(SGU_GROUPS)])
        for i in range(SGU_GROUPS):
            cols = slice(i * gd, (i + 1) * gd)
            dp_out[:, i * gd:(i + 1) * gd] = dus[i].astype(BF16)
            dp_out[:, width + i * gd:width + (i + 1) * gd] = dvs[i].astype(BF16)
            dp_out[:, 2 * width + i * gd:2 * width + (i + 1) * gd] = dgs[i].astype(BF16)
            dlng_out[:, cols] += dlng[i]
            dlnb_out[:, cols] += dlnb[i]
            dws_out[i] += dws[i]
            dbs_out[i] += dbs[i]

    full = lambda a: pl.BlockSpec(a.shape, lambda i: (0,) * a.ndim)
    like = lambda a: jax.ShapeDtypeStruct(a.shape, F32)
    return pl.pallas_call(
        body, name="sgu_bwd", grid=(t // tm,),
        out_shape=(jax.ShapeDtypeStruct((t, 3 * width), BF16), like(ln_g), like(ln_b), like(w_s), like(b_s)),
        in_specs=[pl.BlockSpec((tm, 3 * width), lambda i: (i, 0)), full(ln_g), full(ln_b), full(w_s), full(b_s),
                  pl.BlockSpec((tm, width), lambda i: (i, 0))],
        out_specs=(pl.BlockSpec((tm, 3 * width), lambda i: (i, 0)), full(ln_g), full(ln_b), full(w_s), full(b_s)),
        compiler_params=_params("arbitrary"),
    )(p, ln_g, ln_b, w_s, b_s, dy)


def _sum_slabs(parts, name):
    n_parts, rows, cols = parts.shape
    tr = _tile(rows, max(16, (1 << 18) // cols), 16)

    def body(p_ref, o_ref):
        acc = p_ref[0].astype(F32)
        for d in range(1, n_parts):
            acc = acc + p_ref[d].astype(F32)
        o_ref[...] = acc

    return pl.pallas_call(
        body, name=name, grid=(rows // tr,),
        out_shape=jax.ShapeDtypeStruct((rows, cols), F32),
        in_specs=[pl.BlockSpec((n_parts, tr, cols), lambda i: (0, i, 0))],
        out_specs=pl.BlockSpec((tr, cols), lambda i: (i, 0)),
        compiler_params=_params("parallel"),
    )(parts)


def _adamw(w, g, m, v, name):
    rows, cols = w.shape
    tr = _tile(rows, max(8, (1 << 18) // cols), 8)

    def body(w_ref, g_ref, m_ref, v_ref, d_out, m_out, v_out):
        gv = g_ref[...]
        mn = ADAM_B1 * m_ref[...] + (1.0 - ADAM_B1) * gv
        vn = ADAM_B2 * v_ref[...] + (1.0 - ADAM_B2) * (gv * gv)
        m_hat = mn / (1.0 - ADAM_B1 ** ADAM_STEP)
        v_hat = vn / (1.0 - ADAM_B2 ** ADAM_STEP)
        d_out[...] = -ADAM_LR * (m_hat / (jnp.sqrt(v_hat) + ADAM_EPS) + ADAM_WD * w_ref[...])
        m_out[...] = mn
        v_out[...] = vn

    blk = pl.BlockSpec((tr, cols), lambda i: (i, 0))
    shape = jax.ShapeDtypeStruct((rows, cols), F32)
    return pl.pallas_call(
        body, name=name, grid=(rows // tr,),
        out_shape=(shape, shape, shape),
        in_specs=[blk, blk, blk, blk], out_specs=(blk, blk, blk),
        compiler_params=_params("parallel"),
    )(w, g, m, v)


PACK_COLS = 1024


def _pack(arrays):
    flat = jnp.concatenate([a.reshape(-1).astype(F32) for a in arrays])
    rows = -(-flat.shape[0] // (8 * PACK_COLS)) * 8
    return jnp.pad(flat, (0, rows * PACK_COLS - flat.shape[0])).reshape(rows, PACK_COLS)


def _unpack(packed, shapes):
    flat = packed.reshape(-1)
    out, at = [], 0
    for s in shapes:
        n = 1
        for d in s:
            n *= d
        out.append(flat[at:at + n].reshape(s))
        at += n
    return out


def kernel(x, norm_g, final_norm_g, e_w_in, e_shift_mu, e_w_decay_up, e_w0, e_a_up, e_a0, e_k_k, e_k_a, e_r_k, e_gn_g, e_gn_b, e_w_out, o_w_in, o_ln_g, o_ln_b, o_w_s, o_b_s, o_w_out, loss_target, m_norm_g, m_final_norm_g, m_e_w_in, m_e_shift_mu, m_e_w_decay_up, m_e_w0, m_e_a_up, m_e_a0, m_e_k_k, m_e_k_a, m_e_r_k, m_e_gn_g, m_e_gn_b, m_e_w_out, m_o_w_in, m_o_ln_g, m_o_ln_b, m_o_w_s, m_o_b_s, m_o_w_out, v_norm_g, v_final_norm_g, v_e_w_in, v_e_shift_mu, v_e_w_decay_up, v_e_w0, v_e_a_up, v_e_a0, v_e_k_k, v_e_k_a, v_e_r_k, v_e_gn_g, v_e_gn_b, v_e_w_out, v_o_w_in, v_o_ln_g, v_o_ln_b, v_o_w_s, v_o_b_s, v_o_w_out):
    weights = dict(norm_g=norm_g, final_norm_g=final_norm_g, e_w_in=e_w_in, e_shift_mu=e_shift_mu,
                   e_w_decay_up=e_w_decay_up, e_w0=e_w0, e_a_up=e_a_up, e_a0=e_a0, e_k_k=e_k_k, e_k_a=e_k_a,
                   e_r_k=e_r_k, e_gn_g=e_gn_g, e_gn_b=e_gn_b, e_w_out=e_w_out, o_w_in=o_w_in, o_ln_g=o_ln_g,
                   o_ln_b=o_ln_b, o_w_s=o_w_s, o_b_s=o_b_s, o_w_out=o_w_out)
    mom1 = dict(norm_g=m_norm_g, final_norm_g=m_final_norm_g, e_w_in=m_e_w_in, e_shift_mu=m_e_shift_mu,
                e_w_decay_up=m_e_w_decay_up, e_w0=m_e_w0, e_a_up=m_e_a_up, e_a0=m_e_a0, e_k_k=m_e_k_k,
                e_k_a=m_e_k_a, e_r_k=m_e_r_k, e_gn_g=m_e_gn_g, e_gn_b=m_e_gn_b, e_w_out=m_e_w_out,
                o_w_in=m_o_w_in, o_ln_g=m_o_ln_g, o_ln_b=m_o_ln_b, o_w_s=m_o_w_s, o_b_s=m_o_b_s,
                o_w_out=m_o_w_out)
    mom2 = dict(norm_g=v_norm_g, final_norm_g=v_final_norm_g, e_w_in=v_e_w_in, e_shift_mu=v_e_shift_mu,
                e_w_decay_up=v_e_w_decay_up, e_w0=v_e_w0, e_a_up=v_e_a_up, e_a0=v_e_a0, e_k_k=v_e_k_k,
                e_k_a=v_e_k_a, e_r_k=v_e_r_k, e_gn_g=v_e_gn_g, e_gn_b=v_e_gn_b, e_w_out=v_e_w_out,
                o_w_in=v_o_w_in, o_ln_g=v_o_ln_g, o_ln_b=v_o_ln_b, o_w_s=v_o_w_s, o_b_s=v_o_b_s,
                o_w_out=v_o_w_out)
    names = list(weights)
    big = ("e_w_in", "e_w_out", "o_w_in", "o_w_out")

    bl, seq, d = x.shape
    t = bl * seq
    width = e_w0.shape[1]
    lora = e_w_decay_up.shape[1]
    n_sb = width // SB_HEAD
    me = 4 * lax.axis_index("x") + 2 * lax.axis_index("y") + lax.axis_index("c")

    e_win_t = _all_gather(e_w_in[0].T.astype(BF16), "gather_e_w_in").reshape(-1, d)
    later_shards = (e_w_out[0].astype(BF16), o_w_in[0].T.astype(BF16), o_w_out[0].astype(BF16))
    sharded_small = ("e_w_decay_up", "e_a_up", "o_ln_g", "o_ln_b")
    small_shapes = [weights[n][0].shape for n in sharded_small]
    got = _all_gather(_pack([weights[n][0] for n in sharded_small]), "gather_small")
    per_dev = [_unpack(got[dev], small_shapes) for dev in range(N_DEV)]
    wd, wa, ln_g, ln_b = [jnp.concatenate([per_dev[dev][i] for dev in range(N_DEV)], axis=-1).reshape(
        small_shapes[i][:-1] + (-1,)) for i in range(4)]
    ln_g, ln_b = ln_g.reshape(1, -1), ln_b.reshape(1, -1)
    e_ind, e_ind_t = _head_indicator(width)
    b_s3 = o_b_s[0][:, :, None]

    x2d = x.reshape(t, d)
    target = loss_target.reshape(t, d)
    cols_rwkv = 3 * width + 2 * lora
    assert cols_rwkv % LANES == 0 and width % LANES == 0
    sb0 = (cols_rwkv + width) // SB_HEAD
    sb_cols = (sb0, sb0 + n_sb, sb0 + 2 * n_sb, sb0 + 3 * n_sb)

    h0 = _rms_fwd(x2d, norm_g[0:1], "rms0_fwd")
    p = _matmul(h0, e_win_t, "nt", F32, "e_in_fwd", tm=2048, tn=1280)
    g_rwkv = p[:, cols_rwkv:cols_rwkv + width]
    r, w, kx, v, kk, b = _rwkv_prep(p, seq, (width, lora), e_shift_mu, wd, e_w0, wa, e_a0, e_k_k, e_k_a,
                                    e_ind, e_ind_t)
    sk = [_to_scan_k(a, bl, seq) for a in (kk, w, b, kx, r)]
    sv = _to_scan_v(v, bl, seq)
    ys_scan, states, sa_all, e_wout, o_win_t, o_wout = _scan_fwd(*sk, sv, gathers=later_shards)
    e_wout, o_win_t, o_wout = (a.reshape(-1, d) for a in (e_wout, o_win_t, o_wout))
    ys = _from_scan_v(ys_scan, bl, seq)
    ya = _rwkv_post(ys, r, kx, v, g_rwkv, 0, e_gn_g, e_gn_b, e_r_k, e_ind, e_ind_t)
    yb, sb_o, sb_tot = _sb_fwd(p, bl, seq, sb_cols, n_sb)
    y = jnp.concatenate([ya, yb], axis=1)
    x1 = _matmul(y, e_wout, "nn", F32, "e_out_fwd", res=x2d)
    h1 = _rms_fwd(x1, norm_g[1:2], "rms1_fwd")
    p2 = _matmul(h1, o_win_t, "nt", F32, "o_in_fwd", tm=2048)
    y2 = _sgu_fwd(p2, ln_g, ln_b, o_w_s[0], b_s3)
    x2 = _matmul(y2, o_wout, "nn", F32, "o_out_fwd", res=x1)
    dx2, d_final_g, loss_part = _final_loss(x2, final_norm_g.reshape(1, d), target, "final_loss")

    dy2 = _matmul(dx2, o_wout, "nt", F32, "o_out_bwd_x", tm=2048)
    d_o_wout = _matmul(y2, dx2, "tn", F32, "o_out_bwd_w", tn=2048)
    dp2, d_ln_g, d_ln_b, d_w_s, d_b_s3 = _sgu_bwd(p2, ln_g, ln_b, o_w_s[0], b_s3, dy2)
    dh1 = _matmul(dp2, o_win_t, "nn", F32, "o_in_bwd_x", tm=2048, tk=1536)
    d_o_win_t = _matmul(dp2, h1, "tn", F32, "o_in_bwd_w", tn=2048)
    dx1, d_g1 = _rms_bwd(x1, norm_g[1:2], dh1, dx2, "rms1_bwd")
    dy = _matmul(dx1, e_wout, "nt", F32, "e_out_bwd_x", tm=2048)
    d_e_wout = _matmul(y, dx1, "tn", F32, "e_out_bwd_w", tn=2048)
    core = lax.axis_index("c").astype(jnp.int32).reshape(1)
    by_owner = lambda full: full.reshape((N_CHIPS, 2, full.shape[0] // N_DEV, full.shape[1]))
    early = {"e_w_out": by_owner(d_e_wout), "o_w_in": by_owner(d_o_win_t), "o_w_out": by_owner(d_o_wout)}
    dq, dk, dv_sb, dg_sb, *swapped = _sb_bwd(p, bl, seq, sb_cols, n_sb, dy, n_sb, sb_o, sb_tot,
                                             swaps=tuple(early.values()))
    partials = [_pair_sum(full, got, core, "pairsum_" + n) for (n, full), got in zip(early.items(), swapped)]
    dys, dr1, dkx1, dv1, dg_rwkv, d_gn_g, d_gn_b, d_r_k = _rwkv_post_bwd(
        ys, r, kx, v, g_rwkv, 0, e_gn_g, e_gn_b, e_r_k, e_ind, e_ind_t, dy, 0)
    dkk_s, dw_s, db_s, dkx_s, dr_s, dv_s, *landed = _scan_bwd(*sk, sv, states, sa_all, _to_scan_v(dys, bl, seq),
                                                              scatters=tuple(partials))
    early_sums = {n: _sum_slabs(parts, "sum_" + n) for n, parts in zip(early, landed)}
    dkk, dw, db, dkx2, dr2 = [_from_scan_k(a, bl, seq) for a in (dkk_s, dw_s, db_s, dkx_s, dr_s)]
    dv2 = _from_scan_v(dv_s, bl, seq)
    dp_rwkv, d_mu, d_wd, d_w0, d_wa, d_a0, d_k_k, d_k_a = _rwkv_prep_bwd(
        p, seq, (width, lora), e_shift_mu, wd, e_w0, wa, e_a0, e_k_k, e_k_a, e_ind, e_ind_t,
        dr1 + dr2, dw, dkx1 + dkx2, dv1 + dv2, dkk, db)
    dp = jnp.concatenate([dp_rwkv, dg_rwkv, dq, dk, dv_sb, dg_sb], axis=1)
    last = by_owner(_matmul(dp, h0, "tn", BF16, "e_in_bwd_w", tm=1280, tn=2048))
    last = _pair_sum(last, _pair_swap(last, "swap_e_w_in"), core, "pairsum_e_w_in")
    dh0, last = _matmul(dp, e_win_t, "nn", F32, "e_in_bwd_x", tm=2048, tk=1280, scatters=(last,))
    grad_x, d_g0 = _rms_bwd(x2d, norm_g[0:1], dh0, dx1, "rms0_bwd")

    grads = {
        "e_w_in": _sum_slabs(last, "sum_e_w_in").T[None],
        "e_w_out": early_sums["e_w_out"][None],
        "o_w_in": early_sums["o_w_in"].T[None],
        "o_w_out": early_sums["o_w_out"][None],
    }
    small_full = {
        "norm_g": jnp.concatenate([d_g0, d_g1], axis=0), "final_norm_g": d_final_g.reshape(-1),
        "e_shift_mu": d_mu, "e_w_decay_up": d_wd[None], "e_w0": d_w0, "e_a_up": d_wa[None], "e_a0": d_a0,
        "e_k_k": d_k_k, "e_k_a": d_k_a, "e_r_k": d_r_k, "e_gn_g": d_gn_g, "e_gn_b": d_gn_b,
        "o_ln_g": d_ln_g, "o_ln_b": d_ln_b, "o_w_s": d_w_s[None], "o_b_s": d_b_s3[:, :, 0][None],
    }
    small = [n for n in names if n not in big]
    parts = _all_gather(_pack([small_full[n] for n in small]), "gather_small_grads")
    totals = _unpack(_sum_slabs(parts, "sum_small_grads"), [small_full[n].shape for n in small])
    for n, g in zip(small, totals):
        if n in sharded_small:
            size = weights[n].shape[-1]
            g = lax.dynamic_slice_in_dim(g, me * size, size, axis=g.ndim - 1)
        grads[n] = g.reshape(weights[n].shape)

    delta, new_m, new_v = {}, {}, {}
    for n in big:
        shp = weights[n].shape
        flat = lambda a: a.reshape(shp[-2], shp[-1])
        dl, mn, vn = _adamw(flat(weights[n]), flat(grads[n]), flat(mom1[n]), flat(mom2[n]), "adamw_" + n)
        delta[n], new_m[n], new_v[n] = dl.reshape(shp), mn.reshape(shp), vn.reshape(shp)
    packed = [_pack([src[n] for n in small]) for src in (weights, grads, mom1, mom2)]
    outs = _adamw(*packed, "adamw_small")
    shapes = [weights[n].shape for n in small]
    for dst, arr in zip((delta, new_m, new_v), outs):
        for n, a in zip(small, _unpack(arr, shapes)):
            dst[n] = a

    loss = lax.psum(loss_part[0, 0], ("x", "y", "c"))
    return (loss, grad_x.reshape(bl, seq, d), *[grads[n] for n in names], *[delta[n] for n in names],
            *[new_m[n] for n in names], *[new_v[n] for n in names])
```

```python
import functools

import jax
import jax.numpy as jnp
from jax import lax
from jax.experimental import pallas as pl
from jax.experimental.pallas import tpu as pltpu

F32 = jnp.float32
BF16 = jnp.bfloat16

N_DEV = 8
RWKV_HEAD = 64
SB_HEAD = 128
SB_BLOCK = 128
SGU_CHUNK = 128
SGU_GROUPS = 16
RMS_EPS = 1e-6
GN_EPS = 64e-5
LN_EPS = 1e-5
L2_EPS = 1e-12
ADAM_LR = 0.001
ADAM_B1 = 0.9
ADAM_B2 = 0.999
ADAM_EPS = 1e-08
ADAM_WD = 0.01
ADAM_STEP = 10

VMEM_LIMIT_V7X = 56 * 1024 * 1024
LANES = 128
SCAN_STEPS_PER_BLOCK = 8


def _params(*sem):
    return pltpu.CompilerParams(dimension_semantics=sem, vmem_limit_bytes=VMEM_LIMIT_V7X)


def _tile(n, target, mult):
    best = None
    d = mult
    while d <= min(n, target):
        if n % d == 0:
            best = d
        d += mult
    return n if best is None else best


def _remote(src, dst, send_sems, recv_sems, k, dev):
    return pltpu.make_async_remote_copy(src_ref=src, dst_ref=dst, send_sem=send_sems.at[k], recv_sem=recv_sems.at[k],
                                        device_id=dev, device_id_type=pl.DeviceIdType.MESH)


_HBM = pl.BlockSpec(memory_space=pl.ANY)


GATHER_COPIES = 7


def _gather_phases(src_ref, out_ref, send_sems, recv_sems, local_sems, n):
    x, y, c = lax.axis_index("x"), lax.axis_index("y"), lax.axis_index("c")
    me, sibling = (x, y, c), (x, y, 1 - c)
    chips = [(1 - x, y), (x, 1 - y), (1 - x, 1 - y)]

    def slot(px, py, pc):
        return out_ref.at[4 * px + 2 * py + pc]

    def copy(k, block, to, own=False):
        return _remote(src_ref if own else slot(*block), slot(*block), send_sems, recv_sems,
                       GATHER_COPIES * n + k, to)

    mine = pltpu.make_async_copy(src_ref, slot(*me), local_sems.at[n])
    first = [copy(0, me, sibling, True)] + [copy(1 + j, me, (*chip, c), True) for j, chip in enumerate(chips)]
    passed = [copy(4 + j, (*chip, c), sibling) for j, chip in enumerate(chips)]

    def start():
        mine.start()
        for cp in first:
            cp.start()

    def relay():
        for j, chip in enumerate(chips):
            copy(1 + j, (*chip, c), me).wait_recv()
            passed[j].start()

    def finish():
        copy(0, sibling, me).wait_recv()
        for j, chip in enumerate(chips):
            copy(4 + j, (*chip, 1 - c), me).wait_recv()
        for cp in first + passed:
            cp.wait_send()
        mine.wait()

    return start, relay, finish


def _gather_scratch(n):
    return [pltpu.SemaphoreType.DMA((GATHER_COPIES * n,)), pltpu.SemaphoreType.DMA((GATHER_COPIES * n,)),
            pltpu.SemaphoreType.DMA((n,))]


def _gathered_shape(src):
    return jax.ShapeDtypeStruct((N_DEV,) + tuple(src.shape), src.dtype)


def _all_gather(src, name):
    def body(src_ref, out_ref, send_sems, recv_sems, local_sems):
        for phase in _gather_phases(src_ref, out_ref, send_sems, recv_sems, local_sems, 0):
            phase()

    return pl.pallas_call(
        body, name=name, out_shape=_gathered_shape(src), in_specs=[_HBM], out_specs=_HBM,
        scratch_shapes=_gather_scratch(1),
    )(src)


N_CHIPS = N_DEV // 2
SCATTER_COPIES = N_CHIPS - 1


def _swap_phases(src_ref, out_ref, send_sems, recv_sems, n):
    x, y, c = lax.axis_index("x"), lax.axis_index("y"), lax.axis_index("c")
    copies = [_remote(src_ref.at[q, 1 - c], out_ref.at[q], send_sems, recv_sems, N_CHIPS * n + q, (x, y, 1 - c))
              for q in range(N_CHIPS)]

    def start():
        for cp in copies:
            cp.start()

    def finish():
        for cp in copies:
            cp.wait_recv()
        for cp in copies:
            cp.wait_send()

    return start, finish


def _swapped_shape(full):
    return jax.ShapeDtypeStruct((N_CHIPS,) + tuple(full.shape[2:]), full.dtype)


def _swap_scratch(n):
    return [pltpu.SemaphoreType.DMA((N_CHIPS * n,)), pltpu.SemaphoreType.DMA((N_CHIPS * n,))]


def _pair_swap(full, name):
    def body(src_ref, out_ref, send_sems, recv_sems):
        for phase in _swap_phases(src_ref, out_ref, send_sems, recv_sems, 0):
            phase()

    return pl.pallas_call(body, name=name, out_shape=_swapped_shape(full), in_specs=[_HBM], out_specs=_HBM,
                          scratch_shapes=_swap_scratch(1))(full)


def _pair_sum(full, got, core, name):
    n_chips, _, rows, cols = full.shape
    tr = _tile(rows, max(16, (1 << 19) // cols), 16)

    def body(core_ref, a_ref, b_ref, o_ref):
        o_ref[...] = (a_ref[...].astype(F32) + b_ref[...].astype(F32)).astype(BF16)

    return pl.pallas_call(
        body, name=name,
        grid_spec=pltpu.PrefetchScalarGridSpec(
            num_scalar_prefetch=1, grid=(n_chips, rows // tr),
            in_specs=[pl.BlockSpec((None, None, tr, cols), lambda q, i, s: (q, s[0], i, 0)),
                      pl.BlockSpec((None, tr, cols), lambda q, i, s: (q, i, 0))],
            out_specs=pl.BlockSpec((None, tr, cols), lambda q, i, s: (q, i, 0))),
        out_shape=jax.ShapeDtypeStruct((n_chips, rows, cols), BF16),
        compiler_params=_params("parallel", "parallel"),
    )(core, full, got)


def _scatter_phases(src_ref, out_ref, send_sems, recv_sems, local_sems, n):
    x, y, c = lax.axis_index("x"), lax.axis_index("y"), lax.axis_index("c")
    here = 2 * x + y
    chips = [(1 - x, y), (x, 1 - y), (1 - x, 1 - y)]
    local = pltpu.make_async_copy(src_ref.at[here], out_ref.at[here], local_sems.at[n])
    sends = [_remote(src_ref.at[2 * px + py], out_ref.at[here], send_sems, recv_sems, SCATTER_COPIES * n + j,
                     (px, py, c)) for j, (px, py) in enumerate(chips)]
    recvs = [_remote(src_ref.at[2 * px + py], out_ref.at[2 * px + py], send_sems, recv_sems,
                     SCATTER_COPIES * n + j, (px, py, c)) for j, (px, py) in enumerate(chips)]

    def start():
        local.start()
        for cp in sends:
            cp.start()

    def finish():
        for cp in recvs:
            cp.wait_recv()
        for cp in sends:
            cp.wait_send()
        local.wait()

    return start, finish


def _scatter_scratch(n):
    return [pltpu.SemaphoreType.DMA((SCATTER_COPIES * n,)), pltpu.SemaphoreType.DMA((SCATTER_COPIES * n,)),
            pltpu.SemaphoreType.DMA((n,))]


def _chip_scatter(parts, name):
    def body(src_ref, out_ref, send_sems, recv_sems, local_sems):
        for phase in _scatter_phases(src_ref, out_ref, send_sems, recv_sems, local_sems, 0):
            phase()

    return pl.pallas_call(body, name=name, out_shape=jax.ShapeDtypeStruct(parts.shape, parts.dtype),
                          in_specs=[_HBM], out_specs=_HBM, scratch_shapes=_scatter_scratch(1))(parts)


def _matmul(a, b, mode, out_dtype, name, res=None, tm=1024, tn=1024, tk=1024, scatters=()):
    if mode == "nn":
        (m, k), (k2, n) = a.shape, b.shape
    elif mode == "nt":
        (m, k), (n, k2) = a.shape, b.shape
    else:
        (k, m), (k2, n) = a.shape, b.shape
    assert k == k2, (a.shape, b.shape, mode)
    tm, tn, tk = _tile(m, tm, 128), _tile(n, tn, 128), _tile(k, tk, 128)
    nk = k // tk
    if mode == "nn":
        a_spec = pl.BlockSpec((tm, tk), lambda i, j, kk: (i, kk))
        b_spec = pl.BlockSpec((tk, tn), lambda i, j, kk: (kk, j))
        dims = (((1,), (0,)), ((), ()))
    elif mode == "nt":
        a_spec = pl.BlockSpec((tm, tk), lambda i, j, kk: (i, kk))
        b_spec = pl.BlockSpec((tn, tk), lambda i, j, kk: (j, kk))
        dims = (((1,), (1,)), ((), ()))
    else:
        a_spec = pl.BlockSpec((tk, tm), lambda i, j, kk: (kk, i))
        b_spec = pl.BlockSpec((tk, tn), lambda i, j, kk: (kk, j))
        dims = (((0,), (0,)), ((), ()))
    o_spec = pl.BlockSpec((tm, tn), lambda i, j, kk: (i, j))
    has_res = res is not None
    n_in = 3 if has_res else 2
    nsc = len(scatters)
    grid = (m // tm, n // tn, nk)

    def body(*refs):
        a_ref, b_ref = refs[:2]
        r_ref = refs[2] if has_res else None
        o_ref = refs[n_in + nsc]
        acc_ref = refs[n_in + 2 * nsc + 1]
        kk = pl.program_id(2)

        if nsc:
            send_sems, recv_sems, local_sems = refs[n_in + 2 * nsc + 2:]
            phases = [_scatter_phases(refs[n_in + s], refs[n_in + nsc + 1 + s], send_sems, recv_sems, local_sems, s)
                      for s in range(nsc)]
            for k, at in enumerate(((0, 0, 0), tuple(g - 1 for g in grid))):
                @pl.when((pl.program_id(0) == at[0]) & (pl.program_id(1) == at[1]) & (kk == at[2]))
                def _(k=k):
                    for ph in phases:
                        ph[k]()

        def product():
            return lax.dot_general(a_ref[...].astype(BF16), b_ref[...].astype(BF16), dims,
                                   preferred_element_type=F32)

        @pl.when(kk == 0)
        def _():
            acc_ref[...] = product()

        @pl.when(kk > 0)
        def _():
            acc_ref[...] += product()

        @pl.when(kk == nk - 1)
        def _():
            out = acc_ref[...]
            if has_res:
                out = out + r_ref[...]
            o_ref[...] = out.astype(out_dtype)

    ins = [a, b] + ([res] if has_res else [])
    specs = [a_spec, b_spec] + ([o_spec] if has_res else [])
    out = jax.ShapeDtypeStruct((m, n), out_dtype)
    if not nsc:
        return pl.pallas_call(
            body, name=name, grid=grid, out_shape=out, in_specs=specs, out_specs=o_spec,
            scratch_shapes=[pltpu.VMEM((tm, tn), F32)],
            compiler_params=_params("parallel", "parallel", "arbitrary"),
        )(*ins)
    return pl.pallas_call(
        body, name=name, grid=grid,
        out_shape=(out, *[jax.ShapeDtypeStruct(s.shape, s.dtype) for s in scatters]),
        in_specs=specs + [_HBM] * nsc, out_specs=(o_spec, *[_HBM] * nsc),
        scratch_shapes=[pltpu.VMEM((tm, tn), F32)] + _scatter_scratch(nsc),
        compiler_params=_params("arbitrary", "arbitrary", "arbitrary"),
    )(*ins, *scatters)


def _rms_fwd(x, g, name):
    t, d = x.shape
    tm = _tile(t, 256, 8)

    def body(x_ref, g_ref, h_ref):
        xv = x_ref[...]
        rstd = lax.rsqrt(jnp.mean(xv * xv, axis=-1, keepdims=True) + RMS_EPS)
        h_ref[...] = (xv * rstd * g_ref[...]).astype(BF16)

    return pl.pallas_call(
        body, name=name, grid=(t // tm,),
        out_shape=jax.ShapeDtypeStruct((t, d), BF16),
        in_specs=[pl.BlockSpec((tm, d), lambda i: (i, 0)), pl.BlockSpec((1, d), lambda i: (0, 0))],
        out_specs=pl.BlockSpec((tm, d), lambda i: (i, 0)),
        compiler_params=_params("parallel"),
    )(x, g)


def _rms_bwd(x, g, dh, dres, name):
    t, d = x.shape
    tm = _tile(t, 256, 8)

    def body(x_ref, g_ref, dh_ref, dres_ref, dx_ref, dg_ref):
        @pl.when(pl.program_id(0) == 0)
        def _():
            dg_ref[...] = jnp.zeros_like(dg_ref)

        xv = x_ref[...]
        rstd = lax.rsqrt(jnp.mean(xv * xv, axis=-1, keepdims=True) + RMS_EPS)
        xhat = xv * rstd
        dh_v = dh_ref[...]
        dg_ref[...] += jnp.sum(dh_v * xhat, axis=0, keepdims=True)
        dxh = dh_v * g_ref[...]
        dx_ref[...] = dres_ref[...] + rstd * (dxh - xhat * jnp.mean(dxh * xhat, axis=-1, keepdims=True))

    row = pl.BlockSpec((tm, d), lambda i: (i, 0))
    vec = pl.BlockSpec((1, d), lambda i: (0, 0))
    return pl.pallas_call(
        body, name=name, grid=(t // tm,),
        out_shape=(jax.ShapeDtypeStruct((t, d), F32), jax.ShapeDtypeStruct((1, d), F32)),
        in_specs=[row, vec, row, row], out_specs=(row, vec),
        compiler_params=_params("arbitrary"),
    )(x, g, dh, dres)


def _final_loss(x, g, target, name):
    t, d = x.shape
    tm = _tile(t, 256, 8)

    def body(x_ref, g_ref, t_ref, dx_ref, dg_ref, loss_ref):
        @pl.when(pl.program_id(0) == 0)
        def _():
            dg_ref[...] = jnp.zeros_like(dg_ref)
            loss_ref[...] = jnp.zeros_like(loss_ref)

        xv = x_ref[...]
        rstd = lax.rsqrt(jnp.mean(xv * xv, axis=-1, keepdims=True) + RMS_EPS)
        xhat = xv * rstd
        gv = g_ref[...]
        err = xhat * gv - t_ref[...]
        loss_ref[...] += 0.5 * jnp.sum(jnp.mean(err * err, axis=-1, keepdims=True), axis=0, keepdims=True)
        dout = err * (1.0 / d)
        dg_ref[...] += jnp.sum(dout * xhat, axis=0, keepdims=True)
        dxh = dout * gv
        dx_ref[...] = rstd * (dxh - xhat * jnp.mean(dxh * xhat, axis=-1, keepdims=True))

    row = pl.BlockSpec((tm, d), lambda i: (i, 0))
    vec = pl.BlockSpec((1, d), lambda i: (0, 0))
    return pl.pallas_call(
        body, name=name, grid=(t // tm,),
        out_shape=(jax.ShapeDtypeStruct((t, d), F32), jax.ShapeDtypeStruct((1, d), F32),
                   jax.ShapeDtypeStruct((1, 1), F32)),
        in_specs=[row, vec, row], out_specs=(row, vec, pl.BlockSpec((1, 1), lambda i: (0, 0))),
        compiler_params=_params("arbitrary"),
    )(x, g, target)


def _dot32(a, b):
    return jnp.dot(a, b, precision=lax.Precision.HIGH, preferred_element_type=F32)


def _head_sum(v, e, et):
    return _dot32(_dot32(v, e), et)


def _log_sigmoid(z):
    return jnp.minimum(z, 0.0) - jnp.log1p(jnp.exp(-jnp.abs(z)))


def _silu(g):
    return g * jax.nn.sigmoid(g)


def _prep_math(k, wlo, alo, wd, w0, wa, a0, k_k, k_a, e, et):
    wl = w0 + _dot32(jnp.tanh(wlo), wd)
    w_log = _log_sigmoid(wl) - 0.5
    w = jnp.exp(-jnp.exp(w_log))
    a = jax.nn.sigmoid(a0 + _dot32(alo, wa))
    kk0 = k * k_k
    kk = kk0 * lax.rsqrt(jnp.maximum(_head_sum(kk0 * kk0, e, et), L2_EPS * L2_EPS))
    kx = k * (1.0 + (a - 1.0) * k_a)
    return w, kx, kk, kk * a


def _post_math(ys, r, kx, v, g, gn_g, gn_b, r_k, e, et):
    inv = 1.0 / RWKV_HEAD
    mu = _head_sum(ys, e, et) * inv
    dlt = ys - mu
    var = _head_sum(dlt * dlt, e, et) * inv
    y = dlt * lax.rsqrt(var + GN_EPS) * gn_g + gn_b
    bonus = _head_sum(r * kx * r_k, e, et) * v
    return (y + bonus) * _silu(g)


def _shifted(p, prev_row, first):
    rows = lax.broadcasted_iota(jnp.int32, p.shape, 0)
    prev = jnp.where(first, 0.0, prev_row)
    return jnp.where(rows == 0, prev, pltpu.roll(p, 1, 0))


def _head_indicator(width):
    ch = lax.broadcasted_iota(jnp.int32, (width, width // RWKV_HEAD), 0) // RWKV_HEAD
    hd = lax.broadcasted_iota(jnp.int32, (width, width // RWKV_HEAD), 1)
    e = (ch == hd).astype(F32)
    return e, e.T


def _rwkv_prep(p, seq, dims, mu, wd, w0, wa, a0, k_k, k_a, e, et):
    t = p.shape[0]
    width, lora = dims
    cols = 3 * width + 2 * lora
    tm = 128
    per_seq = seq // tm

    def body(p_ref, prev_ref, mu_ref, wd_ref, w0_ref, wa_ref, a0_ref, kk_ref, ka_ref, e_ref, et_ref,
             r_out, w_out, kx_out, v_out, kkn_out, b_out):
        i = pl.program_id(0)
        pv = p_ref[...]
        psh = _shifted(pv, prev_ref[7:8, :], i % per_seq == 0)
        ps = pv + mu_ref[...] * (psh - pv)
        r, k, v = ps[:, :width], ps[:, width:2 * width], ps[:, 2 * width:3 * width]
        wlo, alo = ps[:, 3 * width:3 * width + lora], ps[:, 3 * width + lora:]
        w, kx, kk, b = _prep_math(k, wlo, alo, wd_ref[...], w0_ref[...], wa_ref[...], a0_ref[...],
                                  kk_ref[...], ka_ref[...], e_ref[...], et_ref[...])
        r_out[...] = r
        w_out[...] = w
        kx_out[...] = kx
        v_out[...] = v
        kkn_out[...] = kk
        b_out[...] = b

    full = lambda a: pl.BlockSpec(a.shape, lambda i: (0,) * a.ndim)
    out = pl.BlockSpec((tm, width), lambda i: (i, 0))
    return pl.pallas_call(
        body, name="rwkv_prep", grid=(t // tm,),
        out_shape=tuple(jax.ShapeDtypeStruct((t, width), F32) for _ in range(6)),
        in_specs=[pl.BlockSpec((tm, cols), lambda i: (i, 0)),
                  pl.BlockSpec((8, cols), lambda i: (jnp.maximum(i * (tm // 8) - 1, 0), 0)),
                  full(mu), full(wd), full(w0), full(wa), full(a0), full(k_k), full(k_a), full(e), full(et)],
        out_specs=tuple(out for _ in range(6)),
        compiler_params=_params("parallel"),
    )(p, p, mu, wd, w0, wa, a0, k_k, k_a, e, et)


def _rwkv_prep_bwd(p, seq, dims, mu, wd, w0, wa, a0, k_k, k_a, e, et, dr, dw, dkx, dv, dkk, db):
    t = p.shape[0]
    width, lora = dims
    cols = 3 * width + 2 * lora
    tm = 128
    n_tiles = t // tm
    per_seq = seq // tm

    def body(p_ref, prev_ref, mu_ref, wd_ref, w0_ref, wa_ref, a0_ref, kk_ref, ka_ref, e_ref, et_ref,
             dr_ref, dw_ref, dkx_ref, dv_ref, dkk_ref, db_ref,
             dp_out, dmu_out, dwd_out, dw0_out, dwa_out, da0_out, dkk_out, dka_out, carry):
        step = pl.program_id(0)
        i = n_tiles - 1 - step

        @pl.when(step == 0)
        def _():
            for ref in (dmu_out, dwd_out, dw0_out, dwa_out, da0_out, dkk_out, dka_out, carry):
                ref[...] = jnp.zeros_like(ref)

        pv = p_ref[...]
        first = i % per_seq == 0
        psh = _shifted(pv, prev_ref[7:8, :], first)
        muv = mu_ref[...]
        ps = pv + muv * (psh - pv)
        k = ps[:, width:2 * width]
        wlo, alo = ps[:, 3 * width:3 * width + lora], ps[:, 3 * width + lora:]
        ev, etv = e_ref[...], et_ref[...]
        _, vjp = jax.vjp(lambda *a: _prep_math(*a, ev, etv), k, wlo, alo, wd_ref[...], w0_ref[...],
                         wa_ref[...], a0_ref[...], kk_ref[...], ka_ref[...])
        dk, dwlo, dalo, dwd, dw0, dwa, da0, dk_k, dk_a = vjp(
            (dw_ref[...], dkx_ref[...], dkk_ref[...], db_ref[...]))
        dps = jnp.concatenate([dr_ref[...], dk, dv_ref[...], dwlo, dalo], axis=1)
        dmu_out[...] += jnp.sum(dps * (psh - pv), axis=0, keepdims=True)
        dwd_out[...] += dwd
        dw0_out[...] += dw0
        dwa_out[...] += dwa
        da0_out[...] += da0
        dkk_out[...] += dk_k
        dka_out[...] += dk_a
        dsh = dps * muv
        rows = lax.broadcasted_iota(jnp.int32, dsh.shape, 0)
        nxt = jnp.where(rows == tm - 1, carry[...], pltpu.roll(dsh, tm - 1, 0))
        dp_out[...] = (dps * (1.0 - muv) + nxt).astype(BF16)
        carry[...] = jnp.where(first, 0.0, dsh[0:1, :])

    full = lambda a: pl.BlockSpec(a.shape, lambda s: (0,) * a.ndim)
    tok = pl.BlockSpec((tm, width), lambda s: (n_tiles - 1 - s, 0))
    vec = lambda n: jax.ShapeDtypeStruct((1, n), F32)
    outs = (jax.ShapeDtypeStruct((t, cols), BF16), vec(cols), jax.ShapeDtypeStruct(wd.shape, F32), vec(width),
            jax.ShapeDtypeStruct(wa.shape, F32), vec(width), vec(width), vec(width))
    return pl.pallas_call(
        body, name="rwkv_prep_bwd", grid=(n_tiles,),
        out_shape=outs,
        in_specs=[pl.BlockSpec((tm, cols), lambda s: (n_tiles - 1 - s, 0)),
                  pl.BlockSpec((8, cols), lambda s: (jnp.maximum((n_tiles - 1 - s) * (tm // 8) - 1, 0), 0)),
                  full(mu), full(wd), full(w0), full(wa), full(a0), full(k_k), full(k_a), full(e), full(et),
                  tok, tok, tok, tok, tok, tok],
        out_specs=(pl.BlockSpec((tm, cols), lambda s: (n_tiles - 1 - s, 0)),) + tuple(
            pl.BlockSpec(o.shape, lambda s: (0, 0)) for o in outs[1:]),
        scratch_shapes=[pltpu.VMEM((1, cols), F32)],
        compiler_params=_params("arbitrary"),
    )(p, p, mu, wd, w0, wa, a0, k_k, k_a, e, et, dr, dw, dkx, dv, dkk, db)


def _rwkv_post(ys, r, kx, v, p, g_col, gn_g, gn_b, r_k, e, et):
    t, width = ys.shape
    tm = 256

    def body(ys_ref, r_ref, kx_ref, v_ref, g_ref, gg_ref, gb_ref, rk_ref, e_ref, et_ref, out_ref):
        out_ref[...] = _post_math(ys_ref[...], r_ref[...], kx_ref[...], v_ref[...], g_ref[...], gg_ref[...],
                                  gb_ref[...], rk_ref[...], e_ref[...], et_ref[...]).astype(BF16)

    tok = pl.BlockSpec((tm, width), lambda i: (i, 0))
    full = lambda a: pl.BlockSpec(a.shape, lambda i: (0,) * a.ndim)
    return pl.pallas_call(
        body, name="rwkv_post", grid=(t // tm,),
        out_shape=jax.ShapeDtypeStruct((t, width), BF16),
        in_specs=[tok, tok, tok, tok, pl.BlockSpec((tm, width), lambda i: (i, g_col)),
                  full(gn_g), full(gn_b), full(r_k), full(e), full(et)],
        out_specs=tok,
        compiler_params=_params("parallel"),
    )(ys, r, kx, v, p, gn_g, gn_b, r_k, e, et)


def _rwkv_post_bwd(ys, r, kx, v, p, g_col, gn_g, gn_b, r_k, e, et, dy, dy_col):
    t, width = ys.shape
    tm = 128

    def body(ys_ref, r_ref, kx_ref, v_ref, g_ref, gg_ref, gb_ref, rk_ref, e_ref, et_ref, dy_ref,
             dys_out, dr_out, dkx_out, dv_out, dg_out, dgg_out, dgb_out, drk_out):
        @pl.when(pl.program_id(0) == 0)
        def _():
            for ref in (dgg_out, dgb_out, drk_out):
                ref[...] = jnp.zeros_like(ref)

        ev, etv = e_ref[...], et_ref[...]
        _, vjp = jax.vjp(lambda *a: _post_math(*a, ev, etv), ys_ref[...], r_ref[...], kx_ref[...], v_ref[...],
                         g_ref[...], gg_ref[...], gb_ref[...], rk_ref[...])
        dys, dr, dkx, dv, dg, dgg, dgb, drk = vjp(dy_ref[...])
        dys_out[...] = dys
        dr_out[...] = dr
        dkx_out[...] = dkx
        dv_out[...] = dv
        dg_out[...] = dg.astype(BF16)
        dgg_out[...] += dgg
        dgb_out[...] += dgb
        drk_out[...] += drk

    tok = pl.BlockSpec((tm, width), lambda i: (i, 0))
    full = lambda a: pl.BlockSpec(a.shape, lambda i: (0,) * a.ndim)
    big = jax.ShapeDtypeStruct((t, width), F32)
    vec = jax.ShapeDtypeStruct((1, width), F32)
    vspec = pl.BlockSpec((1, width), lambda i: (0, 0))
    return pl.pallas_call(
        body, name="rwkv_post_bwd", grid=(t // tm,),
        out_shape=(big, big, big, big, jax.ShapeDtypeStruct((t, width), BF16), vec, vec, vec),
        in_specs=[tok, tok, tok, tok, pl.BlockSpec((tm, width), lambda i: (i, g_col)),
                  full(gn_g), full(gn_b), full(r_k), full(e), full(et),
                  pl.BlockSpec((tm, width), lambda i: (i, dy_col))],
        out_specs=(tok, tok, tok, tok, tok, vspec, vspec, vspec),
        compiler_params=_params("arbitrary"),
    )(ys, r, kx, v, p, gn_g, gn_b, r_k, e, et, dy)


def _to_scan_k(a, bl, seq):
    h = a.shape[1] // RWKV_HEAD
    twice = jnp.broadcast_to(a.reshape(1, bl, seq, h, RWKV_HEAD), (2, bl, seq, h, RWKV_HEAD))
    return twice.transpose(2, 4, 0, 1, 3).reshape(seq, RWKV_HEAD, 2 * bl * h)


def _to_scan_v(a, bl, seq):
    h = a.shape[1] // RWKV_HEAD
    half = RWKV_HEAD // 2
    return a.reshape(bl, seq, h, 2, half).transpose(1, 4, 3, 0, 2).reshape(seq, half, 2 * bl * h)


def _from_scan_k(a, bl, seq):
    h = a.shape[2] // (2 * bl)
    a = a[:, :, :bl * h].reshape(seq, RWKV_HEAD, bl, h).transpose(2, 0, 3, 1)
    return a.reshape(bl * seq, h * RWKV_HEAD)


def _from_scan_v(a, bl, seq):
    half = RWKV_HEAD // 2
    h = a.shape[2] // (2 * bl)
    a = a.reshape(seq, half, 2, bl, h).transpose(3, 0, 4, 2, 1)
    return a.reshape(bl * seq, h * RWKV_HEAD)


def _scan_fwd(kk, w, b, kx, r, v, gathers=()):
    seq, nk, lanes = kk.shape
    nv = v.shape[1]
    tt = SCAN_STEPS_PER_BLOCK
    nblk = seq // tt
    ng = len(gathers)

    def body(*refs):
        kk_ref, w_ref, b_ref, kx_ref, r_ref, v_ref = refs[:6]
        g_src = refs[6:6 + ng]
        y_ref, st_ref, sa_ref = refs[6 + ng:9 + ng]
        g_out = refs[9 + ng:9 + 2 * ng]
        s_ref = refs[9 + 2 * ng]
        pid = pl.program_id(0)

        @pl.when(pid == 0)
        def _():
            s_ref[...] = jnp.zeros_like(s_ref)

        if ng:
            send_sems, recv_sems, local_sems = refs[10 + 2 * ng:]
            phases = [_gather_phases(g_src[n], g_out[n], send_sems, recv_sems, local_sems, n) for n in range(ng)]
            for k, at in enumerate((0, (3 * nblk) // 4, nblk - 1)):
                @pl.when(pid == at)
                def _(k=k):
                    for ph in phases:
                        ph[k]()

        def step(i, carry):
            kkv, wv, bv, kxv, rv = kk_ref[i], w_ref[i], b_ref[i], kx_ref[i], r_ref[i]
            for j in range(nv):
                s_old = s_ref[j]
                sa = -jnp.sum(s_old * kkv, axis=0, keepdims=True)
                s_new = s_old * wv + sa * bv + v_ref[i, j:j + 1, :] * kxv
                s_ref[j] = s_new
                st_ref[i, j] = s_new
                sa_ref[i, j:j + 1, :] = sa
                y_ref[i, j:j + 1, :] = jnp.sum(s_new * rv, axis=0, keepdims=True)
            return carry

        lax.fori_loop(0, tt, step, 0)

    krow = pl.BlockSpec((tt, nk, lanes), lambda i: (i, 0, 0))
    vrow = pl.BlockSpec((tt, nv, lanes), lambda i: (i, 0, 0))
    vshape = jax.ShapeDtypeStruct((seq, nv, lanes), F32)
    return pl.pallas_call(
        body, name="rwkv_scan_fwd", grid=(nblk,),
        out_shape=(vshape, jax.ShapeDtypeStruct((seq, nv, nk, lanes), F32), vshape,
                   *[_gathered_shape(g) for g in gathers]),
        in_specs=[krow, krow, krow, krow, krow, vrow] + [_HBM] * ng,
        out_specs=(vrow, pl.BlockSpec((tt, nv, nk, lanes), lambda i: (i, 0, 0, 0)), vrow, *[_HBM] * ng),
        scratch_shapes=[pltpu.VMEM((nv, nk, lanes), F32)] + (_gather_scratch(ng) if ng else []),
        compiler_params=_params("arbitrary"),
    )(kk, w, b, kx, r, v, *gathers)


def _scan_bwd(kk, w, b, kx, r, v, states, sa_all, dy, scatters=()):
    seq, nk, lanes = kk.shape
    nv = v.shape[1]
    tt = SCAN_STEPS_PER_BLOCK
    nblk = seq // tt
    nsc = len(scatters)

    def both_halves(a):
        return a + pltpu.roll(a, lanes // 2, 1)

    def body(*refs):
        kk_ref, w_ref, b_ref, kx_ref, r_ref, v_ref, st_ref, before_ref, sa_ref, dy_ref = refs[:10]
        dkk_ref, dw_ref, db_ref, dkx_ref, dr_ref, dv_ref = refs[10 + nsc:16 + nsc]
        g_ref = refs[16 + 2 * nsc]

        if nsc:
            send_sems, recv_sems, local_sems = refs[17 + 2 * nsc:]
            phases = [_scatter_phases(refs[10 + n], refs[16 + nsc + n], send_sems, recv_sems, local_sems, n)
                      for n in range(nsc)]
            for k, at in enumerate((0, nblk - 1)):
                @pl.when(pl.program_id(0) == at)
                def _(k=k):
                    for ph in phases:
                        ph[k]()

        @pl.when(pl.program_id(0) == 0)
        def _():
            g_ref[...] = jnp.zeros_like(g_ref)

        def one_step(i, state_before):
            kkv, wv, bv, kxv, rv = kk_ref[i], w_ref[i], b_ref[i], kx_ref[i], r_ref[i]
            zero = jnp.zeros((nk, lanes), F32)
            a_r, a_w, a_b, a_kx, a_kk = zero, zero, zero, zero, zero
            for j in range(nv):
                s_old = state_before(j)
                s_new = st_ref[i, j]
                vv = v_ref[i, j:j + 1, :]
                dyv = dy_ref[i, j:j + 1, :]
                sa = sa_ref[i, j:j + 1, :]
                g = g_ref[j] + dyv * rv
                a_r = a_r + s_new * dyv
                a_w = a_w + g * s_old
                dsa = jnp.sum(g * bv, axis=0, keepdims=True)
                a_b = a_b + g * sa
                dv_ref[i, j:j + 1, :] = jnp.sum(g * kxv, axis=0, keepdims=True)
                a_kx = a_kx + g * vv
                a_kk = a_kk + s_old * dsa
                g_ref[j] = g * wv - dsa * kkv
            dr_ref[i] = both_halves(a_r)
            dw_ref[i] = both_halves(a_w)
            db_ref[i] = both_halves(a_b)
            dkx_ref[i] = both_halves(a_kx)
            dkk_ref[i] = -both_halves(a_kk)

        def step(n, carry):
            i = tt - 1 - n
            one_step(i, lambda j: st_ref[i - 1, j])
            return carry

        lax.fori_loop(0, tt - 1, step, 0)
        at_start = pl.program_id(0) == nblk - 1
        one_step(0, lambda j: jnp.where(at_start, 0.0, before_ref[0, j]))

    rev = lambda i: nblk - 1 - i
    krow = pl.BlockSpec((tt, nk, lanes), lambda i: (rev(i), 0, 0))
    vrow = pl.BlockSpec((tt, nv, lanes), lambda i: (rev(i), 0, 0))
    kshape = jax.ShapeDtypeStruct((seq, nk, lanes), F32)
    return pl.pallas_call(
        body, name="rwkv_scan_bwd", grid=(nblk,),
        out_shape=(kshape, kshape, kshape, kshape, kshape, jax.ShapeDtypeStruct((seq, nv, lanes), F32),
                   *[jax.ShapeDtypeStruct(s.shape, s.dtype) for s in scatters]),
        in_specs=[krow, krow, krow, krow, krow, vrow,
                  pl.BlockSpec((tt, nv, nk, lanes), lambda i: (rev(i), 0, 0, 0)),
                  pl.BlockSpec((1, nv, nk, lanes), lambda i: (jnp.maximum(rev(i) * tt - 1, 0), 0, 0, 0)),
                  vrow, vrow] + [_HBM] * nsc,
        out_specs=(krow, krow, krow, krow, krow, vrow, *[_HBM] * nsc),
        scratch_shapes=[pltpu.VMEM((nv, nk, lanes), F32)] + (_scatter_scratch(nsc) if nsc else []),
        compiler_params=_params("arbitrary"),
    )(kk, w, b, kx, r, v, states, states, sa_all, dy, *scatters)


_NT = (((1,), (1,)), ((), ()))
_TN = (((0,), (0,)), ((), ()))
SB_SCALE = 1.0 / (SB_HEAD ** 0.5)


SB_QUERY_ROWS = 256


def _sb_masks(qr):
    blk = SB_BLOCK
    row = lax.broadcasted_iota(jnp.int32, (qr, blk), 0)
    col = lax.broadcasted_iota(jnp.int32, (qr, blk), 1)
    j_in = lax.broadcasted_iota(jnp.int32, (blk, blk), 0)
    s_in = lax.broadcasted_iota(jnp.int32, (blk, blk), 1)
    ones = jnp.ones((blk, blk), BF16)
    fwd = jnp.concatenate([(j_in > s_in).astype(BF16), ones], axis=1)
    bwd = jnp.concatenate([(s_in > j_in).astype(BF16), ones], axis=1)
    return row, col, fwd, bwd


def _split_dots(xs, b):
    his = [a.astype(BF16) for a in xs]
    los = [(a - hi.astype(F32)).astype(BF16) for a, hi in zip(xs, his)]
    tops = [jnp.dot(hi, b, preferred_element_type=F32) for hi in his]
    return [top + jnp.dot(lo, b, preferred_element_type=F32) for top, lo in zip(tops, los)]


SB_HEADS_PER_STEP = 2


def _sb_chains(bl):
    return [(b, slice(hh * SB_HEAD, (hh + 1) * SB_HEAD)) for b in range(bl) for hh in range(SB_HEADS_PER_STEP)]


def _sb_specs(bl, seq, cols, n_heads):
    hp = SB_HEADS_PER_STEP
    qr = min(SB_QUERY_ROWS, seq)
    assert n_heads % hp == 0 and all(c % hp == 0 for c in cols) and seq % qr == 0 and qr % SB_BLOCK == 0
    qspec = lambda col: pl.BlockSpec((bl, qr, hp * SB_HEAD), lambda h, i: (0, i, col // hp + h))
    kspec = lambda col: pl.BlockSpec((bl, seq, hp * SB_HEAD), lambda h, i: (0, 0, col // hp + h))
    return qr, qspec, kspec


def _sb_fwd(p, bl, seq, cols, n_heads, gathers=()):
    t = p.shape[0]
    q_col, k_col, v_col, g_col = cols
    blk = SB_BLOCK
    chains = _sb_chains(bl)
    qr, qspec, kspec = _sb_specs(bl, seq, cols, n_heads)
    nq = seq // qr
    per_tile = qr // blk

    ng = len(gathers)
    n_groups = n_heads // SB_HEADS_PER_STEP
    n_steps = n_groups * nq

    def body(*refs):
        q_ref, k_ref, v_ref, g_ref = refs[:4]
        out_ref, o_ref, tot_ref = refs[4 + ng:7 + ng]
        qi = pl.program_id(1)

        if ng:
            send_sems, recv_sems, local_sems = refs[7 + 2 * ng:]
            phases = [_gather_phases(refs[4 + n], refs[7 + ng + n], send_sems, recv_sems, local_sems, n)
                      for n in range(ng)]
            now = pl.program_id(0) * nq + qi
            for k, at in enumerate((0, (3 * n_steps) // 4, n_steps - 1)):
                @pl.when(now == at)
                def _(k=k):
                    for ph in phases:
                        ph[k]()

        row, col, mix, _ = _sb_masks(qr)
        qbs = [q_ref[b, :, ln].astype(BF16) for b, ln in chains]
        n_blocks = (qi + 1) * per_tile

        def step(n, carry):
            j = n_blocks - 1 - n
            rows = pl.ds(pl.multiple_of(j * blk, blk), blk)
            causal = (j * blk + col) < (qi * qr + row)
            zs = [lax.dot_general(qb, k_ref[b, rows, ln].astype(BF16), _NT, preferred_element_type=F32) * SB_SCALE
                  for (b, ln), qb in zip(chains, qbs)]
            lszs = [_log_sigmoid(z) for z in zs]
            boths = _split_dots([jnp.where(causal, lsz - z, 0.0) for lsz, z in zip(lszs, zs)], mix)
            atts = [jnp.where(causal, jnp.exp(lsz + both[:, :blk] + later), 0.0).astype(BF16)
                    for lsz, both, (_, later) in zip(lszs, boths, carry)]
            outs = [out + jnp.dot(att, v_ref[b, rows, ln].astype(BF16), preferred_element_type=F32)
                    for (b, ln), att, (out, _) in zip(chains, atts, carry)]
            return tuple((out, later + both[:, blk:]) for out, both, (_, later) in zip(outs, boths, carry))

        zero = jnp.zeros((qr, SB_HEAD), F32)
        done = lax.fori_loop(0, n_blocks, step, tuple((zero, zero) for _ in chains))
        for (b, ln), (out, total) in zip(chains, done):
            o_ref[b, :, ln] = out
            tot_ref[b, :, ln] = total
            out_ref[b, :, ln] = (out * _silu(g_ref[b, :, ln])).astype(BF16)

    width = n_heads * SB_HEAD
    p3 = p.reshape(bl, seq, p.shape[1])
    f32 = jax.ShapeDtypeStruct((bl, seq, width), F32)
    outs = pl.pallas_call(
        body, name="sb_attn_fwd", grid=(n_groups, nq),
        out_shape=(jax.ShapeDtypeStruct((bl, seq, width), BF16), f32, f32, *[_gathered_shape(g) for g in gathers]),
        in_specs=[qspec(q_col), kspec(k_col), kspec(v_col), qspec(g_col)] + [_HBM] * ng,
        out_specs=(qspec(0), qspec(0), qspec(0), *[_HBM] * ng),
        scratch_shapes=_gather_scratch(ng) if ng else [],
        compiler_params=_params("arbitrary", "arbitrary"),
    )(p3, p3, p3, p3, *gathers)
    return tuple(a.reshape(t, width) for a in outs[:3]) + tuple(outs[3:])


def _sb_bwd(p, bl, seq, cols, n_heads, dy, dy_col, o, tot, swaps=()):
    t = p.shape[0]
    q_col, k_col, v_col, g_col = cols
    blk = SB_BLOCK
    ns = len(swaps)
    n_groups = n_heads // SB_HEADS_PER_STEP
    chains = _sb_chains(bl)
    qr, qspec, kspec = _sb_specs(bl, seq, cols + (dy_col,), n_heads)
    nq = seq // qr
    per_tile = qr // blk

    def body(*refs):
        q_ref, k_ref, v_ref, g_ref, dy_ref, o_ref, tot_ref = refs[:7]
        dq_out, dk_out, dv_out, dg_out = refs[7 + ns:11 + ns]
        dk_acc, dv_acc = refs[11 + 2 * ns:13 + 2 * ns]
        group, qi = pl.program_id(0), pl.program_id(1)

        if ns:
            send_sems, recv_sems = refs[13 + 2 * ns:]
            phases = [_swap_phases(refs[7 + n], refs[11 + ns + n], send_sems, recv_sems, n) for n in range(ns)]
            for k, at in enumerate(((0, 0), (n_groups - 1, nq - 1))):
                @pl.when((group == at[0]) & (qi == at[1]))
                def _(k=k):
                    for ph in phases:
                        ph[k]()

        @pl.when(qi == 0)
        def _():
            dk_acc[...] = jnp.zeros_like(dk_acc)
            dv_acc[...] = jnp.zeros_like(dv_acc)

        row, col, mix, mix_t = _sb_masks(qr)
        qbs, dobs, totals = [], [], []
        for b, ln in chains:
            gate = g_ref[b, :, ln]
            sg = jax.nn.sigmoid(gate)
            dyv = dy_ref[b, :, ln]
            dg_out[b, :, ln] = (dyv * o_ref[b, :, ln] * (sg * (1.0 + gate * (1.0 - sg)))).astype(BF16)
            dobs.append((dyv * (gate * sg)).astype(BF16))
            qbs.append(q_ref[b, :, ln].astype(BF16))
            totals.append(tot_ref[b, :, ln])

        def step(j, carry):
            rows = pl.ds(pl.multiple_of(j * blk, blk), blk)
            causal = (j * blk + col) < (qi * qr + row)
            kbs = [k_ref[b, rows, ln].astype(BF16) for b, ln in chains]
            zs = [lax.dot_general(qb, kb, _NT, preferred_element_type=F32) * SB_SCALE for qb, kb in zip(qbs, kbs)]
            datts = [lax.dot_general(dob, v_ref[b, rows, ln].astype(BF16), _NT, preferred_element_type=F32)
                     for (b, ln), dob in zip(chains, dobs)]
            lszs = [_log_sigmoid(z) for z in zs]
            boths = _split_dots([jnp.where(causal, lsz - z, 0.0) for lsz, z in zip(lszs, zs)], mix)
            seens = [seen + both[:, blk:] for both, (_, seen, _) in zip(boths, carry)]
            atts = [jnp.where(causal, jnp.exp(lsz + both[:, :blk] + (total - seen)), 0.0)
                    for lsz, both, total, seen in zip(lszs, boths, totals, seens)]
            dls = [att * datt for att, datt in zip(atts, datts)]
            for (b, ln), att, dob in zip(chains, atts, dobs):
                dv_acc[b, rows, ln] += lax.dot_general(att.astype(BF16), dob, _TN, preferred_element_type=F32)
            boths_t = _split_dots(dls, mix_t)
            dkeeps = [jnp.where(causal, both_t[:, :blk] + dl_before, 0.0)
                      for both_t, (_, _, dl_before) in zip(boths_t, carry)]
            dzbs = [(((dl + dkeep) * jax.nn.sigmoid(-z) - dkeep) * SB_SCALE).astype(BF16)
                    for dl, dkeep, z in zip(dls, dkeeps, zs)]
            dqs = [dq + jnp.dot(dzb, kb, preferred_element_type=F32) for dzb, kb, (dq, _, _) in zip(dzbs, kbs, carry)]
            for (b, ln), dzb, qb in zip(chains, dzbs, qbs):
                dk_acc[b, rows, ln] += lax.dot_general(dzb, qb, _TN, preferred_element_type=F32)
            return tuple((dq, seen, dl_before + both_t[:, blk:])
                         for dq, seen, both_t, (_, _, dl_before) in zip(dqs, seens, boths_t, carry))

        zero = jnp.zeros((qr, SB_HEAD), F32)
        done = lax.fori_loop(0, (qi + 1) * per_tile, step, tuple((zero, zero, zero) for _ in chains))
        for (b, ln), (dq, _, _) in zip(chains, done):
            dq_out[b, :, ln] = dq.astype(BF16)

        @pl.when(qi == nq - 1)
        def _():
            dk_out[...] = dk_acc[...].astype(BF16)
            dv_out[...] = dv_acc[...].astype(BF16)

    width = n_heads * SB_HEAD
    shape = jax.ShapeDtypeStruct((bl, seq, width), BF16)
    as3 = lambda a: a.reshape(bl, seq, a.shape[1])
    p3 = as3(p)
    acc = pltpu.VMEM((bl, seq, SB_HEADS_PER_STEP * SB_HEAD), F32)
    outs = pl.pallas_call(
        body, name="sb_attn_bwd", grid=(n_groups, nq),
        out_shape=(shape, shape, shape, shape, *[_swapped_shape(s) for s in swaps]),
        in_specs=[qspec(q_col), kspec(k_col), kspec(v_col), qspec(g_col), qspec(dy_col), qspec(0), qspec(0)]
        + [_HBM] * ns,
        out_specs=(qspec(0), kspec(0), kspec(0), qspec(0), *[_HBM] * ns),
        scratch_shapes=[acc, acc] + (_swap_scratch(ns) if ns else []),
        compiler_params=_params("arbitrary", "arbitrary"),
    )(p3, p3, p3, p3, as3(dy), as3(o), as3(tot), *swaps)
    return tuple(a.reshape(t, width) for a in outs[:4]) + tuple(outs[4:])


def _gelu(x):
    return 0.5 * x * (1.0 + lax.erf(x * (2.0 ** -0.5)))


def _sgu_math(us, vs, gs, ln_g, ln_b, ws, bs):
    width = sum(v.shape[1] for v in vs)
    vg = [_gelu(v) for v in vs]
    mu = sum(jnp.sum(v, axis=1, keepdims=True) for v in vg) * (1.0 / width)
    dl = [v - mu for v in vg]
    var = sum(jnp.sum(d * d, axis=1, keepdims=True) for d in dl) * (1.0 / width)
    rstd = lax.rsqrt(var + LN_EPS)
    n = ws[0].shape[0]
    tri = lax.broadcasted_iota(jnp.int32, (n, n), 0) >= lax.broadcasted_iota(jnp.int32, (n, n), 1)
    outs = []
    for i in range(len(vs)):
        vn = dl[i] * rstd * ln_g[i] + ln_b[i]
        mixed = _dot32(jnp.where(tri, ws[i], 0.0), vn) + bs[i]
        outs.append(_gelu(us[i]) * mixed * _silu(gs[i]))
    return outs


def _sgu_load(p_ref, lng_ref, lnb_ref, ws_ref, bs_ref, width):
    gd = width // SGU_GROUPS
    grp = lambda ref, base, i: ref[:, base + i * gd:base + (i + 1) * gd]
    idx = range(SGU_GROUPS)
    return ([grp(p_ref, 0, i) for i in idx], [grp(p_ref, width, i) for i in idx],
            [grp(p_ref, 2 * width, i) for i in idx], [grp(lng_ref, 0, i) for i in idx],
            [grp(lnb_ref, 0, i) for i in idx], [ws_ref[i] for i in idx], [bs_ref[i] for i in idx])


def _sgu_fwd(p, ln_g, ln_b, w_s, b_s):
    t = p.shape[0]
    width = p.shape[1] // 3
    gd = width // SGU_GROUPS
    tm = SGU_CHUNK

    def body(p_ref, lng_ref, lnb_ref, ws_ref, bs_ref, y_ref):
        outs = _sgu_math(*_sgu_load(p_ref, lng_ref, lnb_ref, ws_ref, bs_ref, width))
        for i in range(SGU_GROUPS):
            y_ref[:, i * gd:(i + 1) * gd] = outs[i].astype(BF16)

    full = lambda a: pl.BlockSpec(a.shape, lambda i: (0,) * a.ndim)
    return pl.pallas_call(
        body, name="sgu_fwd", grid=(t // tm,),
        out_shape=jax.ShapeDtypeStruct((t, width), BF16),
        in_specs=[pl.BlockSpec((tm, 3 * width), lambda i: (i, 0)), full(ln_g), full(ln_b), full(w_s), full(b_s)],
        out_specs=pl.BlockSpec((tm, width), lambda i: (i, 0)),
        compiler_params=_params("parallel"),
    )(p, ln_g, ln_b, w_s, b_s)


def _sgu_bwd(p, ln_g, ln_b, w_s, b_s, dy):
    t = p.shape[0]
    width = p.shape[1] // 3
    gd = width // SGU_GROUPS
    tm = SGU_CHUNK

    def body(p_ref, lng_ref, lnb_ref, ws_ref, bs_ref, dy_ref, dp_out, dlng_out, dlnb_out, dws_out, dbs_out):
        @pl.when(pl.program_id(0) == 0)
        def _():
            for ref in (dlng_out, dlnb_out, dws_out, dbs_out):
                ref[...] = jnp.zeros_like(ref)

        _, vjp = jax.vjp(_sgu_math, *_sgu_load(p_ref, lng_ref, lnb_ref, ws_ref, bs_ref, width))
        dus, dvs, dgs, dlng, dlnb, dws, dbs = vjp(
            [dy_ref[:, i * gd:(i + 1) * gd] for i in range(SGU_GROUPS)])
        for i in range(SGU_GROUPS):
            cols = slice(i * gd, (i + 1) * gd)
            dp_out[:, i * gd:(i + 1) * gd] = dus[i].astype(BF16)
            dp_out[:, width + i * gd:width + (i + 1) * gd] = dvs[i].astype(BF16)
            dp_out[:, 2 * width + i * gd:2 * width + (i + 1) * gd] = dgs[i].astype(BF16)
            dlng_out[:, cols] += dlng[i]
            dlnb_out[:, cols] += dlnb[i]
            dws_out[i] += dws[i]
            dbs_out[i] += dbs[i]

    full = lambda a: pl.BlockSpec(a.shape, lambda i: (0,) * a.ndim)
    like = lambda a: jax.ShapeDtypeStruct(a.shape, F32)
    return pl.pallas_call(
        body, name="sgu_bwd", grid=(t // tm,),
        out_shape=(jax.ShapeDtypeStruct((t, 3 * width), BF16), like(ln_g), like(ln_b), like(w_s), like(b_s)),
        in_specs=[pl.BlockSpec((tm, 3 * width), lambda i: (i, 0)), full(ln_g), full(ln_b), full(w_s), full(b_s),
                  pl.BlockSpec((tm, width), lambda i: (i, 0))],
        out_specs=(pl.BlockSpec((tm, 3 * width), lambda i: (i, 0)), full(ln_g), full(ln_b), full(w_s), full(b_s)),
        compiler_params=_params("arbitrary"),
    )(p, ln_g, ln_b, w_s, b_s, dy)


def _sum_slabs(parts, name):
    n_parts, rows, cols = parts.shape
    tr = _tile(rows, max(16, (1 << 18) // cols), 16)

    def body(p_ref, o_ref):
        acc = p_ref[0].astype(F32)
        for d in range(1, n_parts):
            acc = acc + p_ref[d].astype(F32)
        o_ref[...] = acc

    return pl.pallas_call(
        body, name=name, grid=(rows // tr,),
        out_shape=jax.ShapeDtypeStruct((rows, cols), F32),
        in_specs=[pl.BlockSpec((n_parts, tr, cols), lambda i: (0, i, 0))],
        out_specs=pl.BlockSpec((tr, cols), lambda i: (i, 0)),
        compiler_params=_params("parallel"),
    )(parts)


def _adamw(w, g, m, v, name):
    rows, cols = w.shape
    tr = _tile(rows, max(8, (1 << 18) // cols), 8)

    def body(w_ref, g_ref, m_ref, v_ref, d_out, m_out, v_out):
        gv = g_ref[...]
        mn = ADAM_B1 * m_ref[...] + (1.0 - ADAM_B1) * gv
        vn = ADAM_B2 * v_ref[...] + (1.0 - ADAM_B2) * (gv * gv)
        m_hat = mn / (1.0 - ADAM_B1 ** ADAM_STEP)
        v_hat = vn / (1.0 - ADAM_B2 ** ADAM_STEP)
        d_out[...] = -ADAM_LR * (m_hat / (jnp.sqrt(v_hat) + ADAM_EPS) + ADAM_WD * w_ref[...])
        m_out[...] = mn
        v_out[...] = vn

    blk = pl.BlockSpec((tr, cols), lambda i: (i, 0))
    shape = jax.ShapeDtypeStruct((rows, cols), F32)
    return pl.pallas_call(
        body, name=name, grid=(rows // tr,),
        out_shape=(shape, shape, shape),
        in_specs=[blk, blk, blk, blk], out_specs=(blk, blk, blk),
        compiler_params=_params("parallel"),
    )(w, g, m, v)


PACK_COLS = 1024


def _pack(arrays):
    flat = jnp.concatenate([a.reshape(-1).astype(F32) for a in arrays])
    rows = -(-flat.shape[0] // (8 * PACK_COLS)) * 8
    return jnp.pad(flat, (0, rows * PACK_COLS - flat.shape[0])).reshape(rows, PACK_COLS)


def _unpack(packed, shapes):
    flat = packed.reshape(-1)
    out, at = [], 0
    for s in shapes:
        n = 1
        for d in s:
            n *= d
        out.append(flat[at:at + n].reshape(s))
        at += n
    return out


def kernel(x, norm_g, final_norm_g, e_w_in, e_shift_mu, e_w_decay_up, e_w0, e_a_up, e_a0, e_k_k, e_k_a, e_r_k, e_gn_g, e_gn_b, e_w_out, o_w_in, o_ln_g, o_ln_b, o_w_s, o_b_s, o_w_out, loss_target, m_norm_g, m_final_norm_g, m_e_w_in, m_e_shift_mu, m_e_w_decay_up, m_e_w0, m_e_a_up, m_e_a0, m_e_k_k, m_e_k_a, m_e_r_k, m_e_gn_g, m_e_gn_b, m_e_w_out, m_o_w_in, m_o_ln_g, m_o_ln_b, m_o_w_s, m_o_b_s, m_o_w_out, v_norm_g, v_final_norm_g, v_e_w_in, v_e_shift_mu, v_e_w_decay_up, v_e_w0, v_e_a_up, v_e_a0, v_e_k_k, v_e_k_a, v_e_r_k, v_e_gn_g, v_e_gn_b, v_e_w_out, v_o_w_in, v_o_ln_g, v_o_ln_b, v_o_w_s, v_o_b_s, v_o_w_out):
    weights = dict(norm_g=norm_g, final_norm_g=final_norm_g, e_w_in=e_w_in, e_shift_mu=e_shift_mu,
                   e_w_decay_up=e_w_decay_up, e_w0=e_w0, e_a_up=e_a_up, e_a0=e_a0, e_k_k=e_k_k, e_k_a=e_k_a,
                   e_r_k=e_r_k, e_gn_g=e_gn_g, e_gn_b=e_gn_b, e_w_out=e_w_out, o_w_in=o_w_in, o_ln_g=o_ln_g,
                   o_ln_b=o_ln_b, o_w_s=o_w_s, o_b_s=o_b_s, o_w_out=o_w_out)
    mom1 = dict(norm_g=m_norm_g, final_norm_g=m_final_norm_g, e_w_in=m_e_w_in, e_shift_mu=m_e_shift_mu,
                e_w_decay_up=m_e_w_decay_up, e_w0=m_e_w0, e_a_up=m_e_a_up, e_a0=m_e_a0, e_k_k=m_e_k_k,
                e_k_a=m_e_k_a, e_r_k=m_e_r_k, e_gn_g=m_e_gn_g, e_gn_b=m_e_gn_b, e_w_out=m_e_w_out,
                o_w_in=m_o_w_in, o_ln_g=m_o_ln_g, o_ln_b=m_o_ln_b, o_w_s=m_o_w_s, o_b_s=m_o_b_s,
                o_w_out=m_o_w_out)
    mom2 = dict(norm_g=v_norm_g, final_norm_g=v_final_norm_g, e_w_in=v_e_w_in, e_shift_mu=v_e_shift_mu,
                e_w_decay_up=v_e_w_decay_up, e_w0=v_e_w0, e_a_up=v_e_a_up, e_a0=v_e_a0, e_k_k=v_e_k_k,
                e_k_a=v_e_k_a, e_r_k=v_e_r_k, e_gn_g=v_e_gn_g, e_gn_b=v_e_gn_b, e_w_out=v_e_w_out,
                o_w_in=v_o_w_in, o_ln_g=v_o_ln_g, o_ln_b=v_o_ln_b, o_w_s=v_o_w_s, o_b_s=v_o_b_s,
                o_w_out=v_o_w_out)
    names = list(weights)
    big = ("e_w_in", "e_w_out", "o_w_in", "o_w_out")

    bl, seq, d = x.shape
    t = bl * seq
    width = e_w0.shape[1]
    lora = e_w_decay_up.shape[1]
    n_sb = width // SB_HEAD
    me = 4 * lax.axis_index("x") + 2 * lax.axis_index("y") + lax.axis_index("c")

    e_win_t = _all_gather(e_w_in[0].T.astype(BF16), "gather_e_w_in").reshape(-1, d)
    later_shards = (e_w_out[0].astype(BF16), o_w_in[0].T.astype(BF16), o_w_out[0].astype(BF16))
    sharded_small = ("e_w_decay_up", "e_a_up", "o_ln_g", "o_ln_b")
    small_shapes = [weights[n][0].shape for n in sharded_small]
    got = _all_gather(_pack([weights[n][0] for n in sharded_small]), "gather_small")
    per_dev = [_unpack(got[dev], small_shapes) for dev in range(N_DEV)]
    wd, wa, ln_g, ln_b = [jnp.concatenate([per_dev[dev][i] for dev in range(N_DEV)], axis=-1).reshape(
        small_shapes[i][:-1] + (-1,)) for i in range(4)]
    ln_g, ln_b = ln_g.reshape(1, -1), ln_b.reshape(1, -1)
    e_ind, e_ind_t = _head_indicator(width)
    b_s3 = o_b_s[0][:, :, None]

    x2d = x.reshape(t, d)
    target = loss_target.reshape(t, d)
    cols_rwkv = 3 * width + 2 * lora
    assert cols_rwkv % LANES == 0 and width % LANES == 0
    sb0 = (cols_rwkv + width) // SB_HEAD
    sb_cols = (sb0, sb0 + n_sb, sb0 + 2 * n_sb, sb0 + 3 * n_sb)

    h0 = _rms_fwd(x2d, norm_g[0:1], "rms0_fwd")
    p = _matmul(h0, e_win_t, "nt", F32, "e_in_fwd", tm=2048, tn=1280)
    g_rwkv = p[:, cols_rwkv:cols_rwkv + width]
    r, w, kx, v, kk, b = _rwkv_prep(p, seq, (width, lora), e_shift_mu, wd, e_w0, wa, e_a0, e_k_k, e_k_a,
                                    e_ind, e_ind_t)
    sk = [_to_scan_k(a, bl, seq) for a in (kk, w, b, kx, r)]
    sv = _to_scan_v(v, bl, seq)
    ys_scan, states, sa_all, e_wout, o_wout = _scan_fwd(*sk, sv, gathers=(later_shards[0], later_shards[2]))
    ys = _from_scan_v(ys_scan, bl, seq)
    ya = _rwkv_post(ys, r, kx, v, g_rwkv, 0, e_gn_g, e_gn_b, e_r_k, e_ind, e_ind_t)
    yb, sb_o, sb_tot, o_win_t = _sb_fwd(p, bl, seq, sb_cols, n_sb, gathers=(later_shards[1],))
    e_wout, o_win_t, o_wout = (a.reshape(-1, d) for a in (e_wout, o_win_t, o_wout))
    y = jnp.concatenate([ya, yb], axis=1)
    x1 = _matmul(y, e_wout, "nn", F32, "e_out_fwd", res=x2d)
    h1 = _rms_fwd(x1, norm_g[1:2], "rms1_fwd")
    p2 = _matmul(h1, o_win_t, "nt", F32, "o_in_fwd", tm=2048)
    y2 = _sgu_fwd(p2, ln_g, ln_b, o_w_s[0], b_s3)
    x2 = _matmul(y2, o_wout, "nn", F32, "o_out_fwd", res=x1)
    dx2, d_final_g, loss_part = _final_loss(x2, final_norm_g.reshape(1, d), target, "final_loss")

    dy2 = _matmul(dx2, o_wout, "nt", F32, "o_out_bwd_x", tm=2048)
    d_o_wout = _matmul(y2, dx2, "tn", F32, "o_out_bwd_w", tn=2048)
    dp2, d_ln_g, d_ln_b, d_w_s, d_b_s3 = _sgu_bwd(p2, ln_g, ln_b, o_w_s[0], b_s3, dy2)
    dh1 = _matmul(dp2, o_win_t, "nn", F32, "o_in_bwd_x", tm=2048, tk=1536)
    d_o_win_t = _matmul(dp2, h1, "tn", F32, "o_in_bwd_w", tn=2048)
    dx1, d_g1 = _rms_bwd(x1, norm_g[1:2], dh1, dx2, "rms1_bwd")
    dy = _matmul(dx1, e_wout, "nt", F32, "e_out_bwd_x", tm=2048)
    d_e_wout = _matmul(y, dx1, "tn", F32, "e_out_bwd_w", tn=2048)
    core = lax.axis_index("c").astype(jnp.int32).reshape(1)
    by_owner = lambda full: full.reshape((N_CHIPS, 2, full.shape[0] // N_DEV, full.shape[1]))
    early = {"e_w_out": by_owner(d_e_wout), "o_w_in": by_owner(d_o_win_t), "o_w_out": by_owner(d_o_wout)}
    dq, dk, dv_sb, dg_sb, *swapped = _sb_bwd(p, bl, seq, sb_cols, n_sb, dy, n_sb, sb_o, sb_tot,
                                             swaps=tuple(early.values()))
    partials = [_pair_sum(full, got, core, "pairsum_" + n) for (n, full), got in zip(early.items(), swapped)]
    dys, dr1, dkx1, dv1, dg_rwkv, d_gn_g, d_gn_b, d_r_k = _rwkv_post_bwd(
        ys, r, kx, v, g_rwkv, 0, e_gn_g, e_gn_b, e_r_k, e_ind, e_ind_t, dy, 0)
    dkk_s, dw_s, db_s, dkx_s, dr_s, dv_s, *landed = _scan_bwd(*sk, sv, states, sa_all, _to_scan_v(dys, bl, seq),
                                                              scatters=tuple(partials))
    early_sums = {n: _sum_slabs(parts, "sum_" + n) for n, parts in zip(early, landed)}
    dkk, dw, db, dkx2, dr2 = [_from_scan_k(a, bl, seq) for a in (dkk_s, dw_s, db_s, dkx_s, dr_s)]
    dv2 = _from_scan_v(dv_s, bl, seq)
    dp_rwkv, d_mu, d_wd, d_w0, d_wa, d_a0, d_k_k, d_k_a = _rwkv_prep_bwd(
        p, seq, (width, lora), e_shift_mu, wd, e_w0, wa, e_a0, e_k_k, e_k_a, e_ind, e_ind_t,
        dr1 + dr2, dw, dkx1 + dkx2, dv1 + dv2, dkk, db)
    dp = jnp.concatenate([dp_rwkv, dg_rwkv, dq, dk, dv_sb, dg_sb], axis=1)
    last = by_owner(_matmul(dp, h0, "tn", BF16, "e_in_bwd_w", tm=1280, tn=2048))
    last = _pair_sum(last, _pair_swap(last, "swap_e_w_in"), core, "pairsum_e_w_in")
    dh0, last = _matmul(dp, e_win_t, "nn", F32, "e_in_bwd_x", tm=2048, tk=1280, scatters=(last,))
    grad_x, d_g0 = _rms_bwd(x2d, norm_g[0:1], dh0, dx1, "rms0_bwd")

    grads = {
        "e_w_in": _sum_slabs(last, "sum_e_w_in").T[None],
        "e_w_out": early_sums["e_w_out"][None],
        "o_w_in": early_sums["o_w_in"].T[None],
        "o_w_out": early_sums["o_w_out"][None],
    }
    small_full = {
        "norm_g": jnp.concatenate([d_g0, d_g1], axis=0), "final_norm_g": d_final_g.reshape(-1),
        "e_shift_mu": d_mu, "e_w_decay_up": d_wd[None], "e_w0": d_w0, "e_a_up": d_wa[None], "e_a0": d_a0,
        "e_k_k": d_k_k, "e_k_a": d_k_a, "e_r_k": d_r_k, "e_gn_g": d_gn_g, "e_gn_b": d_gn_b,
        "o_ln_g": d_ln_g, "o_ln_b": d_ln_b, "o_w_s": d_w_s[None], "o_b_s": d_b_s3[:, :, 0][None],
    }
    small = [n for n in names if n not in big]
    parts = _all_gather(_pack([small_full[n] for n in small]), "gather_small_grads")
    totals = _unpack(_sum_slabs(parts, "sum_small_grads"), [small_full[n].shape for n in small])
    for n, g in zip(small, totals):
        if n in sharded_small:
            size = weights[n].shape[-1]
            g = lax.dynamic_slice_in_dim(g, me * size, size, axis=g.ndim - 1)
        grads[n] = g.reshape(weights[n].shape)

    delta, new_m, new_v = {}, {}, {}
    for n in big:
        shp = weights[n].shape
        flat = lambda a: a.reshape(shp[-2], shp[-1])
        dl, mn, vn = _adamw(flat(weights[n]), flat(grads[n]), flat(mom1[n]), flat(mom2[n]), "adamw_" + n)
        delta[n], new_m[n], new_v[n] = dl.reshape(shp), mn.reshape(shp), vn.reshape(shp)
    packed = [_pack([src[n] for n in small]) for src in (weights, grads, mom1, mom2)]
    outs = _adamw(*packed, "adamw_small")
    shapes = [weights[n].shape for n in small]
    for dst, arr in zip((delta, new_m, new_v), outs):
        for n, a in zip(small, _unpack(arr, shapes)):
            dst[n] = a

    loss = lax.psum(loss_part[0, 0], ("x", "y", "c"))
    return (loss, grad_x.reshape(bl, seq, d), *[grads[n] for n in names], *[delta[n] for n in names],
            *[new_m[n] for n in names], *[new_v[n] for n in names])
```

```python
import functools

import jax
import jax.numpy as jnp
from jax import lax
from jax.experimental import pallas as pl
from jax.experimental.pallas import tpu as pltpu

F32 = jnp.float32
BF16 = jnp.bfloat16

N_DEV = 8
RWKV_HEAD = 64
SB_HEAD = 128
SB_BLOCK = 128
SGU_CHUNK = 128
SGU_GROUPS = 16
RMS_EPS = 1e-6
GN_EPS = 64e-5
LN_EPS = 1e-5
L2_EPS = 1e-12
ADAM_LR = 0.001
ADAM_B1 = 0.9
ADAM_B2 = 0.999
ADAM_EPS = 1e-08
ADAM_WD = 0.01
ADAM_STEP = 10

VMEM_LIMIT_V7X = 56 * 1024 * 1024
LANES = 128
SCAN_STEPS_PER_BLOCK = 16


def _params(*sem):
    return pltpu.CompilerParams(dimension_semantics=sem, vmem_limit_bytes=VMEM_LIMIT_V7X)


def _tile(n, target, mult):
    best = None
    d = mult
    while d <= min(n, target):
        if n % d == 0:
            best = d
        d += mult
    return n if best is None else best


def _remote(src, dst, send_sems, recv_sems, k, dev):
    return pltpu.make_async_remote_copy(src_ref=src, dst_ref=dst, send_sem=send_sems.at[k], recv_sem=recv_sems.at[k],
                                        device_id=dev, device_id_type=pl.DeviceIdType.MESH)


_HBM = pl.BlockSpec(memory_space=pl.ANY)


GATHER_COPIES = 7


def _gather_phases(src_ref, out_ref, send_sems, recv_sems, local_sems, n):
    x, y, c = lax.axis_index("x"), lax.axis_index("y"), lax.axis_index("c")
    me, sibling = (x, y, c), (x, y, 1 - c)
    chips = [(1 - x, y), (x, 1 - y), (1 - x, 1 - y)]

    def slot(px, py, pc):
        return out_ref.at[4 * px + 2 * py + pc]

    def copy(k, block, to, own=False):
        return _remote(src_ref if own else slot(*block), slot(*block), send_sems, recv_sems,
                       GATHER_COPIES * n + k, to)

    mine = pltpu.make_async_copy(src_ref, slot(*me), local_sems.at[n])
    first = [copy(0, me, sibling, True)] + [copy(1 + j, me, (*chip, c), True) for j, chip in enumerate(chips)]
    passed = [copy(4 + j, (*chip, c), sibling) for j, chip in enumerate(chips)]

    def start():
        mine.start()
        for cp in first:
            cp.start()

    def relay():
        for j, chip in enumerate(chips):
            copy(1 + j, (*chip, c), me).wait_recv()
            passed[j].start()

    def finish():
        copy(0, sibling, me).wait_recv()
        for j, chip in enumerate(chips):
            copy(4 + j, (*chip, 1 - c), me).wait_recv()
        for cp in first + passed:
            cp.wait_send()
        mine.wait()

    return start, relay, finish


def _gather_scratch(n):
    return [pltpu.SemaphoreType.DMA((GATHER_COPIES * n,)), pltpu.SemaphoreType.DMA((GATHER_COPIES * n,)),
            pltpu.SemaphoreType.DMA((n,))]


def _gathered_shape(src):
    return jax.ShapeDtypeStruct((N_DEV,) + tuple(src.shape), src.dtype)


def _all_gather(src, name):
    def body(src_ref, out_ref, send_sems, recv_sems, local_sems):
        for phase in _gather_phases(src_ref, out_ref, send_sems, recv_sems, local_sems, 0):
            phase()

    return pl.pallas_call(
        body, name=name, out_shape=_gathered_shape(src), in_specs=[_HBM], out_specs=_HBM,
        scratch_shapes=_gather_scratch(1),
    )(src)


N_CHIPS = N_DEV // 2
SCATTER_COPIES = N_CHIPS - 1


def _swap_phases(src_ref, out_ref, send_sems, recv_sems, n):
    x, y, c = lax.axis_index("x"), lax.axis_index("y"), lax.axis_index("c")
    copies = [_remote(src_ref.at[q, 1 - c], out_ref.at[q], send_sems, recv_sems, N_CHIPS * n + q, (x, y, 1 - c))
              for q in range(N_CHIPS)]

    def start():
        for cp in copies:
            cp.start()

    def finish():
        for cp in copies:
            cp.wait_recv()
        for cp in copies:
            cp.wait_send()

    return start, finish


def _swapped_shape(full):
    return jax.ShapeDtypeStruct((N_CHIPS,) + tuple(full.shape[2:]), full.dtype)


def _swap_scratch(n):
    return [pltpu.SemaphoreType.DMA((N_CHIPS * n,)), pltpu.SemaphoreType.DMA((N_CHIPS * n,))]


def _pair_swap(full, name):
    def body(src_ref, out_ref, send_sems, recv_sems):
        for phase in _swap_phases(src_ref, out_ref, send_sems, recv_sems, 0):
            phase()

    return pl.pallas_call(body, name=name, out_shape=_swapped_shape(full), in_specs=[_HBM], out_specs=_HBM,
                          scratch_shapes=_swap_scratch(1))(full)


def _pair_sum(full, got, core, name):
    n_chips, _, rows, cols = full.shape
    tr = _tile(rows, max(16, (1 << 19) // cols), 16)

    def body(core_ref, a_ref, b_ref, o_ref):
        o_ref[...] = (a_ref[...].astype(F32) + b_ref[...].astype(F32)).astype(BF16)

    return pl.pallas_call(
        body, name=name,
        grid_spec=pltpu.PrefetchScalarGridSpec(
            num_scalar_prefetch=1, grid=(n_chips, rows // tr),
            in_specs=[pl.BlockSpec((None, None, tr, cols), lambda q, i, s: (q, s[0], i, 0)),
                      pl.BlockSpec((None, tr, cols), lambda q, i, s: (q, i, 0))],
            out_specs=pl.BlockSpec((None, tr, cols), lambda q, i, s: (q, i, 0))),
        out_shape=jax.ShapeDtypeStruct((n_chips, rows, cols), BF16),
        compiler_params=_params("parallel", "parallel"),
    )(core, full, got)


def _scatter_phases(src_ref, out_ref, send_sems, recv_sems, local_sems, n):
    x, y, c = lax.axis_index("x"), lax.axis_index("y"), lax.axis_index("c")
    here = 2 * x + y
    chips = [(1 - x, y), (x, 1 - y), (1 - x, 1 - y)]
    local = pltpu.make_async_copy(src_ref.at[here], out_ref.at[here], local_sems.at[n])
    sends = [_remote(src_ref.at[2 * px + py], out_ref.at[here], send_sems, recv_sems, SCATTER_COPIES * n + j,
                     (px, py, c)) for j, (px, py) in enumerate(chips)]
    recvs = [_remote(src_ref.at[2 * px + py], out_ref.at[2 * px + py], send_sems, recv_sems,
                     SCATTER_COPIES * n + j, (px, py, c)) for j, (px, py) in enumerate(chips)]

    def start():
        local.start()
        for cp in sends:
            cp.start()

    def finish():
        for cp in recvs:
            cp.wait_recv()
        for cp in sends:
            cp.wait_send()
        local.wait()

    return start, finish


def _scatter_scratch(n):
    return [pltpu.SemaphoreType.DMA((SCATTER_COPIES * n,)), pltpu.SemaphoreType.DMA((SCATTER_COPIES * n,)),
            pltpu.SemaphoreType.DMA((n,))]


def _chip_scatter(parts, name):
    def body(src_ref, out_ref, send_sems, recv_sems, local_sems):
        for phase in _scatter_phases(src_ref, out_ref, send_sems, recv_sems, local_sems, 0):
            phase()

    return pl.pallas_call(body, name=name, out_shape=jax.ShapeDtypeStruct(parts.shape, parts.dtype),
                          in_specs=[_HBM], out_specs=_HBM, scratch_shapes=_scatter_scratch(1))(parts)


def _matmul(a, b, mode, out_dtype, name, res=None, tm=1024, tn=1024, tk=1024, scatters=()):
    if mode == "nn":
        (m, k), (k2, n) = a.shape, b.shape
    elif mode == "nt":
        (m, k), (n, k2) = a.shape, b.shape
    else:
        (k, m), (k2, n) = a.shape, b.shape
    assert k == k2, (a.shape, b.shape, mode)
    tm, tn, tk = _tile(m, tm, 128), _tile(n, tn, 128), _tile(k, tk, 128)
    nk = k // tk
    if mode == "nn":
        a_spec = pl.BlockSpec((tm, tk), lambda i, j, kk: (i, kk))
        b_spec = pl.BlockSpec((tk, tn), lambda i, j, kk: (kk, j))
        dims = (((1,), (0,)), ((), ()))
    elif mode == "nt":
        a_spec = pl.BlockSpec((tm, tk), lambda i, j, kk: (i, kk))
        b_spec = pl.BlockSpec((tn, tk), lambda i, j, kk: (j, kk))
        dims = (((1,), (1,)), ((), ()))
    else:
        a_spec = pl.BlockSpec((tk, tm), lambda i, j, kk: (kk, i))
        b_spec = pl.BlockSpec((tk, tn), lambda i, j, kk: (kk, j))
        dims = (((0,), (0,)), ((), ()))
    o_spec = pl.BlockSpec((tm, tn), lambda i, j, kk: (i, j))
    has_res = res is not None
    n_in = 3 if has_res else 2
    nsc = len(scatters)
    grid = (m // tm, n // tn, nk)

    def body(*refs):
        a_ref, b_ref = refs[:2]
        r_ref = refs[2] if has_res else None
        o_ref = refs[n_in + nsc]
        acc_ref = refs[n_in + 2 * nsc + 1]
        kk = pl.program_id(2)

        if nsc:
            send_sems, recv_sems, local_sems = refs[n_in + 2 * nsc + 2:]
            phases = [_scatter_phases(refs[n_in + s], refs[n_in + nsc + 1 + s], send_sems, recv_sems, local_sems, s)
                      for s in range(nsc)]
            for k, at in enumerate(((0, 0, 0), tuple(g - 1 for g in grid))):
                @pl.when((pl.program_id(0) == at[0]) & (pl.program_id(1) == at[1]) & (kk == at[2]))
                def _(k=k):
                    for ph in phases:
                        ph[k]()

        def product():
            return lax.dot_general(a_ref[...].astype(BF16), b_ref[...].astype(BF16), dims,
                                   preferred_element_type=F32)

        @pl.when(kk == 0)
        def _():
            acc_ref[...] = product()

        @pl.when(kk > 0)
        def _():
            acc_ref[...] += product()

        @pl.when(kk == nk - 1)
        def _():
            out = acc_ref[...]
            if has_res:
                out = out + r_ref[...]
            o_ref[...] = out.astype(out_dtype)

    ins = [a, b] + ([res] if has_res else [])
    specs = [a_spec, b_spec] + ([o_spec] if has_res else [])
    out = jax.ShapeDtypeStruct((m, n), out_dtype)
    if not nsc:
        return pl.pallas_call(
            body, name=name, grid=grid, out_shape=out, in_specs=specs, out_specs=o_spec,
            scratch_shapes=[pltpu.VMEM((tm, tn), F32)],
            compiler_params=_params("parallel", "parallel", "arbitrary"),
        )(*ins)
    return pl.pallas_call(
        body, name=name, grid=grid,
        out_shape=(out, *[jax.ShapeDtypeStruct(s.shape, s.dtype) for s in scatters]),
        in_specs=specs + [_HBM] * nsc, out_specs=(o_spec, *[_HBM] * nsc),
        scratch_shapes=[pltpu.VMEM((tm, tn), F32)] + _scatter_scratch(nsc),
        compiler_params=_params("arbitrary", "arbitrary", "arbitrary"),
    )(*ins, *scatters)


def _rms_fwd(x, g, name):
    t, d = x.shape
    tm = _tile(t, 256, 8)

    def body(x_ref, g_ref, h_ref):
        xv = x_ref[...]
        rstd = lax.rsqrt(jnp.mean(xv * xv, axis=-1, keepdims=True) + RMS_EPS)
        h_ref[...] = (xv * rstd * g_ref[...]).astype(BF16)

    return pl.pallas_call(
        body, name=name, grid=(t // tm,),
        out_shape=jax.ShapeDtypeStruct((t, d), BF16),
        in_specs=[pl.BlockSpec((tm, d), lambda i: (i, 0)), pl.BlockSpec((1, d), lambda i: (0, 0))],
        out_specs=pl.BlockSpec((tm, d), lambda i: (i, 0)),
        compiler_params=_params("parallel"),
    )(x, g)


def _rms_bwd(x, g, dh, dres, name):
    t, d = x.shape
    tm = _tile(t, 256, 8)

    def body(x_ref, g_ref, dh_ref, dres_ref, dx_ref, dg_ref):
        @pl.when(pl.program_id(0) == 0)
        def _():
            dg_ref[...] = jnp.zeros_like(dg_ref)

        xv = x_ref[...]
        rstd = lax.rsqrt(jnp.mean(xv * xv, axis=-1, keepdims=True) + RMS_EPS)
        xhat = xv * rstd
        dh_v = dh_ref[...]
        dg_ref[...] += jnp.sum(dh_v * xhat, axis=0, keepdims=True)
        dxh = dh_v * g_ref[...]
        dx_ref[...] = dres_ref[...] + rstd * (dxh - xhat * jnp.mean(dxh * xhat, axis=-1, keepdims=True))

    row = pl.BlockSpec((tm, d), lambda i: (i, 0))
    vec = pl.BlockSpec((1, d), lambda i: (0, 0))
    return pl.pallas_call(
        body, name=name, grid=(t // tm,),
        out_shape=(jax.ShapeDtypeStruct((t, d), F32), jax.ShapeDtypeStruct((1, d), F32)),
        in_specs=[row, vec, row, row], out_specs=(row, vec),
        compiler_params=_params("arbitrary"),
    )(x, g, dh, dres)


def _final_loss(x, g, target, name):
    t, d = x.shape
    tm = _tile(t, 256, 8)

    def body(x_ref, g_ref, t_ref, dx_ref, dg_ref, loss_ref):
        @pl.when(pl.program_id(0) == 0)
        def _():
            dg_ref[...] = jnp.zeros_like(dg_ref)
            loss_ref[...] = jnp.zeros_like(loss_ref)

        xv = x_ref[...]
        rstd = lax.rsqrt(jnp.mean(xv * xv, axis=-1, keepdims=True) + RMS_EPS)
        xhat = xv * rstd
        gv = g_ref[...]
        err = xhat * gv - t_ref[...]
        loss_ref[...] += 0.5 * jnp.sum(jnp.mean(err * err, axis=-1, keepdims=True), axis=0, keepdims=True)
        dout = err * (1.0 / d)
        dg_ref[...] += jnp.sum(dout * xhat, axis=0, keepdims=True)
        dxh = dout * gv
        dx_ref[...] = rstd * (dxh - xhat * jnp.mean(dxh * xhat, axis=-1, keepdims=True))

    row = pl.BlockSpec((tm, d), lambda i: (i, 0))
    vec = pl.BlockSpec((1, d), lambda i: (0, 0))
    return pl.pallas_call(
        body, name=name, grid=(t // tm,),
        out_shape=(jax.ShapeDtypeStruct((t, d), F32), jax.ShapeDtypeStruct((1, d), F32),
                   jax.ShapeDtypeStruct((1, 1), F32)),
        in_specs=[row, vec, row], out_specs=(row, vec, pl.BlockSpec((1, 1), lambda i: (0, 0))),
        compiler_params=_params("arbitrary"),
    )(x, g, target)


def _dot32(a, b):
    return jnp.dot(a, b, precision=lax.Precision.HIGH, preferred_element_type=F32)


def _head_sum(v, e, et):
    return _dot32(_dot32(v, e), et)


def _log_sigmoid(z):
    return jnp.minimum(z, 0.0) - jnp.log1p(jnp.exp(-jnp.abs(z)))


def _silu(g):
    return g * jax.nn.sigmoid(g)


def _prep_math(k, wlo, alo, wd, w0, wa, a0, k_k, k_a, e, et):
    wl = w0 + _dot32(jnp.tanh(wlo), wd)
    w_log = _log_sigmoid(wl) - 0.5
    w = jnp.exp(-jnp.exp(w_log))
    a = jax.nn.sigmoid(a0 + _dot32(alo, wa))
    kk0 = k * k_k
    kk = kk0 * lax.rsqrt(jnp.maximum(_head_sum(kk0 * kk0, e, et), L2_EPS * L2_EPS))
    kx = k * (1.0 + (a - 1.0) * k_a)
    return w, kx, kk, kk * a


def _post_math(ys, r, kx, v, g, gn_g, gn_b, r_k, e, et):
    inv = 1.0 / RWKV_HEAD
    mu = _head_sum(ys, e, et) * inv
    dlt = ys - mu
    var = _head_sum(dlt * dlt, e, et) * inv
    y = dlt * lax.rsqrt(var + GN_EPS) * gn_g + gn_b
    bonus = _head_sum(r * kx * r_k, e, et) * v
    return (y + bonus) * _silu(g)


def _shifted(p, prev_row, first):
    rows = lax.broadcasted_iota(jnp.int32, p.shape, 0)
    prev = jnp.where(first, 0.0, prev_row)
    return jnp.where(rows == 0, prev, pltpu.roll(p, 1, 0))


def _head_indicator(width):
    ch = lax.broadcasted_iota(jnp.int32, (width, width // RWKV_HEAD), 0) // RWKV_HEAD
    hd = lax.broadcasted_iota(jnp.int32, (width, width // RWKV_HEAD), 1)
    e = (ch == hd).astype(F32)
    return e, e.T


def _rwkv_prep(p, seq, dims, mu, wd, w0, wa, a0, k_k, k_a, e, et):
    t = p.shape[0]
    width, lora = dims
    cols = 3 * width + 2 * lora
    tm = 128
    per_seq = seq // tm

    def body(p_ref, prev_ref, mu_ref, wd_ref, w0_ref, wa_ref, a0_ref, kk_ref, ka_ref, e_ref, et_ref,
             r_out, w_out, kx_out, v_out, kkn_out, b_out):
        i = pl.program_id(0)
        pv = p_ref[...]
        psh = _shifted(pv, prev_ref[7:8, :], i % per_seq == 0)
        ps = pv + mu_ref[...] * (psh - pv)
        r, k, v = ps[:, :width], ps[:, width:2 * width], ps[:, 2 * width:3 * width]
        wlo, alo = ps[:, 3 * width:3 * width + lora], ps[:, 3 * width + lora:]
        w, kx, kk, b = _prep_math(k, wlo, alo, wd_ref[...], w0_ref[...], wa_ref[...], a0_ref[...],
                                  kk_ref[...], ka_ref[...], e_ref[...], et_ref[...])
        r_out[...] = r
        w_out[...] = w
        kx_out[...] = kx
        v_out[...] = v
        kkn_out[...] = kk
        b_out[...] = b

    full = lambda a: pl.BlockSpec(a.shape, lambda i: (0,) * a.ndim)
    out = pl.BlockSpec((tm, width), lambda i: (i, 0))
    return pl.pallas_call(
        body, name="rwkv_prep", grid=(t // tm,),
        out_shape=tuple(jax.ShapeDtypeStruct((t, width), F32) for _ in range(6)),
        in_specs=[pl.BlockSpec((tm, cols), lambda i: (i, 0)),
                  pl.BlockSpec((8, cols), lambda i: (jnp.maximum(i * (tm // 8) - 1, 0), 0)),
                  full(mu), full(wd), full(w0), full(wa), full(a0), full(k_k), full(k_a), full(e), full(et)],
        out_specs=tuple(out for _ in range(6)),
        compiler_params=_params("parallel"),
    )(p, p, mu, wd, w0, wa, a0, k_k, k_a, e, et)


def _rwkv_prep_bwd(p, seq, dims, mu, wd, w0, wa, a0, k_k, k_a, e, et, dr, dw, dkx, dv, dkk, db, more):
    t = p.shape[0]
    width, lora = dims
    cols = 3 * width + 2 * lora
    tm = 128
    n_tiles = t // tm
    per_seq = seq // tm

    def body(p_ref, prev_ref, mu_ref, wd_ref, w0_ref, wa_ref, a0_ref, kk_ref, ka_ref, e_ref, et_ref,
             dr_ref, dw_ref, dkx_ref, dv_ref, dkk_ref, db_ref, dr2_ref, dkx2_ref, dv2_ref,
             dp_out, dmu_out, dwd_out, dw0_out, dwa_out, da0_out, dkk_out, dka_out, carry):
        step = pl.program_id(0)
        i = n_tiles - 1 - step

        @pl.when(step == 0)
        def _():
            for ref in (dmu_out, dwd_out, dw0_out, dwa_out, da0_out, dkk_out, dka_out, carry):
                ref[...] = jnp.zeros_like(ref)

        pv = p_ref[...]
        first = i % per_seq == 0
        psh = _shifted(pv, prev_ref[7:8, :], first)
        muv = mu_ref[...]
        ps = pv + muv * (psh - pv)
        k = ps[:, width:2 * width]
        wlo, alo = ps[:, 3 * width:3 * width + lora], ps[:, 3 * width + lora:]
        ev, etv = e_ref[...], et_ref[...]
        _, vjp = jax.vjp(lambda *a: _prep_math(*a, ev, etv), k, wlo, alo, wd_ref[...], w0_ref[...],
                         wa_ref[...], a0_ref[...], kk_ref[...], ka_ref[...])
        dk, dwlo, dalo, dwd, dw0, dwa, da0, dk_k, dk_a = vjp(
            (dw_ref[...], dkx_ref[...] + dkx2_ref[...], dkk_ref[...], db_ref[...]))
        dps = jnp.concatenate([dr_ref[...] + dr2_ref[...], dk, dv_ref[...] + dv2_ref[...], dwlo, dalo], axis=1)
        dmu_out[...] += jnp.sum(dps * (psh - pv), axis=0, keepdims=True)
        dwd_out[...] += dwd
        dw0_out[...] += dw0
        dwa_out[...] += dwa
        da0_out[...] += da0
        dkk_out[...] += dk_k
        dka_out[...] += dk_a
        dsh = dps * muv
        rows = lax.broadcasted_iota(jnp.int32, dsh.shape, 0)
        nxt = jnp.where(rows == tm - 1, carry[...], pltpu.roll(dsh, tm - 1, 0))
        dp_out[...] = (dps * (1.0 - muv) + nxt).astype(BF16)
        carry[...] = jnp.where(first, 0.0, dsh[0:1, :])

    full = lambda a: pl.BlockSpec(a.shape, lambda s: (0,) * a.ndim)
    tok = pl.BlockSpec((tm, width), lambda s: (n_tiles - 1 - s, 0))
    vec = lambda n: jax.ShapeDtypeStruct((1, n), F32)
    outs = (jax.ShapeDtypeStruct((t, cols), BF16), vec(cols), jax.ShapeDtypeStruct(wd.shape, F32), vec(width),
            jax.ShapeDtypeStruct(wa.shape, F32), vec(width), vec(width), vec(width))
    return pl.pallas_call(
        body, name="rwkv_prep_bwd", grid=(n_tiles,),
        out_shape=outs,
        in_specs=[pl.BlockSpec((tm, cols), lambda s: (n_tiles - 1 - s, 0)),
                  pl.BlockSpec((8, cols), lambda s: (jnp.maximum((n_tiles - 1 - s) * (tm // 8) - 1, 0), 0)),
                  full(mu), full(wd), full(w0), full(wa), full(a0), full(k_k), full(k_a), full(e), full(et),
                  tok, tok, tok, tok, tok, tok, tok, tok, tok],
        out_specs=(pl.BlockSpec((tm, cols), lambda s: (n_tiles - 1 - s, 0)),) + tuple(
            pl.BlockSpec(o.shape, lambda s: (0, 0)) for o in outs[1:]),
        scratch_shapes=[pltpu.VMEM((1, cols), F32)],
        compiler_params=_params("arbitrary"),
    )(p, p, mu, wd, w0, wa, a0, k_k, k_a, e, et, dr, dw, dkx, dv, dkk, db, *more)


def _rwkv_post(ys, r, kx, v, p, g_col, gn_g, gn_b, r_k, e, et):
    t, width = ys.shape
    tm = 256

    def body(ys_ref, r_ref, kx_ref, v_ref, g_ref, gg_ref, gb_ref, rk_ref, e_ref, et_ref, out_ref):
        out_ref[...] = _post_math(ys_ref[...], r_ref[...], kx_ref[...], v_ref[...], g_ref[...], gg_ref[...],
                                  gb_ref[...], rk_ref[...], e_ref[...], et_ref[...]).astype(BF16)

    tok = pl.BlockSpec((tm, width), lambda i: (i, 0))
    full = lambda a: pl.BlockSpec(a.shape, lambda i: (0,) * a.ndim)
    return pl.pallas_call(
        body, name="rwkv_post", grid=(t // tm,),
        out_shape=jax.ShapeDtypeStruct((t, width), BF16),
        in_specs=[tok, tok, tok, tok, pl.BlockSpec((tm, width), lambda i: (i, g_col)),
                  full(gn_g), full(gn_b), full(r_k), full(e), full(et)],
        out_specs=tok,
        compiler_params=_params("parallel"),
    )(ys, r, kx, v, p, gn_g, gn_b, r_k, e, et)


def _rwkv_post_bwd(ys, r, kx, v, p, g_col, gn_g, gn_b, r_k, e, et, dy, dy_col):
    t, width = ys.shape
    tm = 128

    def body(ys_ref, r_ref, kx_ref, v_ref, g_ref, gg_ref, gb_ref, rk_ref, e_ref, et_ref, dy_ref,
             dys_out, dr_out, dkx_out, dv_out, dg_out, dgg_out, dgb_out, drk_out):
        @pl.when(pl.program_id(0) == 0)
        def _():
            for ref in (dgg_out, dgb_out, drk_out):
                ref[...] = jnp.zeros_like(ref)

        ev, etv = e_ref[...], et_ref[...]
        _, vjp = jax.vjp(lambda *a: _post_math(*a, ev, etv), ys_ref[...], r_ref[...], kx_ref[...], v_ref[...],
                         g_ref[...], gg_ref[...], gb_ref[...], rk_ref[...])
        dys, dr, dkx, dv, dg, dgg, dgb, drk = vjp(dy_ref[...])
        dys_out[...] = dys
        dr_out[...] = dr
        dkx_out[...] = dkx
        dv_out[...] = dv
        dg_out[...] = dg.astype(BF16)
        dgg_out[...] += dgg
        dgb_out[...] += dgb
        drk_out[...] += drk

    tok = pl.BlockSpec((tm, width), lambda i: (i, 0))
    full = lambda a: pl.BlockSpec(a.shape, lambda i: (0,) * a.ndim)
    big = jax.ShapeDtypeStruct((t, width), F32)
    vec = jax.ShapeDtypeStruct((1, width), F32)
    vspec = pl.BlockSpec((1, width), lambda i: (0, 0))
    return pl.pallas_call(
        body, name="rwkv_post_bwd", grid=(t // tm,),
        out_shape=(big, big, big, big, jax.ShapeDtypeStruct((t, width), BF16), vec, vec, vec),
        in_specs=[tok, tok, tok, tok, pl.BlockSpec((tm, width), lambda i: (i, g_col)),
                  full(gn_g), full(gn_b), full(r_k), full(e), full(et),
                  pl.BlockSpec((tm, width), lambda i: (i, dy_col))],
        out_specs=(tok, tok, tok, tok, tok, vspec, vspec, vspec),
        compiler_params=_params("arbitrary"),
    )(ys, r, kx, v, p, gn_g, gn_b, r_k, e, et, dy)


def _to_scan_k(a, bl, seq):
    h = a.shape[1] // RWKV_HEAD
    twice = jnp.broadcast_to(a.reshape(1, bl, seq, h, RWKV_HEAD), (2, bl, seq, h, RWKV_HEAD))
    return twice.transpose(2, 4, 0, 1, 3).reshape(seq, RWKV_HEAD, 2 * bl * h)


def _to_scan_v(a, bl, seq):
    h = a.shape[1] // RWKV_HEAD
    half = RWKV_HEAD // 2
    return a.reshape(bl, seq, h, 2, half).transpose(1, 4, 3, 0, 2).reshape(seq, half, 2 * bl * h)


def _from_scan_k(a, bl, seq):
    h = a.shape[2] // (2 * bl)
    a = a[:, :, :bl * h].reshape(seq, RWKV_HEAD, bl, h).transpose(2, 0, 3, 1)
    return a.reshape(bl * seq, h * RWKV_HEAD)


def _from_scan_v(a, bl, seq):
    half = RWKV_HEAD // 2
    h = a.shape[2] // (2 * bl)
    a = a.reshape(seq, half, 2, bl, h).transpose(3, 0, 4, 2, 1)
    return a.reshape(bl * seq, h * RWKV_HEAD)


def _scan_fwd(kk, w, b, kx, r, v, gathers=()):
    seq, nk, lanes = kk.shape
    nv = v.shape[1]
    tt = SCAN_STEPS_PER_BLOCK
    nblk = seq // tt
    ng = len(gathers)

    def body(*refs):
        kk_ref, w_ref, b_ref, kx_ref, r_ref, v_ref = refs[:6]
        g_src = refs[6:6 + ng]
        y_ref, st_ref, sa_ref = refs[6 + ng:9 + ng]
        g_out = refs[9 + ng:9 + 2 * ng]
        s_ref = refs[9 + 2 * ng]
        pid = pl.program_id(0)

        @pl.when(pid == 0)
        def _():
            s_ref[...] = jnp.zeros_like(s_ref)

        if ng:
            send_sems, recv_sems, local_sems = refs[10 + 2 * ng:]
            phases = [_gather_phases(g_src[n], g_out[n], send_sems, recv_sems, local_sems, n) for n in range(ng)]
            for k, at in enumerate((0, (3 * nblk) // 4, nblk - 1)):
                @pl.when(pid == at)
                def _(k=k):
                    for ph in phases:
                        ph[k]()

        def step(i, carry):
            kkv, wv, bv, kxv, rv = kk_ref[i], w_ref[i], b_ref[i], kx_ref[i], r_ref[i]
            for j in range(nv):
                s_old = s_ref[j]
                sa = -jnp.sum(s_old * kkv, axis=0, keepdims=True)
                s_new = s_old * wv + sa * bv + v_ref[i, j:j + 1, :] * kxv
                s_ref[j] = s_new
                st_ref[i, j] = s_new
                sa_ref[i, j:j + 1, :] = sa
                y_ref[i, j:j + 1, :] = jnp.sum(s_new * rv, axis=0, keepdims=True)
            return carry

        lax.fori_loop(0, tt, step, 0)

    krow = pl.BlockSpec((tt, nk, lanes), lambda i: (i, 0, 0))
    vrow = pl.BlockSpec((tt, nv, lanes), lambda i: (i, 0, 0))
    vshape = jax.ShapeDtypeStruct((seq, nv, lanes), F32)
    return pl.pallas_call(
        body, name="rwkv_scan_fwd", grid=(nblk,),
        out_shape=(vshape, jax.ShapeDtypeStruct((seq, nv, nk, lanes), F32), vshape,
                   *[_gathered_shape(g) for g in gathers]),
        in_specs=[krow, krow, krow, krow, krow, vrow] + [_HBM] * ng,
        out_specs=(vrow, pl.BlockSpec((tt, nv, nk, lanes), lambda i: (i, 0, 0, 0)), vrow, *[_HBM] * ng),
        scratch_shapes=[pltpu.VMEM((nv, nk, lanes), F32)] + (_gather_scratch(ng) if ng else []),
        compiler_params=_params("arbitrary"),
    )(kk, w, b, kx, r, v, *gathers)


def _scan_bwd(kk, w, b, kx, r, v, states, sa_all, dy, scatters=()):
    seq, nk, lanes = kk.shape
    nv = v.shape[1]
    tt = SCAN_STEPS_PER_BLOCK
    nblk = seq // tt
    nsc = len(scatters)

    def both_halves(a):
        return a + pltpu.roll(a, lanes // 2, 1)

    def body(*refs):
        kk_ref, w_ref, b_ref, kx_ref, r_ref, v_ref, st_ref, before_ref, sa_ref, dy_ref = refs[:10]
        dkk_ref, dw_ref, db_ref, dkx_ref, dr_ref, dv_ref = refs[10 + nsc:16 + nsc]
        g_ref = refs[16 + 2 * nsc]

        if nsc:
            send_sems, recv_sems, local_sems = refs[17 + 2 * nsc:]
            phases = [_scatter_phases(refs[10 + n], refs[16 + nsc + n], send_sems, recv_sems, local_sems, n)
                      for n in range(nsc)]
            for k, at in enumerate((0, nblk - 1)):
                @pl.when(pl.program_id(0) == at)
                def _(k=k):
                    for ph in phases:
                        ph[k]()

        @pl.when(pl.program_id(0) == 0)
        def _():
            g_ref[...] = jnp.zeros_like(g_ref)

        def one_step(i, state_before):
            kkv, wv, bv, kxv, rv = kk_ref[i], w_ref[i], b_ref[i], kx_ref[i], r_ref[i]
            zero = jnp.zeros((nk, lanes), F32)
            a_r, a_w, a_b, a_kx, a_kk = zero, zero, zero, zero, zero
            for j in range(nv):
                s_old = state_before(j)
                s_new = st_ref[i, j]
                vv = v_ref[i, j:j + 1, :]
                dyv = dy_ref[i, j:j + 1, :]
                sa = sa_ref[i, j:j + 1, :]
                g = g_ref[j] + dyv * rv
                a_r = a_r + s_new * dyv
                a_w = a_w + g * s_old
                dsa = jnp.sum(g * bv, axis=0, keepdims=True)
                a_b = a_b + g * sa
                dv_ref[i, j:j + 1, :] = jnp.sum(g * kxv, axis=0, keepdims=True)
                a_kx = a_kx + g * vv
                a_kk = a_kk + s_old * dsa
                g_ref[j] = g * wv - dsa * kkv
            dr_ref[i] = both_halves(a_r)
            dw_ref[i] = both_halves(a_w)
            db_ref[i] = both_halves(a_b)
            dkx_ref[i] = both_halves(a_kx)
            dkk_ref[i] = -both_halves(a_kk)

        def step(n, carry):
            i = tt - 1 - n
            one_step(i, lambda j: st_ref[i - 1, j])
            return carry

        lax.fori_loop(0, tt - 1, step, 0)
        at_start = pl.program_id(0) == nblk - 1
        one_step(0, lambda j: jnp.where(at_start, 0.0, before_ref[0, j]))

    rev = lambda i: nblk - 1 - i
    krow = pl.BlockSpec((tt, nk, lanes), lambda i: (rev(i), 0, 0))
    vrow = pl.BlockSpec((tt, nv, lanes), lambda i: (rev(i), 0, 0))
    kshape = jax.ShapeDtypeStruct((seq, nk, lanes), F32)
    return pl.pallas_call(
        body, name="rwkv_scan_bwd", grid=(nblk,),
        out_shape=(kshape, kshape, kshape, kshape, kshape, jax.ShapeDtypeStruct((seq, nv, lanes), F32),
                   *[jax.ShapeDtypeStruct(s.shape, s.dtype) for s in scatters]),
        in_specs=[krow, krow, krow, krow, krow, vrow,
                  pl.BlockSpec((tt, nv, nk, lanes), lambda i: (rev(i), 0, 0, 0)),
                  pl.BlockSpec((1, nv, nk, lanes), lambda i: (jnp.maximum(rev(i) * tt - 1, 0), 0, 0, 0)),
                  vrow, vrow] + [_HBM] * nsc,
        out_specs=(krow, krow, krow, krow, krow, vrow, *[_HBM] * nsc),
        scratch_shapes=[pltpu.VMEM((nv, nk, lanes), F32)] + (_scatter_scratch(nsc) if nsc else []),
        compiler_params=_params("arbitrary"),
    )(kk, w, b, kx, r, v, states, states, sa_all, dy, *scatters)


_NT = (((1,), (1,)), ((), ()))
_TN = (((0,), (0,)), ((), ()))
SB_SCALE = 1.0 / (SB_HEAD ** 0.5)


SB_QUERY_ROWS = 256


def _sb_masks(qr):
    blk = SB_BLOCK
    row = lax.broadcasted_iota(jnp.int32, (qr, blk), 0)
    col = lax.broadcasted_iota(jnp.int32, (qr, blk), 1)
    j_in = lax.broadcasted_iota(jnp.int32, (blk, blk), 0)
    s_in = lax.broadcasted_iota(jnp.int32, (blk, blk), 1)
    ones = jnp.ones((blk, blk), BF16)
    fwd = jnp.concatenate([(j_in > s_in).astype(BF16), ones], axis=1)
    bwd = jnp.concatenate([(s_in > j_in).astype(BF16), ones], axis=1)
    return row, col, fwd, bwd


def _split_dots(xs, b):
    his = [a.astype(BF16) for a in xs]
    los = [(a - hi.astype(F32)).astype(BF16) for a, hi in zip(xs, his)]
    tops = [jnp.dot(hi, b, preferred_element_type=F32) for hi in his]
    return [top + jnp.dot(lo, b, preferred_element_type=F32) for top, lo in zip(tops, los)]


SB_HEADS_PER_STEP = 2


def _sb_chains(bl):
    return [(b, slice(hh * SB_HEAD, (hh + 1) * SB_HEAD)) for b in range(bl) for hh in range(SB_HEADS_PER_STEP)]


def _sb_specs(bl, seq, cols, n_heads):
    hp = SB_HEADS_PER_STEP
    qr = min(SB_QUERY_ROWS, seq)
    assert n_heads % hp == 0 and all(c % hp == 0 for c in cols) and seq % qr == 0 and qr % SB_BLOCK == 0
    qspec = lambda col: pl.BlockSpec((bl, qr, hp * SB_HEAD), lambda h, i: (0, i, col // hp + h))
    kspec = lambda col: pl.BlockSpec((bl, seq, hp * SB_HEAD), lambda h, i: (0, 0, col // hp + h))
    return qr, qspec, kspec


def _sb_fwd(p, bl, seq, cols, n_heads, gathers=()):
    t = p.shape[0]
    q_col, k_col, v_col, g_col = cols
    blk = SB_BLOCK
    chains = _sb_chains(bl)
    qr, qspec, kspec = _sb_specs(bl, seq, cols, n_heads)
    nq = seq // qr
    per_tile = qr // blk

    ng = len(gathers)
    n_groups = n_heads // SB_HEADS_PER_STEP
    n_steps = n_groups * nq

    def body(*refs):
        q_ref, k_ref, v_ref, g_ref = refs[:4]
        out_ref, o_ref, tot_ref = refs[4 + ng:7 + ng]
        qi = pl.program_id(1)

        if ng:
            send_sems, recv_sems, local_sems = refs[7 + 2 * ng:]
            phases = [_gather_phases(refs[4 + n], refs[7 + ng + n], send_sems, recv_sems, local_sems, n)
                      for n in range(ng)]
            now = pl.program_id(0) * nq + qi
            for k, at in enumerate((0, (3 * n_steps) // 4, n_steps - 1)):
                @pl.when(now == at)
                def _(k=k):
                    for ph in phases:
                        ph[k]()

        row, col, mix, _ = _sb_masks(qr)
        qbs = [q_ref[b, :, ln].astype(BF16) for b, ln in chains]
        n_blocks = (qi + 1) * per_tile

        def step(n, carry):
            j = n_blocks - 1 - n
            rows = pl.ds(pl.multiple_of(j * blk, blk), blk)
            causal = (j * blk + col) < (qi * qr + row)
            zs = [lax.dot_general(qb, k_ref[b, rows, ln].astype(BF16), _NT, preferred_element_type=F32) * SB_SCALE
                  for (b, ln), qb in zip(chains, qbs)]
            lszs = [_log_sigmoid(z) for z in zs]
            boths = _split_dots([jnp.where(causal, lsz - z, 0.0) for lsz, z in zip(lszs, zs)], mix)
            atts = [jnp.where(causal, jnp.exp(lsz + both[:, :blk] + later), 0.0).astype(BF16)
                    for lsz, both, (_, later) in zip(lszs, boths, carry)]
            outs = [out + jnp.dot(att, v_ref[b, rows, ln].astype(BF16), preferred_element_type=F32)
                    for (b, ln), att, (out, _) in zip(chains, atts, carry)]
            return tuple((out, later + both[:, blk:]) for out, both, (_, later) in zip(outs, boths, carry))

        zero = jnp.zeros((qr, SB_HEAD), F32)
        done = lax.fori_loop(0, n_blocks, step, tuple((zero, zero) for _ in chains))
        for (b, ln), (out, total) in zip(chains, done):
            o_ref[b, :, ln] = out
            tot_ref[b, :, ln] = total
            out_ref[b, :, ln] = (out * _silu(g_ref[b, :, ln])).astype(BF16)

    width = n_heads * SB_HEAD
    p3 = p.reshape(bl, seq, p.shape[1])
    f32 = jax.ShapeDtypeStruct((bl, seq, width), F32)
    outs = pl.pallas_call(
        body, name="sb_attn_fwd", grid=(n_groups, nq),
        out_shape=(jax.ShapeDtypeStruct((bl, seq, width), BF16), f32, f32, *[_gathered_shape(g) for g in gathers]),
        in_specs=[qspec(q_col), kspec(k_col), kspec(v_col), qspec(g_col)] + [_HBM] * ng,
        out_specs=(qspec(0), qspec(0), qspec(0), *[_HBM] * ng),
        scratch_shapes=_gather_scratch(ng) if ng else [],
        compiler_params=_params("arbitrary", "arbitrary"),
    )(p3, p3, p3, p3, *gathers)
    return tuple(a.reshape(t, width) for a in outs[:3]) + tuple(outs[3:])


def _sb_bwd(p, bl, seq, cols, n_heads, dy, dy_col, o, tot, swaps=()):
    t = p.shape[0]
    q_col, k_col, v_col, g_col = cols
    blk = SB_BLOCK
    ns = len(swaps)
    n_groups = n_heads // SB_HEADS_PER_STEP
    chains = _sb_chains(bl)
    qr, qspec, kspec = _sb_specs(bl, seq, cols + (dy_col,), n_heads)
    nq = seq // qr
    per_tile = qr // blk

    def body(*refs):
        q_ref, k_ref, v_ref, g_ref, dy_ref, o_ref, tot_ref = refs[:7]
        dq_out, dk_out, dv_out, dg_out = refs[7 + ns:11 + ns]
        dk_acc, dv_acc = refs[11 + 2 * ns:13 + 2 * ns]
        group, qi = pl.program_id(0), pl.program_id(1)

        if ns:
            send_sems, recv_sems = refs[13 + 2 * ns:]
            phases = [_swap_phases(refs[7 + n], refs[11 + ns + n], send_sems, recv_sems, n) for n in range(ns)]
            for k, at in enumerate(((0, 0), (n_groups - 1, nq - 1))):
                @pl.when((group == at[0]) & (qi == at[1]))
                def _(k=k):
                    for ph in phases:
                        ph[k]()

        @pl.when(qi == 0)
        def _():
            dk_acc[...] = jnp.zeros_like(dk_acc)
            dv_acc[...] = jnp.zeros_like(dv_acc)

        row, col, mix, mix_t = _sb_masks(qr)
        qbs, dobs, totals = [], [], []
        for b, ln in chains:
            gate = g_ref[b, :, ln]
            sg = jax.nn.sigmoid(gate)
            dyv = dy_ref[b, :, ln]
            dg_out[b, :, ln] = (dyv * o_ref[b, :, ln] * (sg * (1.0 + gate * (1.0 - sg)))).astype(BF16)
            dobs.append((dyv * (gate * sg)).astype(BF16))
            qbs.append(q_ref[b, :, ln].astype(BF16))
            totals.append(tot_ref[b, :, ln])

        def step(j, carry):
            rows = pl.ds(pl.multiple_of(j * blk, blk), blk)
            causal = (j * blk + col) < (qi * qr + row)
            kbs = [k_ref[b, rows, ln].astype(BF16) for b, ln in chains]
            zs = [lax.dot_general(qb, kb, _NT, preferred_element_type=F32) * SB_SCALE for qb, kb in zip(qbs, kbs)]
            datts = [lax.dot_general(dob, v_ref[b, rows, ln].astype(BF16), _NT, preferred_element_type=F32)
                     for (b, ln), dob in zip(chains, dobs)]
            lszs = [_log_sigmoid(z) for z in zs]
            boths = _split_dots([jnp.where(causal, lsz - z, 0.0) for lsz, z in zip(lszs, zs)], mix)
            seens = [seen + both[:, blk:] for both, (_, seen, _) in zip(boths, carry)]
            atts = [jnp.where(causal, jnp.exp(lsz + both[:, :blk] + (total - seen)), 0.0)
                    for lsz, both, total, seen in zip(lszs, boths, totals, seens)]
            dls = [att * datt for att, datt in zip(atts, datts)]
            for (b, ln), att, dob in zip(chains, atts, dobs):
                dv_acc[b, rows, ln] += lax.dot_general(att.astype(BF16), dob, _TN, preferred_element_type=F32)
            boths_t = _split_dots(dls, mix_t)
            dkeeps = [jnp.where(causal, both_t[:, :blk] + dl_before, 0.0)
                      for both_t, (_, _, dl_before) in zip(boths_t, carry)]
            dzbs = [(((dl + dkeep) * jax.nn.sigmoid(-z) - dkeep) * SB_SCALE).astype(BF16)
                    for dl, dkeep, z in zip(dls, dkeeps, zs)]
            dqs = [dq + jnp.dot(dzb, kb, preferred_element_type=F32) for dzb, kb, (dq, _, _) in zip(dzbs, kbs, carry)]
            for (b, ln), dzb, qb in zip(chains, dzbs, qbs):
                dk_acc[b, rows, ln] += lax.dot_general(dzb, qb, _TN, preferred_element_type=F32)
            return tuple((dq, seen, dl_before + both_t[:, blk:])
                         for dq, seen, both_t, (_, _, dl_before) in zip(dqs, seens, boths_t, carry))

        zero = jnp.zeros((qr, SB_HEAD), F32)
        done = lax.fori_loop(0, (qi + 1) * per_tile, step, tuple((zero, zero, zero) for _ in chains))
        for (b, ln), (dq, _, _) in zip(chains, done):
            dq_out[b, :, ln] = dq.astype(BF16)

        @pl.when(qi == nq - 1)
        def _():
            dk_out[...] = dk_acc[...].astype(BF16)
            dv_out[...] = dv_acc[...].astype(BF16)

    width = n_heads * SB_HEAD
    shape = jax.ShapeDtypeStruct((bl, seq, width), BF16)
    as3 = lambda a: a.reshape(bl, seq, a.shape[1])
    p3 = as3(p)
    acc = pltpu.VMEM((bl, seq, SB_HEADS_PER_STEP * SB_HEAD), F32)
    outs = pl.pallas_call(
        body, name="sb_attn_bwd", grid=(n_groups, nq),
        out_shape=(shape, shape, shape, shape, *[_swapped_shape(s) for s in swaps]),
        in_specs=[qspec(q_col), kspec(k_col), kspec(v_col), qspec(g_col), qspec(dy_col), qspec(0), qspec(0)]
        + [_HBM] * ns,
        out_specs=(qspec(0), kspec(0), kspec(0), qspec(0), *[_HBM] * ns),
        scratch_shapes=[acc, acc] + (_swap_scratch(ns) if ns else []),
        compiler_params=_params("arbitrary", "arbitrary"),
    )(p3, p3, p3, p3, as3(dy), as3(o), as3(tot), *swaps)
    return tuple(a.reshape(t, width) for a in outs[:4]) + tuple(outs[4:])


def _gelu(x):
    return 0.5 * x * (1.0 + lax.erf(x * (2.0 ** -0.5)))


def _sgu_math(us, vs, gs, ln_g, ln_b, ws, bs):
    width = sum(v.shape[1] for v in vs)
    vg = [_gelu(v) for v in vs]
    mu = sum(jnp.sum(v, axis=1, keepdims=True) for v in vg) * (1.0 / width)
    dl = [v - mu for v in vg]
    var = sum(jnp.sum(d * d, axis=1, keepdims=True) for d in dl) * (1.0 / width)
    rstd = lax.rsqrt(var + LN_EPS)
    n = ws[0].shape[0]
    tri = lax.broadcasted_iota(jnp.int32, (n, n), 0) >= lax.broadcasted_iota(jnp.int32, (n, n), 1)
    outs = []
    for i in range(len(vs)):
        vn = dl[i] * rstd * ln_g[i] + ln_b[i]
        mixed = _dot32(jnp.where(tri, ws[i], 0.0), vn) + bs[i]
        outs.append(_gelu(us[i]) * mixed * _silu(gs[i]))
    return outs


def _sgu_load(p_ref, lng_ref, lnb_ref, ws_ref, bs_ref, width):
    gd = width // SGU_GROUPS
    grp = lambda ref, base, i: ref[:, base + i * gd:base + (i + 1) * gd]
    idx = range(SGU_GROUPS)
    return ([grp(p_ref, 0, i) for i in idx], [grp(p_ref, width, i) for i in idx],
            [grp(p_ref, 2 * width, i) for i in idx], [grp(lng_ref, 0, i) for i in idx],
            [grp(lnb_ref, 0, i) for i in idx], [ws_ref[i] for i in idx], [bs_ref[i] for i in idx])


def _sgu_fwd(p, ln_g, ln_b, w_s, b_s):
    t = p.shape[0]
    width = p.shape[1] // 3
    gd = width // SGU_GROUPS
    tm = SGU_CHUNK

    def body(p_ref, lng_ref, lnb_ref, ws_ref, bs_ref, y_ref):
        outs = _sgu_math(*_sgu_load(p_ref, lng_ref, lnb_ref, ws_ref, bs_ref, width))
        for i in range(SGU_GROUPS):
            y_ref[:, i * gd:(i + 1) * gd] = outs[i].astype(BF16)

    full = lambda a: pl.BlockSpec(a.shape, lambda i: (0,) * a.ndim)
    return pl.pallas_call(
        body, name="sgu_fwd", grid=(t // tm,),
        out_shape=jax.ShapeDtypeStruct((t, width), BF16),
        in_specs=[pl.BlockSpec((tm, 3 * width), lambda i: (i, 0)), full(ln_g), full(ln_b), full(w_s), full(b_s)],
        out_specs=pl.BlockSpec((tm, width), lambda i: (i, 0)),
        compiler_params=_params("parallel"),
    )(p, ln_g, ln_b, w_s, b_s)


def _sgu_bwd(p, ln_g, ln_b, w_s, b_s, dy):
    t = p.shape[0]
    width = p.shape[1] // 3
    gd = width // SGU_GROUPS
    tm = SGU_CHUNK

    def body(p_ref, lng_ref, lnb_ref, ws_ref, bs_ref, dy_ref, dp_out, dlng_out, dlnb_out, dws_out, dbs_out):
        @pl.when(pl.program_id(0) == 0)
        def _():
            for ref in (dlng_out, dlnb_out, dws_out, dbs_out):
                ref[...] = jnp.zeros_like(ref)

        _, vjp = jax.vjp(_sgu_math, *_sgu_load(p_ref, lng_ref, lnb_ref, ws_ref, bs_ref, width))
        dus, dvs, dgs, dlng, dlnb, dws, dbs = vjp(
            [dy_ref[:, i * gd:(i + 1) * gd] for i in range(SGU_GROUPS)])
        for i in range(SGU_GROUPS):
            cols = slice(i * gd, (i + 1) * gd)
            dp_out[:, i * gd:(i + 1) * gd] = dus[i].astype(BF16)
            dp_out[:, width + i * gd:width + (i + 1) * gd] = dvs[i].astype(BF16)
            dp_out[:, 2 * width + i * gd:2 * width + (i + 1) * gd] = dgs[i].astype(BF16)
            dlng_out[:, cols] += dlng[i]
            dlnb_out[:, cols] += dlnb[i]
            dws_out[i] += dws[i]
            dbs_out[i] += dbs[i]

    full = lambda a: pl.BlockSpec(a.shape, lambda i: (0,) * a.ndim)
    like = lambda a: jax.ShapeDtypeStruct(a.shape, F32)
    return pl.pallas_call(
        body, name="sgu_bwd", grid=(t // tm,),
        out_shape=(jax.ShapeDtypeStruct((t, 3 * width), BF16), like(ln_g), like(ln_b), like(w_s), like(b_s)),
        in_specs=[pl.BlockSpec((tm, 3 * width), lambda i: (i, 0)), full(ln_g), full(ln_b), full(w_s), full(b_s),
                  pl.BlockSpec((tm, width), lambda i: (i, 0))],
        out_specs=(pl.BlockSpec((tm, 3 * width), lambda i: (i, 0)), full(ln_g), full(ln_b), full(w_s), full(b_s)),
        compiler_params=_params("arbitrary"),
    )(p, ln_g, ln_b, w_s, b_s, dy)


def _sum_slabs(parts, name):
    n_parts, rows, cols = parts.shape
    tr = _tile(rows, max(16, (1 << 18) // cols), 16)

    def body(p_ref, o_ref):
        acc = p_ref[0].astype(F32)
        for d in range(1, n_parts):
            acc = acc + p_ref[d].astype(F32)
        o_ref[...] = acc

    return pl.pallas_call(
        body, name=name, grid=(rows // tr,),
        out_shape=jax.ShapeDtypeStruct((rows, cols), F32),
        in_specs=[pl.BlockSpec((n_parts, tr, cols), lambda i: (0, i, 0))],
        out_specs=pl.BlockSpec((tr, cols), lambda i: (i, 0)),
        compiler_params=_params("parallel"),
    )(parts)


def _adamw(w, g, m, v, name):
    rows, cols = w.shape
    tr = _tile(rows, max(8, (1 << 18) // cols), 8)

    def body(w_ref, g_ref, m_ref, v_ref, d_out, m_out, v_out):
        gv = g_ref[...]
        mn = ADAM_B1 * m_ref[...] + (1.0 - ADAM_B1) * gv
        vn = ADAM_B2 * v_ref[...] + (1.0 - ADAM_B2) * (gv * gv)
        m_hat = mn / (1.0 - ADAM_B1 ** ADAM_STEP)
        v_hat = vn / (1.0 - ADAM_B2 ** ADAM_STEP)
        d_out[...] = -ADAM_LR * (m_hat / (jnp.sqrt(v_hat) + ADAM_EPS) + ADAM_WD * w_ref[...])
        m_out[...] = mn
        v_out[...] = vn

    blk = pl.BlockSpec((tr, cols), lambda i: (i, 0))
    shape = jax.ShapeDtypeStruct((rows, cols), F32)
    return pl.pallas_call(
        body, name=name, grid=(rows // tr,),
        out_shape=(shape, shape, shape),
        in_specs=[blk, blk, blk, blk], out_specs=(blk, blk, blk),
        compiler_params=_params("parallel"),
    )(w, g, m, v)


PACK_COLS = 1024


def _pack(arrays):
    flat = jnp.concatenate([a.reshape(-1).astype(F32) for a in arrays])
    rows = -(-flat.shape[0] // (8 * PACK_COLS)) * 8
    return jnp.pad(flat, (0, rows * PACK_COLS - flat.shape[0])).reshape(rows, PACK_COLS)


def _unpack(packed, shapes):
    flat = packed.reshape(-1)
    out, at = [], 0
    for s in shapes:
        n = 1
        for d in s:
            n *= d
        out.append(flat[at:at + n].reshape(s))
        at += n
    return out


def kernel(x, norm_g, final_norm_g, e_w_in, e_shift_mu, e_w_decay_up, e_w0, e_a_up, e_a0, e_k_k, e_k_a, e_r_k, e_gn_g, e_gn_b, e_w_out, o_w_in, o_ln_g, o_ln_b, o_w_s, o_b_s, o_w_out, loss_target, m_norm_g, m_final_norm_g, m_e_w_in, m_e_shift_mu, m_e_w_decay_up, m_e_w0, m_e_a_up, m_e_a0, m_e_k_k, m_e_k_a, m_e_r_k, m_e_gn_g, m_e_gn_b, m_e_w_out, m_o_w_in, m_o_ln_g, m_o_ln_b, m_o_w_s, m_o_b_s, m_o_w_out, v_norm_g, v_final_norm_g, v_e_w_in, v_e_shift_mu, v_e_w_decay_up, v_e_w0, v_e_a_up, v_e_a0, v_e_k_k, v_e_k_a, v_e_r_k, v_e_gn_g, v_e_gn_b, v_e_w_out, v_o_w_in, v_o_ln_g, v_o_ln_b, v_o_w_s, v_o_b_s, v_o_w_out):
    weights = dict(norm_g=norm_g, final_norm_g=final_norm_g, e_w_in=e_w_in, e_shift_mu=e_shift_mu,
                   e_w_decay_up=e_w_decay_up, e_w0=e_w0, e_a_up=e_a_up, e_a0=e_a0, e_k_k=e_k_k, e_k_a=e_k_a,
                   e_r_k=e_r_k, e_gn_g=e_gn_g, e_gn_b=e_gn_b, e_w_out=e_w_out, o_w_in=o_w_in, o_ln_g=o_ln_g,
                   o_ln_b=o_ln_b, o_w_s=o_w_s, o_b_s=o_b_s, o_w_out=o_w_out)
    mom1 = dict(norm_g=m_norm_g, final_norm_g=m_final_norm_g, e_w_in=m_e_w_in, e_shift_mu=m_e_shift_mu,
                e_w_decay_up=m_e_w_decay_up, e_w0=m_e_w0, e_a_up=m_e_a_up, e_a0=m_e_a0, e_k_k=m_e_k_k,
                e_k_a=m_e_k_a, e_r_k=m_e_r_k, e_gn_g=m_e_gn_g, e_gn_b=m_e_gn_b, e_w_out=m_e_w_out,
                o_w_in=m_o_w_in, o_ln_g=m_o_ln_g, o_ln_b=m_o_ln_b, o_w_s=m_o_w_s, o_b_s=m_o_b_s,
                o_w_out=m_o_w_out)
    mom2 = dict(norm_g=v_norm_g, final_norm_g=v_final_norm_g, e_w_in=v_e_w_in, e_shift_mu=v_e_shift_mu,
                e_w_decay_up=v_e_w_decay_up, e_w0=v_e_w0, e_a_up=v_e_a_up, e_a0=v_e_a0, e_k_k=v_e_k_k,
                e_k_a=v_e_k_a, e_r_k=v_e_r_k, e_gn_g=v_e_gn_g, e_gn_b=v_e_gn_b, e_w_out=v_e_w_out,
                o_w_in=v_o_w_in, o_ln_g=v_o_ln_g, o_ln_b=v_o_ln_b, o_w_s=v_o_w_s, o_b_s=v_o_b_s,
                o_w_out=v_o_w_out)
    names = list(weights)
    big = ("e_w_in", "e_w_out", "o_w_in", "o_w_out")

    bl, seq, d = x.shape
    t = bl * seq
    width = e_w0.shape[1]
    lora = e_w_decay_up.shape[1]
    n_sb = width // SB_HEAD
    me = 4 * lax.axis_index("x") + 2 * lax.axis_index("y") + lax.axis_index("c")

    e_win_t = _all_gather(e_w_in[0].T.astype(BF16), "gather_e_w_in").reshape(-1, d)
    later_shards = (e_w_out[0].astype(BF16), o_w_in[0].T.astype(BF16), o_w_out[0].astype(BF16))
    sharded_small = ("e_w_decay_up", "e_a_up", "o_ln_g", "o_ln_b")
    small_shapes = [weights[n][0].shape for n in sharded_small]
    got = _all_gather(_pack([weights[n][0] for n in sharded_small]), "gather_small")
    per_dev = [_unpack(got[dev], small_shapes) for dev in range(N_DEV)]
    wd, wa, ln_g, ln_b = [jnp.concatenate([per_dev[dev][i] for dev in range(N_DEV)], axis=-1).reshape(
        small_shapes[i][:-1] + (-1,)) for i in range(4)]
    ln_g, ln_b = ln_g.reshape(1, -1), ln_b.reshape(1, -1)
    e_ind, e_ind_t = _head_indicator(width)
    b_s3 = o_b_s[0][:, :, None]

    x2d = x.reshape(t, d)
    target = loss_target.reshape(t, d)
    cols_rwkv = 3 * width + 2 * lora
    assert cols_rwkv % LANES == 0 and width % LANES == 0
    sb0 = (cols_rwkv + width) // SB_HEAD
    sb_cols = (sb0, sb0 + n_sb, sb0 + 2 * n_sb, sb0 + 3 * n_sb)

    h0 = _rms_fwd(x2d, norm_g[0:1], "rms0_fwd")
    p = _matmul(h0, e_win_t, "nt", F32, "e_in_fwd", tm=2048, tn=1280)
    g_rwkv = p[:, cols_rwkv:cols_rwkv + width]
    r, w, kx, v, kk, b = _rwkv_prep(p, seq, (width, lora), e_shift_mu, wd, e_w0, wa, e_a0, e_k_k, e_k_a,
                                    e_ind, e_ind_t)
    sk = [_to_scan_k(a, bl, seq) for a in (kk, w, b, kx, r)]
    sv = _to_scan_v(v, bl, seq)
    ys_scan, states, sa_all, e_wout, o_wout = _scan_fwd(*sk, sv, gathers=(later_shards[0], later_shards[2]))
    ys = _from_scan_v(ys_scan, bl, seq)
    ya = _rwkv_post(ys, r, kx, v, g_rwkv, 0, e_gn_g, e_gn_b, e_r_k, e_ind, e_ind_t)
    yb, sb_o, sb_tot, o_win_t = _sb_fwd(p, bl, seq, sb_cols, n_sb, gathers=(later_shards[1],))
    e_wout, o_win_t, o_wout = (a.reshape(-1, d) for a in (e_wout, o_win_t, o_wout))
    y = jnp.concatenate([ya, yb], axis=1)
    x1 = _matmul(y, e_wout, "nn", F32, "e_out_fwd", res=x2d)
    h1 = _rms_fwd(x1, norm_g[1:2], "rms1_fwd")
    p2 = _matmul(h1, o_win_t, "nt", F32, "o_in_fwd", tm=2048)
    y2 = _sgu_fwd(p2, ln_g, ln_b, o_w_s[0], b_s3)
    x2 = _matmul(y2, o_wout, "nn", F32, "o_out_fwd", res=x1)
    dx2, d_final_g, loss_part = _final_loss(x2, final_norm_g.reshape(1, d), target, "final_loss")

    dy2 = _matmul(dx2, o_wout, "nt", F32, "o_out_bwd_x", tm=2048)
    d_o_wout = _matmul(y2, dx2, "tn", F32, "o_out_bwd_w", tn=2048)
    dp2, d_ln_g, d_ln_b, d_w_s, d_b_s3 = _sgu_bwd(p2, ln_g, ln_b, o_w_s[0], b_s3, dy2)
    dh1 = _matmul(dp2, o_win_t, "nn", F32, "o_in_bwd_x", tm=2048, tk=1536)
    d_o_win_t = _matmul(dp2, h1, "tn", F32, "o_in_bwd_w", tn=2048)
    dx1, d_g1 = _rms_bwd(x1, norm_g[1:2], dh1, dx2, "rms1_bwd")
    dy = _matmul(dx1, e_wout, "nt", F32, "e_out_bwd_x", tm=2048)
    d_e_wout = _matmul(y, dx1, "tn", F32, "e_out_bwd_w", tn=2048)
    core = lax.axis_index("c").astype(jnp.int32).reshape(1)
    by_owner = lambda full: full.reshape((N_CHIPS, 2, full.shape[0] // N_DEV, full.shape[1]))
    early = {"e_w_out": by_owner(d_e_wout), "o_w_in": by_owner(d_o_win_t), "o_w_out": by_owner(d_o_wout)}
    dq, dk, dv_sb, dg_sb, *swapped = _sb_bwd(p, bl, seq, sb_cols, n_sb, dy, n_sb, sb_o, sb_tot,
                                             swaps=tuple(early.values()))
    partials = [_pair_sum(full, got, core, "pairsum_" + n) for (n, full), got in zip(early.items(), swapped)]
    dys, dr1, dkx1, dv1, dg_rwkv, d_gn_g, d_gn_b, d_r_k = _rwkv_post_bwd(
        ys, r, kx, v, g_rwkv, 0, e_gn_g, e_gn_b, e_r_k, e_ind, e_ind_t, dy, 0)
    dkk_s, dw_s, db_s, dkx_s, dr_s, dv_s, *landed = _scan_bwd(*sk, sv, states, sa_all, _to_scan_v(dys, bl, seq),
                                                              scatters=tuple(partials))
    early_sums = {n: _sum_slabs(parts, "sum_" + n) for n, parts in zip(early, landed)}
    dkk, dw, db, dkx2, dr2 = [_from_scan_k(a, bl, seq) for a in (dkk_s, dw_s, db_s, dkx_s, dr_s)]
    dv2 = _from_scan_v(dv_s, bl, seq)
    dp_rwkv, d_mu, d_wd, d_w0, d_wa, d_a0, d_k_k, d_k_a = _rwkv_prep_bwd(
        p, seq, (width, lora), e_shift_mu, wd, e_w0, wa, e_a0, e_k_k, e_k_a, e_ind, e_ind_t,
        dr1, dw, dkx1, dv1, dkk, db, (dr2, dkx2, dv2))
    dp = jnp.concatenate([dp_rwkv, dg_rwkv, dq, dk, dv_sb, dg_sb], axis=1)
    last = by_owner(_matmul(dp, h0, "tn", BF16, "e_in_bwd_w", tm=1280, tn=2048))
    last = _pair_sum(last, _pair_swap(last, "swap_e_w_in"), core, "pairsum_e_w_in")
    dh0, last = _matmul(dp, e_win_t, "nn", F32, "e_in_bwd_x", tm=2048, tk=1280, scatters=(last,))
    grad_x, d_g0 = _rms_bwd(x2d, norm_g[0:1], dh0, dx1, "rms0_bwd")

    grads = {
        "e_w_in": _sum_slabs(last, "sum_e_w_in").T[None],
        "e_w_out": early_sums["e_w_out"][None],
        "o_w_in": early_sums["o_w_in"].T[None],
        "o_w_out": early_sums["o_w_out"][None],
    }
    small_full = {
        "norm_g": jnp.concatenate([d_g0, d_g1], axis=0), "final_norm_g": d_final_g.reshape(-1),
        "e_shift_mu": d_mu, "e_w_decay_up": d_wd[None], "e_w0": d_w0, "e_a_up": d_wa[None], "e_a0": d_a0,
        "e_k_k": d_k_k, "e_k_a": d_k_a, "e_r_k": d_r_k, "e_gn_g": d_gn_g, "e_gn_b": d_gn_b,
        "o_ln_g": d_ln_g, "o_ln_b": d_ln_b, "o_w_s": d_w_s[None], "o_b_s": d_b_s3[:, :, 0][None],
    }
    small = [n for n in names if n not in big]
    parts = _all_gather(_pack([small_full[n] for n in small]), "gather_small_grads")
    totals = _unpack(_sum_slabs(parts, "sum_small_grads"), [small_full[n].shape for n in small])
    for n, g in zip(small, totals):
        if n in sharded_small:
            size = weights[n].shape[-1]
            g = lax.dynamic_slice_in_dim(g, me * size, size, axis=g.ndim - 1)
        grads[n] = g.reshape(weights[n].shape)

    delta, new_m, new_v = {}, {}, {}
    for n in big:
        shp = weights[n].shape
        flat = lambda a: a.reshape(shp[-2], shp[-1])
        dl, mn, vn = _adamw(flat(weights[n]), flat(grads[n]), flat(mom1[n]), flat(mom2[n]), "adamw_" + n)
        delta[n], new_m[n], new_v[n] = dl.reshape(shp), mn.reshape(shp), vn.reshape(shp)
    packed = [_pack([src[n] for n in small]) for src in (weights, grads, mom1, mom2)]
    outs = _adamw(*packed, "adamw_small")
    shapes = [weights[n].shape for n in small]
    for dst, arr in zip((delta, new_m, new_v), outs):
        for n, a in zip(small, _unpack(arr, shapes)):
            dst[n] = a

    loss = lax.psum(loss_part[0, 0], ("x", "y", "c"))
    return (loss, grad_x.reshape(bl, seq, d), *[grads[n] for n in names], *[delta[n] for n in names],
            *[new_m[n] for n in names], *[new_v[n] for n in names])
```

```python
import functools

import jax
import jax.numpy as jnp
from jax import lax
from jax.experimental import pallas as pl
from jax.experimental.pallas import tpu as pltpu

F32 = jnp.float32
BF16 = jnp.bfloat16

N_DEV = 8
RWKV_HEAD = 64
SB_HEAD = 128
SB_BLOCK = 128
SGU_CHUNK = 128
SGU_GROUPS = 16
RMS_EPS = 1e-6
GN_EPS = 64e-5
LN_EPS = 1e-5
L2_EPS = 1e-12
ADAM_LR = 0.001
ADAM_B1 = 0.9
ADAM_B2 = 0.999
ADAM_EPS = 1e-08
ADAM_WD = 0.01
ADAM_STEP = 10

VMEM_LIMIT_V7X = 56 * 1024 * 1024
LANES = 128
SCAN_STEPS_PER_BLOCK = 16


def _params(*sem):
    return pltpu.CompilerParams(dimension_semantics=sem, vmem_limit_bytes=VMEM_LIMIT_V7X)


def _tile(n, target, mult):
    best = None
    d = mult
    while d <= min(n, target):
        if n % d == 0:
            best = d
        d += mult
    return n if best is None else best


def _remote(src, dst, send_sems, recv_sems, k, dev):
    return pltpu.make_async_remote_copy(src_ref=src, dst_ref=dst, send_sem=send_sems.at[k], recv_sem=recv_sems.at[k],
                                        device_id=dev, device_id_type=pl.DeviceIdType.MESH)


_HBM = pl.BlockSpec(memory_space=pl.ANY)


GATHER_COPIES = 7


def _gather_phases(src_ref, out_ref, send_sems, recv_sems, local_sems, n):
    x, y, c = lax.axis_index("x"), lax.axis_index("y"), lax.axis_index("c")
    me, sibling = (x, y, c), (x, y, 1 - c)
    chips = [(1 - x, y), (x, 1 - y), (1 - x, 1 - y)]

    def slot(px, py, pc):
        return out_ref.at[4 * px + 2 * py + pc]

    def copy(k, block, to, own=False):
        return _remote(src_ref if own else slot(*block), slot(*block), send_sems, recv_sems,
                       GATHER_COPIES * n + k, to)

    mine = pltpu.make_async_copy(src_ref, slot(*me), local_sems.at[n])
    first = [copy(0, me, sibling, True)] + [copy(1 + j, me, (*chip, c), True) for j, chip in enumerate(chips)]
    passed = [copy(4 + j, (*chip, c), sibling) for j, chip in enumerate(chips)]

    def start():
        mine.start()
        for cp in first:
            cp.start()

    def relay():
        for j, chip in enumerate(chips):
            copy(1 + j, (*chip, c), me).wait_recv()
            passed[j].start()

    def finish():
        copy(0, sibling, me).wait_recv()
        for j, chip in enumerate(chips):
            copy(4 + j, (*chip, 1 - c), me).wait_recv()
        for cp in first + passed:
            cp.wait_send()
        mine.wait()

    return start, relay, finish


def _gather_scratch(n):
    return [pltpu.SemaphoreType.DMA((GATHER_COPIES * n,)), pltpu.SemaphoreType.DMA((GATHER_COPIES * n,)),
            pltpu.SemaphoreType.DMA((n,))]


def _gathered_shape(src):
    return jax.ShapeDtypeStruct((N_DEV,) + tuple(src.shape), src.dtype)


def _all_gather(src, name):
    def body(src_ref, out_ref, send_sems, recv_sems, local_sems):
        for phase in _gather_phases(src_ref, out_ref, send_sems, recv_sems, local_sems, 0):
            phase()

    return pl.pallas_call(
        body, name=name, out_shape=_gathered_shape(src), in_specs=[_HBM], out_specs=_HBM,
        scratch_shapes=_gather_scratch(1),
    )(src)


N_CHIPS = N_DEV // 2
SCATTER_COPIES = N_CHIPS - 1


def _swap_phases(src_ref, out_ref, send_sems, recv_sems, n):
    x, y, c = lax.axis_index("x"), lax.axis_index("y"), lax.axis_index("c")
    copies = [_remote(src_ref.at[q, 1 - c], out_ref.at[q], send_sems, recv_sems, N_CHIPS * n + q, (x, y, 1 - c))
              for q in range(N_CHIPS)]

    def start():
        for cp in copies:
            cp.start()

    def finish():
        for cp in copies:
            cp.wait_recv()
        for cp in copies:
            cp.wait_send()

    return start, finish


def _swapped_shape(full):
    return jax.ShapeDtypeStruct((N_CHIPS,) + tuple(full.shape[2:]), full.dtype)


def _swap_scratch(n):
    return [pltpu.SemaphoreType.DMA((N_CHIPS * n,)), pltpu.SemaphoreType.DMA((N_CHIPS * n,))]


def _pair_swap(full, name):
    def body(src_ref, out_ref, send_sems, recv_sems):
        for phase in _swap_phases(src_ref, out_ref, send_sems, recv_sems, 0):
            phase()

    return pl.pallas_call(body, name=name, out_shape=_swapped_shape(full), in_specs=[_HBM], out_specs=_HBM,
                          scratch_shapes=_swap_scratch(1))(full)


def _pair_sum(full, got, core, name):
    n_chips, _, rows, cols = full.shape
    tr = _tile(rows, max(16, (1 << 19) // cols), 16)

    def body(core_ref, a_ref, b_ref, o_ref):
        o_ref[...] = (a_ref[...].astype(F32) + b_ref[...].astype(F32)).astype(BF16)

    return pl.pallas_call(
        body, name=name,
        grid_spec=pltpu.PrefetchScalarGridSpec(
            num_scalar_prefetch=1, grid=(n_chips, rows // tr),
            in_specs=[pl.BlockSpec((None, None, tr, cols), lambda q, i, s: (q, s[0], i, 0)),
                      pl.BlockSpec((None, tr, cols), lambda q, i, s: (q, i, 0))],
            out_specs=pl.BlockSpec((None, tr, cols), lambda q, i, s: (q, i, 0))),
        out_shape=jax.ShapeDtypeStruct((n_chips, rows, cols), BF16),
        compiler_params=_params("parallel", "parallel"),
    )(core, full, got)


def _scatter_phases(src_ref, out_ref, send_sems, recv_sems, local_sems, n):
    x, y, c = lax.axis_index("x"), lax.axis_index("y"), lax.axis_index("c")
    here = 2 * x + y
    chips = [(1 - x, y), (x, 1 - y), (1 - x, 1 - y)]
    local = pltpu.make_async_copy(src_ref.at[here], out_ref.at[here], local_sems.at[n])
    sends = [_remote(src_ref.at[2 * px + py], out_ref.at[here], send_sems, recv_sems, SCATTER_COPIES * n + j,
                     (px, py, c)) for j, (px, py) in enumerate(chips)]
    recvs = [_remote(src_ref.at[2 * px + py], out_ref.at[2 * px + py], send_sems, recv_sems,
                     SCATTER_COPIES * n + j, (px, py, c)) for j, (px, py) in enumerate(chips)]

    def start():
        local.start()
        for cp in sends:
            cp.start()

    def finish():
        for cp in recvs:
            cp.wait_recv()
        for cp in sends:
            cp.wait_send()
        local.wait()

    return start, finish


def _scatter_scratch(n):
    return [pltpu.SemaphoreType.DMA((SCATTER_COPIES * n,)), pltpu.SemaphoreType.DMA((SCATTER_COPIES * n,)),
            pltpu.SemaphoreType.DMA((n,))]


def _chip_scatter(parts, name):
    def body(src_ref, out_ref, send_sems, recv_sems, local_sems):
        for phase in _scatter_phases(src_ref, out_ref, send_sems, recv_sems, local_sems, 0):
            phase()

    return pl.pallas_call(body, name=name, out_shape=jax.ShapeDtypeStruct(parts.shape, parts.dtype),
                          in_specs=[_HBM], out_specs=_HBM, scratch_shapes=_scatter_scratch(1))(parts)


def _matmul(a, b, mode, out_dtype, name, res=None, tm=1024, tn=1024, tk=1024, scatters=()):
    if mode == "nn":
        (m, k), (k2, n) = a.shape, b.shape
    elif mode == "nt":
        (m, k), (n, k2) = a.shape, b.shape
    else:
        (k, m), (k2, n) = a.shape, b.shape
    assert k == k2, (a.shape, b.shape, mode)
    tm, tn, tk = _tile(m, tm, 128), _tile(n, tn, 128), _tile(k, tk, 128)
    nk = k // tk
    if mode == "nn":
        a_spec = pl.BlockSpec((tm, tk), lambda i, j, kk: (i, kk))
        b_spec = pl.BlockSpec((tk, tn), lambda i, j, kk: (kk, j))
        dims = (((1,), (0,)), ((), ()))
    elif mode == "nt":
        a_spec = pl.BlockSpec((tm, tk), lambda i, j, kk: (i, kk))
        b_spec = pl.BlockSpec((tn, tk), lambda i, j, kk: (j, kk))
        dims = (((1,), (1,)), ((), ()))
    else:
        a_spec = pl.BlockSpec((tk, tm), lambda i, j, kk: (kk, i))
        b_spec = pl.BlockSpec((tk, tn), lambda i, j, kk: (kk, j))
        dims = (((0,), (0,)), ((), ()))
    o_spec = pl.BlockSpec((tm, tn), lambda i, j, kk: (i, j))
    has_res = res is not None
    n_in = 3 if has_res else 2
    nsc = len(scatters)
    grid = (m // tm, n // tn, nk)

    def body(*refs):
        a_ref, b_ref = refs[:2]
        r_ref = refs[2] if has_res else None
        o_ref = refs[n_in + nsc]
        acc_ref = refs[n_in + 2 * nsc + 1]
        kk = pl.program_id(2)

        if nsc:
            send_sems, recv_sems, local_sems = refs[n_in + 2 * nsc + 2:]
            phases = [_scatter_phases(refs[n_in + s], refs[n_in + nsc + 1 + s], send_sems, recv_sems, local_sems, s)
                      for s in range(nsc)]
            for k, at in enumerate(((0, 0, 0), tuple(g - 1 for g in grid))):
                @pl.when((pl.program_id(0) == at[0]) & (pl.program_id(1) == at[1]) & (kk == at[2]))
                def _(k=k):
                    for ph in phases:
                        ph[k]()

        def product():
            return lax.dot_general(a_ref[...].astype(BF16), b_ref[...].astype(BF16), dims,
                                   preferred_element_type=F32)

        @pl.when(kk == 0)
        def _():
            acc_ref[...] = product()

        @pl.when(kk > 0)
        def _():
            acc_ref[...] += product()

        @pl.when(kk == nk - 1)
        def _():
            out = acc_ref[...]
            if has_res:
                out = out + r_ref[...]
            o_ref[...] = out.astype(out_dtype)

    ins = [a, b] + ([res] if has_res else [])
    specs = [a_spec, b_spec] + ([o_spec] if has_res else [])
    out = jax.ShapeDtypeStruct((m, n), out_dtype)
    if not nsc:
        return pl.pallas_call(
            body, name=name, grid=grid, out_shape=out, in_specs=specs, out_specs=o_spec,
            scratch_shapes=[pltpu.VMEM((tm, tn), F32)],
            compiler_params=_params("parallel", "parallel", "arbitrary"),
        )(*ins)
    return pl.pallas_call(
        body, name=name, grid=grid,
        out_shape=(out, *[jax.ShapeDtypeStruct(s.shape, s.dtype) for s in scatters]),
        in_specs=specs + [_HBM] * nsc, out_specs=(o_spec, *[_HBM] * nsc),
        scratch_shapes=[pltpu.VMEM((tm, tn), F32)] + _scatter_scratch(nsc),
        compiler_params=_params("arbitrary", "arbitrary", "arbitrary"),
    )(*ins, *scatters)


def _rms_fwd(x, g, name):
    t, d = x.shape
    tm = _tile(t, 256, 8)

    def body(x_ref, g_ref, h_ref):
        xv = x_ref[...]
        rstd = lax.rsqrt(jnp.mean(xv * xv, axis=-1, keepdims=True) + RMS_EPS)
        h_ref[...] = (xv * rstd * g_ref[...]).astype(BF16)

    return pl.pallas_call(
        body, name=name, grid=(t // tm,),
        out_shape=jax.ShapeDtypeStruct((t, d), BF16),
        in_specs=[pl.BlockSpec((tm, d), lambda i: (i, 0)), pl.BlockSpec((1, d), lambda i: (0, 0))],
        out_specs=pl.BlockSpec((tm, d), lambda i: (i, 0)),
        compiler_params=_params("parallel"),
    )(x, g)


def _rms_bwd(x, g, dh, dres, name):
    t, d = x.shape
    tm = _tile(t, 256, 8)

    def body(x_ref, g_ref, dh_ref, dres_ref, dx_ref, dg_ref):
        @pl.when(pl.program_id(0) == 0)
        def _():
            dg_ref[...] = jnp.zeros_like(dg_ref)

        xv = x_ref[...]
        rstd = lax.rsqrt(jnp.mean(xv * xv, axis=-1, keepdims=True) + RMS_EPS)
        xhat = xv * rstd
        dh_v = dh_ref[...]
        dg_ref[...] += jnp.sum(dh_v * xhat, axis=0, keepdims=True)
        dxh = dh_v * g_ref[...]
        dx_ref[...] = dres_ref[...] + rstd * (dxh - xhat * jnp.mean(dxh * xhat, axis=-1, keepdims=True))

    row = pl.BlockSpec((tm, d), lambda i: (i, 0))
    vec = pl.BlockSpec((1, d), lambda i: (0, 0))
    return pl.pallas_call(
        body, name=name, grid=(t // tm,),
        out_shape=(jax.ShapeDtypeStruct((t, d), F32), jax.ShapeDtypeStruct((1, d), F32)),
        in_specs=[row, vec, row, row], out_specs=(row, vec),
        compiler_params=_params("arbitrary"),
    )(x, g, dh, dres)


def _final_loss(x, g, target, name):
    t, d = x.shape
    tm = _tile(t, 256, 8)

    def body(x_ref, g_ref, t_ref, dx_ref, dg_ref, loss_ref):
        @pl.when(pl.program_id(0) == 0)
        def _():
            dg_ref[...] = jnp.zeros_like(dg_ref)
            loss_ref[...] = jnp.zeros_like(loss_ref)

        xv = x_ref[...]
        rstd = lax.rsqrt(jnp.mean(xv * xv, axis=-1, keepdims=True) + RMS_EPS)
        xhat = xv * rstd
        gv = g_ref[...]
        err = xhat * gv - t_ref[...]
        loss_ref[...] += 0.5 * jnp.sum(jnp.mean(err * err, axis=-1, keepdims=True), axis=0, keepdims=True)
        dout = err * (1.0 / d)
        dg_ref[...] += jnp.sum(dout * xhat, axis=0, keepdims=True)
        dxh = dout * gv
        dx_ref[...] = rstd * (dxh - xhat * jnp.mean(dxh * xhat, axis=-1, keepdims=True))

    row = pl.BlockSpec((tm, d), lambda i: (i, 0))
    vec = pl.BlockSpec((1, d), lambda i: (0, 0))
    return pl.pallas_call(
        body, name=name, grid=(t // tm,),
        out_shape=(jax.ShapeDtypeStruct((t, d), F32), jax.ShapeDtypeStruct((1, d), F32),
                   jax.ShapeDtypeStruct((1, 1), F32)),
        in_specs=[row, vec, row], out_specs=(row, vec, pl.BlockSpec((1, 1), lambda i: (0, 0))),
        compiler_params=_params("arbitrary"),
    )(x, g, target)


def _dot32(a, b):
    return jnp.dot(a, b, precision=lax.Precision.HIGH, preferred_element_type=F32)


def _head_sum(v, e, et):
    return _dot32(_dot32(v, e), et)


def _log_sigmoid(z):
    return jnp.minimum(z, 0.0) - jnp.log1p(jnp.exp(-jnp.abs(z)))


def _silu(g):
    return g * jax.nn.sigmoid(g)


def _prep_math(k, wlo, alo, wd, w0, wa, a0, k_k, k_a, e, et):
    wl = w0 + _dot32(jnp.tanh(wlo), wd)
    w_log = _log_sigmoid(wl) - 0.5
    w = jnp.exp(-jnp.exp(w_log))
    a = jax.nn.sigmoid(a0 + _dot32(alo, wa))
    kk0 = k * k_k
    kk = kk0 * lax.rsqrt(jnp.maximum(_head_sum(kk0 * kk0, e, et), L2_EPS * L2_EPS))
    kx = k * (1.0 + (a - 1.0) * k_a)
    return w, kx, kk, kk * a


def _post_math(ys, r, kx, v, g, gn_g, gn_b, r_k, e, et):
    inv = 1.0 / RWKV_HEAD
    mu = _head_sum(ys, e, et) * inv
    dlt = ys - mu
    var = _head_sum(dlt * dlt, e, et) * inv
    y = dlt * lax.rsqrt(var + GN_EPS) * gn_g + gn_b
    bonus = _head_sum(r * kx * r_k, e, et) * v
    return (y + bonus) * _silu(g)


def _shifted(p, prev_row, first):
    rows = lax.broadcasted_iota(jnp.int32, p.shape, 0)
    prev = jnp.where(first, 0.0, prev_row)
    return jnp.where(rows == 0, prev, pltpu.roll(p, 1, 0))


def _head_indicator(width):
    ch = lax.broadcasted_iota(jnp.int32, (width, width // RWKV_HEAD), 0) // RWKV_HEAD
    hd = lax.broadcasted_iota(jnp.int32, (width, width // RWKV_HEAD), 1)
    e = (ch == hd).astype(F32)
    return e, e.T


def _rwkv_prep(p, seq, dims, mu, wd, w0, wa, a0, k_k, k_a, e, et):
    t = p.shape[0]
    width, lora = dims
    cols = 3 * width + 2 * lora
    tm = 128
    per_seq = seq // tm

    def body(p_ref, prev_ref, mu_ref, wd_ref, w0_ref, wa_ref, a0_ref, kk_ref, ka_ref, e_ref, et_ref,
             r_out, w_out, kx_out, v_out, kkn_out, b_out):
        i = pl.program_id(0)
        pv = p_ref[...]
        psh = _shifted(pv, prev_ref[7:8, :], i % per_seq == 0)
        ps = pv + mu_ref[...] * (psh - pv)
        r, k, v = ps[:, :width], ps[:, width:2 * width], ps[:, 2 * width:3 * width]
        wlo, alo = ps[:, 3 * width:3 * width + lora], ps[:, 3 * width + lora:]
        w, kx, kk, b = _prep_math(k, wlo, alo, wd_ref[...], w0_ref[...], wa_ref[...], a0_ref[...],
                                  kk_ref[...], ka_ref[...], e_ref[...], et_ref[...])
        r_out[...] = r
        w_out[...] = w
        kx_out[...] = kx
        v_out[...] = v
        kkn_out[...] = kk
        b_out[...] = b

    full = lambda a: pl.BlockSpec(a.shape, lambda i: (0,) * a.ndim)
    out = pl.BlockSpec((tm, width), lambda i: (i, 0))
    return pl.pallas_call(
        body, name="rwkv_prep", grid=(t // tm,),
        out_shape=tuple(jax.ShapeDtypeStruct((t, width), F32) for _ in range(6)),
        in_specs=[pl.BlockSpec((tm, cols), lambda i: (i, 0)),
                  pl.BlockSpec((8, cols), lambda i: (jnp.maximum(i * (tm // 8) - 1, 0), 0)),
                  full(mu), full(wd), full(w0), full(wa), full(a0), full(k_k), full(k_a), full(e), full(et)],
        out_specs=tuple(out for _ in range(6)),
        compiler_params=_params("parallel"),
    )(p, p, mu, wd, w0, wa, a0, k_k, k_a, e, et)


def _rwkv_prep_bwd(p, seq, dims, mu, wd, w0, wa, a0, k_k, k_a, e, et, dr, dw, dkx, dv, dkk, db, more):
    t = p.shape[0]
    width, lora = dims
    cols = 3 * width + 2 * lora
    tm = 128
    n_tiles = t // tm
    per_seq = seq // tm

    def body(p_ref, prev_ref, mu_ref, wd_ref, w0_ref, wa_ref, a0_ref, kk_ref, ka_ref, e_ref, et_ref,
             dr_ref, dw_ref, dkx_ref, dv_ref, dkk_ref, db_ref, dr2_ref, dkx2_ref, dv2_ref,
             dp_out, dmu_out, dwd_out, dw0_out, dwa_out, da0_out, dkk_out, dka_out, carry):
        step = pl.program_id(0)
        i = n_tiles - 1 - step

        @pl.when(step == 0)
        def _():
            for ref in (dmu_out, dwd_out, dw0_out, dwa_out, da0_out, dkk_out, dka_out, carry):
                ref[...] = jnp.zeros_like(ref)

        pv = p_ref[...]
        first = i % per_seq == 0
        psh = _shifted(pv, prev_ref[7:8, :], first)
        muv = mu_ref[...]
        ps = pv + muv * (psh - pv)
        k = ps[:, width:2 * width]
        wlo, alo = ps[:, 3 * width:3 * width + lora], ps[:, 3 * width + lora:]
        ev, etv = e_ref[...], et_ref[...]
        _, vjp = jax.vjp(lambda *a: _prep_math(*a, ev, etv), k, wlo, alo, wd_ref[...], w0_ref[...],
                         wa_ref[...], a0_ref[...], kk_ref[...], ka_ref[...])
        dk, dwlo, dalo, dwd, dw0, dwa, da0, dk_k, dk_a = vjp(
            (dw_ref[...], dkx_ref[...] + dkx2_ref[...], dkk_ref[...], db_ref[...]))
        dps = jnp.concatenate([dr_ref[...] + dr2_ref[...], dk, dv_ref[...] + dv2_ref[...], dwlo, dalo], axis=1)
        dmu_out[...] += jnp.sum(dps * (psh - pv), axis=0, keepdims=True)
        dwd_out[...] += dwd
        dw0_out[...] += dw0
        dwa_out[...] += dwa
        da0_out[...] += da0
        dkk_out[...] += dk_k
        dka_out[...] += dk_a
        dsh = dps * muv
        rows = lax.broadcasted_iota(jnp.int32, dsh.shape, 0)
        nxt = jnp.where(rows == tm - 1, carry[...], pltpu.roll(dsh, tm - 1, 0))
        dp_out[...] = (dps * (1.0 - muv) + nxt).astype(BF16)
        carry[...] = jnp.where(first, 0.0, dsh[0:1, :])

    full = lambda a: pl.BlockSpec(a.shape, lambda s: (0,) * a.ndim)
    tok = pl.BlockSpec((tm, width), lambda s: (n_tiles - 1 - s, 0))
    vec = lambda n: jax.ShapeDtypeStruct((1, n), F32)
    outs = (jax.ShapeDtypeStruct((t, cols), BF16), vec(cols), jax.ShapeDtypeStruct(wd.shape, F32), vec(width),
            jax.ShapeDtypeStruct(wa.shape, F32), vec(width), vec(width), vec(width))
    return pl.pallas_call(
        body, name="rwkv_prep_bwd", grid=(n_tiles,),
        out_shape=outs,
        in_specs=[pl.BlockSpec((tm, cols), lambda s: (n_tiles - 1 - s, 0)),
                  pl.BlockSpec((8, cols), lambda s: (jnp.maximum((n_tiles - 1 - s) * (tm // 8) - 1, 0), 0)),
                  full(mu), full(wd), full(w0), full(wa), full(a0), full(k_k), full(k_a), full(e), full(et),
                  tok, tok, tok, tok, tok, tok, tok, tok, tok],
        out_specs=(pl.BlockSpec((tm, cols), lambda s: (n_tiles - 1 - s, 0)),) + tuple(
            pl.BlockSpec(o.shape, lambda s: (0, 0)) for o in outs[1:]),
        scratch_shapes=[pltpu.VMEM((1, cols), F32)],
        compiler_params=_params("arbitrary"),
    )(p, p, mu, wd, w0, wa, a0, k_k, k_a, e, et, dr, dw, dkx, dv, dkk, db, *more)


def _rwkv_post(ys, r, kx, v, p, g_col, gn_g, gn_b, r_k, e, et):
    t, width = ys.shape
    tm = 256

    def body(ys_ref, r_ref, kx_ref, v_ref, g_ref, gg_ref, gb_ref, rk_ref, e_ref, et_ref, out_ref):
        out_ref[...] = _post_math(ys_ref[...], r_ref[...], kx_ref[...], v_ref[...], g_ref[...], gg_ref[...],
                                  gb_ref[...], rk_ref[...], e_ref[...], et_ref[...]).astype(BF16)

    tok = pl.BlockSpec((tm, width), lambda i: (i, 0))
    full = lambda a: pl.BlockSpec(a.shape, lambda i: (0,) * a.ndim)
    return pl.pallas_call(
        body, name="rwkv_post", grid=(t // tm,),
        out_shape=jax.ShapeDtypeStruct((t, width), BF16),
        in_specs=[tok, tok, tok, tok, pl.BlockSpec((tm, width), lambda i: (i, g_col)),
                  full(gn_g), full(gn_b), full(r_k), full(e), full(et)],
        out_specs=tok,
        compiler_params=_params("parallel"),
    )(ys, r, kx, v, p, gn_g, gn_b, r_k, e, et)


def _rwkv_post_bwd(ys, r, kx, v, p, g_col, gn_g, gn_b, r_k, e, et, dy, dy_col):
    t, width = ys.shape
    tm = 128

    def body(ys_ref, r_ref, kx_ref, v_ref, g_ref, gg_ref, gb_ref, rk_ref, e_ref, et_ref, dy_ref,
             dys_out, dr_out, dkx_out, dv_out, dg_out, dgg_out, dgb_out, drk_out):
        @pl.when(pl.program_id(0) == 0)
        def _():
            for ref in (dgg_out, dgb_out, drk_out):
                ref[...] = jnp.zeros_like(ref)

        ev, etv = e_ref[...], et_ref[...]
        _, vjp = jax.vjp(lambda *a: _post_math(*a, ev, etv), ys_ref[...], r_ref[...], kx_ref[...], v_ref[...],
                         g_ref[...], gg_ref[...], gb_ref[...], rk_ref[...])
        dys, dr, dkx, dv, dg, dgg, dgb, drk = vjp(dy_ref[...])
        dys_out[...] = dys
        dr_out[...] = dr
        dkx_out[...] = dkx
        dv_out[...] = dv
        dg_out[...] = dg.astype(BF16)
        dgg_out[...] += dgg
        dgb_out[...] += dgb
        drk_out[...] += drk

    tok = pl.BlockSpec((tm, width), lambda i: (i, 0))
    full = lambda a: pl.BlockSpec(a.shape, lambda i: (0,) * a.ndim)
    big = jax.ShapeDtypeStruct((t, width), F32)
    vec = jax.ShapeDtypeStruct((1, width), F32)
    vspec = pl.BlockSpec((1, width), lambda i: (0, 0))
    return pl.pallas_call(
        body, name="rwkv_post_bwd", grid=(t // tm,),
        out_shape=(big, big, big, big, jax.ShapeDtypeStruct((t, width), BF16), vec, vec, vec),
        in_specs=[tok, tok, tok, tok, pl.BlockSpec((tm, width), lambda i: (i, g_col)),
                  full(gn_g), full(gn_b), full(r_k), full(e), full(et),
                  pl.BlockSpec((tm, width), lambda i: (i, dy_col))],
        out_specs=(tok, tok, tok, tok, tok, vspec, vspec, vspec),
        compiler_params=_params("arbitrary"),
    )(ys, r, kx, v, p, gn_g, gn_b, r_k, e, et, dy)


def _to_scan_k(a, bl, seq):
    h = a.shape[1] // RWKV_HEAD
    twice = jnp.broadcast_to(a.reshape(1, bl, seq, h, RWKV_HEAD), (2, bl, seq, h, RWKV_HEAD))
    return twice.transpose(2, 4, 0, 1, 3).reshape(seq, RWKV_HEAD, 2 * bl * h)


def _to_scan_v(a, bl, seq):
    h = a.shape[1] // RWKV_HEAD
    half = RWKV_HEAD // 2
    return a.reshape(bl, seq, h, 2, half).transpose(1, 4, 3, 0, 2).reshape(seq, half, 2 * bl * h)


def _from_scan_k(a, bl, seq):
    h = a.shape[2] // (2 * bl)
    a = a[:, :, :bl * h].reshape(seq, RWKV_HEAD, bl, h).transpose(2, 0, 3, 1)
    return a.reshape(bl * seq, h * RWKV_HEAD)


def _from_scan_v(a, bl, seq):
    half = RWKV_HEAD // 2
    h = a.shape[2] // (2 * bl)
    a = a.reshape(seq, half, 2, bl, h).transpose(3, 0, 4, 2, 1)
    return a.reshape(bl * seq, h * RWKV_HEAD)


def _scan_fwd(kk, w, b, kx, r, v, gathers=()):
    seq, nk, lanes = kk.shape
    nv = v.shape[1]
    tt = SCAN_STEPS_PER_BLOCK
    nblk = seq // tt
    ng = len(gathers)

    def body(*refs):
        kk_ref, w_ref, b_ref, kx_ref, r_ref, v_ref = refs[:6]
        g_src = refs[6:6 + ng]
        y_ref, st_ref, sa_ref = refs[6 + ng:9 + ng]
        g_out = refs[9 + ng:9 + 2 * ng]
        s_ref = refs[9 + 2 * ng]
        pid = pl.program_id(0)

        @pl.when(pid == 0)
        def _():
            s_ref[...] = jnp.zeros_like(s_ref)

        if ng:
            send_sems, recv_sems, local_sems = refs[10 + 2 * ng:]
            phases = [_gather_phases(g_src[n], g_out[n], send_sems, recv_sems, local_sems, n) for n in range(ng)]
            for k, at in enumerate((0, (3 * nblk) // 4, nblk - 1)):
                @pl.when(pid == at)
                def _(k=k):
                    for ph in phases:
                        ph[k]()

        def step(i, carry):
            kkv, wv, bv, kxv, rv = kk_ref[i], w_ref[i], b_ref[i], kx_ref[i], r_ref[i]
            for j in range(nv):
                s_old = s_ref[j]
                sa = -jnp.sum(s_old * kkv, axis=0, keepdims=True)
                s_new = s_old * wv + sa * bv + v_ref[i, j:j + 1, :] * kxv
                s_ref[j] = s_new
                st_ref[i, j] = s_new
                sa_ref[i, j:j + 1, :] = sa
                y_ref[i, j:j + 1, :] = jnp.sum(s_new * rv, axis=0, keepdims=True)
            return carry

        lax.fori_loop(0, tt, step, 0)

    krow = pl.BlockSpec((tt, nk, lanes), lambda i: (i, 0, 0))
    vrow = pl.BlockSpec((tt, nv, lanes), lambda i: (i, 0, 0))
    vshape = jax.ShapeDtypeStruct((seq, nv, lanes), F32)
    return pl.pallas_call(
        body, name="rwkv_scan_fwd", grid=(nblk,),
        out_shape=(vshape, jax.ShapeDtypeStruct((seq, nv, nk, lanes), F32), vshape,
                   *[_gathered_shape(g) for g in gathers]),
        in_specs=[krow, krow, krow, krow, krow, vrow] + [_HBM] * ng,
        out_specs=(vrow, pl.BlockSpec((tt, nv, nk, lanes), lambda i: (i, 0, 0, 0)), vrow, *[_HBM] * ng),
        scratch_shapes=[pltpu.VMEM((nv, nk, lanes), F32)] + (_gather_scratch(ng) if ng else []),
        compiler_params=_params("arbitrary"),
    )(kk, w, b, kx, r, v, *gathers)


def _scan_bwd(kk, w, b, kx, r, v, states, sa_all, dy, scatters=()):
    seq, nk, lanes = kk.shape
    nv = v.shape[1]
    tt = SCAN_STEPS_PER_BLOCK
    nblk = seq // tt
    nsc = len(scatters)

    def both_halves(a):
        return a + pltpu.roll(a, lanes // 2, 1)

    def body(*refs):
        kk_ref, w_ref, b_ref, kx_ref, r_ref, v_ref, st_ref, before_ref, sa_ref, dy_ref = refs[:10]
        dkk_ref, dw_ref, db_ref, dkx_ref, dr_ref, dv_ref = refs[10 + nsc:16 + nsc]
        g_ref = refs[16 + 2 * nsc]

        if nsc:
            send_sems, recv_sems, local_sems = refs[17 + 2 * nsc:]
            phases = [_scatter_phases(refs[10 + n], refs[16 + nsc + n], send_sems, recv_sems, local_sems, n)
                      for n in range(nsc)]
            for k, at in enumerate((0, nblk - 1)):
                @pl.when(pl.program_id(0) == at)
                def _(k=k):
                    for ph in phases:
                        ph[k]()

        @pl.when(pl.program_id(0) == 0)
        def _():
            g_ref[...] = jnp.zeros_like(g_ref)

        def one_step(i, state_before):
            kkv, wv, bv, kxv, rv = kk_ref[i], w_ref[i], b_ref[i], kx_ref[i], r_ref[i]
            zero = jnp.zeros((nk, lanes), F32)
            a_r, a_w, a_b, a_kx, a_kk = zero, zero, zero, zero, zero
            for j in range(nv):
                s_old = state_before(j)
                s_new = st_ref[i, j]
                vv = v_ref[i, j:j + 1, :]
                dyv = dy_ref[i, j:j + 1, :]
                sa = sa_ref[i, j:j + 1, :]
                g = g_ref[j] + dyv * rv
                a_r = a_r + s_new * dyv
                a_w = a_w + g * s_old
                dsa = jnp.sum(g * bv, axis=0, keepdims=True)
                a_b = a_b + g * sa
                dv_ref[i, j:j + 1, :] = jnp.sum(g * kxv, axis=0, keepdims=True)
                a_kx = a_kx + g * vv
                a_kk = a_kk + s_old * dsa
                g_ref[j] = g * wv - dsa * kkv
            dr_ref[i] = both_halves(a_r)
            dw_ref[i] = both_halves(a_w)
            db_ref[i] = both_halves(a_b)
            dkx_ref[i] = both_halves(a_kx)
            dkk_ref[i] = -both_halves(a_kk)

        def step(n, carry):
            i = tt - 1 - n
            one_step(i, lambda j: st_ref[i - 1, j])
            return carry

        lax.fori_loop(0, tt - 1, step, 0)
        at_start = pl.program_id(0) == nblk - 1
        one_step(0, lambda j: jnp.where(at_start, 0.0, before_ref[0, j]))

    rev = lambda i: nblk - 1 - i
    krow = pl.BlockSpec((tt, nk, lanes), lambda i: (rev(i), 0, 0))
    vrow = pl.BlockSpec((tt, nv, lanes), lambda i: (rev(i), 0, 0))
    kshape = jax.ShapeDtypeStruct((seq, nk, lanes), F32)
    return pl.pallas_call(
        body, name="rwkv_scan_bwd", grid=(nblk,),
        out_shape=(kshape, kshape, kshape, kshape, kshape, jax.ShapeDtypeStruct((seq, nv, lanes), F32),
                   *[jax.ShapeDtypeStruct(s.shape, s.dtype) for s in scatters]),
        in_specs=[krow, krow, krow, krow, krow, vrow,
                  pl.BlockSpec((tt, nv, nk, lanes), lambda i: (rev(i), 0, 0, 0)),
                  pl.BlockSpec((1, nv, nk, lanes), lambda i: (jnp.maximum(rev(i) * tt - 1, 0), 0, 0, 0)),
                  vrow, vrow] + [_HBM] * nsc,
        out_specs=(krow, krow, krow, krow, krow, vrow, *[_HBM] * nsc),
        scratch_shapes=[pltpu.VMEM((nv, nk, lanes), F32)] + (_scatter_scratch(nsc) if nsc else []),
        compiler_params=_params("arbitrary"),
    )(kk, w, b, kx, r, v, states, states, sa_all, dy, *scatters)


_NT = (((1,), (1,)), ((), ()))
_TN = (((0,), (0,)), ((), ()))
SB_SCALE = 1.0 / (SB_HEAD ** 0.5)


SB_QUERY_ROWS = 256
SB_FWD_QUERY_ROWS = 512


def _sb_masks(qr):
    blk = SB_BLOCK
    row = lax.broadcasted_iota(jnp.int32, (qr, blk), 0)
    col = lax.broadcasted_iota(jnp.int32, (qr, blk), 1)
    j_in = lax.broadcasted_iota(jnp.int32, (blk, blk), 0)
    s_in = lax.broadcasted_iota(jnp.int32, (blk, blk), 1)
    ones = jnp.ones((blk, blk), BF16)
    fwd = jnp.concatenate([(j_in > s_in).astype(BF16), ones], axis=1)
    bwd = jnp.concatenate([(s_in > j_in).astype(BF16), ones], axis=1)
    return row, col, fwd, bwd


def _split_dots(xs, b):
    his = [a.astype(BF16) for a in xs]
    los = [(a - hi.astype(F32)).astype(BF16) for a, hi in zip(xs, his)]
    tops = [jnp.dot(hi, b, preferred_element_type=F32) for hi in his]
    return [top + jnp.dot(lo, b, preferred_element_type=F32) for top, lo in zip(tops, los)]


SB_HEADS_PER_STEP = 2


def _sb_chains(bl):
    return [(b, slice(hh * SB_HEAD, (hh + 1) * SB_HEAD)) for b in range(bl) for hh in range(SB_HEADS_PER_STEP)]


def _sb_specs(bl, seq, cols, n_heads, rows):
    hp = SB_HEADS_PER_STEP
    qr = min(rows, seq)
    assert n_heads % hp == 0 and all(c % hp == 0 for c in cols) and seq % qr == 0 and qr % SB_BLOCK == 0
    qspec = lambda col: pl.BlockSpec((bl, qr, hp * SB_HEAD), lambda h, i: (0, i, col // hp + h))
    kspec = lambda col: pl.BlockSpec((bl, seq, hp * SB_HEAD), lambda h, i: (0, 0, col // hp + h))
    return qr, qspec, kspec


def _sb_fwd(p, bl, seq, cols, n_heads, gathers=()):
    t = p.shape[0]
    q_col, k_col, v_col, g_col = cols
    blk = SB_BLOCK
    chains = _sb_chains(bl)
    qr, qspec, kspec = _sb_specs(bl, seq, cols, n_heads, SB_FWD_QUERY_ROWS)
    nq = seq // qr
    per_tile = qr // blk

    ng = len(gathers)
    n_groups = n_heads // SB_HEADS_PER_STEP
    n_steps = n_groups * nq

    def body(*refs):
        q_ref, k_ref, v_ref, g_ref = refs[:4]
        out_ref, o_ref, tot_ref = refs[4 + ng:7 + ng]
        qi = pl.program_id(1)

        if ng:
            send_sems, recv_sems, local_sems = refs[7 + 2 * ng:]
            phases = [_gather_phases(refs[4 + n], refs[7 + ng + n], send_sems, recv_sems, local_sems, n)
                      for n in range(ng)]
            now = pl.program_id(0) * nq + qi
            for k, at in enumerate((0, (3 * n_steps) // 4, n_steps - 1)):
                @pl.when(now == at)
                def _(k=k):
                    for ph in phases:
                        ph[k]()

        row, col, mix, _ = _sb_masks(qr)
        qbs = [q_ref[b, :, ln].astype(BF16) for b, ln in chains]
        n_blocks = (qi + 1) * per_tile

        def step(n, carry):
            j = n_blocks - 1 - n
            rows = pl.ds(pl.multiple_of(j * blk, blk), blk)
            causal = (j * blk + col) < (qi * qr + row)
            zs = [lax.dot_general(qb, k_ref[b, rows, ln].astype(BF16), _NT, preferred_element_type=F32) * SB_SCALE
                  for (b, ln), qb in zip(chains, qbs)]
            lszs = [_log_sigmoid(z) for z in zs]
            boths = _split_dots([jnp.where(causal, lsz - z, 0.0) for lsz, z in zip(lszs, zs)], mix)
            atts = [jnp.where(causal, jnp.exp(lsz + both[:, :blk] + later), 0.0).astype(BF16)
                    for lsz, both, (_, later) in zip(lszs, boths, carry)]
            outs = [out + jnp.dot(att, v_ref[b, rows, ln].astype(BF16), preferred_element_type=F32)
                    for (b, ln), att, (out, _) in zip(chains, atts, carry)]
            return tuple((out, later + both[:, blk:]) for out, both, (_, later) in zip(outs, boths, carry))

        zero = jnp.zeros((qr, SB_HEAD), F32)
        done = lax.fori_loop(0, n_blocks, step, tuple((zero, zero) for _ in chains))
        for (b, ln), (out, total) in zip(chains, done):
            o_ref[b, :, ln] = out
            tot_ref[b, :, ln] = total
            out_ref[b, :, ln] = (out * _silu(g_ref[b, :, ln])).astype(BF16)

    width = n_heads * SB_HEAD
    p3 = p.reshape(bl, seq, p.shape[1])
    f32 = jax.ShapeDtypeStruct((bl, seq, width), F32)
    outs = pl.pallas_call(
        body, name="sb_attn_fwd", grid=(n_groups, nq),
        out_shape=(jax.ShapeDtypeStruct((bl, seq, width), BF16), f32, f32, *[_gathered_shape(g) for g in gathers]),
        in_specs=[qspec(q_col), kspec(k_col), kspec(v_col), qspec(g_col)] + [_HBM] * ng,
        out_specs=(qspec(0), qspec(0), qspec(0), *[_HBM] * ng),
        scratch_shapes=_gather_scratch(ng) if ng else [],
        compiler_params=_params("arbitrary", "arbitrary"),
    )(p3, p3, p3, p3, *gathers)
    return tuple(a.reshape(t, width) for a in outs[:3]) + tuple(outs[3:])


def _sb_bwd(p, bl, seq, cols, n_heads, dy, dy_col, o, tot, swaps=()):
    t = p.shape[0]
    q_col, k_col, v_col, g_col = cols
    blk = SB_BLOCK
    ns = len(swaps)
    n_groups = n_heads // SB_HEADS_PER_STEP
    chains = _sb_chains(bl)
    qr, qspec, kspec = _sb_specs(bl, seq, cols + (dy_col,), n_heads, SB_QUERY_ROWS)
    nq = seq // qr
    per_tile = qr // blk

    def body(*refs):
        q_ref, k_ref, v_ref, g_ref, dy_ref, o_ref, tot_ref = refs[:7]
        dq_out, dk_out, dv_out, dg_out = refs[7 + ns:11 + ns]
        dk_acc, dv_acc = refs[11 + 2 * ns:13 + 2 * ns]
        group, qi = pl.program_id(0), pl.program_id(1)

        if ns:
            send_sems, recv_sems = refs[13 + 2 * ns:]
            phases = [_swap_phases(refs[7 + n], refs[11 + ns + n], send_sems, recv_sems, n) for n in range(ns)]
            for k, at in enumerate(((0, 0), (n_groups - 1, nq - 1))):
                @pl.when((group == at[0]) & (qi == at[1]))
                def _(k=k):
                    for ph in phases:
                        ph[k]()

        @pl.when(qi == 0)
        def _():
            dk_acc[...] = jnp.zeros_like(dk_acc)
            dv_acc[...] = jnp.zeros_like(dv_acc)

        row, col, mix, mix_t = _sb_masks(qr)
        qbs, dobs, totals = [], [], []
        for b, ln in chains:
            gate = g_ref[b, :, ln]
            sg = jax.nn.sigmoid(gate)
            dyv = dy_ref[b, :, ln]
            dg_out[b, :, ln] = (dyv * o_ref[b, :, ln] * (sg * (1.0 + gate * (1.0 - sg)))).astype(BF16)
            dobs.append((dyv * (gate * sg)).astype(BF16))
            qbs.append(q_ref[b, :, ln].astype(BF16))
            totals.append(tot_ref[b, :, ln])

        def step(j, carry):
            rows = pl.ds(pl.multiple_of(j * blk, blk), blk)
            causal = (j * blk + col) < (qi * qr + row)
            kbs = [k_ref[b, rows, ln].astype(BF16) for b, ln in chains]
            zs = [lax.dot_general(qb, kb, _NT, preferred_element_type=F32) * SB_SCALE for qb, kb in zip(qbs, kbs)]
            datts = [lax.dot_general(dob, v_ref[b, rows, ln].astype(BF16), _NT, preferred_element_type=F32)
                     for (b, ln), dob in zip(chains, dobs)]
            lszs = [_log_sigmoid(z) for z in zs]
            boths = _split_dots([jnp.where(causal, lsz - z, 0.0) for lsz, z in zip(lszs, zs)], mix)
            seens = [seen + both[:, blk:] for both, (_, seen, _) in zip(boths, carry)]
            atts = [jnp.where(causal, jnp.exp(lsz + both[:, :blk] + (total - seen)), 0.0)
                    for lsz, both, total, seen in zip(lszs, boths, totals, seens)]
            dls = [att * datt for att, datt in zip(atts, datts)]
            for (b, ln), att, dob in zip(chains, atts, dobs):
                dv_acc[b, rows, ln] += lax.dot_general(att.astype(BF16), dob, _TN, preferred_element_type=F32)
            boths_t = _split_dots(dls, mix_t)
            dkeeps = [jnp.where(causal, both_t[:, :blk] + dl_before, 0.0)
                      for both_t, (_, _, dl_before) in zip(boths_t, carry)]
            dzbs = [(((dl + dkeep) * jax.nn.sigmoid(-z) - dkeep) * SB_SCALE).astype(BF16)
                    for dl, dkeep, z in zip(dls, dkeeps, zs)]
            dqs = [dq + jnp.dot(dzb, kb, preferred_element_type=F32) for dzb, kb, (dq, _, _) in zip(dzbs, kbs, carry)]
            for (b, ln), dzb, qb in zip(chains, dzbs, qbs):
                dk_acc[b, rows, ln] += lax.dot_general(dzb, qb, _TN, preferred_element_type=F32)
            return tuple((dq, seen, dl_before + both_t[:, blk:])
                         for dq, seen, both_t, (_, _, dl_before) in zip(dqs, seens, boths_t, carry))

        zero = jnp.zeros((qr, SB_HEAD), F32)
        done = lax.fori_loop(0, (qi + 1) * per_tile, step, tuple((zero, zero, zero) for _ in chains))
        for (b, ln), (dq, _, _) in zip(chains, done):
            dq_out[b, :, ln] = dq.astype(BF16)

        @pl.when(qi == nq - 1)
        def _():
            dk_out[...] = dk_acc[...].astype(BF16)
            dv_out[...] = dv_acc[...].astype(BF16)

    width = n_heads * SB_HEAD
    shape = jax.ShapeDtypeStruct((bl, seq, width), BF16)
    as3 = lambda a: a.reshape(bl, seq, a.shape[1])
    p3 = as3(p)
    acc = pltpu.VMEM((bl, seq, SB_HEADS_PER_STEP * SB_HEAD), F32)
    outs = pl.pallas_call(
        body, name="sb_attn_bwd", grid=(n_groups, nq),
        out_shape=(shape, shape, shape, shape, *[_swapped_shape(s) for s in swaps]),
        in_specs=[qspec(q_col), kspec(k_col), kspec(v_col), qspec(g_col), qspec(dy_col), qspec(0), qspec(0)]
        + [_HBM] * ns,
        out_specs=(qspec(0), kspec(0), kspec(0), qspec(0), *[_HBM] * ns),
        scratch_shapes=[acc, acc] + (_swap_scratch(ns) if ns else []),
        compiler_params=_params("arbitrary", "arbitrary"),
    )(p3, p3, p3, p3, as3(dy), as3(o), as3(tot), *swaps)
    return tuple(a.reshape(t, width) for a in outs[:4]) + tuple(outs[4:])


def _gelu(x):
    return 0.5 * x * (1.0 + lax.erf(x * (2.0 ** -0.5)))


def _sgu_math(us, vs, gs, ln_g, ln_b, ws, bs):
    width = sum(v.shape[1] for v in vs)
    vg = [_gelu(v) for v in vs]
    mu = sum(jnp.sum(v, axis=1, keepdims=True) for v in vg) * (1.0 / width)
    dl = [v - mu for v in vg]
    var = sum(jnp.sum(d * d, axis=1, keepdims=True) for d in dl) * (1.0 / width)
    rstd = lax.rsqrt(var + LN_EPS)
    n = ws[0].shape[0]
    tri = lax.broadcasted_iota(jnp.int32, (n, n), 0) >= lax.broadcasted_iota(jnp.int32, (n, n), 1)
    outs = []
    for i in range(len(vs)):
        vn = dl[i] * rstd * ln_g[i] + ln_b[i]
        mixed = _dot32(jnp.where(tri, ws[i], 0.0), vn) + bs[i]
        outs.append(_gelu(us[i]) * mixed * _silu(gs[i]))
    return outs


def _sgu_load(p_ref, lng_ref, lnb_ref, ws_ref, bs_ref, width):
    gd = width // SGU_GROUPS
    grp = lambda ref, base, i: ref[:, base + i * gd:base + (i + 1) * gd]
    idx = range(SGU_GROUPS)
    return ([grp(p_ref, 0, i) for i in idx], [grp(p_ref, width, i) for i in idx],
            [grp(p_ref, 2 * width, i) for i in idx], [grp(lng_ref, 0, i) for i in idx],
            [grp(lnb_ref, 0, i) for i in idx], [ws_ref[i] for i in idx], [bs_ref[i] for i in idx])


def _sgu_fwd(p, ln_g, ln_b, w_s, b_s):
    t = p.shape[0]
    width = p.shape[1] // 3
    gd = width // SGU_GROUPS
    tm = SGU_CHUNK

    def body(p_ref, lng_ref, lnb_ref, ws_ref, bs_ref, y_ref):
        outs = _sgu_math(*_sgu_load(p_ref, lng_ref, lnb_ref, ws_ref, bs_ref, width))
        for i in range(SGU_GROUPS):
            y_ref[:, i * gd:(i + 1) * gd] = outs[i].astype(BF16)

    full = lambda a: pl.BlockSpec(a.shape, lambda i: (0,) * a.ndim)
    return pl.pallas_call(
        body, name="sgu_fwd", grid=(t // tm,),
        out_shape=jax.ShapeDtypeStruct((t, width), BF16),
        in_specs=[pl.BlockSpec((tm, 3 * width), lambda i: (i, 0)), full(ln_g), full(ln_b), full(w_s), full(b_s)],
        out_specs=pl.BlockSpec((tm, width), lambda i: (i, 0)),
        compiler_params=_params("parallel"),
    )(p, ln_g, ln_b, w_s, b_s)


def _sgu_bwd(p, ln_g, ln_b, w_s, b_s, dy):
    t = p.shape[0]
    width = p.shape[1] // 3
    gd = width // SGU_GROUPS
    tm = SGU_CHUNK

    def body(p_ref, lng_ref, lnb_ref, ws_ref, bs_ref, dy_ref, dp_out, dlng_out, dlnb_out, dws_out, dbs_out):
        @pl.when(pl.program_id(0) == 0)
        def _():
            for ref in (dlng_out, dlnb_out, dws_out, dbs_out):
                ref[...] = jnp.zeros_like(ref)

        _, vjp = jax.vjp(_sgu_math, *_sgu_load(p_ref, lng_ref, lnb_ref, ws_ref, bs_ref, width))
        dus, dvs, dgs, dlng, dlnb, dws, dbs = vjp(
            [dy_ref[:, i * gd:(i + 1) * gd] for i in range(SGU_GROUPS)])
        for i in range(SGU_GROUPS):
            cols = slice(i * gd, (i + 1) * gd)
            dp_out[:, i * gd:(i + 1) * gd] = dus[i].astype(BF16)
            dp_out[:, width + i * gd:width + (i + 1) * gd] = dvs[i].astype(BF16)
            dp_out[:, 2 * width + i * gd:2 * width + (i + 1) * gd] = dgs[i].astype(BF16)
            dlng_out[:, cols] += dlng[i]
            dlnb_out[:, cols] += dlnb[i]
            dws_out[i] += dws[i]
            dbs_out[i] += dbs[i]

    full = lambda a: pl.BlockSpec(a.shape, lambda i: (0,) * a.ndim)
    like = lambda a: jax.ShapeDtypeStruct(a.shape, F32)
    return pl.pallas_call(
        body, name="sgu_bwd", grid=(t // tm,),
        out_shape=(jax.ShapeDtypeStruct((t, 3 * width), BF16), like(ln_g), like(ln_b), like(w_s), like(b_s)),
        in_specs=[pl.BlockSpec((tm, 3 * width), lambda i: (i, 0)), full(ln_g), full(ln_b), full(w_s), full(b_s),
                  pl.BlockSpec((tm, width), lambda i: (i, 0))],
        out_specs=(pl.BlockSpec((tm, 3 * width), lambda i: (i, 0)), full(ln_g), full(ln_b), full(w_s), full(b_s)),
        compiler_params=_params("arbitrary"),
    )(p, ln_g, ln_b, w_s, b_s, dy)


def _sum_slabs(parts, name):
    n_parts, rows, cols = parts.shape
    tr = _tile(rows, max(16, (1 << 18) // cols), 16)

    def body(p_ref, o_ref):
        acc = p_ref[0].astype(F32)
        for d in range(1, n_parts):
            acc = acc + p_ref[d].astype(F32)
        o_ref[...] = acc

    return pl.pallas_call(
        body, name=name, grid=(rows // tr,),
        out_shape=jax.ShapeDtypeStruct((rows, cols), F32),
        in_specs=[pl.BlockSpec((n_parts, tr, cols), lambda i: (0, i, 0))],
        out_specs=pl.BlockSpec((tr, cols), lambda i: (i, 0)),
        compiler_params=_params("parallel"),
    )(parts)


def _adamw(w, g, m, v, name):
    rows, cols = w.shape
    tr = _tile(rows, max(8, (1 << 18) // cols), 8)

    def body(w_ref, g_ref, m_ref, v_ref, d_out, m_out, v_out):
        gv = g_ref[...]
        mn = ADAM_B1 * m_ref[...] + (1.0 - ADAM_B1) * gv
        vn = ADAM_B2 * v_ref[...] + (1.0 - ADAM_B2) * (gv * gv)
        m_hat = mn / (1.0 - ADAM_B1 ** ADAM_STEP)
        v_hat = vn / (1.0 - ADAM_B2 ** ADAM_STEP)
        d_out[...] = -ADAM_LR * (m_hat / (jnp.sqrt(v_hat) + ADAM_EPS) + ADAM_WD * w_ref[...])
        m_out[...] = mn
        v_out[...] = vn

    blk = pl.BlockSpec((tr, cols), lambda i: (i, 0))
    shape = jax.ShapeDtypeStruct((rows, cols), F32)
    return pl.pallas_call(
        body, name=name, grid=(rows // tr,),
        out_shape=(shape, shape, shape),
        in_specs=[blk, blk, blk, blk], out_specs=(blk, blk, blk),
        compiler_params=_params("parallel"),
    )(w, g, m, v)


PACK_COLS = 1024


def _pack(arrays):
    flat = jnp.concatenate([a.reshape(-1).astype(F32) for a in arrays])
    rows = -(-flat.shape[0] // (8 * PACK_COLS)) * 8
    return jnp.pad(flat, (0, rows * PACK_COLS - flat.shape[0])).reshape(rows, PACK_COLS)


def _unpack(packed, shapes):
    flat = packed.reshape(-1)
    out, at = [], 0
    for s in shapes:
        n = 1
        for d in s:
            n *= d
        out.append(flat[at:at + n].reshape(s))
        at += n
    return out


def kernel(x, norm_g, final_norm_g, e_w_in, e_shift_mu, e_w_decay_up, e_w0, e_a_up, e_a0, e_k_k, e_k_a, e_r_k, e_gn_g, e_gn_b, e_w_out, o_w_in, o_ln_g, o_ln_b, o_w_s, o_b_s, o_w_out, loss_target, m_norm_g, m_final_norm_g, m_e_w_in, m_e_shift_mu, m_e_w_decay_up, m_e_w0, m_e_a_up, m_e_a0, m_e_k_k, m_e_k_a, m_e_r_k, m_e_gn_g, m_e_gn_b, m_e_w_out, m_o_w_in, m_o_ln_g, m_o_ln_b, m_o_w_s, m_o_b_s, m_o_w_out, v_norm_g, v_final_norm_g, v_e_w_in, v_e_shift_mu, v_e_w_decay_up, v_e_w0, v_e_a_up, v_e_a0, v_e_k_k, v_e_k_a, v_e_r_k, v_e_gn_g, v_e_gn_b, v_e_w_out, v_o_w_in, v_o_ln_g, v_o_ln_b, v_o_w_s, v_o_b_s, v_o_w_out):
    weights = dict(norm_g=norm_g, final_norm_g=final_norm_g, e_w_in=e_w_in, e_shift_mu=e_shift_mu,
                   e_w_decay_up=e_w_decay_up, e_w0=e_w0, e_a_up=e_a_up, e_a0=e_a0, e_k_k=e_k_k, e_k_a=e_k_a,
                   e_r_k=e_r_k, e_gn_g=e_gn_g, e_gn_b=e_gn_b, e_w_out=e_w_out, o_w_in=o_w_in, o_ln_g=o_ln_g,
                   o_ln_b=o_ln_b, o_w_s=o_w_s, o_b_s=o_b_s, o_w_out=o_w_out)
    mom1 = dict(norm_g=m_norm_g, final_norm_g=m_final_norm_g, e_w_in=m_e_w_in, e_shift_mu=m_e_shift_mu,
                e_w_decay_up=m_e_w_decay_up, e_w0=m_e_w0, e_a_up=m_e_a_up, e_a0=m_e_a0, e_k_k=m_e_k_k,
                e_k_a=m_e_k_a, e_r_k=m_e_r_k, e_gn_g=m_e_gn_g, e_gn_b=m_e_gn_b, e_w_out=m_e_w_out,
                o_w_in=m_o_w_in, o_ln_g=m_o_ln_g, o_ln_b=m_o_ln_b, o_w_s=m_o_w_s, o_b_s=m_o_b_s,
                o_w_out=m_o_w_out)
    mom2 = dict(norm_g=v_norm_g, final_norm_g=v_final_norm_g, e_w_in=v_e_w_in, e_shift_mu=v_e_shift_mu,
                e_w_decay_up=v_e_w_decay_up, e_w0=v_e_w0, e_a_up=v_e_a_up, e_a0=v_e_a0, e_k_k=v_e_k_k,
                e_k_a=v_e_k_a, e_r_k=v_e_r_k, e_gn_g=v_e_gn_g, e_gn_b=v_e_gn_b, e_w_out=v_e_w_out,
                o_w_in=v_o_w_in, o_ln_g=v_o_ln_g, o_ln_b=v_o_ln_b, o_w_s=v_o_w_s, o_b_s=v_o_b_s,
                o_w_out=v_o_w_out)
    names = list(weights)
    big = ("e_w_in", "e_w_out", "o_w_in", "o_w_out")

    bl, seq, d = x.shape
    t = bl * seq
    width = e_w0.shape[1]
    lora = e_w_decay_up.shape[1]
    n_sb = width // SB_HEAD
    me = 4 * lax.axis_index("x") + 2 * lax.axis_index("y") + lax.axis_index("c")

    e_win_t = _all_gather(e_w_in[0].T.astype(BF16), "gather_e_w_in").reshape(-1, d)
    later_shards = (e_w_out[0].astype(BF16), o_w_in[0].T.astype(BF16), o_w_out[0].astype(BF16))
    sharded_small = ("e_w_decay_up", "e_a_up", "o_ln_g", "o_ln_b")
    small_shapes = [weights[n][0].shape for n in sharded_small]
    got = _all_gather(_pack([weights[n][0] for n in sharded_small]), "gather_small")
    per_dev = [_unpack(got[dev], small_shapes) for dev in range(N_DEV)]
    wd, wa, ln_g, ln_b = [jnp.concatenate([per_dev[dev][i] for dev in range(N_DEV)], axis=-1).reshape(
        small_shapes[i][:-1] + (-1,)) for i in range(4)]
    ln_g, ln_b = ln_g.reshape(1, -1), ln_b.reshape(1, -1)
    e_ind, e_ind_t = _head_indicator(width)
    b_s3 = o_b_s[0][:, :, None]

    x2d = x.reshape(t, d)
    target = loss_target.reshape(t, d)
    cols_rwkv = 3 * width + 2 * lora
    assert cols_rwkv % LANES == 0 and width % LANES == 0
    sb0 = (cols_rwkv + width) // SB_HEAD
    sb_cols = (sb0, sb0 + n_sb, sb0 + 2 * n_sb, sb0 + 3 * n_sb)

    h0 = _rms_fwd(x2d, norm_g[0:1], "rms0_fwd")
    p = _matmul(h0, e_win_t, "nt", F32, "e_in_fwd", tm=2048, tn=1280)
    g_rwkv = p[:, cols_rwkv:cols_rwkv + width]
    r, w, kx, v, kk, b = _rwkv_prep(p, seq, (width, lora), e_shift_mu, wd, e_w0, wa, e_a0, e_k_k, e_k_a,
                                    e_ind, e_ind_t)
    sk = [_to_scan_k(a, bl, seq) for a in (kk, w, b, kx, r)]
    sv = _to_scan_v(v, bl, seq)
    ys_scan, states, sa_all, e_wout, o_wout = _scan_fwd(*sk, sv, gathers=(later_shards[0], later_shards[2]))
    ys = _from_scan_v(ys_scan, bl, seq)
    ya = _rwkv_post(ys, r, kx, v, g_rwkv, 0, e_gn_g, e_gn_b, e_r_k, e_ind, e_ind_t)
    yb, sb_o, sb_tot, o_win_t = _sb_fwd(p, bl, seq, sb_cols, n_sb, gathers=(later_shards[1],))
    e_wout, o_win_t, o_wout = (a.reshape(-1, d) for a in (e_wout, o_win_t, o_wout))
    y = jnp.concatenate([ya, yb], axis=1)
    x1 = _matmul(y, e_wout, "nn", F32, "e_out_fwd", res=x2d)
    h1 = _rms_fwd(x1, norm_g[1:2], "rms1_fwd")
    p2 = _matmul(h1, o_win_t, "nt", F32, "o_in_fwd", tm=2048)
    y2 = _sgu_fwd(p2, ln_g, ln_b, o_w_s[0], b_s3)
    x2 = _matmul(y2, o_wout, "nn", F32, "o_out_fwd", res=x1)
    dx2, d_final_g, loss_part = _final_loss(x2, final_norm_g.reshape(1, d), target, "final_loss")

    dy2 = _matmul(dx2, o_wout, "nt", F32, "o_out_bwd_x", tm=2048)
    d_o_wout = _matmul(y2, dx2, "tn", F32, "o_out_bwd_w", tn=2048)
    dp2, d_ln_g, d_ln_b, d_w_s, d_b_s3 = _sgu_bwd(p2, ln_g, ln_b, o_w_s[0], b_s3, dy2)
    dh1 = _matmul(dp2, o_win_t, "nn", F32, "o_in_bwd_x", tm=2048, tk=1536)
    d_o_win_t = _matmul(dp2, h1, "tn", F32, "o_in_bwd_w", tn=2048)
    dx1, d_g1 = _rms_bwd(x1, norm_g[1:2], dh1, dx2, "rms1_bwd")
    dy = _matmul(dx1, e_wout, "nt", F32, "e_out_bwd_x", tm=2048)
    d_e_wout = _matmul(y, dx1, "tn", F32, "e_out_bwd_w", tn=2048)
    core = lax.axis_index("c").astype(jnp.int32).reshape(1)
    by_owner = lambda full: full.reshape((N_CHIPS, 2, full.shape[0] // N_DEV, full.shape[1]))
    early = {"e_w_out": by_owner(d_e_wout), "o_w_in": by_owner(d_o_win_t), "o_w_out": by_owner(d_o_wout)}
    dq, dk, dv_sb, dg_sb, *swapped = _sb_bwd(p, bl, seq, sb_cols, n_sb, dy, n_sb, sb_o, sb_tot,
                                             swaps=tuple(early.values()))
    partials = [_pair_sum(full, got, core, "pairsum_" + n) for (n, full), got in zip(early.items(), swapped)]
    dys, dr1, dkx1, dv1, dg_rwkv, d_gn_g, d_gn_b, d_r_k = _rwkv_post_bwd(
        ys, r, kx, v, g_rwkv, 0, e_gn_g, e_gn_b, e_r_k, e_ind, e_ind_t, dy, 0)
    dkk_s, dw_s, db_s, dkx_s, dr_s, dv_s, *landed = _scan_bwd(*sk, sv, states, sa_all, _to_scan_v(dys, bl, seq),
                                                              scatters=tuple(partials))
    early_sums = {n: _sum_slabs(parts, "sum_" + n) for n, parts in zip(early, landed)}
    dkk, dw, db, dkx2, dr2 = [_from_scan_k(a, bl, seq) for a in (dkk_s, dw_s, db_s, dkx_s, dr_s)]
    dv2 = _from_scan_v(dv_s, bl, seq)
    dp_rwkv, d_mu, d_wd, d_w0, d_wa, d_a0, d_k_k, d_k_a = _rwkv_prep_bwd(
        p, seq, (width, lora), e_shift_mu, wd, e_w0, wa, e_a0, e_k_k, e_k_a, e_ind, e_ind_t,
        dr1, dw, dkx1, dv1, dkk, db, (dr2, dkx2, dv2))
    dp = jnp.concatenate([dp_rwkv, dg_rwkv, dq, dk, dv_sb, dg_sb], axis=1)
    last = by_owner(_matmul(dp, h0, "tn", BF16, "e_in_bwd_w", tm=1280, tn=2048))
    last = _pair_sum(last, _pair_swap(last, "swap_e_w_in"), core, "pairsum_e_w_in")
    dh0, last = _matmul(dp, e_win_t, "nn", F32, "e_in_bwd_x", tm=2048, tk=1280, scatters=(last,))
    grad_x, d_g0 = _rms_bwd(x2d, norm_g[0:1], dh0, dx1, "rms0_bwd")

    grads = {
        "e_w_in": _sum_slabs(last, "sum_e_w_in").T[None],
        "e_w_out": early_sums["e_w_out"][None],
        "o_w_in": early_sums["o_w_in"].T[None],
        "o_w_out": early_sums["o_w_out"][None],
    }
    small_full = {
        "norm_g": jnp.concatenate([d_g0, d_g1], axis=0), "final_norm_g": d_final_g.reshape(-1),
        "e_shift_mu": d_mu, "e_w_decay_up": d_wd[None], "e_w0": d_w0, "e_a_up": d_wa[None], "e_a0": d_a0,
        "e_k_k": d_k_k, "e_k_a": d_k_a, "e_r_k": d_r_k, "e_gn_g": d_gn_g, "e_gn_b": d_gn_b,
        "o_ln_g": d_ln_g, "o_ln_b": d_ln_b, "o_w_s": d_w_s[None], "o_b_s": d_b_s3[:, :, 0][None],
    }
    small = [n for n in names if n not in big]
    parts = _all_gather(_pack([small_full[n] for n in small]), "gather_small_grads")
    totals = _unpack(_sum_slabs(parts, "sum_small_grads"), [small_full[n].shape for n in small])
    for n, g in zip(small, totals):
        if n in sharded_small:
            size = weights[n].shape[-1]
            g = lax.dynamic_slice_in_dim(g, me * size, size, axis=g.ndim - 1)
        grads[n] = g.reshape(weights[n].shape)

    delta, new_m, new_v = {}, {}, {}
    for n in big:
        shp = weights[n].shape
        flat = lambda a: a.reshape(shp[-2], shp[-1])
        dl, mn, vn = _adamw(flat(weights[n]), flat(grads[n]), flat(mom1[n]), flat(mom2[n]), "adamw_" + n)
        delta[n], new_m[n], new_v[n] = dl.reshape(shp), mn.reshape(shp), vn.reshape(shp)
    packed = [_pack([src[n] for n in small]) for src in (weights, grads, mom1, mom2)]
    outs = _adamw(*packed, "adamw_small")
    shapes = [weights[n].shape for n in small]
    for dst, arr in zip((delta, new_m, new_v), outs):
        for n, a in zip(small, _unpack(arr, shapes)):
            dst[n] = a

    loss = lax.psum(loss_part[0, 0], ("x", "y", "c"))
    return (loss, grad_x.reshape(bl, seq, d), *[grads[n] for n in names], *[delta[n] for n in names],
            *[new_m[n] for n in names], *[new_v[n] for n in names])
```

```python
import functools

import jax
import jax.numpy as jnp
from jax import lax
from jax.experimental import pallas as pl
from jax.experimental.pallas import tpu as pltpu

F32 = jnp.float32
BF16 = jnp.bfloat16

N_DEV = 8
RWKV_HEAD = 64
SB_HEAD = 128
SB_BLOCK = 128
SGU_CHUNK = 128
SGU_GROUPS = 16
RMS_EPS = 1e-6
GN_EPS = 64e-5
LN_EPS = 1e-5
L2_EPS = 1e-12
ADAM_LR = 0.001
ADAM_B1 = 0.9
ADAM_B2 = 0.999
ADAM_EPS = 1e-08
ADAM_WD = 0.01
ADAM_STEP = 10

VMEM_LIMIT_V7X = 56 * 1024 * 1024
LANES = 128
SCAN_STEPS_PER_BLOCK = 16


def _params(*sem):
    return pltpu.CompilerParams(dimension_semantics=sem, vmem_limit_bytes=VMEM_LIMIT_V7X)


def _tile(n, target, mult):
    best = None
    d = mult
    while d <= min(n, target):
        if n % d == 0:
            best = d
        d += mult
    return n if best is None else best


def _remote(src, dst, send_sems, recv_sems, k, dev):
    return pltpu.make_async_remote_copy(src_ref=src, dst_ref=dst, send_sem=send_sems.at[k], recv_sem=recv_sems.at[k],
                                        device_id=dev, device_id_type=pl.DeviceIdType.MESH)


_HBM = pl.BlockSpec(memory_space=pl.ANY)


GATHER_COPIES = 7


def _gather_phases(src_ref, out_ref, send_sems, recv_sems, local_sems, n):
    x, y, c = lax.axis_index("x"), lax.axis_index("y"), lax.axis_index("c")
    me, sibling = (x, y, c), (x, y, 1 - c)
    chips = [(1 - x, y), (x, 1 - y), (1 - x, 1 - y)]

    def slot(px, py, pc):
        return out_ref.at[4 * px + 2 * py + pc]

    def copy(k, block, to, own=False):
        return _remote(src_ref if own else slot(*block), slot(*block), send_sems, recv_sems,
                       GATHER_COPIES * n + k, to)

    mine = pltpu.make_async_copy(src_ref, slot(*me), local_sems.at[n])
    first = [copy(0, me, sibling, True)] + [copy(1 + j, me, (*chip, c), True) for j, chip in enumerate(chips)]
    passed = [copy(4 + j, (*chip, c), sibling) for j, chip in enumerate(chips)]

    def start():
        mine.start()
        for cp in first:
            cp.start()

    def relay():
        for j, chip in enumerate(chips):
            copy(1 + j, (*chip, c), me).wait_recv()
            passed[j].start()

    def finish():
        copy(0, sibling, me).wait_recv()
        for j, chip in enumerate(chips):
            copy(4 + j, (*chip, 1 - c), me).wait_recv()
        for cp in first + passed:
            cp.wait_send()
        mine.wait()

    return start, relay, finish


def _gather_scratch(n):
    return [pltpu.SemaphoreType.DMA((GATHER_COPIES * n,)), pltpu.SemaphoreType.DMA((GATHER_COPIES * n,)),
            pltpu.SemaphoreType.DMA((n,))]


def _gathered_shape(src):
    return jax.ShapeDtypeStruct((N_DEV,) + tuple(src.shape), src.dtype)


def _all_gather(src, name):
    def body(src_ref, out_ref, send_sems, recv_sems, local_sems):
        for phase in _gather_phases(src_ref, out_ref, send_sems, recv_sems, local_sems, 0):
            phase()

    return pl.pallas_call(
        body, name=name, out_shape=_gathered_shape(src), in_specs=[_HBM], out_specs=_HBM,
        scratch_shapes=_gather_scratch(1),
    )(src)


N_CHIPS = N_DEV // 2
SCATTER_COPIES = N_CHIPS - 1


def _swap_phases(src_ref, out_ref, send_sems, recv_sems, n):
    x, y, c = lax.axis_index("x"), lax.axis_index("y"), lax.axis_index("c")
    copies = [_remote(src_ref.at[q, 1 - c], out_ref.at[q], send_sems, recv_sems, N_CHIPS * n + q, (x, y, 1 - c))
              for q in range(N_CHIPS)]

    def start():
        for cp in copies:
            cp.start()

    def finish():
        for cp in copies:
            cp.wait_recv()
        for cp in copies:
            cp.wait_send()

    return start, finish


def _swapped_shape(full):
    return jax.ShapeDtypeStruct((N_CHIPS,) + tuple(full.shape[2:]), full.dtype)


def _swap_scratch(n):
    return [pltpu.SemaphoreType.DMA((N_CHIPS * n,)), pltpu.SemaphoreType.DMA((N_CHIPS * n,))]


def _pair_swap(full, name):
    def body(src_ref, out_ref, send_sems, recv_sems):
        for phase in _swap_phases(src_ref, out_ref, send_sems, recv_sems, 0):
            phase()

    return pl.pallas_call(body, name=name, out_shape=_swapped_shape(full), in_specs=[_HBM], out_specs=_HBM,
                          scratch_shapes=_swap_scratch(1))(full)


def _pair_sum(full, got, core, name):
    n_chips, _, rows, cols = full.shape
    tr = _tile(rows, max(16, (1 << 19) // cols), 16)

    def body(core_ref, a_ref, b_ref, o_ref):
        o_ref[...] = (a_ref[...].astype(F32) + b_ref[...].astype(F32)).astype(BF16)

    return pl.pallas_call(
        body, name=name,
        grid_spec=pltpu.PrefetchScalarGridSpec(
            num_scalar_prefetch=1, grid=(n_chips, rows // tr),
            in_specs=[pl.BlockSpec((None, None, tr, cols), lambda q, i, s: (q, s[0], i, 0)),
                      pl.BlockSpec((None, tr, cols), lambda q, i, s: (q, i, 0))],
            out_specs=pl.BlockSpec((None, tr, cols), lambda q, i, s: (q, i, 0))),
        out_shape=jax.ShapeDtypeStruct((n_chips, rows, cols), BF16),
        compiler_params=_params("parallel", "parallel"),
    )(core, full, got)


def _scatter_phases(src_ref, out_ref, send_sems, recv_sems, local_sems, n):
    x, y, c = lax.axis_index("x"), lax.axis_index("y"), lax.axis_index("c")
    here = 2 * x + y
    chips = [(1 - x, y), (x, 1 - y), (1 - x, 1 - y)]
    local = pltpu.make_async_copy(src_ref.at[here], out_ref.at[here], local_sems.at[n])
    sends = [_remote(src_ref.at[2 * px + py], out_ref.at[here], send_sems, recv_sems, SCATTER_COPIES * n + j,
                     (px, py, c)) for j, (px, py) in enumerate(chips)]
    recvs = [_remote(src_ref.at[2 * px + py], out_ref.at[2 * px + py], send_sems, recv_sems,
                     SCATTER_COPIES * n + j, (px, py, c)) for j, (px, py) in enumerate(chips)]

    def start():
        local.start()
        for cp in sends:
            cp.start()

    def finish():
        for cp in recvs:
            cp.wait_recv()
        for cp in sends:
            cp.wait_send()
        local.wait()

    return start, finish


def _scatter_scratch(n):
    return [pltpu.SemaphoreType.DMA((SCATTER_COPIES * n,)), pltpu.SemaphoreType.DMA((SCATTER_COPIES * n,)),
            pltpu.SemaphoreType.DMA((n,))]


def _chip_scatter(parts, name):
    def body(src_ref, out_ref, send_sems, recv_sems, local_sems):
        for phase in _scatter_phases(src_ref, out_ref, send_sems, recv_sems, local_sems, 0):
            phase()

    return pl.pallas_call(body, name=name, out_shape=jax.ShapeDtypeStruct(parts.shape, parts.dtype),
                          in_specs=[_HBM], out_specs=_HBM, scratch_shapes=_scatter_scratch(1))(parts)


def _matmul(a, b, mode, out_dtype, name, res=None, tm=1024, tn=1024, tk=1024, scatters=()):
    if mode == "nn":
        (m, k), (k2, n) = a.shape, b.shape
    elif mode == "nt":
        (m, k), (n, k2) = a.shape, b.shape
    else:
        (k, m), (k2, n) = a.shape, b.shape
    assert k == k2, (a.shape, b.shape, mode)
    tm, tn, tk = _tile(m, tm, 128), _tile(n, tn, 128), _tile(k, tk, 128)
    nk = k // tk
    if mode == "nn":
        a_spec = pl.BlockSpec((tm, tk), lambda i, j, kk: (i, kk))
        b_spec = pl.BlockSpec((tk, tn), lambda i, j, kk: (kk, j))
        dims = (((1,), (0,)), ((), ()))
    elif mode == "nt":
        a_spec = pl.BlockSpec((tm, tk), lambda i, j, kk: (i, kk))
        b_spec = pl.BlockSpec((tn, tk), lambda i, j, kk: (j, kk))
        dims = (((1,), (1,)), ((), ()))
    else:
        a_spec = pl.BlockSpec((tk, tm), lambda i, j, kk: (kk, i))
        b_spec = pl.BlockSpec((tk, tn), lambda i, j, kk: (kk, j))
        dims = (((0,), (0,)), ((), ()))
    o_spec = pl.BlockSpec((tm, tn), lambda i, j, kk: (i, j))
    has_res = res is not None
    n_in = 3 if has_res else 2
    nsc = len(scatters)
    grid = (m // tm, n // tn, nk)

    def body(*refs):
        a_ref, b_ref = refs[:2]
        r_ref = refs[2] if has_res else None
        o_ref = refs[n_in + nsc]
        acc_ref = refs[n_in + 2 * nsc + 1]
        kk = pl.program_id(2)

        if nsc:
            send_sems, recv_sems, local_sems = refs[n_in + 2 * nsc + 2:]
            phases = [_scatter_phases(refs[n_in + s], refs[n_in + nsc + 1 + s], send_sems, recv_sems, local_sems, s)
                      for s in range(nsc)]
            for k, at in enumerate(((0, 0, 0), tuple(g - 1 for g in grid))):
                @pl.when((pl.program_id(0) == at[0]) & (pl.program_id(1) == at[1]) & (kk == at[2]))
                def _(k=k):
                    for ph in phases:
                        ph[k]()

        def product():
            return lax.dot_general(a_ref[...].astype(BF16), b_ref[...].astype(BF16), dims,
                                   preferred_element_type=F32)

        @pl.when(kk == 0)
        def _():
            acc_ref[...] = product()

        @pl.when(kk > 0)
        def _():
            acc_ref[...] += product()

        @pl.when(kk == nk - 1)
        def _():
            out = acc_ref[...]
            if has_res:
                out = out + r_ref[...]
            o_ref[...] = out.astype(out_dtype)

    ins = [a, b] + ([res] if has_res else [])
    specs = [a_spec, b_spec] + ([o_spec] if has_res else [])
    out = jax.ShapeDtypeStruct((m, n), out_dtype)
    if not nsc:
        return pl.pallas_call(
            body, name=name, grid=grid, out_shape=out, in_specs=specs, out_specs=o_spec,
            scratch_shapes=[pltpu.VMEM((tm, tn), F32)],
            compiler_params=_params("parallel", "parallel", "arbitrary"),
        )(*ins)
    return pl.pallas_call(
        body, name=name, grid=grid,
        out_shape=(out, *[jax.ShapeDtypeStruct(s.shape, s.dtype) for s in scatters]),
        in_specs=specs + [_HBM] * nsc, out_specs=(o_spec, *[_HBM] * nsc),
        scratch_shapes=[pltpu.VMEM((tm, tn), F32)] + _scatter_scratch(nsc),
        compiler_params=_params("arbitrary", "arbitrary", "arbitrary"),
    )(*ins, *scatters)


def _matmul_panel(a, b, name, tm=1024, tn=1280):
    (m, k), (n, k2) = a.shape, b.shape
    assert k == k2
    tm, tn = _tile(m, tm, 128), _tile(n, tn, 128)

    def body(a_ref, b_ref, o_ref):
        o_ref[...] = lax.dot_general(a_ref[...], b_ref[...], _NT_DIMS, preferred_element_type=F32)

    return pl.pallas_call(
        body, name=name, grid=(m // tm, n // tn),
        out_shape=jax.ShapeDtypeStruct((m, n), F32),
        in_specs=[pl.BlockSpec((tm, k), lambda i, j: (i, 0)), pl.BlockSpec((tn, k), lambda i, j: (j, 0))],
        out_specs=pl.BlockSpec((tm, tn), lambda i, j: (i, j)),
        compiler_params=_params("parallel", "arbitrary"),
    )(a, b)


_NT_DIMS = (((1,), (1,)), ((), ()))


def _rms_fwd(x, g, name):
    t, d = x.shape
    tm = _tile(t, 256, 8)

    def body(x_ref, g_ref, h_ref):
        xv = x_ref[...]
        rstd = lax.rsqrt(jnp.mean(xv * xv, axis=-1, keepdims=True) + RMS_EPS)
        h_ref[...] = (xv * rstd * g_ref[...]).astype(BF16)

    return pl.pallas_call(
        body, name=name, grid=(t // tm,),
        out_shape=jax.ShapeDtypeStruct((t, d), BF16),
        in_specs=[pl.BlockSpec((tm, d), lambda i: (i, 0)), pl.BlockSpec((1, d), lambda i: (0, 0))],
        out_specs=pl.BlockSpec((tm, d), lambda i: (i, 0)),
        compiler_params=_params("parallel"),
    )(x, g)


def _rms_bwd(x, g, dh, dres, name):
    t, d = x.shape
    tm = _tile(t, 256, 8)

    def body(x_ref, g_ref, dh_ref, dres_ref, dx_ref, dg_ref):
        @pl.when(pl.program_id(0) == 0)
        def _():
            dg_ref[...] = jnp.zeros_like(dg_ref)

        xv = x_ref[...]
        rstd = lax.rsqrt(jnp.mean(xv * xv, axis=-1, keepdims=True) + RMS_EPS)
        xhat = xv * rstd
        dh_v = dh_ref[...]
        dg_ref[...] += jnp.sum(dh_v * xhat, axis=0, keepdims=True)
        dxh = dh_v * g_ref[...]
        dx_ref[...] = dres_ref[...] + rstd * (dxh - xhat * jnp.mean(dxh * xhat, axis=-1, keepdims=True))

    row = pl.BlockSpec((tm, d), lambda i: (i, 0))
    vec = pl.BlockSpec((1, d), lambda i: (0, 0))
    return pl.pallas_call(
        body, name=name, grid=(t // tm,),
        out_shape=(jax.ShapeDtypeStruct((t, d), F32), jax.ShapeDtypeStruct((1, d), F32)),
        in_specs=[row, vec, row, row], out_specs=(row, vec),
        compiler_params=_params("arbitrary"),
    )(x, g, dh, dres)


def _final_loss(x, g, target, name):
    t, d = x.shape
    tm = _tile(t, 256, 8)

    def body(x_ref, g_ref, t_ref, dx_ref, dg_ref, loss_ref):
        @pl.when(pl.program_id(0) == 0)
        def _():
            dg_ref[...] = jnp.zeros_like(dg_ref)
            loss_ref[...] = jnp.zeros_like(loss_ref)

        xv = x_ref[...]
        rstd = lax.rsqrt(jnp.mean(xv * xv, axis=-1, keepdims=True) + RMS_EPS)
        xhat = xv * rstd
        gv = g_ref[...]
        err = xhat * gv - t_ref[...]
        loss_ref[...] += 0.5 * jnp.sum(jnp.mean(err * err, axis=-1, keepdims=True), axis=0, keepdims=True)
        dout = err * (1.0 / d)
        dg_ref[...] += jnp.sum(dout * xhat, axis=0, keepdims=True)
        dxh = dout * gv
        dx_ref[...] = rstd * (dxh - xhat * jnp.mean(dxh * xhat, axis=-1, keepdims=True))

    row = pl.BlockSpec((tm, d), lambda i: (i, 0))
    vec = pl.BlockSpec((1, d), lambda i: (0, 0))
    return pl.pallas_call(
        body, name=name, grid=(t // tm,),
        out_shape=(jax.ShapeDtypeStruct((t, d), F32), jax.ShapeDtypeStruct((1, d), F32),
                   jax.ShapeDtypeStruct((1, 1), F32)),
        in_specs=[row, vec, row], out_specs=(row, vec, pl.BlockSpec((1, 1), lambda i: (0, 0))),
        compiler_params=_params("arbitrary"),
    )(x, g, target)


def _dot32(a, b):
    return jnp.dot(a, b, precision=lax.Precision.HIGH, preferred_element_type=F32)


def _head_sum(v, e, et):
    return _dot32(_dot32(v, e), et)


def _log_sigmoid(z):
    return jnp.minimum(z, 0.0) - jnp.log1p(jnp.exp(-jnp.abs(z)))


def _silu(g):
    return g * jax.nn.sigmoid(g)


def _prep_math(k, wlo, alo, wd, w0, wa, a0, k_k, k_a, e, et):
    wl = w0 + _dot32(jnp.tanh(wlo), wd)
    w_log = _log_sigmoid(wl) - 0.5
    w = jnp.exp(-jnp.exp(w_log))
    a = jax.nn.sigmoid(a0 + _dot32(alo, wa))
    kk0 = k * k_k
    kk = kk0 * lax.rsqrt(jnp.maximum(_head_sum(kk0 * kk0, e, et), L2_EPS * L2_EPS))
    kx = k * (1.0 + (a - 1.0) * k_a)
    return w, kx, kk, kk * a


def _post_math(ys, r, kx, v, g, gn_g, gn_b, r_k, e, et):
    inv = 1.0 / RWKV_HEAD
    mu = _head_sum(ys, e, et) * inv
    dlt = ys - mu
    var = _head_sum(dlt * dlt, e, et) * inv
    y = dlt * lax.rsqrt(var + GN_EPS) * gn_g + gn_b
    bonus = _head_sum(r * kx * r_k, e, et) * v
    return (y + bonus) * _silu(g)


def _shifted(p, prev_row, first):
    rows = lax.broadcasted_iota(jnp.int32, p.shape, 0)
    prev = jnp.where(first, 0.0, prev_row)
    return jnp.where(rows == 0, prev, pltpu.roll(p, 1, 0))


def _head_indicator(width):
    ch = lax.broadcasted_iota(jnp.int32, (width, width // RWKV_HEAD), 0) // RWKV_HEAD
    hd = lax.broadcasted_iota(jnp.int32, (width, width // RWKV_HEAD), 1)
    e = (ch == hd).astype(F32)
    return e, e.T


def _rwkv_prep(p, seq, dims, mu, wd, w0, wa, a0, k_k, k_a, e, et):
    t = p.shape[0]
    width, lora = dims
    cols = 3 * width + 2 * lora
    tm = 128
    per_seq = seq // tm

    def body(p_ref, prev_ref, mu_ref, wd_ref, w0_ref, wa_ref, a0_ref, kk_ref, ka_ref, e_ref, et_ref,
             r_out, w_out, kx_out, v_out, kkn_out, b_out):
        i = pl.program_id(0)
        pv = p_ref[...]
        psh = _shifted(pv, prev_ref[7:8, :], i % per_seq == 0)
        ps = pv + mu_ref[...] * (psh - pv)
        r, k, v = ps[:, :width], ps[:, width:2 * width], ps[:, 2 * width:3 * width]
        wlo, alo = ps[:, 3 * width:3 * width + lora], ps[:, 3 * width + lora:]
        w, kx, kk, b = _prep_math(k, wlo, alo, wd_ref[...], w0_ref[...], wa_ref[...], a0_ref[...],
                                  kk_ref[...], ka_ref[...], e_ref[...], et_ref[...])
        r_out[...] = r
        w_out[...] = w
        kx_out[...] = kx
        v_out[...] = v
        kkn_out[...] = kk
        b_out[...] = b

    full = lambda a: pl.BlockSpec(a.shape, lambda i: (0,) * a.ndim)
    out = pl.BlockSpec((tm, width), lambda i: (i, 0))
    return pl.pallas_call(
        body, name="rwkv_prep", grid=(t // tm,),
        out_shape=tuple(jax.ShapeDtypeStruct((t, width), F32) for _ in range(6)),
        in_specs=[pl.BlockSpec((tm, cols), lambda i: (i, 0)),
                  pl.BlockSpec((8, cols), lambda i: (jnp.maximum(i * (tm // 8) - 1, 0), 0)),
                  full(mu), full(wd), full(w0), full(wa), full(a0), full(k_k), full(k_a), full(e), full(et)],
        out_specs=tuple(out for _ in range(6)),
        compiler_params=_params("parallel"),
    )(p, p, mu, wd, w0, wa, a0, k_k, k_a, e, et)


def _rwkv_prep_bwd(p, seq, dims, mu, wd, w0, wa, a0, k_k, k_a, e, et, dr, dw, dkx, dv, dkk, db, more):
    t = p.shape[0]
    width, lora = dims
    cols = 3 * width + 2 * lora
    tm = 128
    n_tiles = t // tm
    per_seq = seq // tm

    def body(p_ref, prev_ref, mu_ref, wd_ref, w0_ref, wa_ref, a0_ref, kk_ref, ka_ref, e_ref, et_ref,
             dr_ref, dw_ref, dkx_ref, dv_ref, dkk_ref, db_ref, dr2_ref, dkx2_ref, dv2_ref,
             dp_out, dmu_out, dwd_out, dw0_out, dwa_out, da0_out, dkk_out, dka_out, carry):
        step = pl.program_id(0)
        i = n_tiles - 1 - step

        @pl.when(step == 0)
        def _():
            for ref in (dmu_out, dwd_out, dw0_out, dwa_out, da0_out, dkk_out, dka_out, carry):
                ref[...] = jnp.zeros_like(ref)

        pv = p_ref[...]
        first = i % per_seq == 0
        psh = _shifted(pv, prev_ref[7:8, :], first)
        muv = mu_ref[...]
        ps = pv + muv * (psh - pv)
        k = ps[:, width:2 * width]
        wlo, alo = ps[:, 3 * width:3 * width + lora], ps[:, 3 * width + lora:]
        ev, etv = e_ref[...], et_ref[...]
        _, vjp = jax.vjp(lambda *a: _prep_math(*a, ev, etv), k, wlo, alo, wd_ref[...], w0_ref[...],
                         wa_ref[...], a0_ref[...], kk_ref[...], ka_ref[...])
        dk, dwlo, dalo, dwd, dw0, dwa, da0, dk_k, dk_a = vjp(
            (dw_ref[...], dkx_ref[...] + dkx2_ref[...], dkk_ref[...], db_ref[...]))
        dps = jnp.concatenate([dr_ref[...] + dr2_ref[...], dk, dv_ref[...] + dv2_ref[...], dwlo, dalo], axis=1)
        dmu_out[...] += jnp.sum(dps * (psh - pv), axis=0, keepdims=True)
        dwd_out[...] += dwd
        dw0_out[...] += dw0
        dwa_out[...] += dwa
        da0_out[...] += da0
        dkk_out[...] += dk_k
        dka_out[...] += dk_a
        dsh = dps * muv
        rows = lax.broadcasted_iota(jnp.int32, dsh.shape, 0)
        nxt = jnp.where(rows == tm - 1, carry[...], pltpu.roll(dsh, tm - 1, 0))
        dp_out[...] = (dps * (1.0 - muv) + nxt).astype(BF16)
        carry[...] = jnp.where(first, 0.0, dsh[0:1, :])

    full = lambda a: pl.BlockSpec(a.shape, lambda s: (0,) * a.ndim)
    tok = pl.BlockSpec((tm, width), lambda s: (n_tiles - 1 - s, 0))
    vec = lambda n: jax.ShapeDtypeStruct((1, n), F32)
    outs = (jax.ShapeDtypeStruct((t, cols), BF16), vec(cols), jax.ShapeDtypeStruct(wd.shape, F32), vec(width),
            jax.ShapeDtypeStruct(wa.shape, F32), vec(width), vec(width), vec(width))
    return pl.pallas_call(
        body, name="rwkv_prep_bwd", grid=(n_tiles,),
        out_shape=outs,
        in_specs=[pl.BlockSpec((tm, cols), lambda s: (n_tiles - 1 - s, 0)),
                  pl.BlockSpec((8, cols), lambda s: (jnp.maximum((n_tiles - 1 - s) * (tm // 8) - 1, 0), 0)),
                  full(mu), full(wd), full(w0), full(wa), full(a0), full(k_k), full(k_a), full(e), full(et),
                  tok, tok, tok, tok, tok, tok, tok, tok, tok],
        out_specs=(pl.BlockSpec((tm, cols), lambda s: (n_tiles - 1 - s, 0)),) + tuple(
            pl.BlockSpec(o.shape, lambda s: (0, 0)) for o in outs[1:]),
        scratch_shapes=[pltpu.VMEM((1, cols), F32)],
        compiler_params=_params("arbitrary"),
    )(p, p, mu, wd, w0, wa, a0, k_k, k_a, e, et, dr, dw, dkx, dv, dkk, db, *more)


def _rwkv_post(ys, r, kx, v, p, g_col, gn_g, gn_b, r_k, e, et):
    t, width = ys.shape
    tm = 256

    def body(ys_ref, r_ref, kx_ref, v_ref, g_ref, gg_ref, gb_ref, rk_ref, e_ref, et_ref, out_ref):
        out_ref[...] = _post_math(ys_ref[...], r_ref[...], kx_ref[...], v_ref[...], g_ref[...], gg_ref[...],
                                  gb_ref[...], rk_ref[...], e_ref[...], et_ref[...]).astype(BF16)

    tok = pl.BlockSpec((tm, width), lambda i: (i, 0))
    full = lambda a: pl.BlockSpec(a.shape, lambda i: (0,) * a.ndim)
    return pl.pallas_call(
        body, name="rwkv_post", grid=(t // tm,),
        out_shape=jax.ShapeDtypeStruct((t, width), BF16),
        in_specs=[tok, tok, tok, tok, pl.BlockSpec((tm, width), lambda i: (i, g_col)),
                  full(gn_g), full(gn_b), full(r_k), full(e), full(et)],
        out_specs=tok,
        compiler_params=_params("parallel"),
    )(ys, r, kx, v, p, gn_g, gn_b, r_k, e, et)


def _rwkv_post_bwd(ys, r, kx, v, p, g_col, gn_g, gn_b, r_k, e, et, dy, dy_col):
    t, width = ys.shape
    tm = 128

    def body(ys_ref, r_ref, kx_ref, v_ref, g_ref, gg_ref, gb_ref, rk_ref, e_ref, et_ref, dy_ref,
             dys_out, dr_out, dkx_out, dv_out, dg_out, dgg_out, dgb_out, drk_out):
        @pl.when(pl.program_id(0) == 0)
        def _():
            for ref in (dgg_out, dgb_out, drk_out):
                ref[...] = jnp.zeros_like(ref)

        ev, etv = e_ref[...], et_ref[...]
        _, vjp = jax.vjp(lambda *a: _post_math(*a, ev, etv), ys_ref[...], r_ref[...], kx_ref[...], v_ref[...],
                         g_ref[...], gg_ref[...], gb_ref[...], rk_ref[...])
        dys, dr, dkx, dv, dg, dgg, dgb, drk = vjp(dy_ref[...])
        dys_out[...] = dys
        dr_out[...] = dr
        dkx_out[...] = dkx
        dv_out[...] = dv
        dg_out[...] = dg.astype(BF16)
        dgg_out[...] += dgg
        dgb_out[...] += dgb
        drk_out[...] += drk

    tok = pl.BlockSpec((tm, width), lambda i: (i, 0))
    full = lambda a: pl.BlockSpec(a.shape, lambda i: (0,) * a.ndim)
    big = jax.ShapeDtypeStruct((t, width), F32)
    vec = jax.ShapeDtypeStruct((1, width), F32)
    vspec = pl.BlockSpec((1, width), lambda i: (0, 0))
    return pl.pallas_call(
        body, name="rwkv_post_bwd", grid=(t // tm,),
        out_shape=(big, big, big, big, jax.ShapeDtypeStruct((t, width), BF16), vec, vec, vec),
        in_specs=[tok, tok, tok, tok, pl.BlockSpec((tm, width), lambda i: (i, g_col)),
                  full(gn_g), full(gn_b), full(r_k), full(e), full(et),
                  pl.BlockSpec((tm, width), lambda i: (i, dy_col))],
        out_specs=(tok, tok, tok, tok, tok, vspec, vspec, vspec),
        compiler_params=_params("arbitrary"),
    )(ys, r, kx, v, p, gn_g, gn_b, r_k, e, et, dy)


def _to_scan_k(a, bl, seq):
    h = a.shape[1] // RWKV_HEAD
    twice = jnp.broadcast_to(a.reshape(1, bl, seq, h, RWKV_HEAD), (2, bl, seq, h, RWKV_HEAD))
    return twice.transpose(2, 4, 0, 1, 3).reshape(seq, RWKV_HEAD, 2 * bl * h)


def _to_scan_v(a, bl, seq):
    h = a.shape[1] // RWKV_HEAD
    half = RWKV_HEAD // 2
    return a.reshape(bl, seq, h, 2, half).transpose(1, 4, 3, 0, 2).reshape(seq, half, 2 * bl * h)


def _from_scan_k(a, bl, seq):
    h = a.shape[2] // (2 * bl)
    a = a[:, :, :bl * h].reshape(seq, RWKV_HEAD, bl, h).transpose(2, 0, 3, 1)
    return a.reshape(bl * seq, h * RWKV_HEAD)


def _from_scan_v(a, bl, seq):
    half = RWKV_HEAD // 2
    h = a.shape[2] // (2 * bl)
    a = a.reshape(seq, half, 2, bl, h).transpose(3, 0, 4, 2, 1)
    return a.reshape(bl * seq, h * RWKV_HEAD)


def _scan_fwd(kk, w, b, kx, r, v, gathers=()):
    seq, nk, lanes = kk.shape
    nv = v.shape[1]
    tt = SCAN_STEPS_PER_BLOCK
    nblk = seq // tt
    ng = len(gathers)

    def body(*refs):
        kk_ref, w_ref, b_ref, kx_ref, r_ref, v_ref = refs[:6]
        g_src = refs[6:6 + ng]
        y_ref, st_ref, sa_ref = refs[6 + ng:9 + ng]
        g_out = refs[9 + ng:9 + 2 * ng]
        s_ref = refs[9 + 2 * ng]
        pid = pl.program_id(0)

        @pl.when(pid == 0)
        def _():
            s_ref[...] = jnp.zeros_like(s_ref)

        if ng:
            send_sems, recv_sems, local_sems = refs[10 + 2 * ng:]
            phases = [_gather_phases(g_src[n], g_out[n], send_sems, recv_sems, local_sems, n) for n in range(ng)]
            for k, at in enumerate((0, (3 * nblk) // 4, nblk - 1)):
                @pl.when(pid == at)
                def _(k=k):
                    for ph in phases:
                        ph[k]()

        def step(i, carry):
            kkv, wv, bv, kxv, rv = kk_ref[i], w_ref[i], b_ref[i], kx_ref[i], r_ref[i]
            for j in range(nv):
                s_old = s_ref[j]
                sa = -jnp.sum(s_old * kkv, axis=0, keepdims=True)
                s_new = s_old * wv + sa * bv + v_ref[i, j:j + 1, :] * kxv
                s_ref[j] = s_new
                st_ref[i, j] = s_new
                sa_ref[i, j:j + 1, :] = sa
                y_ref[i, j:j + 1, :] = jnp.sum(s_new * rv, axis=0, keepdims=True)
            return carry

        lax.fori_loop(0, tt, step, 0)

    krow = pl.BlockSpec((tt, nk, lanes), lambda i: (i, 0, 0))
    vrow = pl.BlockSpec((tt, nv, lanes), lambda i: (i, 0, 0))
    vshape = jax.ShapeDtypeStruct((seq, nv, lanes), F32)
    return pl.pallas_call(
        body, name="rwkv_scan_fwd", grid=(nblk,),
        out_shape=(vshape, jax.ShapeDtypeStruct((seq, nv, nk, lanes), F32), vshape,
                   *[_gathered_shape(g) for g in gathers]),
        in_specs=[krow, krow, krow, krow, krow, vrow] + [_HBM] * ng,
        out_specs=(vrow, pl.BlockSpec((tt, nv, nk, lanes), lambda i: (i, 0, 0, 0)), vrow, *[_HBM] * ng),
        scratch_shapes=[pltpu.VMEM((nv, nk, lanes), F32)] + (_gather_scratch(ng) if ng else []),
        compiler_params=_params("arbitrary"),
    )(kk, w, b, kx, r, v, *gathers)


def _scan_bwd(kk, w, b, kx, r, v, states, sa_all, dy, scatters=()):
    seq, nk, lanes = kk.shape
    nv = v.shape[1]
    tt = SCAN_STEPS_PER_BLOCK
    nblk = seq // tt
    nsc = len(scatters)

    def both_halves(a):
        return a + pltpu.roll(a, lanes // 2, 1)

    def body(*refs):
        kk_ref, w_ref, b_ref, kx_ref, r_ref, v_ref, st_ref, before_ref, sa_ref, dy_ref = refs[:10]
        dkk_ref, dw_ref, db_ref, dkx_ref, dr_ref, dv_ref = refs[10 + nsc:16 + nsc]
        g_ref = refs[16 + 2 * nsc]

        if nsc:
            send_sems, recv_sems, local_sems = refs[17 + 2 * nsc:]
            phases = [_scatter_phases(refs[10 + n], refs[16 + nsc + n], send_sems, recv_sems, local_sems, n)
                      for n in range(nsc)]
            for k, at in enumerate((0, nblk - 1)):
                @pl.when(pl.program_id(0) == at)
                def _(k=k):
                    for ph in phases:
                        ph[k]()

        @pl.when(pl.program_id(0) == 0)
        def _():
            g_ref[...] = jnp.zeros_like(g_ref)

        def one_step(i, state_before):
            kkv, wv, bv, kxv, rv = kk_ref[i], w_ref[i], b_ref[i], kx_ref[i], r_ref[i]
            zero = jnp.zeros((nk, lanes), F32)
            a_r, a_w, a_b, a_kx, a_kk = zero, zero, zero, zero, zero
            for j in range(nv):
                s_old = state_before(j)
                s_new = st_ref[i, j]
                vv = v_ref[i, j:j + 1, :]
                dyv = dy_ref[i, j:j + 1, :]
                sa = sa_ref[i, j:j + 1, :]
                g = g_ref[j] + dyv * rv
                a_r = a_r + s_new * dyv
                a_w = a_w + g * s_old
                dsa = jnp.sum(g * bv, axis=0, keepdims=True)
                a_b = a_b + g * sa
                dv_ref[i, j:j + 1, :] = jnp.sum(g * kxv, axis=0, keepdims=True)
                a_kx = a_kx + g * vv
                a_kk = a_kk + s_old * dsa
                g_ref[j] = g * wv - dsa * kkv
            dr_ref[i] = both_halves(a_r)
            dw_ref[i] = both_halves(a_w)
            db_ref[i] = both_halves(a_b)
            dkx_ref[i] = both_halves(a_kx)
            dkk_ref[i] = -both_halves(a_kk)

        def step(n, carry):
            i = tt - 1 - n
            one_step(i, lambda j: st_ref[i - 1, j])
            return carry

        lax.fori_loop(0, tt - 1, step, 0)
        at_start = pl.program_id(0) == nblk - 1
        one_step(0, lambda j: jnp.where(at_start, 0.0, before_ref[0, j]))

    rev = lambda i: nblk - 1 - i
    krow = pl.BlockSpec((tt, nk, lanes), lambda i: (rev(i), 0, 0))
    vrow = pl.BlockSpec((tt, nv, lanes), lambda i: (rev(i), 0, 0))
    kshape = jax.ShapeDtypeStruct((seq, nk, lanes), F32)
    return pl.pallas_call(
        body, name="rwkv_scan_bwd", grid=(nblk,),
        out_shape=(kshape, kshape, kshape, kshape, kshape, jax.ShapeDtypeStruct((seq, nv, lanes), F32),
                   *[jax.ShapeDtypeStruct(s.shape, s.dtype) for s in scatters]),
        in_specs=[krow, krow, krow, krow, krow, vrow,
                  pl.BlockSpec((tt, nv, nk, lanes), lambda i: (rev(i), 0, 0, 0)),
                  pl.BlockSpec((1, nv, nk, lanes), lambda i: (jnp.maximum(rev(i) * tt - 1, 0), 0, 0, 0)),
                  vrow, vrow] + [_HBM] * nsc,
        out_specs=(krow, krow, krow, krow, krow, vrow, *[_HBM] * nsc),
        scratch_shapes=[pltpu.VMEM((nv, nk, lanes), F32)] + (_scatter_scratch(nsc) if nsc else []),
        compiler_params=_params("arbitrary"),
    )(kk, w, b, kx, r, v, states, states, sa_all, dy, *scatters)


_NT = (((1,), (1,)), ((), ()))
_TN = (((0,), (0,)), ((), ()))
SB_SCALE = 1.0 / (SB_HEAD ** 0.5)


SB_QUERY_ROWS = 256
SB_FWD_QUERY_ROWS = 512


def _sb_masks(qr):
    blk = SB_BLOCK
    row = lax.broadcasted_iota(jnp.int32, (qr, blk), 0)
    col = lax.broadcasted_iota(jnp.int32, (qr, blk), 1)
    j_in = lax.broadcasted_iota(jnp.int32, (blk, blk), 0)
    s_in = lax.broadcasted_iota(jnp.int32, (blk, blk), 1)
    ones = jnp.ones((blk, blk), BF16)
    fwd = jnp.concatenate([(j_in > s_in).astype(BF16), ones], axis=1)
    bwd = jnp.concatenate([(s_in > j_in).astype(BF16), ones], axis=1)
    return row, col, fwd, bwd


def _split_dots(xs, b):
    his = [a.astype(BF16) for a in xs]
    los = [(a - hi.astype(F32)).astype(BF16) for a, hi in zip(xs, his)]
    tops = [jnp.dot(hi, b, preferred_element_type=F32) for hi in his]
    return [top + jnp.dot(lo, b, preferred_element_type=F32) for top, lo in zip(tops, los)]


SB_HEADS_PER_STEP = 2


def _sb_chains(bl):
    return [(b, slice(hh * SB_HEAD, (hh + 1) * SB_HEAD)) for b in range(bl) for hh in range(SB_HEADS_PER_STEP)]


def _sb_specs(bl, seq, cols, n_heads, rows):
    hp = SB_HEADS_PER_STEP
    qr = min(rows, seq)
    assert n_heads % hp == 0 and all(c % hp == 0 for c in cols) and seq % qr == 0 and qr % SB_BLOCK == 0
    qspec = lambda col: pl.BlockSpec((bl, qr, hp * SB_HEAD), lambda h, i: (0, i, col // hp + h))
    kspec = lambda col: pl.BlockSpec((bl, seq, hp * SB_HEAD), lambda h, i: (0, 0, col // hp + h))
    return qr, qspec, kspec


def _sb_fwd(p, bl, seq, cols, n_heads, gathers=()):
    t = p.shape[0]
    q_col, k_col, v_col, g_col = cols
    blk = SB_BLOCK
    chains = _sb_chains(bl)
    qr, qspec, kspec = _sb_specs(bl, seq, cols, n_heads, SB_FWD_QUERY_ROWS)
    nq = seq // qr
    per_tile = qr // blk

    ng = len(gathers)
    n_groups = n_heads // SB_HEADS_PER_STEP
    n_steps = n_groups * nq

    def body(*refs):
        q_ref, k_ref, v_ref, g_ref = refs[:4]
        out_ref, o_ref, tot_ref = refs[4 + ng:7 + ng]
        qi = pl.program_id(1)

        if ng:
            send_sems, recv_sems, local_sems = refs[7 + 2 * ng:]
            phases = [_gather_phases(refs[4 + n], refs[7 + ng + n], send_sems, recv_sems, local_sems, n)
                      for n in range(ng)]
            now = pl.program_id(0) * nq + qi
            for k, at in enumerate((0, (3 * n_steps) // 4, n_steps - 1)):
                @pl.when(now == at)
                def _(k=k):
                    for ph in phases:
                        ph[k]()

        row, col, mix, _ = _sb_masks(qr)
        qbs = [q_ref[b, :, ln].astype(BF16) for b, ln in chains]
        n_blocks = (qi + 1) * per_tile

        def step(n, carry):
            j = n_blocks - 1 - n
            rows = pl.ds(pl.multiple_of(j * blk, blk), blk)
            causal = (j * blk + col) < (qi * qr + row)
            zs = [lax.dot_general(qb, k_ref[b, rows, ln].astype(BF16), _NT, preferred_element_type=F32) * SB_SCALE
                  for (b, ln), qb in zip(chains, qbs)]
            lszs = [_log_sigmoid(z) for z in zs]
            boths = _split_dots([jnp.where(causal, lsz - z, 0.0) for lsz, z in zip(lszs, zs)], mix)
            atts = [jnp.where(causal, jnp.exp(lsz + both[:, :blk] + later), 0.0).astype(BF16)
                    for lsz, both, (_, later) in zip(lszs, boths, carry)]
            outs = [out + jnp.dot(att, v_ref[b, rows, ln].astype(BF16), preferred_element_type=F32)
                    for (b, ln), att, (out, _) in zip(chains, atts, carry)]
            return tuple((out, later + both[:, blk:]) for out, both, (_, later) in zip(outs, boths, carry))

        zero = jnp.zeros((qr, SB_HEAD), F32)
        done = lax.fori_loop(0, n_blocks, step, tuple((zero, zero) for _ in chains))
        for (b, ln), (out, total) in zip(chains, done):
            o_ref[b, :, ln] = out
            tot_ref[b, :, ln] = total
            out_ref[b, :, ln] = (out * _silu(g_ref[b, :, ln])).astype(BF16)

    width = n_heads * SB_HEAD
    p3 = p.reshape(bl, seq, p.shape[1])
    f32 = jax.ShapeDtypeStruct((bl, seq, width), F32)
    outs = pl.pallas_call(
        body, name="sb_attn_fwd", grid=(n_groups, nq),
        out_shape=(jax.ShapeDtypeStruct((bl, seq, width), BF16), f32, f32, *[_gathered_shape(g) for g in gathers]),
        in_specs=[qspec(q_col), kspec(k_col), kspec(v_col), qspec(g_col)] + [_HBM] * ng,
        out_specs=(qspec(0), qspec(0), qspec(0), *[_HBM] * ng),
        scratch_shapes=_gather_scratch(ng) if ng else [],
        compiler_params=_params("arbitrary", "arbitrary"),
    )(p3, p3, p3, p3, *gathers)
    return tuple(a.reshape(t, width) for a in outs[:3]) + tuple(outs[3:])


def _sb_bwd(p, bl, seq, cols, n_heads, dy, dy_col, o, tot, swaps=()):
    t = p.shape[0]
    q_col, k_col, v_col, g_col = cols
    blk = SB_BLOCK
    ns = len(swaps)
    n_groups = n_heads // SB_HEADS_PER_STEP
    chains = _sb_chains(bl)
    qr, qspec, kspec = _sb_specs(bl, seq, cols + (dy_col,), n_heads, SB_QUERY_ROWS)
    nq = seq // qr
    per_tile = qr // blk

    def body(*refs):
        q_ref, k_ref, v_ref, g_ref, dy_ref, o_ref, tot_ref = refs[:7]
        dq_out, dk_out, dv_out, dg_out = refs[7 + ns:11 + ns]
        dk_acc, dv_acc = refs[11 + 2 * ns:13 + 2 * ns]
        group, qi = pl.program_id(0), pl.program_id(1)

        if ns:
            send_sems, recv_sems = refs[13 + 2 * ns:]
            phases = [_swap_phases(refs[7 + n], refs[11 + ns + n], send_sems, recv_sems, n) for n in range(ns)]
            for k, at in enumerate(((0, 0), (n_groups - 1, nq - 1))):
                @pl.when((group == at[0]) & (qi == at[1]))
                def _(k=k):
                    for ph in phases:
                        ph[k]()

        @pl.when(qi == 0)
        def _():
            dk_acc[...] = jnp.zeros_like(dk_acc)
            dv_acc[...] = jnp.zeros_like(dv_acc)

        row, col, mix, mix_t = _sb_masks(qr)
        qbs, dobs, totals = [], [], []
        for b, ln in chains:
            gate = g_ref[b, :, ln]
            sg = jax.nn.sigmoid(gate)
            dyv = dy_ref[b, :, ln]
            dg_out[b, :, ln] = (dyv * o_ref[b, :, ln] * (sg * (1.0 + gate * (1.0 - sg)))).astype(BF16)
            dobs.append((dyv * (gate * sg)).astype(BF16))
            qbs.append(q_ref[b, :, ln].astype(BF16))
            totals.append(tot_ref[b, :, ln])

        def step(j, carry):
            rows = pl.ds(pl.multiple_of(j * blk, blk), blk)
            causal = (j * blk + col) < (qi * qr + row)
            kbs = [k_ref[b, rows, ln].astype(BF16) for b, ln in chains]
            zs = [lax.dot_general(qb, kb, _NT, preferred_element_type=F32) * SB_SCALE for qb, kb in zip(qbs, kbs)]
            datts = [lax.dot_general(dob, v_ref[b, rows, ln].astype(BF16), _NT, preferred_element_type=F32)
                     for (b, ln), dob in zip(chains, dobs)]
            lszs = [_log_sigmoid(z) for z in zs]
            boths = _split_dots([jnp.where(causal, lsz - z, 0.0) for lsz, z in zip(lszs, zs)], mix)
            seens = [seen + both[:, blk:] for both, (_, seen, _) in zip(boths, carry)]
            atts = [jnp.where(causal, jnp.exp(lsz + both[:, :blk] + (total - seen)), 0.0)
                    for lsz, both, total, seen in zip(lszs, boths, totals, seens)]
            dls = [att * datt for att, datt in zip(atts, datts)]
            for (b, ln), att, dob in zip(chains, atts, dobs):
                dv_acc[b, rows, ln] += lax.dot_general(att.astype(BF16), dob, _TN, preferred_element_type=F32)
            boths_t = _split_dots(dls, mix_t)
            dkeeps = [jnp.where(causal, both_t[:, :blk] + dl_before, 0.0)
                      for both_t, (_, _, dl_before) in zip(boths_t, carry)]
            dzbs = [(((dl + dkeep) * jax.nn.sigmoid(-z) - dkeep) * SB_SCALE).astype(BF16)
                    for dl, dkeep, z in zip(dls, dkeeps, zs)]
            dqs = [dq + jnp.dot(dzb, kb, preferred_element_type=F32) for dzb, kb, (dq, _, _) in zip(dzbs, kbs, carry)]
            for (b, ln), dzb, qb in zip(chains, dzbs, qbs):
                dk_acc[b, rows, ln] += lax.dot_general(dzb, qb, _TN, preferred_element_type=F32)
            return tuple((dq, seen, dl_before + both_t[:, blk:])
                         for dq, seen, both_t, (_, _, dl_before) in zip(dqs, seens, boths_t, carry))

        zero = jnp.zeros((qr, SB_HEAD), F32)
        done = lax.fori_loop(0, (qi + 1) * per_tile, step, tuple((zero, zero, zero) for _ in chains))
        for (b, ln), (dq, _, _) in zip(chains, done):
            dq_out[b, :, ln] = dq.astype(BF16)

        @pl.when(qi == nq - 1)
        def _():
            dk_out[...] = dk_acc[...].astype(BF16)
            dv_out[...] = dv_acc[...].astype(BF16)

    width = n_heads * SB_HEAD
    shape = jax.ShapeDtypeStruct((bl, seq, width), BF16)
    as3 = lambda a: a.reshape(bl, seq, a.shape[1])
    p3 = as3(p)
    acc = pltpu.VMEM((bl, seq, SB_HEADS_PER_STEP * SB_HEAD), F32)
    outs = pl.pallas_call(
        body, name="sb_attn_bwd", grid=(n_groups, nq),
        out_shape=(shape, shape, shape, shape, *[_swapped_shape(s) for s in swaps]),
        in_specs=[qspec(q_col), kspec(k_col), kspec(v_col), qspec(g_col), qspec(dy_col), qspec(0), qspec(0)]
        + [_HBM] * ns,
        out_specs=(qspec(0), kspec(0), kspec(0), qspec(0), *[_HBM] * ns),
        scratch_shapes=[acc, acc] + (_swap_scratch(ns) if ns else []),
        compiler_params=_params("arbitrary", "arbitrary"),
    )(p3, p3, p3, p3, as3(dy), as3(o), as3(tot), *swaps)
    return tuple(a.reshape(t, width) for a in outs[:4]) + tuple(outs[4:])


def _gelu(x):
    return 0.5 * x * (1.0 + lax.erf(x * (2.0 ** -0.5)))


def _sgu_math(us, vs, gs, ln_g, ln_b, ws, bs):
    width = sum(v.shape[1] for v in vs)
    vg = [_gelu(v) for v in vs]
    mu = sum(jnp.sum(v, axis=1, keepdims=True) for v in vg) * (1.0 / width)
    dl = [v - mu for v in vg]
    var = sum(jnp.sum(d * d, axis=1, keepdims=True) for d in dl) * (1.0 / width)
    rstd = lax.rsqrt(var + LN_EPS)
    n = ws[0].shape[0]
    tri = lax.broadcasted_iota(jnp.int32, (n, n), 0) >= lax.broadcasted_iota(jnp.int32, (n, n), 1)
    outs = []
    for i in range(len(vs)):
        vn = dl[i] * rstd * ln_g[i] + ln_b[i]
        mixed = _dot32(jnp.where(tri, ws[i], 0.0), vn) + bs[i]
        outs.append(_gelu(us[i]) * mixed * _silu(gs[i]))
    return outs


def _sgu_load(p_ref, lng_ref, lnb_ref, ws_ref, bs_ref, width):
    gd = width // SGU_GROUPS
    grp = lambda ref, base, i: ref[:, base + i * gd:base + (i + 1) * gd]
    idx = range(SGU_GROUPS)
    return ([grp(p_ref, 0, i) for i in idx], [grp(p_ref, width, i) for i in idx],
            [grp(p_ref, 2 * width, i) for i in idx], [grp(lng_ref, 0, i) for i in idx],
            [grp(lnb_ref, 0, i) for i in idx], [ws_ref[i] for i in idx], [bs_ref[i] for i in idx])


def _sgu_fwd(p, ln_g, ln_b, w_s, b_s):
    t = p.shape[0]
    width = p.shape[1] // 3
    gd = width // SGU_GROUPS
    tm = SGU_CHUNK

    def body(p_ref, lng_ref, lnb_ref, ws_ref, bs_ref, y_ref):
        outs = _sgu_math(*_sgu_load(p_ref, lng_ref, lnb_ref, ws_ref, bs_ref, width))
        for i in range(SGU_GROUPS):
            y_ref[:, i * gd:(i + 1) * gd] = outs[i].astype(BF16)

    full = lambda a: pl.BlockSpec(a.shape, lambda i: (0,) * a.ndim)
    return pl.pallas_call(
        body, name="sgu_fwd", grid=(t // tm,),
        out_shape=jax.ShapeDtypeStruct((t, width), BF16),
        in_specs=[pl.BlockSpec((tm, 3 * width), lambda i: (i, 0)), full(ln_g), full(ln_b), full(w_s), full(b_s)],
        out_specs=pl.BlockSpec((tm, width), lambda i: (i, 0)),
        compiler_params=_params("parallel"),
    )(p, ln_g, ln_b, w_s, b_s)


def _sgu_bwd(p, ln_g, ln_b, w_s, b_s, dy):
    t = p.shape[0]
    width = p.shape[1] // 3
    gd = width // SGU_GROUPS
    tm = SGU_CHUNK

    def body(p_ref, lng_ref, lnb_ref, ws_ref, bs_ref, dy_ref, dp_out, dlng_out, dlnb_out, dws_out, dbs_out):
        @pl.when(pl.program_id(0) == 0)
        def _():
            for ref in (dlng_out, dlnb_out, dws_out, dbs_out):
                ref[...] = jnp.zeros_like(ref)

        _, vjp = jax.vjp(_sgu_math, *_sgu_load(p_ref, lng_ref, lnb_ref, ws_ref, bs_ref, width))
        dus, dvs, dgs, dlng, dlnb, dws, dbs = vjp(
            [dy_ref[:, i * gd:(i + 1) * gd] for i in range(SGU_GROUPS)])
        for i in range(SGU_GROUPS):
            cols = slice(i * gd, (i + 1) * gd)
            dp_out[:, i * gd:(i + 1) * gd] = dus[i].astype(BF16)
            dp_out[:, width + i * gd:width + (i + 1) * gd] = dvs[i].astype(BF16)
            dp_out[:, 2 * width + i * gd:2 * width + (i + 1) * gd] = dgs[i].astype(BF16)
            dlng_out[:, cols] += dlng[i]
            dlnb_out[:, cols] += dlnb[i]
            dws_out[i] += dws[i]
            dbs_out[i] += dbs[i]

    full = lambda a: pl.BlockSpec(a.shape, lambda i: (0,) * a.ndim)
    like = lambda a: jax.ShapeDtypeStruct(a.shape, F32)
    return pl.pallas_call(
        body, name="sgu_bwd", grid=(t // tm,),
        out_shape=(jax.ShapeDtypeStruct((t, 3 * width), BF16), like(ln_g), like(ln_b), like(w_s), like(b_s)),
        in_specs=[pl.BlockSpec((tm, 3 * width), lambda i: (i, 0)), full(ln_g), full(ln_b), full(w_s), full(b_s),
                  pl.BlockSpec((tm, width), lambda i: (i, 0))],
        out_specs=(pl.BlockSpec((tm, 3 * width), lambda i: (i, 0)), full(ln_g), full(ln_b), full(w_s), full(b_s)),
        compiler_params=_params("arbitrary"),
    )(p, ln_g, ln_b, w_s, b_s, dy)


def _sum_slabs(parts, name):
    n_parts, rows, cols = parts.shape
    tr = _tile(rows, max(16, (1 << 18) // cols), 16)

    def body(p_ref, o_ref):
        acc = p_ref[0].astype(F32)
        for d in range(1, n_parts):
            acc = acc + p_ref[d].astype(F32)
        o_ref[...] = acc

    return pl.pallas_call(
        body, name=name, grid=(rows // tr,),
        out_shape=jax.ShapeDtypeStruct((rows, cols), F32),
        in_specs=[pl.BlockSpec((n_parts, tr, cols), lambda i: (0, i, 0))],
        out_specs=pl.BlockSpec((tr, cols), lambda i: (i, 0)),
        compiler_params=_params("parallel"),
    )(parts)


def _adamw(w, g, m, v, name):
    rows, cols = w.shape
    tr = _tile(rows, max(8, (1 << 18) // cols), 8)

    def body(w_ref, g_ref, m_ref, v_ref, d_out, m_out, v_out):
        gv = g_ref[...]
        mn = ADAM_B1 * m_ref[...] + (1.0 - ADAM_B1) * gv
        vn = ADAM_B2 * v_ref[...] + (1.0 - ADAM_B2) * (gv * gv)
        m_hat = mn / (1.0 - ADAM_B1 ** ADAM_STEP)
        v_hat = vn / (1.0 - ADAM_B2 ** ADAM_STEP)
        d_out[...] = -ADAM_LR * (m_hat / (jnp.sqrt(v_hat) + ADAM_EPS) + ADAM_WD * w_ref[...])
        m_out[...] = mn
        v_out[...] = vn

    blk = pl.BlockSpec((tr, cols), lambda i: (i, 0))
    shape = jax.ShapeDtypeStruct((rows, cols), F32)
    return pl.pallas_call(
        body, name=name, grid=(rows // tr,),
        out_shape=(shape, shape, shape),
        in_specs=[blk, blk, blk, blk], out_specs=(blk, blk, blk),
        compiler_params=_params("parallel"),
    )(w, g, m, v)


PACK_COLS = 1024


def _pack(arrays):
    flat = jnp.concatenate([a.reshape(-1).astype(F32) for a in arrays])
    rows = -(-flat.shape[0] // (8 * PACK_COLS)) * 8
    return jnp.pad(flat, (0, rows * PACK_COLS - flat.shape[0])).reshape(rows, PACK_COLS)


def _unpack(packed, shapes):
    flat = packed.reshape(-1)
    out, at = [], 0
    for s in shapes:
        n = 1
        for d in s:
            n *= d
        out.append(flat[at:at + n].reshape(s))
        at += n
    return out


def kernel(x, norm_g, final_norm_g, e_w_in, e_shift_mu, e_w_decay_up, e_w0, e_a_up, e_a0, e_k_k, e_k_a, e_r_k, e_gn_g, e_gn_b, e_w_out, o_w_in, o_ln_g, o_ln_b, o_w_s, o_b_s, o_w_out, loss_target, m_norm_g, m_final_norm_g, m_e_w_in, m_e_shift_mu, m_e_w_decay_up, m_e_w0, m_e_a_up, m_e_a0, m_e_k_k, m_e_k_a, m_e_r_k, m_e_gn_g, m_e_gn_b, m_e_w_out, m_o_w_in, m_o_ln_g, m_o_ln_b, m_o_w_s, m_o_b_s, m_o_w_out, v_norm_g, v_final_norm_g, v_e_w_in, v_e_shift_mu, v_e_w_decay_up, v_e_w0, v_e_a_up, v_e_a0, v_e_k_k, v_e_k_a, v_e_r_k, v_e_gn_g, v_e_gn_b, v_e_w_out, v_o_w_in, v_o_ln_g, v_o_ln_b, v_o_w_s, v_o_b_s, v_o_w_out):
    weights = dict(norm_g=norm_g, final_norm_g=final_norm_g, e_w_in=e_w_in, e_shift_mu=e_shift_mu,
                   e_w_decay_up=e_w_decay_up, e_w0=e_w0, e_a_up=e_a_up, e_a0=e_a0, e_k_k=e_k_k, e_k_a=e_k_a,
                   e_r_k=e_r_k, e_gn_g=e_gn_g, e_gn_b=e_gn_b, e_w_out=e_w_out, o_w_in=o_w_in, o_ln_g=o_ln_g,
                   o_ln_b=o_ln_b, o_w_s=o_w_s, o_b_s=o_b_s, o_w_out=o_w_out)
    mom1 = dict(norm_g=m_norm_g, final_norm_g=m_final_norm_g, e_w_in=m_e_w_in, e_shift_mu=m_e_shift_mu,
                e_w_decay_up=m_e_w_decay_up, e_w0=m_e_w0, e_a_up=m_e_a_up, e_a0=m_e_a0, e_k_k=m_e_k_k,
                e_k_a=m_e_k_a, e_r_k=m_e_r_k, e_gn_g=m_e_gn_g, e_gn_b=m_e_gn_b, e_w_out=m_e_w_out,
                o_w_in=m_o_w_in, o_ln_g=m_o_ln_g, o_ln_b=m_o_ln_b, o_w_s=m_o_w_s, o_b_s=m_o_b_s,
                o_w_out=m_o_w_out)
    mom2 = dict(norm_g=v_norm_g, final_norm_g=v_final_norm_g, e_w_in=v_e_w_in, e_shift_mu=v_e_shift_mu,
                e_w_decay_up=v_e_w_decay_up, e_w0=v_e_w0, e_a_up=v_e_a_up, e_a0=v_e_a0, e_k_k=v_e_k_k,
                e_k_a=v_e_k_a, e_r_k=v_e_r_k, e_gn_g=v_e_gn_g, e_gn_b=v_e_gn_b, e_w_out=v_e_w_out,
                o_w_in=v_o_w_in, o_ln_g=v_o_ln_g, o_ln_b=v_o_ln_b, o_w_s=v_o_w_s, o_b_s=v_o_b_s,
                o_w_out=v_o_w_out)
    names = list(weights)
    big = ("e_w_in", "e_w_out", "o_w_in", "o_w_out")

    bl, seq, d = x.shape
    t = bl * seq
    width = e_w0.shape[1]
    lora = e_w_decay_up.shape[1]
    n_sb = width // SB_HEAD
    me = 4 * lax.axis_index("x") + 2 * lax.axis_index("y") + lax.axis_index("c")

    e_win_t = _all_gather(e_w_in[0].T.astype(BF16), "gather_e_w_in").reshape(-1, d)
    later_shards = (e_w_out[0].astype(BF16), o_w_in[0].T.astype(BF16), o_w_out[0].astype(BF16))
    sharded_small = ("e_w_decay_up", "e_a_up", "o_ln_g", "o_ln_b")
    small_shapes = [weights[n][0].shape for n in sharded_small]
    got = _all_gather(_pack([weights[n][0] for n in sharded_small]), "gather_small")
    per_dev = [_unpack(got[dev], small_shapes) for dev in range(N_DEV)]
    wd, wa, ln_g, ln_b = [jnp.concatenate([per_dev[dev][i] for dev in range(N_DEV)], axis=-1).reshape(
        small_shapes[i][:-1] + (-1,)) for i in range(4)]
    ln_g, ln_b = ln_g.reshape(1, -1), ln_b.reshape(1, -1)
    e_ind, e_ind_t = _head_indicator(width)
    b_s3 = o_b_s[0][:, :, None]

    x2d = x.reshape(t, d)
    target = loss_target.reshape(t, d)
    cols_rwkv = 3 * width + 2 * lora
    assert cols_rwkv % LANES == 0 and width % LANES == 0
    sb0 = (cols_rwkv + width) // SB_HEAD
    sb_cols = (sb0, sb0 + n_sb, sb0 + 2 * n_sb, sb0 + 3 * n_sb)

    h0 = _rms_fwd(x2d, norm_g[0:1], "rms0_fwd")
    p = _matmul_panel(h0, e_win_t, "e_in_fwd", tm=1024, tn=1280)
    g_rwkv = p[:, cols_rwkv:cols_rwkv + width]
    r, w, kx, v, kk, b = _rwkv_prep(p, seq, (width, lora), e_shift_mu, wd, e_w0, wa, e_a0, e_k_k, e_k_a,
                                    e_ind, e_ind_t)
    sk = [_to_scan_k(a, bl, seq) for a in (kk, w, b, kx, r)]
    sv = _to_scan_v(v, bl, seq)
    ys_scan, states, sa_all, e_wout, o_wout = _scan_fwd(*sk, sv, gathers=(later_shards[0], later_shards[2]))
    ys = _from_scan_v(ys_scan, bl, seq)
    ya = _rwkv_post(ys, r, kx, v, g_rwkv, 0, e_gn_g, e_gn_b, e_r_k, e_ind, e_ind_t)
    yb, sb_o, sb_tot, o_win_t = _sb_fwd(p, bl, seq, sb_cols, n_sb, gathers=(later_shards[1],))
    e_wout, o_win_t, o_wout = (a.reshape(-1, d) for a in (e_wout, o_win_t, o_wout))
    y = jnp.concatenate([ya, yb], axis=1)
    x1 = _matmul(y, e_wout, "nn", F32, "e_out_fwd", res=x2d)
    h1 = _rms_fwd(x1, norm_g[1:2], "rms1_fwd")
    p2 = _matmul_panel(h1, o_win_t, "o_in_fwd", tm=1024, tn=1024)
    y2 = _sgu_fwd(p2, ln_g, ln_b, o_w_s[0], b_s3)
    x2 = _matmul(y2, o_wout, "nn", F32, "o_out_fwd", res=x1)
    dx2, d_final_g, loss_part = _final_loss(x2, final_norm_g.reshape(1, d), target, "final_loss")

    dy2 = _matmul(dx2, o_wout, "nt", F32, "o_out_bwd_x", tm=2048)
    d_o_wout = _matmul(y2, dx2, "tn", F32, "o_out_bwd_w", tn=2048)
    dp2, d_ln_g, d_ln_b, d_w_s, d_b_s3 = _sgu_bwd(p2, ln_g, ln_b, o_w_s[0], b_s3, dy2)
    dh1 = _matmul(dp2, o_win_t, "nn", F32, "o_in_bwd_x", tm=2048, tk=1536)
    d_o_win_t = _matmul(dp2, h1, "tn", F32, "o_in_bwd_w", tn=2048)
    dx1, d_g1 = _rms_bwd(x1, norm_g[1:2], dh1, dx2, "rms1_bwd")
    dy = _matmul(dx1, e_wout, "nt", F32, "e_out_bwd_x", tm=2048)
    d_e_wout = _matmul(y, dx1, "tn", F32, "e_out_bwd_w", tn=2048)
    core = lax.axis_index("c").astype(jnp.int32).reshape(1)
    by_owner = lambda full: full.reshape((N_CHIPS, 2, full.shape[0] // N_DEV, full.shape[1]))
    early = {"e_w_out": by_owner(d_e_wout), "o_w_in": by_owner(d_o_win_t), "o_w_out": by_owner(d_o_wout)}
    dq, dk, dv_sb, dg_sb, *swapped = _sb_bwd(p, bl, seq, sb_cols, n_sb, dy, n_sb, sb_o, sb_tot,
                                             swaps=tuple(early.values()))
    partials = [_pair_sum(full, got, core, "pairsum_" + n) for (n, full), got in zip(early.items(), swapped)]
    dys, dr1, dkx1, dv1, dg_rwkv, d_gn_g, d_gn_b, d_r_k = _rwkv_post_bwd(
        ys, r, kx, v, g_rwkv, 0, e_gn_g, e_gn_b, e_r_k, e_ind, e_ind_t, dy, 0)
    dkk_s, dw_s, db_s, dkx_s, dr_s, dv_s, *landed = _scan_bwd(*sk, sv, states, sa_all, _to_scan_v(dys, bl, seq),
                                                              scatters=tuple(partials))
    early_sums = {n: _sum_slabs(parts, "sum_" + n) for n, parts in zip(early, landed)}
    dkk, dw, db, dkx2, dr2 = [_from_scan_k(a, bl, seq) for a in (dkk_s, dw_s, db_s, dkx_s, dr_s)]
    dv2 = _from_scan_v(dv_s, bl, seq)
    dp_rwkv, d_mu, d_wd, d_w0, d_wa, d_a0, d_k_k, d_k_a = _rwkv_prep_bwd(
        p, seq, (width, lora), e_shift_mu, wd, e_w0, wa, e_a0, e_k_k, e_k_a, e_ind, e_ind_t,
        dr1, dw, dkx1, dv1, dkk, db, (dr2, dkx2, dv2))
    dp = jnp.concatenate([dp_rwkv, dg_rwkv, dq, dk, dv_sb, dg_sb], axis=1)
    last = by_owner(_matmul(dp, h0, "tn", BF16, "e_in_bwd_w", tm=1280, tn=2048))
    last = _pair_sum(last, _pair_swap(last, "swap_e_w_in"), core, "pairsum_e_w_in")
    dh0, last = _matmul(dp, e_win_t, "nn", F32, "e_in_bwd_x", tm=2048, tk=1280, scatters=(last,))
    grad_x, d_g0 = _rms_bwd(x2d, norm_g[0:1], dh0, dx1, "rms0_bwd")

    grads = {
        "e_w_in": _sum_slabs(last, "sum_e_w_in").T[None],
        "e_w_out": early_sums["e_w_out"][None],
        "o_w_in": early_sums["o_w_in"].T[None],
        "o_w_out": early_sums["o_w_out"][None],
    }
    small_full = {
        "norm_g": jnp.concatenate([d_g0, d_g1], axis=0), "final_norm_g": d_final_g.reshape(-1),
        "e_shift_mu": d_mu, "e_w_decay_up": d_wd[None], "e_w0": d_w0, "e_a_up": d_wa[None], "e_a0": d_a0,
        "e_k_k": d_k_k, "e_k_a": d_k_a, "e_r_k": d_r_k, "e_gn_g": d_gn_g, "e_gn_b": d_gn_b,
        "o_ln_g": d_ln_g, "o_ln_b": d_ln_b, "o_w_s": d_w_s[None], "o_b_s": d_b_s3[:, :, 0][None],
    }
    small = [n for n in names if n not in big]
    parts = _all_gather(_pack([small_full[n] for n in small]), "gather_small_grads")
    totals = _unpack(_sum_slabs(parts, "sum_small_grads"), [small_full[n].shape for n in small])
    for n, g in zip(small, totals):
        if n in sharded_small:
            size = weights[n].shape[-1]
            g = lax.dynamic_slice_in_dim(g, me * size, size, axis=g.ndim - 1)
        grads[n] = g.reshape(weights[n].shape)

    delta, new_m, new_v = {}, {}, {}
    for n in big:
        shp = weights[n].shape
        flat = lambda a: a.reshape(shp[-2], shp[-1])
        dl, mn, vn = _adamw(flat(weights[n]), flat(grads[n]), flat(mom1[n]), flat(mom2[n]), "adamw_" + n)
        delta[n], new_m[n], new_v[n] = dl.reshape(shp), mn.reshape(shp), vn.reshape(shp)
    packed = [_pack([src[n] for n in small]) for src in (weights, grads, mom1, mom2)]
    outs = _adamw(*packed, "adamw_small")
    shapes = [weights[n].shape for n in small]
    for dst, arr in zip((delta, new_m, new_v), outs):
        for n, a in zip(small, _unpack(arr, shapes)):
            dst[n] = a

    loss = lax.psum(loss_part[0, 0], ("x", "y", "c"))
    return (loss, grad_x.reshape(bl, seq, d), *[grads[n] for n in names], *[delta[n] for n in names],
            *[new_m[n] for n in names], *[new_v[n] for n in names])
```
